```python
import jax, jax.numpy as jnp
from jax import lax
import numpy as np

D_MODEL = 1024
BATCH = 4
SEQ = 4096
DEPTH = 1

A_HEAD_DIM = 64
A_WIDTH = D_MODEL // 2
A_HEADS = A_WIDTH // A_HEAD_DIM
CHUNK = 128
B_HEAD_DIM = 64
B_WIDTH = D_MODEL // 2
B_HEADS = B_WIDTH // B_HEAD_DIM
DILATED_PATTERNS = ((128, 1), (512, 4), (2048, 16))
BAND_BLOCK = 128
MIX_WIDTH = A_WIDTH + B_WIDTH
IN_COLS = 2 * A_WIDTH + 3 * B_WIDTH
N_EXPERTS = 32
TOP_K = 4
D_EXPERT = D_MODEL
SWIGLU_ALPHA = 1.702
SWIGLU_LIMIT = 7.0
MOE_BLOCK = 128
DN_ALPHA = (2 * DEPTH) ** 0.25
DN_BETA = (8 * DEPTH) ** -0.25
LN_EPS = 1e-5

kernel_name = "hybrid_sgu_dilated_attn_moe_deepnorm"


def layer_norm(t, g, b):
    tf = t.astype(jnp.float32)
    mu = tf.mean(-1, keepdims=True)
    var = jnp.square(tf - mu).mean(-1, keepdims=True)
    return ((tf - mu) * lax.rsqrt(var + LN_EPS) * g + b).astype(t.dtype)


def rms_norm(t, g):
    tf = t.astype(jnp.float32)
    return (tf * lax.rsqrt(jnp.square(tf).mean(-1, keepdims=True) + LN_EPS) * g).astype(t.dtype)


def chunked_sgu(u, v, w_s, b_s, ln_g, ln_b):
    B, S, _ = u.shape
    nc = S // CHUNK
    vg = v.reshape(B, S, A_HEADS, A_HEAD_DIM).astype(jnp.float32)
    mu = vg.mean(-1, keepdims=True)
    var = jnp.square(vg - mu).mean(-1, keepdims=True)
    vn = ((vg - mu) * lax.rsqrt(var + LN_EPS) * ln_g.reshape(A_HEADS, A_HEAD_DIM)
          + ln_b.reshape(A_HEADS, A_HEAD_DIM)).astype(v.dtype)
    vc = vn.reshape(B, nc, CHUNK, A_HEADS, A_HEAD_DIM)
    w_causal = jnp.tril(w_s)
    gate = jnp.einsum('gts,bnsgc->bntgc', w_causal, vc) + b_s.T[None, None, :, :, None]
    return u * gate.reshape(B, S, A_WIDTH)


def dilated_band_attention(q, k, v, window, dilation):
    B, S, H, Dh = q.shape
    reach = window // dilation
    span = dilation * BAND_BLOCK
    L = -(-S // span) * span
    nb = L // span

    def to_sub(t):
        t = jnp.pad(t, ((0, 0), (0, L - S), (0, 0), (0, 0)))
        t = t.reshape(B, L // dilation, dilation, H, Dh).transpose(0, 2, 3, 1, 4)
        return t.reshape(B, dilation, H, nb, BAND_BLOCK, Dh)

    def with_prev(t):
        prev = jnp.pad(t[:, :, :, :-1], ((0, 0), (0, 0), (0, 0), (1, 0), (0, 0), (0, 0)))
        return jnp.concatenate([prev, t], axis=4)

    qs = to_sub(q)
    kk = with_prev(to_sub(k))
    vv = with_prev(to_sub(v)).astype(jnp.float32)
    s = jnp.einsum('bdhnqc,bdhnkc->bdhnqk', qs, kk, preferred_element_type=jnp.float32)
    qi = jnp.arange(BAND_BLOCK)[:, None] + BAND_BLOCK
    kj = jnp.arange(2 * BAND_BLOCK)[None, :]
    dist = qi - kj
    band = (dist >= 0) & (dist <= reach)
    first = (jnp.arange(nb) == 0)[:, None, None]
    mask = band[None] & ~(first & (kj < BAND_BLOCK)[None])
    s = jnp.where(mask, s, -jnp.inf)
    m = s.max(-1)
    p = jnp.exp(s - m[..., None])
    l = p.sum(-1)
    o = jnp.einsum('bdhnqk,bdhnkc->bdhnqc', p, vv) / l[..., None]

    def from_sub(t):
        tail = t.shape[5:]
        t = t.reshape((B, dilation, H, L // dilation) + tail)
        t = jnp.moveaxis(t, 3, 1)
        return t.reshape((B, L, H) + tail)[:, :S]

    return from_sub(o), from_sub(m), from_sub(l)


def dilated_mixture_attention(q, k, v):
    B, S, _ = q.shape
    q = (q * (B_HEAD_DIM ** -0.5)).reshape(B, S, B_HEADS, B_HEAD_DIM)
    k = k.reshape(B, S, B_HEADS, B_HEAD_DIM)
    v = v.reshape(B, S, B_HEADS, B_HEAD_DIM)
    res = [dilated_band_attention(q, k, v, w, d) for (w, d) in DILATED_PATTERNS]
    os_ = jnp.stack([r[0] for r in res], 0)
    ms = jnp.stack([r[1] for r in res], 0)
    ls = jnp.stack([r[2] for r in res], 0)
    wts = ls * jnp.exp(ms - ms.max(0, keepdims=True))
    out = (wts[..., None] * os_).sum(0) / wts.sum(0)[..., None]
    return out.reshape(B, S, B_WIDTH).astype(q.dtype)


def moe_ffn(h, w_router, b_router, w_gate, b_gate, w_up, b_up, w_down, b_down):
    B, S, D = h.shape
    T = B * S
    A = T * TOP_K
    hf = h.reshape(T, D)
    logits = jnp.dot(hf, w_router).astype(jnp.float32) + b_router.astype(jnp.float32)
    top_vals, top_idx = lax.top_k(logits, TOP_K)
    gates = jax.nn.softmax(top_vals, axis=-1)
    flat_e = top_idx.reshape(A)
    flat_tok = jnp.repeat(jnp.arange(T, dtype=jnp.int32), TOP_K)
    flat_g = gates.reshape(A)
    order = jnp.argsort(flat_e)
    se, stok, sg = flat_e[order], flat_tok[order], flat_g[order]
    counts = jnp.bincount(flat_e, length=N_EXPERTS)
    start = jnp.cumsum(counts) - counts
    pcounts = (counts + MOE_BLOCK - 1) // MOE_BLOCK * MOE_BLOCK
    pend = jnp.cumsum(pcounts)
    pstart = pend - pcounts
    dest = pstart[se] + (jnp.arange(A) - start[se])
    nblk = -(-A // MOE_BLOCK) + N_EXPERTS
    P = nblk * MOE_BLOCK
    row_tok = jnp.full((P,), T, jnp.int32).at[dest].set(stok)
    row_gate = jnp.zeros((P,), jnp.float32).at[dest].set(sg)
    block_e = jnp.clip(jnp.searchsorted(pend, jnp.arange(nblk) * MOE_BLOCK, side='right'), 0, N_EXPERTS - 1)
    h_pad = jnp.concatenate([hf, jnp.zeros((1, D), hf.dtype)], 0)

    def expert_block(args):
        tok, e = args
        xb = h_pad[tok]
        g = jnp.minimum(xb @ w_gate[e] + b_gate[e], SWIGLU_LIMIT)
        u = jnp.clip(xb @ w_up[e] + b_up[e], -SWIGLU_LIMIT, SWIGLU_LIMIT)
        act = (u + 1.0) * (g * jax.nn.sigmoid(SWIGLU_ALPHA * g))
        return act @ w_down[e] + b_down[e]

    out = lax.map(expert_block, (row_tok.reshape(nblk, MOE_BLOCK), block_e)).reshape(P, D)
    y = jnp.zeros((T + 1, D), h.dtype).at[row_tok].add((out * row_gate[:, None]).astype(h.dtype))
    return y[:T].reshape(B, S, D)


def setup_inputs(seed: int = 0) -> dict:
    key = jax.random.key(seed)
    ks = jax.random.split(key, 24)
    n = lambda i, shape: jax.random.normal(ks[i], shape, jnp.float32)
    L, D, E, F = DEPTH, D_MODEL, N_EXPERTS, D_EXPERT
    return {
        "x": n(0, (BATCH, SEQ, D)),
        "w_in": n(1, (L, D, IN_COLS)) * D ** -0.5,
        "sgu_w": n(2, (L, A_HEADS, CHUNK, CHUNK)) * CHUNK ** -0.5,
        "sgu_b": 1.0 + 0.02 * n(3, (L, A_HEADS, CHUNK)),
        "sgu_ln_g": 1.0 + 0.02 * n(4, (L, A_WIDTH)),
        "sgu_ln_b": 0.02 * n(5, (L, A_WIDTH)),
        "mix_norm_g": 1.0 + 0.02 * n(6, (L, MIX_WIDTH)),
        "w_out": n(7, (L, MIX_WIDTH, D)) * MIX_WIDTH ** -0.5 * DN_BETA,
        "ln1_g": 1.0 + 0.02 * n(8, (L, D)),
        "ln1_b": 0.02 * n(9, (L, D)),
        "w_router": n(10, (L, D, E)) * D ** -0.5,
        "b_router": 0.01 * n(11, (L, E)),
        "w_gate": n(12, (L, E, D, F)) * D ** -0.5,
        "b_gate": 0.01 * n(13, (L, E, F)),
        "w_up": n(14, (L, E, D, F)) * D ** -0.5,
        "b_up": 0.01 * n(15, (L, E, F)),
        "w_down": n(16, (L, E, F, D)) * F ** -0.5 * DN_BETA,
        "b_down": 0.01 * n(17, (L, E, D)),
        "ln2_g": 1.0 + 0.02 * n(18, (L, D)),
        "ln2_b": 0.02 * n(19, (L, D)),
    }


def reference(x, w_in, sgu_w, sgu_b, sgu_ln_g, sgu_ln_b, mix_norm_g, w_out, ln1_g, ln1_b,
              w_router, b_router, w_gate, b_gate, w_up, b_up, w_down, b_down, ln2_g, ln2_b):
    for i in range(DEPTH):
        p = x @ w_in[i]
        za = jax.nn.gelu(p[..., :2 * A_WIDTH])
        q = p[..., 2 * A_WIDTH:2 * A_WIDTH + B_WIDTH]
        k = p[..., 2 * A_WIDTH + B_WIDTH:2 * A_WIDTH + 2 * B_WIDTH]
        v = p[..., 2 * A_WIDTH + 2 * B_WIDTH:]
        a = chunked_sgu(za[..., :A_WIDTH], za[..., A_WIDTH:], sgu_w[i], sgu_b[i], sgu_ln_g[i], sgu_ln_b[i])
        b = dilated_mixture_attention(q, k, v)
        mixed = jnp.concatenate([rms_norm(a, mix_norm_g[i, :A_WIDTH]),
                                 rms_norm(b, mix_norm_g[i, A_WIDTH:])], axis=-1) @ w_out[i]
        h = layer_norm(DN_ALPHA * x + mixed, ln1_g[i], ln1_b[i])
        y = moe_ffn(h, w_router[i], b_router[i], w_gate[i], b_gate[i], w_up[i], b_up[i], w_down[i], b_down[i])
        x = layer_norm(DN_ALPHA * h + y, ln2_g[i], ln2_b[i])
    return x
```

```python
import jax
import jax.numpy as jnp
from jax import lax
from jax.experimental import pallas as pl
from jax.experimental.pallas import tpu as pltpu

F32 = jnp.float32
BF16 = jnp.bfloat16
I32 = jnp.int32

LN_EPS = 1e-5
HEAD_DIM = 64
SGU_CHUNK = 128
BAND_BLOCK = 128
DILATIONS = (1, 4, 16)
ATTN_SPAN = BAND_BLOCK * DILATIONS[-1]
N_EXPERTS = 32
TOP_K = 4
SWIGLU_ALPHA = 1.702
SWIGLU_LIMIT = 7.0
DEPTH = 1
DN_ALPHA = (2 * DEPTH) ** 0.25

LANES = 128
PROJ_ROWS = 512
MOE_ROWS = 256
VMEM_LIMIT = 56 * 1024 * 1024


def _params(sem):
    return pltpu.CompilerParams(dimension_semantics=sem, vmem_limit_bytes=VMEM_LIMIT)


def _dot(a, b):
    return jnp.dot(a, b, preferred_element_type=F32)


def _proj_kernel(x_ref, w_ref, gmat_ref, wpair_ref, sbias_ref, lng_ref, lnb_ref, mg_ref,
                 a_ref, q1_ref, k1_ref, v1_ref, q4_ref, k4_ref, v4_ref, q16_ref, k16_ref, v16_ref,
                 a_scr, t_scr):
    width = a_ref.shape[-1]
    xb = x_ref[...].astype(BF16)

    def proj(c):
        return _dot(xb, w_ref[:, c * width:(c + 1) * width])

    u = jax.nn.gelu(proj(0))
    v = jax.nn.gelu(proj(1))
    mean = _dot(v.astype(BF16), gmat_ref[...])
    d = v - mean
    var = _dot((d * d).astype(BF16), gmat_ref[...])
    vn = (d * lax.rsqrt(var + LN_EPS) * lng_ref[...] + lnb_ref[...]).astype(BF16)

    lane = lax.broadcasted_iota(I32, (SGU_CHUNK, LANES), 1)
    low = lane < HEAD_DIM
    zero = jnp.zeros((SGU_CHUNK, LANES), BF16)
    for c in range(PROJ_ROWS // SGU_CHUNK):
        rows = slice(c * SGU_CHUNK, (c + 1) * SGU_CHUNK)
        for j in range(width // LANES):
            cols = slice(j * LANES, (j + 1) * LANES)
            vp = vn[rows, cols]
            rhs = jnp.concatenate([jnp.where(low, vp, zero), jnp.where(low, zero, vp)], axis=0)
            gate = _dot(wpair_ref[j], rhs) + sbias_ref[:, cols]
            a_scr[rows, cols] = u[rows, cols] * gate
    a = a_scr[...]
    ms = jnp.mean(a * a, axis=-1, keepdims=True)
    a_ref[...] = (a * lax.rsqrt(ms + LN_EPS) * mg_ref[...]).astype(BF16)

    outs = ((q1_ref, q4_ref, q16_ref), (k1_ref, k4_ref, k16_ref), (v1_ref, v4_ref, v16_ref))
    for c, (o1, o4, o16) in enumerate(outs):
        t = proj(2 + c)
        if c == 0:
            t = t * (HEAD_DIM ** -0.5)
        o1[...] = t.astype(BF16)
        for j in range(width // LANES):
            cols = slice(j * LANES, (j + 1) * LANES)
            t_scr[j] = t[:, cols]
            for r in range(4):
                o4[r, :, cols] = t_scr[j, pl.ds(r, PROJ_ROWS // 4, stride=4), :].astype(BF16)
            for r in range(16):
                o16[r, :, cols] = t_scr[j, pl.ds(r, PROJ_ROWS // 16, stride=16), :].astype(BF16)


def _project(x2, w_in_b, gmat, wpair, sbias, lng, lnb, mg, batch, seq):
    t, dm = x2.shape
    width = dm // 2
    nt = seq // PROJ_ROWS
    per_span = ATTN_SPAN // PROJ_ROWS
    const = lambda *shape: pl.BlockSpec(shape, lambda b, m: (0,) * len(shape))
    o1 = jax.ShapeDtypeStruct((batch, seq, width), BF16)
    o4 = jax.ShapeDtypeStruct((batch, seq // 512, 4, BAND_BLOCK, width), BF16)
    o16 = jax.ShapeDtypeStruct((batch, seq // ATTN_SPAN, 16, BAND_BLOCK, width), BF16)
    s1 = pl.BlockSpec((None, PROJ_ROWS, width), lambda b, m: (b, m, 0))
    s4 = pl.BlockSpec((None, None, 4, BAND_BLOCK, width), lambda b, m: (b, m, 0, 0, 0))
    s16 = pl.BlockSpec((None, None, 16, PROJ_ROWS // 16, width),
                       lambda b, m: (b, m // per_span, 0, m % per_span, 0))
    return pl.pallas_call(
        _proj_kernel,
        grid=(batch, nt),
        in_specs=[pl.BlockSpec((PROJ_ROWS, dm), lambda b, m: (b * nt + m, 0)),
                  const(*w_in_b.shape), const(*gmat.shape), const(*wpair.shape), const(*sbias.shape),
                  const(1, width), const(1, width), const(1, width)],
        out_specs=[s1] + [s1, s1, s1] + [s4, s4, s4] + [s16, s16, s16],
        out_shape=[o1] + [o1, o1, o1] + [o4, o4, o4] + [o16, o16, o16],
        scratch_shapes=[pltpu.VMEM((PROJ_ROWS, width), F32),
                        pltpu.VMEM((width // LANES, PROJ_ROWS, LANES), F32)],
        compiler_params=_params(("parallel", "parallel")),
        name="proj_sgu",
    )(x2, w_in_b, gmat, wpair, sbias, lng, lnb, mg)


def _attn_kernel(q1, k1, v1, kp1, vp1, q4, k4, v4, kp4, vp4, q16, k16, v16, kp16, vp16,
                 o_ref, acc_o, acc_m, acc_l):
    not_first = pl.program_id(1) > 0
    blk = BAND_BLOCK
    row = lax.broadcasted_iota(I32, (blk, 2 * blk), 0)
    col = lax.broadcasted_iota(I32, (blk, 2 * blk), 1)
    band = jnp.logical_or(jnp.logical_and(col < blk, col >= row), jnp.logical_and(col >= blk, col - blk <= row))
    neg = jnp.where(not_first, 0.0, -jnp.inf).astype(F32)
    first_pen = jnp.where(col < blk, neg, 0.0)
    low = lax.broadcasted_iota(I32, (blk, LANES), 1) < HEAD_DIM
    zero = jnp.zeros((blk, LANES), BF16)

    def unit(q, kprev, kcur, vprev, vcur, maybe_first, rows, init):
        kk = jnp.concatenate([kprev, kcur], axis=0)
        vv = jnp.concatenate([vprev, vcur], axis=0)
        parts = []
        for head_low in (True, False):
            qm = jnp.where(low, q, zero) if head_low else jnp.where(low, zero, q)
            s = lax.dot_general(qm, kk, (((1,), (1,)), ((), ())), preferred_element_type=F32)
            s = jnp.where(band, s, -jnp.inf)
            if maybe_first:
                s = s + first_pen
            m = jnp.max(s, axis=-1, keepdims=True)
            p = jnp.exp(s - m)
            l = jnp.sum(p, axis=-1, keepdims=True)
            parts.append((_dot(p.astype(BF16), vv), m, l))
        (o0, m0, l0), (o1, m1, l1) = parts
        o = jnp.where(low, o0, o1)
        m = jnp.where(low, m0, m1)
        l = jnp.where(low, l0, l1)
        if init:
            acc_o[rows, :] = o
            acc_m[rows, :] = m
            acc_l[rows, :] = l
        else:
            mo = acc_m[rows, :]
            mn = jnp.maximum(mo, m)
            so = jnp.exp(mo - mn)
            sn = jnp.exp(m - mn)
            acc_o[rows, :] = acc_o[rows, :] * so + o * sn
            acc_l[rows, :] = acc_l[rows, :] * so + l * sn
            acc_m[rows, :] = mn

    unit(q1[0:blk, :], kp1[...], k1[0:blk, :], vp1[...], v1[0:blk, :], True, pl.ds(0, blk), True)

    def body1(n, c):
        cur = pl.ds(pl.multiple_of(n * blk, blk), blk)
        prev = pl.ds(pl.multiple_of((n - 1) * blk, blk), blk)
        unit(q1[cur, :], k1[prev, :], k1[cur, :], v1[prev, :], v1[cur, :], False, cur, True)
        return c

    lax.fori_loop(1, ATTN_SPAN // blk, body1, 0)

    def body4_first(r, c):
        unit(q4[0, r], kp4[r], k4[0, r], vp4[r], v4[0, r], True, pl.ds(r, blk, stride=4), False)
        return c

    lax.fori_loop(0, 4, body4_first, 0)

    def body4(n, c):
        s = n // 4
        r = n % 4
        unit(q4[s, r], k4[s - 1, r], k4[s, r], v4[s - 1, r], v4[s, r], False,
             pl.ds(s * (4 * blk) + r, blk, stride=4), False)
        return c

    lax.fori_loop(4, 4 * (ATTN_SPAN // (4 * blk)), body4, 0)

    def body16(r, c):
        unit(q16[r], kp16[r], k16[r], vp16[r], v16[r], True, pl.ds(r, blk, stride=16), False)
        return c

    lax.fori_loop(0, 16, body16, 0)

    o_ref[...] = (acc_o[...] / acc_l[...]).astype(BF16)


def _attention(qkv1, qkv4, qkv16, batch, seq):
    width = qkv1[0].shape[-1]
    nspan = seq // ATTN_SPAN
    blk = BAND_BLOCK
    n1 = ATTN_SPAN // blk
    n4 = ATTN_SPAN // (4 * blk)

    cur1 = pl.BlockSpec((None, ATTN_SPAN, LANES), lambda b, i, p: (b, i, p))
    prev1 = pl.BlockSpec((None, None, blk, LANES), lambda b, i, p: (b, jnp.maximum(i * n1 - 1, 0), 0, p))
    cur4 = pl.BlockSpec((None, n4, 4, blk, LANES), lambda b, i, p: (b, i, 0, 0, p))
    prev4 = pl.BlockSpec((None, None, 4, blk, LANES), lambda b, i, p: (b, jnp.maximum(i * n4 - 1, 0), 0, 0, p))
    cur16 = pl.BlockSpec((None, None, 16, blk, LANES), lambda b, i, p: (b, i, 0, 0, p))
    prev16 = pl.BlockSpec((None, None, 16, blk, LANES), lambda b, i, p: (b, jnp.maximum(i - 1, 0), 0, 0, p))

    q1, k1, v1 = qkv1
    q4, k4, v4 = qkv4
    q16, k16, v16 = qkv16
    k1b = k1.reshape(batch, seq // blk, blk, width)
    v1b = v1.reshape(batch, seq // blk, blk, width)
    return pl.pallas_call(
        _attn_kernel,
        grid=(batch, nspan, width // LANES),
        in_specs=[cur1, cur1, cur1, prev1, prev1,
                  cur4, cur4, cur4, prev4, prev4,
                  cur16, cur16, cur16, prev16, prev16],
        out_specs=pl.BlockSpec((None, ATTN_SPAN, LANES), lambda b, i, p: (b, i, p)),
        out_shape=jax.ShapeDtypeStruct((batch, seq, width), BF16),
        scratch_shapes=[pltpu.VMEM((ATTN_SPAN, LANES), F32)] * 3,
        compiler_params=_params(("parallel", "parallel", "parallel")),
        name="dilated_attn",
    )(q1, k1, v1, k1b, v1b, q4, k4, v4, k4, v4, q16, k16, v16, k16, v16)


def _outproj_kernel(x_ref, a_ref, b_ref, wo_ref, mgb_ref, g1_ref, b1_ref, wr_ref, br_ref, tri_ref,
                    h_ref, idx_ref, gate_ref, rank_ref, cnt_ref, run_scr):
    @pl.when(pl.program_id(0) == 0)
    def _():
        run_scr[...] = jnp.zeros_like(run_scr)

    width = a_ref.shape[-1]
    bf = b_ref[...].astype(F32)
    bn = (bf * lax.rsqrt(jnp.mean(bf * bf, axis=-1, keepdims=True) + LN_EPS) * mgb_ref[...]).astype(BF16)
    mixed = _dot(a_ref[...], wo_ref[0:width, :]) + _dot(bn, wo_ref[width:2 * width, :])
    z = DN_ALPHA * x_ref[...] + mixed
    mu = jnp.mean(z, axis=-1, keepdims=True)
    zc = z - mu
    var = jnp.mean(zc * zc, axis=-1, keepdims=True)
    h = zc * lax.rsqrt(var + LN_EPS) * g1_ref[...] + b1_ref[...]
    h_ref[...] = h

    logits = _dot(h.astype(BF16), wr_ref[...]) + br_ref[...]
    lane = lax.broadcasted_iota(I32, logits.shape, 1).astype(F32)
    work = logits
    vals, idxs, hots = [], [], []
    for _ in range(TOP_K):
        mv = jnp.max(work, axis=-1, keepdims=True)
        ix = jnp.min(jnp.where(work == mv, lane, float(LANES)), axis=-1, keepdims=True)
        hot = lane == ix
        work = jnp.where(hot, -jnp.inf, work)
        vals.append(mv)
        idxs.append(ix)
        hots.append(hot)
    exps = [jnp.exp(v - vals[0]) for v in vals]
    den = exps[0] + exps[1] + exps[2] + exps[3]
    gates = [e / den for e in exps]

    member = jnp.zeros(logits.shape, F32)
    for hot in hots:
        member = jnp.where(hot, 1.0, member)
    before = _dot(tri_ref[...], member.astype(BF16)) + run_scr[...]
    ranks = [jnp.sum(jnp.where(hot, before, 0.0), axis=-1, keepdims=True) for hot in hots]
    run_scr[...] = run_scr[...] + jnp.sum(member, axis=0, keepdims=True)
    cnt_ref[...] = run_scr[...]

    def spread(cols):
        out = jnp.zeros(logits.shape, F32)
        for k, cval in enumerate(cols):
            out = jnp.where(lane == float(k), cval, out)
        return out

    idx_ref[...] = spread(idxs).astype(I32)
    gate_ref[...] = spread(gates)
    rank_ref[...] = spread(ranks).astype(I32)


def _outproj(x2, a_n, b2, w_out_b, mgb, g1, b1, wr, br, tri):
    t, dm = x2.shape
    width = dm // 2
    const = lambda *shape: pl.BlockSpec(shape, lambda m: (0,) * len(shape))
    rowblk = lambda w: pl.BlockSpec((PROJ_ROWS, w), lambda m: (m, 0))
    return pl.pallas_call(
        _outproj_kernel,
        grid=(t // PROJ_ROWS,),
        in_specs=[rowblk(dm), rowblk(width), rowblk(width), const(dm, dm), const(1, width),
                  const(1, dm), const(1, dm), const(dm, LANES), const(1, LANES),
                  const(PROJ_ROWS, PROJ_ROWS)],
        out_specs=[rowblk(dm), rowblk(LANES), rowblk(LANES), rowblk(LANES), const(1, LANES)],
        out_shape=[jax.ShapeDtypeStruct((t, dm), F32), jax.ShapeDtypeStruct((t, LANES), I32),
                   jax.ShapeDtypeStruct((t, LANES), F32), jax.ShapeDtypeStruct((t, LANES), I32),
                   jax.ShapeDtypeStruct((1, LANES), F32)],
        scratch_shapes=[pltpu.VMEM((1, LANES), F32)],
        compiler_params=_params(("arbitrary",)),
        name="outproj_router",
    )(x2, a_n, b2, w_out_b, mgb, g1, b1, wr, br, tri)


def _moe_kernel(be_ref, nv_ref, ord_cur, ord_nxt, h_hbm, wg_ref, bg_ref, wu_ref, bu_ref, wd_ref, bd_ref,
                ya_hbm, xbuf, obuf, wgb, wub, wdb, gsem, ssem):
    i = pl.program_id(0)
    last = pl.num_programs(0) - 1
    slot = i % 2
    nv = nv_ref[i]
    nv_next = jnp.where(i < last, nv_ref[jnp.minimum(i + 1, last)], 0)

    def gather_copy(tok, r, s):
        return pltpu.make_async_copy(h_hbm.at[pl.ds(tok, 1), :], xbuf.at[s, pl.ds(r, 1), :], gsem.at[s])

    def scatter_copy(dst, r, s):
        return pltpu.make_async_copy(obuf.at[s, pl.ds(r, 1), :], ya_hbm.at[pl.ds(dst, 1), :], ssem.at[s])

    def start_gather(ord_ref, n, s):
        def body(r, c):
            gather_copy(lax.shift_right_logical(ord_ref[0, 0, r], 2), r, s).start()
            return c

        lax.fori_loop(0, n, body, 0)

    def wait_rows(make_copy, n, s):
        def body(r, c):
            make_copy(0, r, s).wait()
            return c

        lax.fori_loop(0, n, body, 0)

    @pl.when(i == 0)
    def _():
        xbuf[...] = jnp.zeros_like(xbuf)
        start_gather(ord_cur, nv, 0)

    @pl.when(nv_next > 0)
    def _():
        start_gather(ord_nxt, nv_next, 1 - slot)

    @pl.when(nv > 0)
    def _():
        wait_rows(gather_copy, nv, slot)
        changed = jnp.logical_or(i == 0, be_ref[i] != be_ref[jnp.maximum(i - 1, 0)])

        @pl.when(changed)
        def _():
            wgb[...] = wg_ref[...].astype(BF16)
            wub[...] = wu_ref[...].astype(BF16)
            wdb[...] = wd_ref[...].astype(BF16)

        x = xbuf[slot].astype(BF16)
        g = jnp.minimum(_dot(x, wgb[...]) + bg_ref[...], SWIGLU_LIMIT)
        u = jnp.clip(_dot(x, wub[...]) + bu_ref[...], -SWIGLU_LIMIT, SWIGLU_LIMIT)
        act = (u + 1.0) * (g * jax.nn.sigmoid(SWIGLU_ALPHA * g))
        out = _dot(act.astype(BF16), wdb[...]) + bd_ref[...]

        @pl.when(i >= 2)
        def _():
            wait_rows(scatter_copy, nv_ref[jnp.maximum(i - 2, 0)], slot)

        obuf[slot] = out

        def sbody(r, c):
            scatter_copy(ord_cur[0, 0, r], r, slot).start()
            return c

        lax.fori_loop(0, nv, sbody, 0)

        @pl.when(nv_next == 0)
        def _():
            wait_rows(scatter_copy, nv, slot)

            @pl.when(i >= 1)
            def _():
                wait_rows(scatter_copy, nv_ref[jnp.maximum(i - 1, 0)], 1 - slot)


def _moe(h, order3, block_e, nvalid, w_gate, b_gate, w_up, b_up, w_down, b_down):
    t, dm = h.shape
    nblk = order3.shape[0]
    ne, _, df = w_gate.shape
    wspec = lambda r, c: pl.BlockSpec((None, r, c), lambda i, be, nv: (be[i], 0, 0))
    ospec = lambda f: pl.BlockSpec((1, 1, MOE_ROWS), lambda i, be, nv: (f(i), 0, 0), memory_space=pltpu.SMEM)
    grid_spec = pltpu.PrefetchScalarGridSpec(
        num_scalar_prefetch=2,
        grid=(nblk,),
        in_specs=[ospec(lambda i: i), ospec(lambda i: jnp.minimum(i + 1, nblk - 1)),
                  pl.BlockSpec(memory_space=pl.ANY),
                  wspec(dm, df), wspec(1, df), wspec(dm, df), wspec(1, df), wspec(df, dm), wspec(1, dm)],
        out_specs=pl.BlockSpec(memory_space=pl.ANY),
        scratch_shapes=[pltpu.VMEM((2, MOE_ROWS, dm), F32), pltpu.VMEM((2, MOE_ROWS, dm), F32),
                        pltpu.VMEM((dm, df), BF16), pltpu.VMEM((dm, df), BF16), pltpu.VMEM((df, dm), BF16),
                        pltpu.SemaphoreType.DMA((2,)), pltpu.SemaphoreType.DMA((2,))],
    )
    return pl.pallas_call(
        _moe_kernel,
        grid_spec=grid_spec,
        out_shape=jax.ShapeDtypeStruct((t * TOP_K, dm), F32),
        compiler_params=_params(("arbitrary",)),
        name="moe_experts",
    )(block_e, nvalid, order3, order3, h, w_gate, b_gate.reshape(ne, 1, df), w_up, b_up.reshape(ne, 1, df),
      w_down, b_down.reshape(ne, 1, dm))


def _combine_kernel(h_ref, ya_ref, gate_ref, g2_ref, b2_ref, o_ref):
    dm = h_ref.shape[-1]
    gates = gate_ref[...]
    y = jnp.zeros(h_ref.shape, F32)
    for k in range(TOP_K):
        y = y + gates[:, k:k + 1] * ya_ref[:, k * dm:(k + 1) * dm]
    z = DN_ALPHA * h_ref[...] + y
    mu = jnp.mean(z, axis=-1, keepdims=True)
    zc = z - mu
    var = jnp.mean(zc * zc, axis=-1, keepdims=True)
    o_ref[...] = zc * lax.rsqrt(var + LN_EPS) * g2_ref[...] + b2_ref[...]


def _combine(h, ya, gates, g2, b2):
    t, dm = h.shape
    ya4 = ya.reshape(ya.shape[0] // TOP_K, TOP_K * dm)
    rowblk = lambda w: pl.BlockSpec((PROJ_ROWS, w), lambda m: (m, 0))
    const = pl.BlockSpec((1, dm), lambda m: (0, 0))
    return pl.pallas_call(
        _combine_kernel,
        grid=(t // PROJ_ROWS,),
        in_specs=[rowblk(dm), rowblk(TOP_K * dm), rowblk(LANES), const, const],
        out_specs=rowblk(dm),
        out_shape=jax.ShapeDtypeStruct((t, dm), F32),
        compiler_params=_params(("parallel",)),
        name="combine_ln",
    )(h, ya4, gates, g2, b2)


def _layer(x, w_in, sgu_w, sgu_b, sgu_ln_g, sgu_ln_b, mix_norm_g, w_out, ln1_g, ln1_b,
           w_router, b_router, w_gate, b_gate, w_up, b_up, w_down, b_down, ln2_g, ln2_b):
    batch, seq, dm = x.shape
    width = dm // 2
    heads = width // HEAD_DIM
    t = batch * seq
    assert seq % ATTN_SPAN == 0 and dm % (2 * LANES) == 0 and w_router.shape[-1] == N_EXPERTS
    x2 = x.reshape(t, dm)

    wc = jnp.tril(sgu_w)
    wpair = jnp.concatenate([wc[0::2], wc[1::2]], axis=-1).astype(BF16)
    sbias = jnp.repeat(sgu_b.T, HEAD_DIM, axis=1)
    grp = jnp.arange(width) // HEAD_DIM
    gmat = jnp.where(grp[:, None] == grp[None, :], 1.0 / HEAD_DIM, 0.0).astype(BF16)
    row = lambda v: v.reshape(1, -1)

    a_n, q1, k1, v1, q4, k4, v4, q16, k16, v16 = _project(
        x2, w_in.astype(BF16), gmat, wpair, sbias, row(sgu_ln_g), row(sgu_ln_b),
        row(mix_norm_g[:width]), batch, seq)
    b = _attention((q1, k1, v1), (q4, k4, v4), (q16, k16, v16), batch, seq)

    wr = jnp.pad(w_router, ((0, 0), (0, LANES - N_EXPERTS))).astype(BF16)
    br = jnp.concatenate([b_router.astype(F32), jnp.full((LANES - N_EXPERTS,), -1e30, F32)]).reshape(1, LANES)
    ti = jnp.arange(PROJ_ROWS)
    tri = (ti[None, :] < ti[:, None]).astype(BF16)
    h, idx, gates, rank, cnt = _outproj(x2, a_n.reshape(t, width), b.reshape(t, width), w_out.astype(BF16),
                                        row(mix_norm_g[width:]), row(ln1_g), row(ln1_b), wr, br, tri)

    n_assign = t * TOP_K
    nblk = n_assign // MOE_ROWS + N_EXPERTS
    counts = cnt[0, :N_EXPERTS].astype(I32)
    pcounts = (counts + MOE_ROWS - 1) // MOE_ROWS * MOE_ROWS
    pend = jnp.cumsum(pcounts)
    pstart = pend - pcounts
    dest = pstart[idx[:, :TOP_K]] + rank[:, :TOP_K]
    order = jnp.zeros((nblk * MOE_ROWS,), I32).at[dest.reshape(-1)].set(jnp.arange(n_assign, dtype=I32))
    blk_start = jnp.arange(nblk, dtype=I32) * MOE_ROWS
    block_e = jnp.clip(jnp.searchsorted(pend, blk_start, side="right"), 0, N_EXPERTS - 1).astype(I32)
    nvalid = jnp.where(blk_start < pend[-1],
                       jnp.clip(pstart[block_e] + counts[block_e] - blk_start, 0, MOE_ROWS), 0).astype(I32)

    ya = _moe(h, order.reshape(nblk, 1, MOE_ROWS), block_e, nvalid, w_gate, b_gate, w_up, b_up, w_down, b_down)
    out = _combine(h, ya, gates, row(ln2_g), row(ln2_b))
    return out.reshape(batch, seq, dm)


def kernel(x, w_in, sgu_w, sgu_b, sgu_ln_g, sgu_ln_b, mix_norm_g, w_out, ln1_g, ln1_b, w_router, b_router,
           w_gate, b_gate, w_up, b_up, w_down, b_down, ln2_g, ln2_b):
    assert w_in.shape[0] == DEPTH
    return _layer(x, w_in[0], sgu_w[0], sgu_b[0], sgu_ln_g[0], sgu_ln_b[0], mix_norm_g[0], w_out[0],
                  ln1_g[0], ln1_b[0], w_router[0], b_router[0], w_gate[0], b_gate[0], w_up[0], b_up[0],
                  w_down[0], b_down[0], ln2_g[0], ln2_b[0])
```

```python
import jax
import jax.numpy as jnp
from jax import lax
from jax.experimental import pallas as pl
from jax.experimental.pallas import tpu as pltpu

F32 = jnp.float32
BF16 = jnp.bfloat16
I32 = jnp.int32

LN_EPS = 1e-5
HEAD_DIM = 64
SGU_CHUNK = 128
BAND_BLOCK = 128
DILATIONS = (1, 4, 16)
ATTN_SPAN = BAND_BLOCK * DILATIONS[-1]
N_EXPERTS = 32
TOP_K = 4
SWIGLU_ALPHA = 1.702
SWIGLU_LIMIT = 7.0
DEPTH = 1
DN_ALPHA = (2 * DEPTH) ** 0.25

LANES = 128
PROJ_ROWS = 512
MOE_ROWS = 256
ROUTE_ROWS = 256
DMA_UNROLL = 8
VMEM_LIMIT = 56 * 1024 * 1024


def _params(sem):
    return pltpu.CompilerParams(dimension_semantics=sem, vmem_limit_bytes=VMEM_LIMIT)


def _dot(a, b):
    return jnp.dot(a, b, preferred_element_type=F32)


def _proj_kernel(x_ref, w_ref, gmat_ref, wpair_ref, sbias_ref, lng_ref, lnb_ref, mg_ref,
                 a_ref, q1_ref, k1_ref, v1_ref, q4_ref, k4_ref, v4_ref, q16_ref, k16_ref, v16_ref,
                 a_scr, t_scr):
    width = a_ref.shape[-1]
    xb = x_ref[...].astype(BF16)

    def proj(c):
        return _dot(xb, w_ref[:, c * width:(c + 1) * width])

    u = jax.nn.gelu(proj(0))
    v = jax.nn.gelu(proj(1))
    mean = _dot(v.astype(BF16), gmat_ref[...])
    d = v - mean
    var = _dot((d * d).astype(BF16), gmat_ref[...])
    vn = (d * lax.rsqrt(var + LN_EPS) * lng_ref[...] + lnb_ref[...]).astype(BF16)

    lane = lax.broadcasted_iota(I32, (SGU_CHUNK, LANES), 1)
    low = lane < HEAD_DIM
    zero = jnp.zeros((SGU_CHUNK, LANES), BF16)
    for c in range(PROJ_ROWS // SGU_CHUNK):
        rows = slice(c * SGU_CHUNK, (c + 1) * SGU_CHUNK)
        for j in range(width // LANES):
            cols = slice(j * LANES, (j + 1) * LANES)
            vp = vn[rows, cols]
            rhs = jnp.concatenate([jnp.where(low, vp, zero), jnp.where(low, zero, vp)], axis=0)
            gate = _dot(wpair_ref[j], rhs) + sbias_ref[:, cols]
            a_scr[rows, cols] = u[rows, cols] * gate
    a = a_scr[...]
    ms = jnp.mean(a * a, axis=-1, keepdims=True)
    a_ref[...] = (a * lax.rsqrt(ms + LN_EPS) * mg_ref[...]).astype(BF16)

    outs = ((q1_ref, q4_ref, q16_ref), (k1_ref, k4_ref, k16_ref), (v1_ref, v4_ref, v16_ref))
    for c, (o1, o4, o16) in enumerate(outs):
        t = proj(2 + c)
        if c == 0:
            t = t * (HEAD_DIM ** -0.5)
        o1[...] = t.astype(BF16)
        for j in range(width // LANES):
            cols = slice(j * LANES, (j + 1) * LANES)
            t_scr[j] = t[:, cols]
            for r in range(4):
                o4[r, :, cols] = t_scr[j, pl.ds(r, PROJ_ROWS // 4, stride=4), :].astype(BF16)
            for r in range(16):
                o16[r, :, cols] = t_scr[j, pl.ds(r, PROJ_ROWS // 16, stride=16), :].astype(BF16)


def _project(x2, w_in_b, gmat, wpair, sbias, lng, lnb, mg, batch, seq):
    t, dm = x2.shape
    width = dm // 2
    nt = seq // PROJ_ROWS
    per_span = ATTN_SPAN // PROJ_ROWS
    const = lambda *shape: pl.BlockSpec(shape, lambda b, m: (0,) * len(shape))
    o1 = jax.ShapeDtypeStruct((batch, seq, width), BF16)
    o4 = jax.ShapeDtypeStruct((batch, seq // 512, 4, BAND_BLOCK, width), BF16)
    o16 = jax.ShapeDtypeStruct((batch, seq // ATTN_SPAN, 16, BAND_BLOCK, width), BF16)
    s1 = pl.BlockSpec((None, PROJ_ROWS, width), lambda b, m: (b, m, 0))
    s4 = pl.BlockSpec((None, None, 4, BAND_BLOCK, width), lambda b, m: (b, m, 0, 0, 0))
    s16 = pl.BlockSpec((None, None, 16, PROJ_ROWS // 16, width),
                       lambda b, m: (b, m // per_span, 0, m % per_span, 0))
    return pl.pallas_call(
        _proj_kernel,
        grid=(batch, nt),
        in_specs=[pl.BlockSpec((PROJ_ROWS, dm), lambda b, m: (b * nt + m, 0)),
                  const(*w_in_b.shape), const(*gmat.shape), const(*wpair.shape), const(*sbias.shape),
                  const(1, width), const(1, width), const(1, width)],
        out_specs=[s1] + [s1, s1, s1] + [s4, s4, s4] + [s16, s16, s16],
        out_shape=[o1] + [o1, o1, o1] + [o4, o4, o4] + [o16, o16, o16],
        scratch_shapes=[pltpu.VMEM((PROJ_ROWS, width), F32),
                        pltpu.VMEM((width // LANES, PROJ_ROWS, LANES), F32)],
        compiler_params=_params(("parallel", "parallel")),
        name="proj_sgu",
    )(x2, w_in_b, gmat, wpair, sbias, lng, lnb, mg)


def _attn_kernel(q1, k1, v1, kp1, vp1, q4, k4, v4, kp4, vp4, q16, k16, v16, kp16, vp16,
                 o_ref, acc_o, acc_m, acc_l):
    not_first = pl.program_id(1) > 0
    blk = BAND_BLOCK
    row = lax.broadcasted_iota(I32, (2 * blk, 2 * blk), 0) % blk
    col = lax.broadcasted_iota(I32, (2 * blk, 2 * blk), 1)
    band = jnp.logical_or(jnp.logical_and(col < blk, col >= row), jnp.logical_and(col >= blk, col - blk <= row))
    neg = jnp.where(not_first, 0.0, -jnp.inf).astype(F32)
    first_pen = jnp.where(col < blk, neg, 0.0)
    low = lax.broadcasted_iota(I32, (blk, LANES), 1) < HEAD_DIM
    zero = jnp.zeros((blk, LANES), BF16)

    def unit(q, kprev, kcur, vprev, vcur, maybe_first, rows, init):
        kk = jnp.concatenate([kprev, kcur], axis=0)
        vv = jnp.concatenate([vprev, vcur], axis=0)
        qq = jnp.concatenate([jnp.where(low, q, zero), jnp.where(low, zero, q)], axis=0)
        s = lax.dot_general(qq, kk, (((1,), (1,)), ((), ())), preferred_element_type=F32)
        s = jnp.where(band, s, -jnp.inf)
        if maybe_first:
            s = s + first_pen
        m2 = jnp.max(s, axis=-1, keepdims=True)
        p = jnp.exp(s - m2)
        l2 = jnp.sum(p, axis=-1, keepdims=True)
        o2 = _dot(p.astype(BF16), vv)
        o = jnp.where(low, o2[:blk], o2[blk:])
        m = jnp.where(low, m2[:blk], m2[blk:])
        l = jnp.where(low, l2[:blk], l2[blk:])
        if init:
            acc_o[rows, :] = o
            acc_m[rows, :] = m
            acc_l[rows, :] = l
        else:
            mo = acc_m[rows, :]
            mn = jnp.maximum(mo, m)
            so = jnp.exp(mo - mn)
            sn = jnp.exp(m - mn)
            acc_o[rows, :] = acc_o[rows, :] * so + o * sn
            acc_l[rows, :] = acc_l[rows, :] * so + l * sn
            acc_m[rows, :] = mn

    group = 4

    def rows1(base, j):
        return pl.ds(base + j * blk, blk)

    for j in range(group):
        cur = rows1(0, j)
        if j == 0:
            unit(q1[cur, :], kp1[...], k1[cur, :], vp1[...], v1[cur, :], True, cur, True)
        else:
            prev = rows1(0, j - 1)
            unit(q1[cur, :], k1[prev, :], k1[cur, :], v1[prev, :], v1[cur, :], False, cur, True)

    def body1(g, c):
        base = pl.multiple_of(g * (group * blk), group * blk)
        for j in range(group):
            cur = rows1(base, j)
            prev = rows1(base, j - 1)
            unit(q1[cur, :], k1[prev, :], k1[cur, :], v1[prev, :], v1[cur, :], False, cur, True)
        return c

    lax.fori_loop(1, ATTN_SPAN // (group * blk), body1, 0)

    for r in range(4):
        unit(q4[0, r], kp4[r], k4[0, r], vp4[r], v4[0, r], True, pl.ds(r, blk, stride=4), False)

    def body4(s, c):
        for r in range(4):
            unit(q4[s, r], k4[s - 1, r], k4[s, r], v4[s - 1, r], v4[s, r], False,
                 pl.ds(s * (4 * blk) + r, blk, stride=4), False)
        return c

    lax.fori_loop(1, ATTN_SPAN // (4 * blk), body4, 0)

    def body16(g, c):
        for j in range(group):
            r = g * group + j
            unit(q16[r], kp16[r], k16[r], vp16[r], v16[r], True, pl.ds(r, blk, stride=16), False)
        return c

    lax.fori_loop(0, 16 // group, body16, 0)

    o_ref[...] = (acc_o[...] / acc_l[...]).astype(BF16)


def _attention(qkv1, qkv4, qkv16, batch, seq):
    width = qkv1[0].shape[-1]
    nspan = seq // ATTN_SPAN
    blk = BAND_BLOCK
    n1 = ATTN_SPAN // blk
    n4 = ATTN_SPAN // (4 * blk)

    cur1 = pl.BlockSpec((None, ATTN_SPAN, LANES), lambda b, i, p: (b, i, p))
    prev1 = pl.BlockSpec((None, None, blk, LANES), lambda b, i, p: (b, jnp.maximum(i * n1 - 1, 0), 0, p))
    cur4 = pl.BlockSpec((None, n4, 4, blk, LANES), lambda b, i, p: (b, i, 0, 0, p))
    prev4 = pl.BlockSpec((None, None, 4, blk, LANES), lambda b, i, p: (b, jnp.maximum(i * n4 - 1, 0), 0, 0, p))
    cur16 = pl.BlockSpec((None, None, 16, blk, LANES), lambda b, i, p: (b, i, 0, 0, p))
    prev16 = pl.BlockSpec((None, None, 16, blk, LANES), lambda b, i, p: (b, jnp.maximum(i - 1, 0), 0, 0, p))

    q1, k1, v1 = qkv1
    q4, k4, v4 = qkv4
    q16, k16, v16 = qkv16
    k1b = k1.reshape(batch, seq // blk, blk, width)
    v1b = v1.reshape(batch, seq // blk, blk, width)
    return pl.pallas_call(
        _attn_kernel,
        grid=(batch, nspan, width // LANES),
        in_specs=[cur1, cur1, cur1, prev1, prev1,
                  cur4, cur4, cur4, prev4, prev4,
                  cur16, cur16, cur16, prev16, prev16],
        out_specs=pl.BlockSpec((None, ATTN_SPAN, LANES), lambda b, i, p: (b, i, p)),
        out_shape=jax.ShapeDtypeStruct((batch, seq, width), BF16),
        scratch_shapes=[pltpu.VMEM((ATTN_SPAN, LANES), F32)] * 3,
        compiler_params=_params(("parallel", "parallel", "parallel")),
        name="dilated_attn",
    )(q1, k1, v1, k1b, v1b, q4, k4, v4, k4, v4, q16, k16, v16, k16, v16)


def _outproj_kernel(x_ref, a_ref, b_ref, wo_ref, mgb_ref, g1_ref, b1_ref, wr_ref, br_ref, tri_ref,
                    h_ref, idx_ref, gate_ref, rank_ref, cnt_ref, run_scr):
    @pl.when(pl.program_id(0) == 0)
    def _():
        run_scr[...] = jnp.zeros_like(run_scr)

    width = a_ref.shape[-1]
    bf = b_ref[...].astype(F32)
    bn = (bf * lax.rsqrt(jnp.mean(bf * bf, axis=-1, keepdims=True) + LN_EPS) * mgb_ref[...]).astype(BF16)
    mixed = _dot(a_ref[...], wo_ref[0:width, :]) + _dot(bn, wo_ref[width:2 * width, :])
    z = DN_ALPHA * x_ref[...] + mixed
    mu = jnp.mean(z, axis=-1, keepdims=True)
    zc = z - mu
    var = jnp.mean(zc * zc, axis=-1, keepdims=True)
    h = zc * lax.rsqrt(var + LN_EPS) * g1_ref[...] + b1_ref[...]
    h_ref[...] = h

    logits = _dot(h.astype(BF16), wr_ref[...]) + br_ref[...]
    lane = lax.broadcasted_iota(I32, logits.shape, 1).astype(F32)
    work = logits
    vals, idxs, hots = [], [], []
    for _ in range(TOP_K):
        mv = jnp.max(work, axis=-1, keepdims=True)
        ix = jnp.min(jnp.where(work == mv, lane, float(LANES)), axis=-1, keepdims=True)
        hot = lane == ix
        work = jnp.where(hot, -jnp.inf, work)
        vals.append(mv)
        idxs.append(ix)
        hots.append(hot)
    exps = [jnp.exp(v - vals[0]) for v in vals]
    den = exps[0] + exps[1] + exps[2] + exps[3]
    gates = [e / den for e in exps]

    member = jnp.zeros(logits.shape, F32)
    for hot in hots:
        member = jnp.where(hot, 1.0, member)
    before = _dot(tri_ref[...], member.astype(BF16)) + run_scr[...]
    ranks = [jnp.sum(jnp.where(hot, before, 0.0), axis=-1, keepdims=True) for hot in hots]
    run_scr[...] = run_scr[...] + jnp.sum(member, axis=0, keepdims=True)
    cnt_ref[...] = run_scr[...]

    def spread(cols):
        out = jnp.zeros(logits.shape, F32)
        for k, cval in enumerate(cols):
            out = jnp.where(lane == float(k), cval, out)
        return out

    idx_ref[...] = spread(idxs).astype(I32)
    gate_ref[...] = spread(gates)
    rank_ref[...] = spread(ranks).astype(I32)


def _outproj(x2, a_n, b2, w_out_b, mgb, g1, b1, wr, br, tri):
    t, dm = x2.shape
    width = dm // 2
    const = lambda *shape: pl.BlockSpec(shape, lambda m: (0,) * len(shape))
    rowblk = lambda w: pl.BlockSpec((PROJ_ROWS, w), lambda m: (m, 0))
    return pl.pallas_call(
        _outproj_kernel,
        grid=(t // PROJ_ROWS,),
        in_specs=[rowblk(dm), rowblk(width), rowblk(width), const(dm, dm), const(1, width),
                  const(1, dm), const(1, dm), const(dm, LANES), const(1, LANES),
                  const(PROJ_ROWS, PROJ_ROWS)],
        out_specs=[rowblk(dm), rowblk(LANES), rowblk(LANES), rowblk(LANES), const(1, LANES)],
        out_shape=[jax.ShapeDtypeStruct((t, dm), F32), jax.ShapeDtypeStruct((t, LANES), I32),
                   jax.ShapeDtypeStruct((t, LANES), F32), jax.ShapeDtypeStruct((t, LANES), I32),
                   jax.ShapeDtypeStruct((1, LANES), F32)],
        scratch_shapes=[pltpu.VMEM((1, LANES), F32)],
        compiler_params=_params(("arbitrary",)),
        name="outproj_router",
    )(x2, a_n, b2, w_out_b, mgb, g1, b1, wr, br, tri)


def _dispatch_kernel(dest_ref, h_ref, xs_hbm, sem):
    n = ROUTE_ROWS * TOP_K

    def body(g, c):
        for j in range(DMA_UNROLL):
            tok = g * (DMA_UNROLL // TOP_K) + j // TOP_K
            dst = dest_ref[0, 0, g * DMA_UNROLL + j]
            pltpu.make_async_copy(h_ref.at[pl.ds(tok, 1), :], xs_hbm.at[pl.ds(dst, 1), :], sem).start()
        return c

    lax.fori_loop(0, n // DMA_UNROLL, body, 0)
    pltpu.make_async_copy(xs_hbm.at[pl.ds(0, n), :], xs_hbm.at[pl.ds(0, n), :], sem).wait()


def _dispatch(h, dest3, nrows):
    t, dm = h.shape
    return pl.pallas_call(
        _dispatch_kernel,
        grid=(t // ROUTE_ROWS,),
        in_specs=[pl.BlockSpec((1, 1, ROUTE_ROWS * TOP_K), lambda m: (m, 0, 0), memory_space=pltpu.SMEM),
                  pl.BlockSpec((ROUTE_ROWS, dm), lambda m: (m, 0))],
        out_specs=pl.BlockSpec(memory_space=pl.ANY),
        out_shape=jax.ShapeDtypeStruct((nrows, dm), F32),
        scratch_shapes=[pltpu.SemaphoreType.DMA(())],
        compiler_params=_params(("arbitrary",)),
        name="moe_dispatch",
    )(dest3, h)


def _moe_kernel(ib_ref, ie_ref, lo_ref, hi_ref, xs_ref, wg_ref, bg_ref, wu_ref, bu_ref, wd_ref, bd_ref,
                ys_ref, wgb, wub, wdb):
    i = pl.program_id(0)
    prev = jnp.maximum(i - 1, 0)
    lo = lo_ref[i]
    hi = hi_ref[i]

    @pl.when(hi > lo)
    def _():
        @pl.when(jnp.logical_or(i == 0, ie_ref[i] != ie_ref[prev]))
        def _():
            wgb[...] = wg_ref[...].astype(BF16)
            wub[...] = wu_ref[...].astype(BF16)
            wdb[...] = wd_ref[...].astype(BF16)

        x = xs_ref[...].astype(BF16)
        g = jnp.minimum(_dot(x, wgb[...]) + bg_ref[...], SWIGLU_LIMIT)
        u = jnp.clip(_dot(x, wub[...]) + bu_ref[...], -SWIGLU_LIMIT, SWIGLU_LIMIT)
        act = (u + 1.0) * (g * jax.nn.sigmoid(SWIGLU_ALPHA * g))
        out = _dot(act.astype(BF16), wdb[...]) + bd_ref[...]
        rid = lax.broadcasted_iota(I32, out.shape, 0)
        mine = jnp.logical_and(rid >= lo, rid < hi)
        first_visit = jnp.logical_or(i == 0, ib_ref[i] != ib_ref[prev])

        @pl.when(first_visit)
        def _():
            ys_ref[...] = jnp.where(mine, out, 0.0)

        @pl.when(jnp.logical_not(first_visit))
        def _():
            ys_ref[...] = jnp.where(mine, out, ys_ref[...])


def _moe(xs, items, w_gate, b_gate, w_up, b_up, w_down, b_down):
    nrows, dm = xs.shape
    ne, _, df = w_gate.shape
    nitems = items[0].shape[0]
    wspec = lambda r, c: pl.BlockSpec((None, r, c), lambda i, ib, ie, lo, hi: (ie[i], 0, 0))
    rspec = pl.BlockSpec((MOE_ROWS, dm), lambda i, ib, ie, lo, hi: (ib[i], 0))
    grid_spec = pltpu.PrefetchScalarGridSpec(
        num_scalar_prefetch=4,
        grid=(nitems,),
        in_specs=[rspec, wspec(dm, df), wspec(1, df), wspec(dm, df), wspec(1, df), wspec(df, dm), wspec(1, dm)],
        out_specs=rspec,
        scratch_shapes=[pltpu.VMEM((dm, df), BF16), pltpu.VMEM((dm, df), BF16), pltpu.VMEM((df, dm), BF16)],
    )
    return pl.pallas_call(
        _moe_kernel,
        grid_spec=grid_spec,
        out_shape=jax.ShapeDtypeStruct((nrows, dm), F32),
        compiler_params=_params(("arbitrary",)),
        name="moe_experts",
    )(*items, xs, w_gate, b_gate.reshape(ne, 1, df), w_up, b_up.reshape(ne, 1, df),
      w_down, b_down.reshape(ne, 1, dm))


def _work_items(counts, nrows):
    nblk = nrows // MOE_ROWS
    nitems = nblk + N_EXPERTS - 1
    ends = jnp.cumsum(counts)
    starts = ends - counts
    b0 = jnp.arange(nblk, dtype=I32)[:, None] * MOE_ROWS
    lo = jnp.maximum(starts[None, :], b0)
    hi = jnp.minimum(ends[None, :], b0 + MOE_ROWS)
    nonempty = (hi > lo).reshape(-1)
    csum = jnp.cumsum(nonempty.astype(I32))
    j = jnp.arange(nitems, dtype=I32)
    pos = jnp.sum(csum[None, :] <= j[:, None], axis=1).astype(I32)
    used = j < csum[-1]
    pos = jnp.where(used, pos, jnp.max(jnp.where(nonempty, jnp.arange(nonempty.shape[0], dtype=I32), 0)))
    ib = pos // N_EXPERTS
    ie = pos % N_EXPERTS
    ilo = jnp.where(used, lo.reshape(-1)[pos] - ib * MOE_ROWS, 0).astype(I32)
    ihi = jnp.where(used, hi.reshape(-1)[pos] - ib * MOE_ROWS, 0).astype(I32)
    return ib.astype(I32), ie.astype(I32), ilo, ihi


def _combine_kernel(dest_cur, dest_nxt, h_ref, gate_ref, g2_ref, b2_ref, ys_hbm, o_ref, buf, sem):
    m = pl.program_id(0)
    slot = m % 2
    n = ROUTE_ROWS * TOP_K

    def start_fetch(dest_ref, s):
        def body(g, c):
            for j in range(DMA_UNROLL):
                row = (j % TOP_K) * ROUTE_ROWS + g * (DMA_UNROLL // TOP_K) + j // TOP_K
                pltpu.make_async_copy(ys_hbm.at[pl.ds(dest_ref[0, 0, g * DMA_UNROLL + j], 1), :],
                                      buf.at[s, pl.ds(row, 1), :], sem.at[s]).start()
            return c

        lax.fori_loop(0, n // DMA_UNROLL, body, 0)

    @pl.when(m == 0)
    def _():
        start_fetch(dest_cur, 0)

    @pl.when(m + 1 < pl.num_programs(0))
    def _():
        start_fetch(dest_nxt, 1 - slot)

    pltpu.make_async_copy(buf.at[slot], buf.at[slot], sem.at[slot]).wait()

    gates = gate_ref[...]
    y = jnp.zeros(h_ref.shape, F32)
    for k in range(TOP_K):
        y = y + gates[:, k:k + 1] * buf[slot, k * ROUTE_ROWS:(k + 1) * ROUTE_ROWS, :]
    z = DN_ALPHA * h_ref[...] + y
    mu = jnp.mean(z, axis=-1, keepdims=True)
    zc = z - mu
    var = jnp.mean(zc * zc, axis=-1, keepdims=True)
    o_ref[...] = zc * lax.rsqrt(var + LN_EPS) * g2_ref[...] + b2_ref[...]


def _combine(h, ys, dest3, gates, g2, b2):
    t, dm = h.shape
    nt = t // ROUTE_ROWS
    rowblk = lambda w: pl.BlockSpec((ROUTE_ROWS, w), lambda m: (m, 0))
    const = pl.BlockSpec((1, dm), lambda m: (0, 0))
    dspec = lambda f: pl.BlockSpec((1, 1, ROUTE_ROWS * TOP_K), lambda m: (f(m), 0, 0), memory_space=pltpu.SMEM)
    return pl.pallas_call(
        _combine_kernel,
        grid=(nt,),
        in_specs=[dspec(lambda m: m), dspec(lambda m: jnp.minimum(m + 1, nt - 1)),
                  rowblk(dm), rowblk(LANES), const, const, pl.BlockSpec(memory_space=pl.ANY)],
        out_specs=rowblk(dm),
        out_shape=jax.ShapeDtypeStruct((t, dm), F32),
        scratch_shapes=[pltpu.VMEM((2, ROUTE_ROWS * TOP_K, dm), F32), pltpu.SemaphoreType.DMA((2,))],
        compiler_params=_params(("arbitrary",)),
        name="combine_ln",
    )(dest3, dest3, h, gates, g2, b2, ys)


def _layer(x, w_in, sgu_w, sgu_b, sgu_ln_g, sgu_ln_b, mix_norm_g, w_out, ln1_g, ln1_b,
           w_router, b_router, w_gate, b_gate, w_up, b_up, w_down, b_down, ln2_g, ln2_b):
    batch, seq, dm = x.shape
    width = dm // 2
    t = batch * seq
    assert seq % ATTN_SPAN == 0 and dm % (2 * LANES) == 0 and w_router.shape[-1] == N_EXPERTS
    x2 = x.reshape(t, dm)

    wc = jnp.tril(sgu_w)
    wpair = jnp.concatenate([wc[0::2], wc[1::2]], axis=-1).astype(BF16)
    sbias = jnp.repeat(sgu_b.T, HEAD_DIM, axis=1)
    grp = jnp.arange(width) // HEAD_DIM
    gmat = jnp.where(grp[:, None] == grp[None, :], 1.0 / HEAD_DIM, 0.0).astype(BF16)
    row = lambda v: v.reshape(1, -1)

    a_n, q1, k1, v1, q4, k4, v4, q16, k16, v16 = _project(
        x2, w_in.astype(BF16), gmat, wpair, sbias, row(sgu_ln_g), row(sgu_ln_b),
        row(mix_norm_g[:width]), batch, seq)
    b = _attention((q1, k1, v1), (q4, k4, v4), (q16, k16, v16), batch, seq)

    wr = jnp.pad(w_router, ((0, 0), (0, LANES - N_EXPERTS))).astype(BF16)
    br = jnp.concatenate([b_router.astype(F32), jnp.full((LANES - N_EXPERTS,), -1e30, F32)]).reshape(1, LANES)
    ti = jnp.arange(PROJ_ROWS)
    tri = (ti[None, :] < ti[:, None]).astype(BF16)
    h, idx, gates, rank, cnt = _outproj(x2, a_n.reshape(t, width), b.reshape(t, width), w_out.astype(BF16),
                                        row(mix_norm_g[width:]), row(ln1_g), row(ln1_b), wr, br, tri)

    counts = cnt[0, :N_EXPERTS].astype(I32)
    starts = jnp.cumsum(counts) - counts
    dest = starts[idx[:, :TOP_K]] + rank[:, :TOP_K]
    dest3 = dest.reshape(t // ROUTE_ROWS, 1, ROUTE_ROWS * TOP_K)

    xs = _dispatch(h, dest3, t * TOP_K)
    ys = _moe(xs, _work_items(counts, t * TOP_K), w_gate, b_gate, w_up, b_up, w_down, b_down)
    out = _combine(h, ys, dest3, gates, row(ln2_g), row(ln2_b))
    return out.reshape(batch, seq, dm)


def kernel(x, w_in, sgu_w, sgu_b, sgu_ln_g, sgu_ln_b, mix_norm_g, w_out, ln1_g, ln1_b, w_router, b_router,
           w_gate, b_gate, w_up, b_up, w_down, b_down, ln2_g, ln2_b):
    assert w_in.shape[0] == DEPTH
    return _layer(x, w_in[0], sgu_w[0], sgu_b[0], sgu_ln_g[0], sgu_ln_b[0], mix_norm_g[0], w_out[0],
                  ln1_g[0], ln1_b[0], w_router[0], b_router[0], w_gate[0], b_gate[0], w_up[0], b_up[0],
                  w_down[0], b_down[0], ln2_g[0], ln2_b[0])
```

```python
import jax
import jax.numpy as jnp
from jax import lax
from jax.experimental import pallas as pl
from jax.experimental.pallas import tpu as pltpu

F32 = jnp.float32
BF16 = jnp.bfloat16
I32 = jnp.int32

LN_EPS = 1e-5
HEAD_DIM = 64
SGU_CHUNK = 128
BAND_BLOCK = 128
DILATIONS = (1, 4, 16)
ATTN_SPAN = BAND_BLOCK * DILATIONS[-1]
N_EXPERTS = 32
TOP_K = 4
SWIGLU_ALPHA = 1.702
SWIGLU_LIMIT = 7.0
DEPTH = 1
DN_ALPHA = (2 * DEPTH) ** 0.25

LANES = 128
PROJ_ROWS = 512
MOE_ROWS = 256
ROUTE_ROWS = 256
OCTET = 8
STAGE_ROWS = ROUTE_ROWS * TOP_K + N_EXPERTS * OCTET
STAGE_OCTETS = STAGE_ROWS // OCTET
VMEM_LIMIT = 56 * 1024 * 1024


def _params(sem):
    return pltpu.CompilerParams(dimension_semantics=sem, vmem_limit_bytes=VMEM_LIMIT)


def _dot(a, b):
    return jnp.dot(a, b, preferred_element_type=F32)


def _proj_kernel(x_ref, w_ref, gmat_ref, wpair_ref, sbias_ref, lng_ref, lnb_ref, mg_ref,
                 a_ref, q1_ref, k1_ref, v1_ref, q4_ref, k4_ref, v4_ref, q16_ref, k16_ref, v16_ref,
                 a_scr, t_scr):
    width = a_ref.shape[-1]
    xb = x_ref[...].astype(BF16)

    def proj(c):
        return _dot(xb, w_ref[:, c * width:(c + 1) * width])

    u = jax.nn.gelu(proj(0))
    v = jax.nn.gelu(proj(1))
    mean = _dot(v.astype(BF16), gmat_ref[...])
    d = v - mean
    var = _dot((d * d).astype(BF16), gmat_ref[...])
    vn = (d * lax.rsqrt(var + LN_EPS) * lng_ref[...] + lnb_ref[...]).astype(BF16)

    lane = lax.broadcasted_iota(I32, (SGU_CHUNK, LANES), 1)
    low = lane < HEAD_DIM
    zero = jnp.zeros((SGU_CHUNK, LANES), BF16)
    for c in range(PROJ_ROWS // SGU_CHUNK):
        rows = slice(c * SGU_CHUNK, (c + 1) * SGU_CHUNK)
        for j in range(width // LANES):
            cols = slice(j * LANES, (j + 1) * LANES)
            vp = vn[rows, cols]
            rhs = jnp.concatenate([jnp.where(low, vp, zero), jnp.where(low, zero, vp)], axis=0)
            gate = _dot(wpair_ref[j], rhs) + sbias_ref[:, cols]
            a_scr[rows, cols] = u[rows, cols] * gate
    a = a_scr[...]
    ms = jnp.mean(a * a, axis=-1, keepdims=True)
    a_ref[...] = (a * lax.rsqrt(ms + LN_EPS) * mg_ref[...]).astype(BF16)

    outs = ((q1_ref, q4_ref, q16_ref), (k1_ref, k4_ref, k16_ref), (v1_ref, v4_ref, v16_ref))
    for c, (o1, o4, o16) in enumerate(outs):
        t = proj(2 + c)
        if c == 0:
            t = t * (HEAD_DIM ** -0.5)
        o1[...] = t.astype(BF16)
        for j in range(width // LANES):
            cols = slice(j * LANES, (j + 1) * LANES)
            t_scr[j] = t[:, cols]
            for r in range(4):
                o4[r, :, cols] = t_scr[j, pl.ds(r, PROJ_ROWS // 4, stride=4), :].astype(BF16)
            for r in range(16):
                o16[r, :, cols] = t_scr[j, pl.ds(r, PROJ_ROWS // 16, stride=16), :].astype(BF16)


def _project(x2, w_in_b, gmat, wpair, sbias, lng, lnb, mg, batch, seq):
    t, dm = x2.shape
    width = dm // 2
    nt = seq // PROJ_ROWS
    per_span = ATTN_SPAN // PROJ_ROWS
    const = lambda *shape: pl.BlockSpec(shape, lambda b, m: (0,) * len(shape))
    o1 = jax.ShapeDtypeStruct((batch, seq, width), BF16)
    o4 = jax.ShapeDtypeStruct((batch, seq // 512, 4, BAND_BLOCK, width), BF16)
    o16 = jax.ShapeDtypeStruct((batch, seq // ATTN_SPAN, 16, BAND_BLOCK, width), BF16)
    s1 = pl.BlockSpec((None, PROJ_ROWS, width), lambda b, m: (b, m, 0))
    s4 = pl.BlockSpec((None, None, 4, BAND_BLOCK, width), lambda b, m: (b, m, 0, 0, 0))
    s16 = pl.BlockSpec((None, None, 16, PROJ_ROWS // 16, width),
                       lambda b, m: (b, m // per_span, 0, m % per_span, 0))
    return pl.pallas_call(
        _proj_kernel,
        grid=(batch, nt),
        in_specs=[pl.BlockSpec((PROJ_ROWS, dm), lambda b, m: (b * nt + m, 0)),
                  const(*w_in_b.shape), const(*gmat.shape), const(*wpair.shape), const(*sbias.shape),
                  const(1, width), const(1, width), const(1, width)],
        out_specs=[s1] + [s1, s1, s1] + [s4, s4, s4] + [s16, s16, s16],
        out_shape=[o1] + [o1, o1, o1] + [o4, o4, o4] + [o16, o16, o16],
        scratch_shapes=[pltpu.VMEM((PROJ_ROWS, width), F32),
                        pltpu.VMEM((width // LANES, PROJ_ROWS, LANES), F32)],
        compiler_params=_params(("parallel", "parallel")),
        name="proj_sgu",
    )(x2, w_in_b, gmat, wpair, sbias, lng, lnb, mg)


def _attn_kernel(q1, k1, v1, kp1, vp1, q4, k4, v4, kp4, vp4, q16, k16, v16, kp16, vp16,
                 o_ref, acc_o, acc_m, acc_l):
    not_first = pl.program_id(1) > 0
    blk = BAND_BLOCK
    row = lax.broadcasted_iota(I32, (2 * blk, 2 * blk), 0) % blk
    col = lax.broadcasted_iota(I32, (2 * blk, 2 * blk), 1)
    band = jnp.logical_or(jnp.logical_and(col < blk, col >= row), jnp.logical_and(col >= blk, col - blk <= row))
    neg = jnp.where(not_first, 0.0, -jnp.inf).astype(F32)
    first_pen = jnp.where(col < blk, neg, 0.0)
    low = lax.broadcasted_iota(I32, (blk, LANES), 1) < HEAD_DIM
    zero = jnp.zeros((blk, LANES), BF16)

    def unit(q, kprev, kcur, vprev, vcur, maybe_first, rows, init):
        kk = jnp.concatenate([kprev, kcur], axis=0)
        vv = jnp.concatenate([vprev, vcur], axis=0)
        qq = jnp.concatenate([jnp.where(low, q, zero), jnp.where(low, zero, q)], axis=0)
        s = lax.dot_general(qq, kk, (((1,), (1,)), ((), ())), preferred_element_type=F32)
        s = jnp.where(band, s, -jnp.inf)
        if maybe_first:
            s = s + first_pen
        m2 = jnp.max(s, axis=-1, keepdims=True)
        p = jnp.exp(s - m2)
        l2 = jnp.sum(p, axis=-1, keepdims=True)
        o2 = _dot(p.astype(BF16), vv)
        o = jnp.where(low, o2[:blk], o2[blk:])
        m = jnp.where(low, m2[:blk], m2[blk:])
        l = jnp.where(low, l2[:blk], l2[blk:])
        if init:
            acc_o[rows, :] = o
            acc_m[rows, :] = m
            acc_l[rows, :] = l
        else:
            mo = acc_m[rows, :]
            mn = jnp.maximum(mo, m)
            so = jnp.exp(mo - mn)
            sn = jnp.exp(m - mn)
            acc_o[rows, :] = acc_o[rows, :] * so + o * sn
            acc_l[rows, :] = acc_l[rows, :] * so + l * sn
            acc_m[rows, :] = mn

    group = 4

    def rows1(base, j):
        return pl.ds(base + j * blk, blk)

    for j in range(group):
        cur = rows1(0, j)
        if j == 0:
            unit(q1[cur, :], kp1[...], k1[cur, :], vp1[...], v1[cur, :], True, cur, True)
        else:
            prev = rows1(0, j - 1)
            unit(q1[cur, :], k1[prev, :], k1[cur, :], v1[prev, :], v1[cur, :], False, cur, True)

    def body1(g, c):
        base = pl.multiple_of(g * (group * blk), group * blk)
        for j in range(group):
            cur = rows1(base, j)
            prev = rows1(base, j - 1)
            unit(q1[cur, :], k1[prev, :], k1[cur, :], v1[prev, :], v1[cur, :], False, cur, True)
        return c

    lax.fori_loop(1, ATTN_SPAN // (group * blk), body1, 0)

    for r in range(4):
        unit(q4[0, r], kp4[r], k4[0, r], vp4[r], v4[0, r], True, pl.ds(r, blk, stride=4), False)

    def body4(s, c):
        for r in range(4):
            unit(q4[s, r], k4[s - 1, r], k4[s, r], v4[s - 1, r], v4[s, r], False,
                 pl.ds(s * (4 * blk) + r, blk, stride=4), False)
        return c

    lax.fori_loop(1, ATTN_SPAN // (4 * blk), body4, 0)

    def body16(g, c):
        for j in range(group):
            r = g * group + j
            unit(q16[r], kp16[r], k16[r], vp16[r], v16[r], True, pl.ds(r, blk, stride=16), False)
        return c

    lax.fori_loop(0, 16 // group, body16, 0)

    o_ref[...] = (acc_o[...] / acc_l[...]).astype(BF16)


def _attention(qkv1, qkv4, qkv16, batch, seq):
    width = qkv1[0].shape[-1]
    nspan = seq // ATTN_SPAN
    blk = BAND_BLOCK
    n1 = ATTN_SPAN // blk
    n4 = ATTN_SPAN // (4 * blk)

    cur1 = pl.BlockSpec((None, ATTN_SPAN, LANES), lambda b, i, p: (b, i, p))
    prev1 = pl.BlockSpec((None, None, blk, LANES), lambda b, i, p: (b, jnp.maximum(i * n1 - 1, 0), 0, p))
    cur4 = pl.BlockSpec((None, n4, 4, blk, LANES), lambda b, i, p: (b, i, 0, 0, p))
    prev4 = pl.BlockSpec((None, None, 4, blk, LANES), lambda b, i, p: (b, jnp.maximum(i * n4 - 1, 0), 0, 0, p))
    cur16 = pl.BlockSpec((None, None, 16, blk, LANES), lambda b, i, p: (b, i, 0, 0, p))
    prev16 = pl.BlockSpec((None, None, 16, blk, LANES), lambda b, i, p: (b, jnp.maximum(i - 1, 0), 0, 0, p))

    q1, k1, v1 = qkv1
    q4, k4, v4 = qkv4
    q16, k16, v16 = qkv16
    k1b = k1.reshape(batch, seq // blk, blk, width)
    v1b = v1.reshape(batch, seq // blk, blk, width)
    return pl.pallas_call(
        _attn_kernel,
        grid=(batch, nspan, width // LANES),
        in_specs=[cur1, cur1, cur1, prev1, prev1,
                  cur4, cur4, cur4, prev4, prev4,
                  cur16, cur16, cur16, prev16, prev16],
        out_specs=pl.BlockSpec((None, ATTN_SPAN, LANES), lambda b, i, p: (b, i, p)),
        out_shape=jax.ShapeDtypeStruct((batch, seq, width), BF16),
        scratch_shapes=[pltpu.VMEM((ATTN_SPAN, LANES), F32)] * 3,
        compiler_params=_params(("parallel", "parallel", "parallel")),
        name="dilated_attn",
    )(q1, k1, v1, k1b, v1b, q4, k4, v4, k4, v4, q16, k16, v16, k16, v16)


def _outproj_kernel(x_ref, a_ref, b_ref, wo_ref, mgb_ref, g1_ref, b1_ref, wr_ref, br_ref, tri_ref,
                    h_ref, idx_ref, gate_ref, rank_ref, idxt_ref, rankt_ref, runs_ref, cnt_ref, run_scr):
    @pl.when(pl.program_id(0) == 0)
    def _():
        run_scr[...] = jnp.zeros_like(run_scr)

    width = a_ref.shape[-1]
    bf = b_ref[...].astype(F32)
    bn = (bf * lax.rsqrt(jnp.mean(bf * bf, axis=-1, keepdims=True) + LN_EPS) * mgb_ref[...]).astype(BF16)
    mixed = _dot(a_ref[...], wo_ref[0:width, :]) + _dot(bn, wo_ref[width:2 * width, :])
    z = DN_ALPHA * x_ref[...] + mixed
    mu = jnp.mean(z, axis=-1, keepdims=True)
    zc = z - mu
    var = jnp.mean(zc * zc, axis=-1, keepdims=True)
    h = zc * lax.rsqrt(var + LN_EPS) * g1_ref[...] + b1_ref[...]
    h_ref[...] = h

    logits = _dot(h.astype(BF16), wr_ref[...]) + br_ref[...]
    lane = lax.broadcasted_iota(I32, logits.shape, 1).astype(F32)
    work = logits
    vals, idxs, hots = [], [], []
    for _ in range(TOP_K):
        mv = jnp.max(work, axis=-1, keepdims=True)
        ix = jnp.min(jnp.where(work == mv, lane, float(LANES)), axis=-1, keepdims=True)
        hot = lane == ix
        work = jnp.where(hot, -jnp.inf, work)
        vals.append(mv)
        idxs.append(ix)
        hots.append(hot)
    exps = [jnp.exp(v - vals[0]) for v in vals]
    den = exps[0] + exps[1] + exps[2] + exps[3]
    gates = [e / den for e in exps]

    member = jnp.zeros(logits.shape, F32)
    for hot in hots:
        member = jnp.where(hot, 1.0, member)
    before = _dot(tri_ref[...], member.astype(BF16)) + run_scr[...]
    ranks = [jnp.sum(jnp.where(hot, before, 0.0), axis=-1, keepdims=True) for hot in hots]
    runs_ref[0] = run_scr[...]
    for j in range(1, PROJ_ROWS // ROUTE_ROWS):
        runs_ref[j] = run_scr[...] + jnp.sum(member[:j * ROUTE_ROWS], axis=0, keepdims=True)
    run_scr[...] = run_scr[...] + jnp.sum(member, axis=0, keepdims=True)
    cnt_ref[...] = run_scr[...]

    def spread(cols):
        out = jnp.zeros(logits.shape, F32)
        for k, cval in enumerate(cols):
            out = jnp.where(lane == float(k), cval, out)
        return out

    idx_all = spread(idxs)
    rank_all = spread(ranks)
    idx_ref[...] = idx_all.astype(I32)
    gate_ref[...] = spread(gates)
    rank_ref[...] = rank_all.astype(I32)
    idxt_ref[...] = idx_all.T[:OCTET].astype(I32)
    rankt_ref[...] = rank_all.T[:OCTET].astype(I32)


def _outproj(x2, a_n, b2, w_out_b, mgb, g1, b1, wr, br, tri):
    t, dm = x2.shape
    width = dm // 2
    const = lambda *shape: pl.BlockSpec(shape, lambda m: (0,) * len(shape))
    rowblk = lambda w: pl.BlockSpec((PROJ_ROWS, w), lambda m: (m, 0))
    colblk = pl.BlockSpec((OCTET, PROJ_ROWS), lambda m: (0, m))
    sub = PROJ_ROWS // ROUTE_ROWS
    return pl.pallas_call(
        _outproj_kernel,
        grid=(t // PROJ_ROWS,),
        in_specs=[rowblk(dm), rowblk(width), rowblk(width), const(dm, dm), const(1, width),
                  const(1, dm), const(1, dm), const(dm, LANES), const(1, LANES),
                  const(PROJ_ROWS, PROJ_ROWS)],
        out_specs=[rowblk(dm), rowblk(LANES), rowblk(LANES), rowblk(LANES), colblk, colblk,
                   pl.BlockSpec((None, sub, 1, LANES), lambda m: (m, 0, 0, 0)), const(1, LANES)],
        out_shape=[jax.ShapeDtypeStruct((t, dm), F32), jax.ShapeDtypeStruct((t, LANES), I32),
                   jax.ShapeDtypeStruct((t, LANES), F32), jax.ShapeDtypeStruct((t, LANES), I32),
                   jax.ShapeDtypeStruct((OCTET, t), I32), jax.ShapeDtypeStruct((OCTET, t), I32),
                   jax.ShapeDtypeStruct((t // PROJ_ROWS, sub, 1, LANES), F32),
                   jax.ShapeDtypeStruct((1, LANES), F32)],
        scratch_shapes=[pltpu.VMEM((1, LANES), F32)],
        compiler_params=_params(("arbitrary",)),
        name="outproj_router",
    )(x2, a_n, b2, w_out_b, mgb, g1, b1, wr, br, tri)


def _octet(ref, q):
    return ref.at[pl.ds(pl.multiple_of(q * OCTET, OCTET), OCTET), :]


def _wait_octets(n, ref, sem):
    for s in (128, 64, 32, 16, 8, 4, 2, 1):
        @pl.when((n & s) != 0)
        def _():
            d = ref.at[pl.ds(0, s * OCTET), :]
            pltpu.make_async_copy(d, d, sem).wait()


def _dispatch_kernel(nq_ref, tail_ref, gq_ref, h_ref, idxt_ref, rankt_ref, tabt_ref, xs_hbm,
                     stage, zero8, sem, zsem):
    m = pl.program_id(0)
    eid = lax.broadcasted_iota(I32, (N_EXPERTS, ROUTE_ROWS), 0)
    pos = lax.broadcasted_iota(I32, (STAGE_ROWS, ROUTE_ROWS), 0).astype(F32)
    sel = None
    for k in range(TOP_K):
        hot = eid == idxt_ref[k:k + 1, :]
        lpos = (jnp.sum(jnp.where(hot, tabt_ref[...], 0.0), axis=0, keepdims=True)
                + rankt_ref[k:k + 1, :].astype(F32))
        hit = pos == lpos
        sel = hit if sel is None else jnp.logical_or(sel, hit)
    stage[...] = _dot(jnp.where(sel, 1.0, 0.0).astype(BF16), h_ref[...].astype(BF16))

    nq = nq_ref[m]

    def body(q, c):
        pltpu.make_async_copy(_octet(stage, q), _octet(xs_hbm, gq_ref[0, 0, q]), sem).start()
        return c

    lax.fori_loop(0, nq, body, 0)

    @pl.when(m == pl.num_programs(0) - 1)
    def _():
        zero8[...] = jnp.zeros_like(zero8)

        def zstart(i, c):
            pltpu.make_async_copy(zero8, _octet(xs_hbm, tail_ref[0] + i), zsem).start()
            return c

        def zwait(i, c):
            pltpu.make_async_copy(zero8, _octet(xs_hbm, 0), zsem).wait()
            return c

        lax.fori_loop(0, tail_ref[1], zstart, 0)
        lax.fori_loop(0, tail_ref[1], zwait, 0)

    _wait_octets(nq, xs_hbm, sem)


def _dispatch(h, idxt, rankt, tabt, gq3, nq, tail, nrows):
    t, dm = h.shape
    grid_spec = pltpu.PrefetchScalarGridSpec(
        num_scalar_prefetch=2,
        grid=(t // ROUTE_ROWS,),
        in_specs=[pl.BlockSpec((1, 1, STAGE_OCTETS), lambda m, nq, tl: (m, 0, 0), memory_space=pltpu.SMEM),
                  pl.BlockSpec((ROUTE_ROWS, dm), lambda m, nq, tl: (m, 0)),
                  pl.BlockSpec((OCTET, ROUTE_ROWS), lambda m, nq, tl: (0, m)),
                  pl.BlockSpec((OCTET, ROUTE_ROWS), lambda m, nq, tl: (0, m)),
                  pl.BlockSpec((None, N_EXPERTS, ROUTE_ROWS), lambda m, nq, tl: (m, 0, 0))],
        out_specs=pl.BlockSpec(memory_space=pl.ANY),
        scratch_shapes=[pltpu.VMEM((STAGE_ROWS, dm), F32), pltpu.VMEM((OCTET, dm), F32),
                        pltpu.SemaphoreType.DMA(()), pltpu.SemaphoreType.DMA(())],
    )
    return pl.pallas_call(
        _dispatch_kernel,
        grid_spec=grid_spec,
        out_shape=jax.ShapeDtypeStruct((nrows, dm), F32),
        compiler_params=_params(("arbitrary",)),
        name="moe_dispatch",
    )(nq, tail, gq3, h, idxt, rankt, tabt)


def _moe_kernel(ib_ref, ie_ref, lo_ref, hi_ref, xs_ref, wg_ref, bg_ref, wu_ref, bu_ref, wd_ref, bd_ref,
                ys_ref, wgb, wub, wdb):
    i = pl.program_id(0)
    prev = jnp.maximum(i - 1, 0)
    lo = lo_ref[i]
    hi = hi_ref[i]

    @pl.when(jnp.logical_and(lo >= 0, hi > lo))
    def _():
        @pl.when(jnp.logical_or(i == 0, ie_ref[i] != ie_ref[prev]))
        def _():
            wgb[...] = wg_ref[...].astype(BF16)
            wub[...] = wu_ref[...].astype(BF16)
            wdb[...] = wd_ref[...].astype(BF16)

        x = xs_ref[...].astype(BF16)
        g = jnp.minimum(_dot(x, wgb[...]) + bg_ref[...], SWIGLU_LIMIT)
        u = jnp.clip(_dot(x, wub[...]) + bu_ref[...], -SWIGLU_LIMIT, SWIGLU_LIMIT)
        act = (u + 1.0) * (g * jax.nn.sigmoid(SWIGLU_ALPHA * g))
        out = _dot(act.astype(BF16), wdb[...]) + bd_ref[...]
        rid = lax.broadcasted_iota(I32, out.shape, 0)
        mine = jnp.logical_and(rid >= lo, rid < hi)
        first_visit = jnp.logical_or(i == 0, ib_ref[i] != ib_ref[prev])

        @pl.when(first_visit)
        def _():
            ys_ref[...] = jnp.where(mine, out, 0.0)

        @pl.when(jnp.logical_not(first_visit))
        def _():
            ys_ref[...] = jnp.where(mine, out, ys_ref[...])

    @pl.when(lo < 0)
    def _():
        ys_ref[...] = jnp.zeros_like(ys_ref)


def _moe(xs, items, w_gate, b_gate, w_up, b_up, w_down, b_down):
    nrows = xs.shape[0]
    ne, dm, df = w_gate.shape
    nitems = items[0].shape[0]
    wspec = lambda r, c: pl.BlockSpec((None, r, c), lambda i, ib, ie, lo, hi: (ie[i], 0, 0))
    rspec = lambda w: pl.BlockSpec((MOE_ROWS, w), lambda i, ib, ie, lo, hi: (ib[i], 0))
    grid_spec = pltpu.PrefetchScalarGridSpec(
        num_scalar_prefetch=4,
        grid=(nitems,),
        in_specs=[rspec(dm), wspec(dm, df), wspec(1, df), wspec(dm, df), wspec(1, df), wspec(df, dm),
                  wspec(1, dm)],
        out_specs=rspec(dm),
        scratch_shapes=[pltpu.VMEM((dm, df), BF16), pltpu.VMEM((dm, df), BF16), pltpu.VMEM((df, dm), BF16)],
    )
    return pl.pallas_call(
        _moe_kernel,
        grid_spec=grid_spec,
        out_shape=jax.ShapeDtypeStruct((nrows, dm), F32),
        compiler_params=_params(("arbitrary",)),
        name="moe_experts",
    )(*items, xs, w_gate, b_gate.reshape(ne, 1, df), w_up, b_up.reshape(ne, 1, df),
      w_down, b_down.reshape(ne, 1, dm))


def _work_items(counts, nrows):
    nblk = nrows // MOE_ROWS
    nitems = nblk + N_EXPERTS - 1
    ends = jnp.cumsum(counts)
    starts = ends - counts
    b0 = jnp.arange(nblk, dtype=I32)[:, None] * MOE_ROWS
    lo = jnp.maximum(starts[None, :], b0)
    hi = jnp.minimum(ends[None, :], b0 + MOE_ROWS)
    nonempty = (hi > lo).reshape(-1)
    csum = jnp.cumsum(nonempty.astype(I32))
    j = jnp.arange(nitems, dtype=I32)
    pos = jnp.sum(csum[None, :] <= j[:, None], axis=1).astype(I32)
    used = j < csum[-1]
    pos = jnp.where(used, pos, jnp.max(jnp.where(nonempty, jnp.arange(nonempty.shape[0], dtype=I32), 0)))
    ib = pos // N_EXPERTS
    ie = pos % N_EXPERTS
    ilo = jnp.where(used, lo.reshape(-1)[pos] - ib * MOE_ROWS, 0)
    ihi = jnp.where(used, hi.reshape(-1)[pos] - ib * MOE_ROWS, 0)
    spare_blk = ib + 1 + (j - csum[-1])
    fill = jnp.logical_and(jnp.logical_not(used), spare_blk < nblk)
    ib = jnp.where(used, ib, jnp.minimum(spare_blk, nblk - 1))
    ilo = jnp.where(fill, -1, ilo)
    return ib.astype(I32), ie.astype(I32), ilo.astype(I32), ihi.astype(I32)


def _combine_kernel(nq_ref, gq_cur, gq_nxt, h_ref, idx_ref, rank_ref, gate_ref, tab_ref, g2_ref, b2_ref, ys_hbm,
                    o_ref, stage, sem):
    m = pl.program_id(0)
    last = pl.num_programs(0) - 1
    slot = m % 2

    def start_fetch(gq_ref, n, s):
        def body(q, c):
            pltpu.make_async_copy(_octet(ys_hbm, gq_ref[0, 0, q]), _octet(stage.at[s], q), sem.at[s]).start()
            return c

        lax.fori_loop(0, n, body, 0)

    @pl.when(m == 0)
    def _():
        stage[...] = jnp.zeros_like(stage)
        start_fetch(gq_cur, nq_ref[0], 0)

    @pl.when(m < last)
    def _():
        start_fetch(gq_nxt, nq_ref[jnp.minimum(m + 1, last)], 1 - slot)

    _wait_octets(nq_ref[m], stage.at[slot], sem.at[slot])

    lane_e = lax.broadcasted_iota(I32, idx_ref.shape, 1)
    lane_p = lax.broadcasted_iota(I32, (ROUTE_ROWS, STAGE_ROWS), 1).astype(F32)
    idx = idx_ref[...]
    rank = rank_ref[...].astype(F32)
    gates = gate_ref[...]
    wsel = jnp.zeros((ROUTE_ROWS, STAGE_ROWS), F32)
    for k in range(TOP_K):
        hot = lane_e == idx[:, k:k + 1]
        lpos = jnp.sum(jnp.where(hot, tab_ref[...], 0.0), axis=-1, keepdims=True) + rank[:, k:k + 1]
        wsel = jnp.where(lane_p == lpos, gates[:, k:k + 1], wsel)
    y = _dot(wsel.astype(BF16), stage[slot].astype(BF16))
    z = DN_ALPHA * h_ref[...] + y
    mu = jnp.mean(z, axis=-1, keepdims=True)
    zc = z - mu
    var = jnp.mean(zc * zc, axis=-1, keepdims=True)
    o_ref[...] = zc * lax.rsqrt(var + LN_EPS) * g2_ref[...] + b2_ref[...]


def _combine(h, ys, idx, rank, gates, tab, gq3, nq, g2, b2):
    t, dm = h.shape
    nt = t // ROUTE_ROWS
    rowblk = lambda w: pl.BlockSpec((ROUTE_ROWS, w), lambda m, nq: (m, 0))
    const = pl.BlockSpec((1, dm), lambda m, nq: (0, 0))
    qspec = lambda f: pl.BlockSpec((1, 1, STAGE_OCTETS), lambda m, nq: (f(m), 0, 0), memory_space=pltpu.SMEM)
    grid_spec = pltpu.PrefetchScalarGridSpec(
        num_scalar_prefetch=1,
        grid=(nt,),
        in_specs=[qspec(lambda m: m), qspec(lambda m: jnp.minimum(m + 1, nt - 1)),
                  rowblk(dm), rowblk(LANES), rowblk(LANES), rowblk(LANES),
                  pl.BlockSpec((None, 1, LANES), lambda m, nq: (m, 0, 0)), const, const,
                  pl.BlockSpec(memory_space=pl.ANY)],
        out_specs=rowblk(dm),
        scratch_shapes=[pltpu.VMEM((2, STAGE_ROWS, dm), F32), pltpu.SemaphoreType.DMA((2,))],
    )
    return pl.pallas_call(
        _combine_kernel,
        grid_spec=grid_spec,
        out_shape=jax.ShapeDtypeStruct((t, dm), F32),
        compiler_params=_params(("arbitrary",)),
        name="combine_ln",
    )(nq, gq3, gq3, h, idx, rank, gates, tab, g2, b2, ys)


def _route_tables(runs, total, nrows):
    nt = runs.shape[0]
    nxt = jnp.concatenate([runs[1:], total[None, :]], axis=0)
    c8 = (nxt - runs + OCTET - 1) // OCTET
    q_end = jnp.cumsum(c8, axis=1)
    q0 = q_end - c8
    nq = q_end[:, -1]
    per_expert = jnp.sum(c8, axis=0)
    start8 = jnp.cumsum(per_expert) - per_expert
    g8 = start8[None, :] + jnp.cumsum(c8, axis=0) - c8
    tab = OCTET * q0 - runs
    q = jnp.arange(STAGE_OCTETS, dtype=I32)
    e_of_q = jnp.minimum(jnp.sum(q[None, :, None] >= q_end[:, None, :], axis=2), N_EXPERTS - 1)
    gq = jnp.take_along_axis(g8 - q0, e_of_q, axis=1) + q[None, :]
    gq = jnp.where(q[None, :] < nq[:, None], gq, 0).astype(I32)
    tot8 = jnp.sum(per_expert)
    tail = jnp.stack([tot8, nrows // OCTET - tot8]).astype(I32)
    return tab, gq.reshape(nt, 1, STAGE_OCTETS), nq.astype(I32), tail, (per_expert * OCTET).astype(I32)


def _layer(x, w_in, sgu_w, sgu_b, sgu_ln_g, sgu_ln_b, mix_norm_g, w_out, ln1_g, ln1_b,
           w_router, b_router, w_gate, b_gate, w_up, b_up, w_down, b_down, ln2_g, ln2_b):
    batch, seq, dm = x.shape
    width = dm // 2
    t = batch * seq
    assert seq % ATTN_SPAN == 0 and dm % (2 * LANES) == 0 and w_router.shape[-1] == N_EXPERTS
    x2 = x.reshape(t, dm)

    wc = jnp.tril(sgu_w)
    wpair = jnp.concatenate([wc[0::2], wc[1::2]], axis=-1).astype(BF16)
    sbias = jnp.repeat(sgu_b.T, HEAD_DIM, axis=1)
    grp = jnp.arange(width) // HEAD_DIM
    gmat = jnp.where(grp[:, None] == grp[None, :], 1.0 / HEAD_DIM, 0.0).astype(BF16)
    row = lambda v: v.reshape(1, -1)

    a_n, q1, k1, v1, q4, k4, v4, q16, k16, v16 = _project(
        x2, w_in.astype(BF16), gmat, wpair, sbias, row(sgu_ln_g), row(sgu_ln_b),
        row(mix_norm_g[:width]), batch, seq)
    b = _attention((q1, k1, v1), (q4, k4, v4), (q16, k16, v16), batch, seq)

    wr = jnp.pad(w_router, ((0, 0), (0, LANES - N_EXPERTS))).astype(BF16)
    br = jnp.concatenate([b_router.astype(F32), jnp.full((LANES - N_EXPERTS,), -1e30, F32)]).reshape(1, LANES)
    ti = jnp.arange(PROJ_ROWS)
    tri = (ti[None, :] < ti[:, None]).astype(BF16)
    h, idx, gates, rank, idxt, rankt, runs, cnt = _outproj(
        x2, a_n.reshape(t, width), b.reshape(t, width), w_out.astype(BF16),
        row(mix_norm_g[width:]), row(ln1_g), row(ln1_b), wr, br, tri)

    nt = t // ROUTE_ROWS
    nrows = (t * TOP_K + nt * N_EXPERTS * (OCTET - 1) + MOE_ROWS - 1) // MOE_ROWS * MOE_ROWS
    runs = runs.reshape(nt, LANES)[:, :N_EXPERTS].astype(I32)
    total = cnt[0, :N_EXPERTS].astype(I32)
    tab, gq3, nq, tail, rows_e = _route_tables(runs, total, nrows)
    tabf = tab.astype(F32)
    tab_lane = jnp.pad(tabf, ((0, 0), (0, LANES - N_EXPERTS))).reshape(nt, 1, LANES)
    tab_sub = jnp.broadcast_to(tabf[:, :, None], (nt, N_EXPERTS, ROUTE_ROWS))

    xs = _dispatch(h, idxt, rankt, tab_sub, gq3, nq, tail, nrows)
    ys = _moe(xs, _work_items(rows_e, nrows), w_gate, b_gate, w_up, b_up, w_down, b_down)
    out = _combine(h, ys, idx, rank, gates, tab_lane, gq3, nq, row(ln2_g), row(ln2_b))
    return out.reshape(batch, seq, dm)


def kernel(x, w_in, sgu_w, sgu_b, sgu_ln_g, sgu_ln_b, mix_norm_g, w_out, ln1_g, ln1_b, w_router, b_router,
           w_gate, b_gate, w_up, b_up, w_down, b_down, ln2_g, ln2_b):
    assert w_in.shape[0] == DEPTH
    return _layer(x, w_in[0], sgu_w[0], sgu_b[0], sgu_ln_g[0], sgu_ln_b[0], mix_norm_g[0], w_out[0],
                  ln1_g[0], ln1_b[0], w_router[0], b_router[0], w_gate[0], b_gate[0], w_up[0], b_up[0],
                  w_down[0], b_down[0], ln2_g[0], ln2_b[0])
```

```python
import jax
import jax.numpy as jnp
from jax import lax
from jax.experimental import pallas as pl
from jax.experimental.pallas import tpu as pltpu

F32 = jnp.float32
BF16 = jnp.bfloat16
I32 = jnp.int32

LN_EPS = 1e-5
HEAD_DIM = 64
SGU_CHUNK = 128
BAND_BLOCK = 128
DILATIONS = (1, 4, 16)
ATTN_SPAN = BAND_BLOCK * DILATIONS[-1]
N_EXPERTS = 32
TOP_K = 4
SWIGLU_ALPHA = 1.702
SWIGLU_LIMIT = 7.0
DEPTH = 1
DN_ALPHA = (2 * DEPTH) ** 0.25

LANES = 128
PROJ_ROWS = 512
MOE_ROWS = 256
ROUTE_ROWS = 256
OCTET = 8
STAGE_ROWS = ROUTE_ROWS * TOP_K + N_EXPERTS * OCTET
STAGE_OCTETS = STAGE_ROWS // OCTET
VMEM_LIMIT = 56 * 1024 * 1024


def _params(sem):
    return pltpu.CompilerParams(dimension_semantics=sem, vmem_limit_bytes=VMEM_LIMIT)


def _dot(a, b):
    return jnp.dot(a, b, preferred_element_type=F32)


def _proj_kernel(x_ref, w_ref, gmat_ref, wpair_ref, sbias_ref, lng_ref, lnb_ref, mg_ref,
                 a_ref, q1_ref, k1_ref, v1_ref, q4_ref, k4_ref, v4_ref, q16_ref, k16_ref, v16_ref,
                 a_scr, t_scr):
    width = a_ref.shape[-1]
    xb = x_ref[...].astype(BF16)

    def proj(c):
        return _dot(xb, w_ref[:, c * width:(c + 1) * width])

    u = jax.nn.gelu(proj(0))
    v = jax.nn.gelu(proj(1))
    mean = _dot(v.astype(BF16), gmat_ref[...])
    d = v - mean
    var = _dot((d * d).astype(BF16), gmat_ref[...])
    vn = (d * lax.rsqrt(var + LN_EPS) * lng_ref[...] + lnb_ref[...]).astype(BF16)

    lane = lax.broadcasted_iota(I32, (SGU_CHUNK, LANES), 1)
    low = lane < HEAD_DIM
    zero = jnp.zeros((SGU_CHUNK, LANES), BF16)
    for c in range(PROJ_ROWS // SGU_CHUNK):
        rows = slice(c * SGU_CHUNK, (c + 1) * SGU_CHUNK)
        for j in range(width // LANES):
            cols = slice(j * LANES, (j + 1) * LANES)
            vp = vn[rows, cols]
            rhs = jnp.concatenate([jnp.where(low, vp, zero), jnp.where(low, zero, vp)], axis=0)
            gate = _dot(wpair_ref[j], rhs) + sbias_ref[:, cols]
            a_scr[rows, cols] = u[rows, cols] * gate
    a = a_scr[...]
    ms = jnp.mean(a * a, axis=-1, keepdims=True)
    a_ref[...] = (a * lax.rsqrt(ms + LN_EPS) * mg_ref[...]).astype(BF16)

    outs = ((q1_ref, q4_ref, q16_ref), (k1_ref, k4_ref, k16_ref), (v1_ref, v4_ref, v16_ref))
    for c, (o1, o4, o16) in enumerate(outs):
        t = proj(2 + c)
        if c == 0:
            t = t * (HEAD_DIM ** -0.5)
        o1[...] = t.astype(BF16)
        for j in range(width // LANES):
            cols = slice(j * LANES, (j + 1) * LANES)
            t_scr[j] = t[:, cols]
            for r in range(4):
                o4[r, :, cols] = t_scr[j, pl.ds(r, PROJ_ROWS // 4, stride=4), :].astype(BF16)
            for r in range(16):
                o16[r, :, cols] = t_scr[j, pl.ds(r, PROJ_ROWS // 16, stride=16), :].astype(BF16)


def _project(x2, w_in_b, gmat, wpair, sbias, lng, lnb, mg, batch, seq):
    t, dm = x2.shape
    width = dm // 2
    nt = seq // PROJ_ROWS
    per_span = ATTN_SPAN // PROJ_ROWS
    const = lambda *shape: pl.BlockSpec(shape, lambda b, m: (0,) * len(shape))
    o1 = jax.ShapeDtypeStruct((batch, seq, width), BF16)
    o4 = jax.ShapeDtypeStruct((batch, seq // 512, 4, BAND_BLOCK, width), BF16)
    o16 = jax.ShapeDtypeStruct((batch, seq // ATTN_SPAN, 16, BAND_BLOCK, width), BF16)
    s1 = pl.BlockSpec((None, PROJ_ROWS, width), lambda b, m: (b, m, 0))
    s4 = pl.BlockSpec((None, None, 4, BAND_BLOCK, width), lambda b, m: (b, m, 0, 0, 0))
    s16 = pl.BlockSpec((None, None, 16, PROJ_ROWS // 16, width),
                       lambda b, m: (b, m // per_span, 0, m % per_span, 0))
    return pl.pallas_call(
        _proj_kernel,
        grid=(batch, nt),
        in_specs=[pl.BlockSpec((PROJ_ROWS, dm), lambda b, m: (b * nt + m, 0)),
                  const(*w_in_b.shape), const(*gmat.shape), const(*wpair.shape), const(*sbias.shape),
                  const(1, width), const(1, width), const(1, width)],
        out_specs=[s1] + [s1, s1, s1] + [s4, s4, s4] + [s16, s16, s16],
        out_shape=[o1] + [o1, o1, o1] + [o4, o4, o4] + [o16, o16, o16],
        scratch_shapes=[pltpu.VMEM((PROJ_ROWS, width), F32),
                        pltpu.VMEM((width // LANES, PROJ_ROWS, LANES), F32)],
        compiler_params=_params(("parallel", "parallel")),
        name="proj_sgu",
    )(x2, w_in_b, gmat, wpair, sbias, lng, lnb, mg)


def _attn_kernel(q1, k1, v1, kp1, vp1, q4, k4, v4, kp4, vp4, q16, k16, v16, kp16, vp16,
                 o_ref, acc_o, acc_m, acc_l):
    not_first = pl.program_id(1) > 0
    blk = BAND_BLOCK
    row = lax.broadcasted_iota(I32, (2 * blk, 2 * blk), 0) % blk
    col = lax.broadcasted_iota(I32, (2 * blk, 2 * blk), 1)
    band = jnp.logical_or(jnp.logical_and(col < blk, col >= row), jnp.logical_and(col >= blk, col - blk <= row))
    neg = jnp.where(not_first, 0.0, -jnp.inf).astype(F32)
    first_pen = jnp.where(col < blk, neg, 0.0)
    low = lax.broadcasted_iota(I32, (blk, LANES), 1) < HEAD_DIM
    zero = jnp.zeros((blk, LANES), BF16)

    def unit(q, kprev, kcur, vprev, vcur, maybe_first, rows, init):
        kk = jnp.concatenate([kprev, kcur], axis=0)
        vv = jnp.concatenate([vprev, vcur], axis=0)
        qq = jnp.concatenate([jnp.where(low, q, zero), jnp.where(low, zero, q)], axis=0)
        s = lax.dot_general(qq, kk, (((1,), (1,)), ((), ())), preferred_element_type=F32)
        s = jnp.where(band, s, -jnp.inf)
        if maybe_first:
            s = s + first_pen
        m2 = jnp.max(s, axis=-1, keepdims=True)
        p = jnp.exp(s - m2)
        l2 = jnp.sum(p, axis=-1, keepdims=True)
        o2 = _dot(p.astype(BF16), vv)
        o = jnp.where(low, o2[:blk], o2[blk:])
        m = jnp.where(low, m2[:blk], m2[blk:])
        l = jnp.where(low, l2[:blk], l2[blk:])
        if init:
            acc_o[rows, :] = o
            acc_m[rows, :] = m
            acc_l[rows, :] = l
        else:
            mo = acc_m[rows, :]
            mn = jnp.maximum(mo, m)
            so = jnp.exp(mo - mn)
            sn = jnp.exp(m - mn)
            acc_o[rows, :] = acc_o[rows, :] * so + o * sn
            acc_l[rows, :] = acc_l[rows, :] * so + l * sn
            acc_m[rows, :] = mn

    group = 4

    def rows1(base, j):
        return pl.ds(base + j * blk, blk)

    for j in range(group):
        cur = rows1(0, j)
        if j == 0:
            unit(q1[cur, :], kp1[...], k1[cur, :], vp1[...], v1[cur, :], True, cur, True)
        else:
            prev = rows1(0, j - 1)
            unit(q1[cur, :], k1[prev, :], k1[cur, :], v1[prev, :], v1[cur, :], False, cur, True)

    def body1(g, c):
        base = pl.multiple_of(g * (group * blk), group * blk)
        for j in range(group):
            cur = rows1(base, j)
            prev = rows1(base, j - 1)
            unit(q1[cur, :], k1[prev, :], k1[cur, :], v1[prev, :], v1[cur, :], False, cur, True)
        return c

    lax.fori_loop(1, ATTN_SPAN // (group * blk), body1, 0)

    for r in range(4):
        unit(q4[0, r], kp4[r], k4[0, r], vp4[r], v4[0, r], True, pl.ds(r, blk, stride=4), False)

    def body4(s, c):
        for r in range(4):
            unit(q4[s, r], k4[s - 1, r], k4[s, r], v4[s - 1, r], v4[s, r], False,
                 pl.ds(s * (4 * blk) + r, blk, stride=4), False)
        return c

    lax.fori_loop(1, ATTN_SPAN // (4 * blk), body4, 0)

    def body16(g, c):
        for j in range(group):
            r = g * group + j
            unit(q16[r], kp16[r], k16[r], vp16[r], v16[r], True, pl.ds(r, blk, stride=16), False)
        return c

    lax.fori_loop(0, 16 // group, body16, 0)

    o_ref[...] = (acc_o[...] / acc_l[...]).astype(BF16)


def _attention(qkv1, qkv4, qkv16, batch, seq):
    width = qkv1[0].shape[-1]
    nspan = seq // ATTN_SPAN
    blk = BAND_BLOCK
    n1 = ATTN_SPAN // blk
    n4 = ATTN_SPAN // (4 * blk)

    cur1 = pl.BlockSpec((None, ATTN_SPAN, LANES), lambda b, i, p: (b, i, p))
    prev1 = pl.BlockSpec((None, None, blk, LANES), lambda b, i, p: (b, jnp.maximum(i * n1 - 1, 0), 0, p))
    cur4 = pl.BlockSpec((None, n4, 4, blk, LANES), lambda b, i, p: (b, i, 0, 0, p))
    prev4 = pl.BlockSpec((None, None, 4, blk, LANES), lambda b, i, p: (b, jnp.maximum(i * n4 - 1, 0), 0, 0, p))
    cur16 = pl.BlockSpec((None, None, 16, blk, LANES), lambda b, i, p: (b, i, 0, 0, p))
    prev16 = pl.BlockSpec((None, None, 16, blk, LANES), lambda b, i, p: (b, jnp.maximum(i - 1, 0), 0, 0, p))

    q1, k1, v1 = qkv1
    q4, k4, v4 = qkv4
    q16, k16, v16 = qkv16
    k1b = k1.reshape(batch, seq // blk, blk, width)
    v1b = v1.reshape(batch, seq // blk, blk, width)
    return pl.pallas_call(
        _attn_kernel,
        grid=(batch, nspan, width // LANES),
        in_specs=[cur1, cur1, cur1, prev1, prev1,
                  cur4, cur4, cur4, prev4, prev4,
                  cur16, cur16, cur16, prev16, prev16],
        out_specs=pl.BlockSpec((None, ATTN_SPAN, LANES), lambda b, i, p: (b, i, p)),
        out_shape=jax.ShapeDtypeStruct((batch, seq, width), BF16),
        scratch_shapes=[pltpu.VMEM((ATTN_SPAN, LANES), F32)] * 3,
        compiler_params=_params(("parallel", "parallel", "parallel")),
        name="dilated_attn",
    )(q1, k1, v1, k1b, v1b, q4, k4, v4, k4, v4, q16, k16, v16, k16, v16)


def _outproj_kernel(x_ref, a_ref, b_ref, wo_ref, mgb_ref, g1_ref, b1_ref, wr_ref, br_ref, tri_ref,
                    h_ref, idx_ref, gate_ref, rank_ref, idxt_ref, rankt_ref, runs_ref, cnt_ref, run_scr):
    @pl.when(pl.program_id(0) == 0)
    def _():
        run_scr[...] = jnp.zeros_like(run_scr)

    width = a_ref.shape[-1]
    bf = b_ref[...].astype(F32)
    bn = (bf * lax.rsqrt(jnp.mean(bf * bf, axis=-1, keepdims=True) + LN_EPS) * mgb_ref[...]).astype(BF16)
    mixed = _dot(a_ref[...], wo_ref[0:width, :]) + _dot(bn, wo_ref[width:2 * width, :])
    z = DN_ALPHA * x_ref[...] + mixed
    mu = jnp.mean(z, axis=-1, keepdims=True)
    zc = z - mu
    var = jnp.mean(zc * zc, axis=-1, keepdims=True)
    h = zc * lax.rsqrt(var + LN_EPS) * g1_ref[...] + b1_ref[...]
    h_ref[...] = h

    logits = _dot(h.astype(BF16), wr_ref[...]) + br_ref[...]
    lane = lax.broadcasted_iota(I32, logits.shape, 1).astype(F32)
    work = logits
    vals, idxs, hots = [], [], []
    for _ in range(TOP_K):
        mv = jnp.max(work, axis=-1, keepdims=True)
        ix = jnp.min(jnp.where(work == mv, lane, float(LANES)), axis=-1, keepdims=True)
        hot = lane == ix
        work = jnp.where(hot, -jnp.inf, work)
        vals.append(mv)
        idxs.append(ix)
        hots.append(hot)
    exps = [jnp.exp(v - vals[0]) for v in vals]
    den = exps[0] + exps[1] + exps[2] + exps[3]
    gates = [e / den for e in exps]

    member = jnp.zeros(logits.shape, F32)
    for hot in hots:
        member = jnp.where(hot, 1.0, member)
    before = _dot(tri_ref[...], member.astype(BF16)) + run_scr[...]
    ranks = [jnp.sum(jnp.where(hot, before, 0.0), axis=-1, keepdims=True) for hot in hots]
    runs_ref[0] = run_scr[...]
    for j in range(1, PROJ_ROWS // ROUTE_ROWS):
        runs_ref[j] = run_scr[...] + jnp.sum(member[:j * ROUTE_ROWS], axis=0, keepdims=True)
    run_scr[...] = run_scr[...] + jnp.sum(member, axis=0, keepdims=True)
    cnt_ref[...] = run_scr[...]

    def spread(cols):
        out = jnp.zeros(logits.shape, F32)
        for k, cval in enumerate(cols):
            out = jnp.where(lane == float(k), cval, out)
        return out

    idx_all = spread(idxs)
    rank_all = spread(ranks)
    idx_ref[...] = idx_all.astype(I32)
    gate_ref[...] = spread(gates)
    rank_ref[...] = rank_all.astype(I32)
    idxt_ref[...] = idx_all.T[:OCTET].astype(I32)
    rankt_ref[...] = rank_all.T[:OCTET].astype(I32)


def _outproj(x2, a_n, b2, w_out_b, mgb, g1, b1, wr, br, tri):
    t, dm = x2.shape
    width = dm // 2
    const = lambda *shape: pl.BlockSpec(shape, lambda m: (0,) * len(shape))
    rowblk = lambda w: pl.BlockSpec((PROJ_ROWS, w), lambda m: (m, 0))
    colblk = pl.BlockSpec((OCTET, PROJ_ROWS), lambda m: (0, m))
    sub = PROJ_ROWS // ROUTE_ROWS
    return pl.pallas_call(
        _outproj_kernel,
        grid=(t // PROJ_ROWS,),
        in_specs=[rowblk(dm), rowblk(width), rowblk(width), const(dm, dm), const(1, width),
                  const(1, dm), const(1, dm), const(dm, LANES), const(1, LANES),
                  const(PROJ_ROWS, PROJ_ROWS)],
        out_specs=[rowblk(dm), rowblk(LANES), rowblk(LANES), rowblk(LANES), colblk, colblk,
                   pl.BlockSpec((None, sub, 1, LANES), lambda m: (m, 0, 0, 0)), const(1, LANES)],
        out_shape=[jax.ShapeDtypeStruct((t, dm), F32), jax.ShapeDtypeStruct((t, LANES), I32),
                   jax.ShapeDtypeStruct((t, LANES), F32), jax.ShapeDtypeStruct((t, LANES), I32),
                   jax.ShapeDtypeStruct((OCTET, t), I32), jax.ShapeDtypeStruct((OCTET, t), I32),
                   jax.ShapeDtypeStruct((t // PROJ_ROWS, sub, 1, LANES), F32),
                   jax.ShapeDtypeStruct((1, LANES), F32)],
        scratch_shapes=[pltpu.VMEM((1, LANES), F32)],
        compiler_params=_params(("arbitrary",)),
        name="outproj_router",
    )(x2, a_n, b2, w_out_b, mgb, g1, b1, wr, br, tri)


def _octet(ref, q):
    return ref.at[pl.ds(pl.multiple_of(q * OCTET, OCTET), OCTET), :]


def _for_octets(n, start):
    def body(i, c):
        start(2 * i, 0)
        start(2 * i + 1, 1)
        return c

    lax.fori_loop(0, n // 2, body, 0)

    @pl.when(n % 2 == 1)
    def _():
        start(n - 1, 0)


def _wait_octets(n, ref, sem):
    for s in (128, 64, 32, 16, 8, 4, 2, 1):
        @pl.when((n & s) != 0)
        def _():
            d = ref.at[pl.ds(0, s * OCTET), :]
            pltpu.make_async_copy(d, d, sem).wait()


def _dispatch_kernel(nq_ref, tail_ref, gq_ref, h_ref, idxt_ref, rankt_ref, tabt_ref, xs_hbm,
                     stage, zero8, sem, zsem):
    m = pl.program_id(0)
    eid = lax.broadcasted_iota(I32, (N_EXPERTS, ROUTE_ROWS), 0)
    pos = lax.broadcasted_iota(I32, (STAGE_ROWS, ROUTE_ROWS), 0).astype(F32)
    sel = None
    for k in range(TOP_K):
        hot = eid == idxt_ref[k:k + 1, :]
        lpos = (jnp.sum(jnp.where(hot, tabt_ref[...], 0.0), axis=0, keepdims=True)
                + rankt_ref[k:k + 1, :].astype(F32))
        hit = pos == lpos
        sel = hit if sel is None else jnp.logical_or(sel, hit)
    stage[...] = _dot(jnp.where(sel, 1.0, 0.0).astype(BF16), h_ref[...].astype(BF16))

    nq = nq_ref[m]
    _for_octets(nq, lambda q, pri: pltpu.make_async_copy(
        _octet(stage, q), _octet(xs_hbm, gq_ref[0, 0, q]), sem).start(priority=pri))

    @pl.when(m == pl.num_programs(0) - 1)
    def _():
        zero8[...] = jnp.zeros_like(zero8)

        def zstart(i, c):
            pltpu.make_async_copy(zero8, _octet(xs_hbm, tail_ref[0] + i), zsem).start()
            return c

        def zwait(i, c):
            pltpu.make_async_copy(zero8, _octet(xs_hbm, 0), zsem).wait()
            return c

        lax.fori_loop(0, tail_ref[1], zstart, 0)
        lax.fori_loop(0, tail_ref[1], zwait, 0)

    _wait_octets(nq, xs_hbm, sem)


def _dispatch(h, idxt, rankt, tabt, gq3, nq, tail, nrows):
    t, dm = h.shape
    grid_spec = pltpu.PrefetchScalarGridSpec(
        num_scalar_prefetch=2,
        grid=(t // ROUTE_ROWS,),
        in_specs=[pl.BlockSpec((1, 1, STAGE_OCTETS), lambda m, nq, tl: (m, 0, 0), memory_space=pltpu.SMEM),
                  pl.BlockSpec((ROUTE_ROWS, dm), lambda m, nq, tl: (m, 0)),
                  pl.BlockSpec((OCTET, ROUTE_ROWS), lambda m, nq, tl: (0, m)),
                  pl.BlockSpec((OCTET, ROUTE_ROWS), lambda m, nq, tl: (0, m)),
                  pl.BlockSpec((None, N_EXPERTS, ROUTE_ROWS), lambda m, nq, tl: (m, 0, 0))],
        out_specs=pl.BlockSpec(memory_space=pl.ANY),
        scratch_shapes=[pltpu.VMEM((STAGE_ROWS, dm), F32), pltpu.VMEM((OCTET, dm), F32),
                        pltpu.SemaphoreType.DMA(()), pltpu.SemaphoreType.DMA(())],
    )
    return pl.pallas_call(
        _dispatch_kernel,
        grid_spec=grid_spec,
        out_shape=jax.ShapeDtypeStruct((nrows, dm), F32),
        compiler_params=_params(("arbitrary",)),
        name="moe_dispatch",
    )(nq, tail, gq3, h, idxt, rankt, tabt)


def _moe_kernel(ib_ref, ie_ref, lo_ref, hi_ref, nxt_ref, slot_ref, xs_ref, bg_ref, bu_ref, bd_ref,
                wg_hbm, wu_hbm, wd_hbm, ys_ref, wf, wb, wsem):
    i = pl.program_id(0)
    prev = jnp.maximum(i - 1, 0)
    lo = lo_ref[i]
    hi = hi_ref[i]

    def weight_copies(e, s):
        return [pltpu.make_async_copy(w.at[e], wf.at[s, j], wsem.at[s])
                for j, w in enumerate((wg_hbm, wu_hbm, wd_hbm))]

    @pl.when(jnp.logical_and(lo >= 0, hi > lo))
    def _():
        @pl.when(jnp.logical_or(i == 0, ie_ref[i] != ie_ref[prev]))
        def _():
            s = slot_ref[i]

            @pl.when(i == 0)
            def _():
                for cp in weight_copies(ie_ref[i], s):
                    cp.start()

            for cp in weight_copies(ie_ref[i], s):
                cp.wait()
            for j in range(3):
                wb[j] = wf[s, j].astype(BF16)

            @pl.when(nxt_ref[i] >= 0)
            def _():
                for cp in weight_copies(nxt_ref[i], 1 - s):
                    cp.start()

        x = xs_ref[...].astype(BF16)
        g = jnp.minimum(_dot(x, wb[0]) + bg_ref[...], SWIGLU_LIMIT)
        u = jnp.clip(_dot(x, wb[1]) + bu_ref[...], -SWIGLU_LIMIT, SWIGLU_LIMIT)
        act = (u + 1.0) * (g * jax.nn.sigmoid(SWIGLU_ALPHA * g))
        out = _dot(act.astype(BF16), wb[2]) + bd_ref[...]
        rid = lax.broadcasted_iota(I32, out.shape, 0)
        mine = jnp.logical_and(rid >= lo, rid < hi)
        first_visit = jnp.logical_or(i == 0, ib_ref[i] != ib_ref[prev])

        @pl.when(first_visit)
        def _():
            ys_ref[...] = jnp.where(mine, out, 0.0)

        @pl.when(jnp.logical_not(first_visit))
        def _():
            ys_ref[...] = jnp.where(mine, out, ys_ref[...])

    @pl.when(lo < 0)
    def _():
        ys_ref[...] = jnp.zeros_like(ys_ref)


def _moe(xs, items, w_gate, b_gate, w_up, b_up, w_down, b_down):
    nrows = xs.shape[0]
    ne, dm, df = w_gate.shape
    assert dm == df
    nitems = items[0].shape[0]
    bspec = lambda c: pl.BlockSpec((None, 1, c), lambda i, ib, ie, *_: (ie[i], 0, 0))
    rspec = pl.BlockSpec((MOE_ROWS, dm), lambda i, ib, *_: (ib[i], 0))
    hbm = pl.BlockSpec(memory_space=pl.ANY)
    grid_spec = pltpu.PrefetchScalarGridSpec(
        num_scalar_prefetch=6,
        grid=(nitems,),
        in_specs=[rspec, bspec(df), bspec(df), bspec(dm), hbm, hbm, hbm],
        out_specs=rspec,
        scratch_shapes=[pltpu.VMEM((2, 3, dm, df), F32), pltpu.VMEM((3, dm, df), BF16),
                        pltpu.SemaphoreType.DMA((2,))],
    )
    return pl.pallas_call(
        _moe_kernel,
        grid_spec=grid_spec,
        out_shape=jax.ShapeDtypeStruct((nrows, dm), F32),
        compiler_params=_params(("arbitrary",)),
        name="moe_experts",
    )(*items, xs, b_gate.reshape(ne, 1, df), b_up.reshape(ne, 1, df), b_down.reshape(ne, 1, dm),
      w_gate, w_up, w_down)


def _work_items(counts, nrows):
    nblk = nrows // MOE_ROWS
    nitems = nblk + N_EXPERTS - 1
    ends = jnp.cumsum(counts)
    starts = ends - counts
    b0 = jnp.arange(nblk, dtype=I32)[:, None] * MOE_ROWS
    lo = jnp.maximum(starts[None, :], b0)
    hi = jnp.minimum(ends[None, :], b0 + MOE_ROWS)
    nonempty = (hi > lo).reshape(-1)
    csum = jnp.cumsum(nonempty.astype(I32))
    j = jnp.arange(nitems, dtype=I32)
    pos = jnp.sum(csum[None, :] <= j[:, None], axis=1).astype(I32)
    used = j < csum[-1]
    pos = jnp.where(used, pos, jnp.max(jnp.where(nonempty, jnp.arange(nonempty.shape[0], dtype=I32), 0)))
    ib = pos // N_EXPERTS
    ie = pos % N_EXPERTS
    ilo = jnp.where(used, lo.reshape(-1)[pos] - ib * MOE_ROWS, 0)
    ihi = jnp.where(used, hi.reshape(-1)[pos] - ib * MOE_ROWS, 0)
    spare_blk = ib + 1 + (j - csum[-1])
    fill = jnp.logical_and(jnp.logical_not(used), spare_blk < nblk)
    ib = jnp.where(used, ib, jnp.minimum(spare_blk, nblk - 1))
    ilo = jnp.where(fill, -1, ilo)
    ordinal = jnp.cumsum(jnp.concatenate([jnp.zeros((1,), I32), (ie[1:] != ie[:-1]).astype(I32)]))
    first_next = jnp.sum(ordinal[None, :] <= ordinal[:, None], axis=1)
    has_next = first_next < nitems
    nxt = jnp.where(has_next, ie[jnp.minimum(first_next, nitems - 1)], -1)
    return (ib.astype(I32), ie.astype(I32), ilo.astype(I32), ihi.astype(I32), nxt.astype(I32),
            (ordinal % 2).astype(I32))


def _combine_kernel(nq_ref, gq_cur, gq_nxt, h_ref, idx_ref, rank_ref, gate_ref, tab_ref, g2_ref, b2_ref, ys_hbm,
                    o_ref, stage, sem):
    m = pl.program_id(0)
    last = pl.num_programs(0) - 1
    slot = m % 2

    def start_fetch(gq_ref, n, s):
        _for_octets(n, lambda q, pri: pltpu.make_async_copy(
            _octet(ys_hbm, gq_ref[0, 0, q]), _octet(stage.at[s], q), sem.at[s]).start(priority=pri))

    @pl.when(m == 0)
    def _():
        stage[...] = jnp.zeros_like(stage)
        start_fetch(gq_cur, nq_ref[0], 0)

    @pl.when(m < last)
    def _():
        start_fetch(gq_nxt, nq_ref[jnp.minimum(m + 1, last)], 1 - slot)

    _wait_octets(nq_ref[m], stage.at[slot], sem.at[slot])

    lane_e = lax.broadcasted_iota(I32, idx_ref.shape, 1)
    lane_p = lax.broadcasted_iota(I32, (ROUTE_ROWS, STAGE_ROWS), 1).astype(F32)
    idx = idx_ref[...]
    rank = rank_ref[...].astype(F32)
    gates = gate_ref[...]
    wsel = jnp.zeros((ROUTE_ROWS, STAGE_ROWS), F32)
    for k in range(TOP_K):
        hot = lane_e == idx[:, k:k + 1]
        lpos = jnp.sum(jnp.where(hot, tab_ref[...], 0.0), axis=-1, keepdims=True) + rank[:, k:k + 1]
        wsel = jnp.where(lane_p == lpos, gates[:, k:k + 1], wsel)
    y = _dot(wsel.astype(BF16), stage[slot].astype(BF16))
    z = DN_ALPHA * h_ref[...] + y
    mu = jnp.mean(z, axis=-1, keepdims=True)
    zc = z - mu
    var = jnp.mean(zc * zc, axis=-1, keepdims=True)
    o_ref[...] = zc * lax.rsqrt(var + LN_EPS) * g2_ref[...] + b2_ref[...]


def _combine(h, ys, idx, rank, gates, tab, gq3, nq, g2, b2):
    t, dm = h.shape
    nt = t // ROUTE_ROWS
    rowblk = lambda w: pl.BlockSpec((ROUTE_ROWS, w), lambda m, nq: (m, 0))
    const = pl.BlockSpec((1, dm), lambda m, nq: (0, 0))
    qspec = lambda f: pl.BlockSpec((1, 1, STAGE_OCTETS), lambda m, nq: (f(m), 0, 0), memory_space=pltpu.SMEM)
    grid_spec = pltpu.PrefetchScalarGridSpec(
        num_scalar_prefetch=1,
        grid=(nt,),
        in_specs=[qspec(lambda m: m), qspec(lambda m: jnp.minimum(m + 1, nt - 1)),
                  rowblk(dm), rowblk(LANES), rowblk(LANES), rowblk(LANES),
                  pl.BlockSpec((None, 1, LANES), lambda m, nq: (m, 0, 0)), const, const,
                  pl.BlockSpec(memory_space=pl.ANY)],
        out_specs=rowblk(dm),
        scratch_shapes=[pltpu.VMEM((2, STAGE_ROWS, dm), F32), pltpu.SemaphoreType.DMA((2,))],
    )
    return pl.pallas_call(
        _combine_kernel,
        grid_spec=grid_spec,
        out_shape=jax.ShapeDtypeStruct((t, dm), F32),
        compiler_params=_params(("arbitrary",)),
        name="combine_ln",
    )(nq, gq3, gq3, h, idx, rank, gates, tab, g2, b2, ys)


def _route_tables(runs, total, nrows):
    nt = runs.shape[0]
    nxt = jnp.concatenate([runs[1:], total[None, :]], axis=0)
    c8 = (nxt - runs + OCTET - 1) // OCTET
    q_end = jnp.cumsum(c8, axis=1)
    q0 = q_end - c8
    nq = q_end[:, -1]
    per_expert = jnp.sum(c8, axis=0)
    start8 = jnp.cumsum(per_expert) - per_expert
    g8 = start8[None, :] + jnp.cumsum(c8, axis=0) - c8
    tab = OCTET * q0 - runs
    q = jnp.arange(STAGE_OCTETS, dtype=I32)
    e_of_q = jnp.minimum(jnp.sum(q[None, :, None] >= q_end[:, None, :], axis=2), N_EXPERTS - 1)
    pick = e_of_q[:, :, None] == jnp.arange(N_EXPERTS, dtype=I32)[None, None, :]
    gq = jnp.sum(jnp.where(pick, (g8 - q0)[:, None, :], 0), axis=2) + q[None, :]
    gq = jnp.where(q[None, :] < nq[:, None], gq, 0).astype(I32)
    tot8 = jnp.sum(per_expert)
    tail = jnp.stack([tot8, nrows // OCTET - tot8]).astype(I32)
    return tab, gq.reshape(nt, 1, STAGE_OCTETS), nq.astype(I32), tail, (per_expert * OCTET).astype(I32)


def _layer(x, w_in, sgu_w, sgu_b, sgu_ln_g, sgu_ln_b, mix_norm_g, w_out, ln1_g, ln1_b,
           w_router, b_router, w_gate, b_gate, w_up, b_up, w_down, b_down, ln2_g, ln2_b):
    batch, seq, dm = x.shape
    width = dm // 2
    t = batch * seq
    assert seq % ATTN_SPAN == 0 and dm % (2 * LANES) == 0 and w_router.shape[-1] == N_EXPERTS
    x2 = x.reshape(t, dm)

    wc = jnp.tril(sgu_w)
    wpair = jnp.concatenate([wc[0::2], wc[1::2]], axis=-1).astype(BF16)
    sbias = jnp.repeat(sgu_b.T, HEAD_DIM, axis=1)
    grp = jnp.arange(width) // HEAD_DIM
    gmat = jnp.where(grp[:, None] == grp[None, :], 1.0 / HEAD_DIM, 0.0).astype(BF16)
    row = lambda v: v.reshape(1, -1)

    a_n, q1, k1, v1, q4, k4, v4, q16, k16, v16 = _project(
        x2, w_in.astype(BF16), gmat, wpair, sbias, row(sgu_ln_g), row(sgu_ln_b),
        row(mix_norm_g[:width]), batch, seq)
    b = _attention((q1, k1, v1), (q4, k4, v4), (q16, k16, v16), batch, seq)

    wr = jnp.pad(w_router, ((0, 0), (0, LANES - N_EXPERTS))).astype(BF16)
    br = jnp.concatenate([b_router.astype(F32), jnp.full((LANES - N_EXPERTS,), -1e30, F32)]).reshape(1, LANES)
    ti = jnp.arange(PROJ_ROWS)
    tri = (ti[None, :] < ti[:, None]).astype(BF16)
    h, idx, gates, rank, idxt, rankt, runs, cnt = _outproj(
        x2, a_n.reshape(t, width), b.reshape(t, width), w_out.astype(BF16),
        row(mix_norm_g[width:]), row(ln1_g), row(ln1_b), wr, br, tri)

    nt = t // ROUTE_ROWS
    nrows = (t * TOP_K + nt * N_EXPERTS * (OCTET - 1) + MOE_ROWS - 1) // MOE_ROWS * MOE_ROWS
    runs = runs.reshape(nt, LANES)[:, :N_EXPERTS].astype(I32)
    total = cnt[0, :N_EXPERTS].astype(I32)
    tab, gq3, nq, tail, rows_e = _route_tables(runs, total, nrows)
    tabf = tab.astype(F32)
    tab_lane = jnp.pad(tabf, ((0, 0), (0, LANES - N_EXPERTS))).reshape(nt, 1, LANES)
    tab_sub = jnp.broadcast_to(tabf[:, :, None], (nt, N_EXPERTS, ROUTE_ROWS))

    xs = _dispatch(h, idxt, rankt, tab_sub, gq3, nq, tail, nrows)
    ys = _moe(xs, _work_items(rows_e, nrows), w_gate, b_gate, w_up, b_up, w_down, b_down)
    out = _combine(h, ys, idx, rank, gates, tab_lane, gq3, nq, row(ln2_g), row(ln2_b))
    return out.reshape(batch, seq, dm)


def kernel(x, w_in, sgu_w, sgu_b, sgu_ln_g, sgu_ln_b, mix_norm_g, w_out, ln1_g, ln1_b, w_router, b_router,
           w_gate, b_gate, w_up, b_up, w_down, b_down, ln2_g, ln2_b):
    assert w_in.shape[0] == DEPTH
    return _layer(x, w_in[0], sgu_w[0], sgu_b[0], sgu_ln_g[0], sgu_ln_b[0], mix_norm_g[0], w_out[0],
                  ln1_g[0], ln1_b[0], w_router[0], b_router[0], w_gate[0], b_gate[0], w_up[0], b_up[0],
                  w_down[0], b_down[0], ln2_g[0], ln2_b[0])
```

```python
import jax
import jax.numpy as jnp
from jax import lax
from jax.experimental import pallas as pl
from jax.experimental.pallas import tpu as pltpu

F32 = jnp.float32
BF16 = jnp.bfloat16
I32 = jnp.int32

LN_EPS = 1e-5
HEAD_DIM = 64
SGU_CHUNK = 128
BAND_BLOCK = 128
DILATIONS = (1, 4, 16)
ATTN_SPAN = BAND_BLOCK * DILATIONS[-1]
N_EXPERTS = 32
TOP_K = 4
SWIGLU_ALPHA = 1.702
SWIGLU_LIMIT = 7.0
DEPTH = 1
DN_ALPHA = (2 * DEPTH) ** 0.25

LANES = 128
PROJ_ROWS = 512
MOE_ROWS = 512
MOE_SUB = 256
ROUTE_ROWS = 256
OCTET = 8
STAGE_ROWS = ROUTE_ROWS * TOP_K + N_EXPERTS * OCTET
STAGE_OCTETS = STAGE_ROWS // OCTET
VMEM_LIMIT = 56 * 1024 * 1024


def _params(sem):
    return pltpu.CompilerParams(dimension_semantics=sem, vmem_limit_bytes=VMEM_LIMIT)


def _dot(a, b):
    return jnp.dot(a, b, preferred_element_type=F32)


def _proj_kernel(x_ref, w_ref, gmat_ref, wpair_ref, sbias_ref, lng_ref, lnb_ref, mg_ref,
                 a_ref, q1_ref, k1_ref, v1_ref, q4_ref, k4_ref, v4_ref, q16_ref, k16_ref, v16_ref,
                 a_scr, t_scr):
    width = a_ref.shape[-1]
    xb = x_ref[...].astype(BF16)

    def proj(c):
        return _dot(xb, w_ref[:, c * width:(c + 1) * width])

    u = jax.nn.gelu(proj(0))
    v = jax.nn.gelu(proj(1))
    mean = _dot(v.astype(BF16), gmat_ref[...])
    d = v - mean
    var = _dot((d * d).astype(BF16), gmat_ref[...])
    vn = (d * lax.rsqrt(var + LN_EPS) * lng_ref[...] + lnb_ref[...]).astype(BF16)

    lane = lax.broadcasted_iota(I32, (SGU_CHUNK, LANES), 1)
    low = lane < HEAD_DIM
    zero = jnp.zeros((SGU_CHUNK, LANES), BF16)
    for c in range(PROJ_ROWS // SGU_CHUNK):
        rows = slice(c * SGU_CHUNK, (c + 1) * SGU_CHUNK)
        for j in range(width // LANES):
            cols = slice(j * LANES, (j + 1) * LANES)
            vp = vn[rows, cols]
            rhs = jnp.concatenate([jnp.where(low, vp, zero), jnp.where(low, zero, vp)], axis=0)
            gate = _dot(wpair_ref[j], rhs) + sbias_ref[:, cols]
            a_scr[rows, cols] = u[rows, cols] * gate
    a = a_scr[...]
    ms = jnp.mean(a * a, axis=-1, keepdims=True)
    a_ref[...] = (a * lax.rsqrt(ms + LN_EPS) * mg_ref[...]).astype(BF16)

    outs = ((q1_ref, q4_ref, q16_ref), (k1_ref, k4_ref, k16_ref), (v1_ref, v4_ref, v16_ref))
    for c, (o1, o4, o16) in enumerate(outs):
        t = proj(2 + c)
        if c == 0:
            t = t * (HEAD_DIM ** -0.5)
        o1[...] = t.astype(BF16)
        for j in range(width // LANES):
            cols = slice(j * LANES, (j + 1) * LANES)
            t_scr[j] = t[:, cols]
            for r in range(4):
                o4[r, :, cols] = t_scr[j, pl.ds(r, PROJ_ROWS // 4, stride=4), :].astype(BF16)
            for r in range(16):
                o16[r, :, cols] = t_scr[j, pl.ds(r, PROJ_ROWS // 16, stride=16), :].astype(BF16)


def _project(x2, w_in_b, gmat, wpair, sbias, lng, lnb, mg, batch, seq):
    t, dm = x2.shape
    width = dm // 2
    nt = seq // PROJ_ROWS
    per_span = ATTN_SPAN // PROJ_ROWS
    const = lambda *shape: pl.BlockSpec(shape, lambda b, m: (0,) * len(shape))
    o1 = jax.ShapeDtypeStruct((batch, seq, width), BF16)
    o4 = jax.ShapeDtypeStruct((batch, seq // 512, 4, BAND_BLOCK, width), BF16)
    o16 = jax.ShapeDtypeStruct((batch, seq // ATTN_SPAN, 16, BAND_BLOCK, width), BF16)
    s1 = pl.BlockSpec((None, PROJ_ROWS, width), lambda b, m: (b, m, 0))
    s4 = pl.BlockSpec((None, None, 4, BAND_BLOCK, width), lambda b, m: (b, m, 0, 0, 0))
    s16 = pl.BlockSpec((None, None, 16, PROJ_ROWS // 16, width),
                       lambda b, m: (b, m // per_span, 0, m % per_span, 0))
    return pl.pallas_call(
        _proj_kernel,
        grid=(batch, nt),
        in_specs=[pl.BlockSpec((PROJ_ROWS, dm), lambda b, m: (b * nt + m, 0)),
                  const(*w_in_b.shape), const(*gmat.shape), const(*wpair.shape), const(*sbias.shape),
                  const(1, width), const(1, width), const(1, width)],
        out_specs=[s1] + [s1, s1, s1] + [s4, s4, s4] + [s16, s16, s16],
        out_shape=[o1] + [o1, o1, o1] + [o4, o4, o4] + [o16, o16, o16],
        scratch_shapes=[pltpu.VMEM((PROJ_ROWS, width), F32),
                        pltpu.VMEM((width // LANES, PROJ_ROWS, LANES), F32)],
        compiler_params=_params(("parallel", "parallel")),
        name="proj_sgu",
    )(x2, w_in_b, gmat, wpair, sbias, lng, lnb, mg)


def _attn_kernel(q1, k1, v1, kp1, vp1, q4, k4, v4, kp4, vp4, q16, k16, v16, kp16, vp16,
                 o_ref, acc_o, acc_m, acc_l):
    not_first = pl.program_id(1) > 0
    blk = BAND_BLOCK
    row = lax.broadcasted_iota(I32, (2 * blk, 2 * blk), 0) % blk
    col = lax.broadcasted_iota(I32, (2 * blk, 2 * blk), 1)
    band = jnp.logical_or(jnp.logical_and(col < blk, col >= row), jnp.logical_and(col >= blk, col - blk <= row))
    neg = jnp.where(not_first, 0.0, -jnp.inf).astype(F32)
    first_pen = jnp.where(col < blk, neg, 0.0)
    low = lax.broadcasted_iota(I32, (blk, LANES), 1) < HEAD_DIM
    zero = jnp.zeros((blk, LANES), BF16)

    def unit(q, kprev, kcur, vprev, vcur, maybe_first, rows, init):
        kk = jnp.concatenate([kprev, kcur], axis=0)
        vv = jnp.concatenate([vprev, vcur], axis=0)
        qq = jnp.concatenate([jnp.where(low, q, zero), jnp.where(low, zero, q)], axis=0)
        s = lax.dot_general(qq, kk, (((1,), (1,)), ((), ())), preferred_element_type=F32)
        s = jnp.where(band, s, -jnp.inf)
        if maybe_first:
            s = s + first_pen
        m2 = jnp.max(s, axis=-1, keepdims=True)
        p = jnp.exp(s - m2)
        l2 = jnp.sum(p, axis=-1, keepdims=True)
        o2 = _dot(p.astype(BF16), vv)
        o = jnp.where(low, o2[:blk], o2[blk:])
        m = jnp.where(low, m2[:blk], m2[blk:])
        l = jnp.where(low, l2[:blk], l2[blk:])
        if init:
            acc_o[rows, :] = o
            acc_m[rows, :] = m
            acc_l[rows, :] = l
        else:
            mo = acc_m[rows, :]
            mn = jnp.maximum(mo, m)
            so = jnp.exp(mo - mn)
            sn = jnp.exp(m - mn)
            acc_o[rows, :] = acc_o[rows, :] * so + o * sn
            acc_l[rows, :] = acc_l[rows, :] * so + l * sn
            acc_m[rows, :] = mn

    group = 4

    def rows1(base, j):
        return pl.ds(base + j * blk, blk)

    for j in range(group):
        cur = rows1(0, j)
        if j == 0:
            unit(q1[cur, :], kp1[...], k1[cur, :], vp1[...], v1[cur, :], True, cur, True)
        else:
            prev = rows1(0, j - 1)
            unit(q1[cur, :], k1[prev, :], k1[cur, :], v1[prev, :], v1[cur, :], False, cur, True)

    def body1(g, c):
        base = pl.multiple_of(g * (group * blk), group * blk)
        for j in range(group):
            cur = rows1(base, j)
            prev = rows1(base, j - 1)
            unit(q1[cur, :], k1[prev, :], k1[cur, :], v1[prev, :], v1[cur, :], False, cur, True)
        return c

    lax.fori_loop(1, ATTN_SPAN // (group * blk), body1, 0)

    for r in range(4):
        unit(q4[0, r], kp4[r], k4[0, r], vp4[r], v4[0, r], True, pl.ds(r, blk, stride=4), False)

    def body4(s, c):
        for r in range(4):
            unit(q4[s, r], k4[s - 1, r], k4[s, r], v4[s - 1, r], v4[s, r], False,
                 pl.ds(s * (4 * blk) + r, blk, stride=4), False)
        return c

    lax.fori_loop(1, ATTN_SPAN // (4 * blk), body4, 0)

    def body16(g, c):
        for j in range(group):
            r = g * group + j
            unit(q16[r], kp16[r], k16[r], vp16[r], v16[r], True, pl.ds(r, blk, stride=16), False)
        return c

    lax.fori_loop(0, 16 // group, body16, 0)

    o_ref[...] = (acc_o[...] / acc_l[...]).astype(BF16)


def _attention(qkv1, qkv4, qkv16, batch, seq):
    width = qkv1[0].shape[-1]
    nspan = seq // ATTN_SPAN
    blk = BAND_BLOCK
    n1 = ATTN_SPAN // blk
    n4 = ATTN_SPAN // (4 * blk)

    cur1 = pl.BlockSpec((None, ATTN_SPAN, LANES), lambda b, i, p: (b, i, p))
    prev1 = pl.BlockSpec((None, None, blk, LANES), lambda b, i, p: (b, jnp.maximum(i * n1 - 1, 0), 0, p))
    cur4 = pl.BlockSpec((None, n4, 4, blk, LANES), lambda b, i, p: (b, i, 0, 0, p))
    prev4 = pl.BlockSpec((None, None, 4, blk, LANES), lambda b, i, p: (b, jnp.maximum(i * n4 - 1, 0), 0, 0, p))
    cur16 = pl.BlockSpec((None, None, 16, blk, LANES), lambda b, i, p: (b, i, 0, 0, p))
    prev16 = pl.BlockSpec((None, None, 16, blk, LANES), lambda b, i, p: (b, jnp.maximum(i - 1, 0), 0, 0, p))

    q1, k1, v1 = qkv1
    q4, k4, v4 = qkv4
    q16, k16, v16 = qkv16
    k1b = k1.reshape(batch, seq // blk, blk, width)
    v1b = v1.reshape(batch, seq // blk, blk, width)
    return pl.pallas_call(
        _attn_kernel,
        grid=(batch, nspan, width // LANES),
        in_specs=[cur1, cur1, cur1, prev1, prev1,
                  cur4, cur4, cur4, prev4, prev4,
                  cur16, cur16, cur16, prev16, prev16],
        out_specs=pl.BlockSpec((None, ATTN_SPAN, LANES), lambda b, i, p: (b, i, p)),
        out_shape=jax.ShapeDtypeStruct((batch, seq, width), BF16),
        scratch_shapes=[pltpu.VMEM((ATTN_SPAN, LANES), F32)] * 3,
        compiler_params=_params(("parallel", "parallel", "parallel")),
        name="dilated_attn",
    )(q1, k1, v1, k1b, v1b, q4, k4, v4, k4, v4, q16, k16, v16, k16, v16)


def _outproj_kernel(x_ref, a_ref, b_ref, wo_ref, mgb_ref, g1_ref, b1_ref, wr_ref, br_ref, tri_ref,
                    h_ref, idx_ref, gate_ref, rank_ref, idxt_ref, rankt_ref, runs_ref, cnt_ref, run_scr):
    @pl.when(pl.program_id(0) == 0)
    def _():
        run_scr[...] = jnp.zeros_like(run_scr)

    width = a_ref.shape[-1]
    bf = b_ref[...].astype(F32)
    bn = (bf * lax.rsqrt(jnp.mean(bf * bf, axis=-1, keepdims=True) + LN_EPS) * mgb_ref[...]).astype(BF16)
    mixed = _dot(a_ref[...], wo_ref[0:width, :]) + _dot(bn, wo_ref[width:2 * width, :])
    z = DN_ALPHA * x_ref[...] + mixed
    mu = jnp.mean(z, axis=-1, keepdims=True)
    zc = z - mu
    var = jnp.mean(zc * zc, axis=-1, keepdims=True)
    h = zc * lax.rsqrt(var + LN_EPS) * g1_ref[...] + b1_ref[...]
    h_ref[...] = h

    logits = _dot(h.astype(BF16), wr_ref[...]) + br_ref[...]
    lane = lax.broadcasted_iota(I32, logits.shape, 1).astype(F32)
    work = logits
    vals, idxs, hots = [], [], []
    for _ in range(TOP_K):
        mv = jnp.max(work, axis=-1, keepdims=True)
        ix = jnp.min(jnp.where(work == mv, lane, float(LANES)), axis=-1, keepdims=True)
        hot = lane == ix
        work = jnp.where(hot, -jnp.inf, work)
        vals.append(mv)
        idxs.append(ix)
        hots.append(hot)
    exps = [jnp.exp(v - vals[0]) for v in vals]
    den = exps[0] + exps[1] + exps[2] + exps[3]
    gates = [e / den for e in exps]

    member = jnp.zeros(logits.shape, F32)
    for hot in hots:
        member = jnp.where(hot, 1.0, member)
    before = _dot(tri_ref[...], member.astype(BF16)) + run_scr[...]
    ranks = [jnp.sum(jnp.where(hot, before, 0.0), axis=-1, keepdims=True) for hot in hots]
    runs_ref[0] = run_scr[...]
    for j in range(1, PROJ_ROWS // ROUTE_ROWS):
        runs_ref[j] = run_scr[...] + jnp.sum(member[:j * ROUTE_ROWS], axis=0, keepdims=True)
    run_scr[...] = run_scr[...] + jnp.sum(member, axis=0, keepdims=True)
    cnt_ref[...] = run_scr[...]

    def spread(cols):
        out = jnp.zeros(logits.shape, F32)
        for k, cval in enumerate(cols):
            out = jnp.where(lane == float(k), cval, out)
        return out

    idx_all = spread(idxs)
    rank_all = spread(ranks)
    idx_ref[...] = idx_all.astype(I32)
    gate_ref[...] = spread(gates)
    rank_ref[...] = rank_all.astype(I32)
    idxt_ref[...] = idx_all.T[:OCTET].astype(I32)
    rankt_ref[...] = rank_all.T[:OCTET].astype(I32)


def _outproj(x2, a_n, b2, w_out_b, mgb, g1, b1, wr, br, tri):
    t, dm = x2.shape
    width = dm // 2
    const = lambda *shape: pl.BlockSpec(shape, lambda m: (0,) * len(shape))
    rowblk = lambda w: pl.BlockSpec((PROJ_ROWS, w), lambda m: (m, 0))
    colblk = pl.BlockSpec((OCTET, PROJ_ROWS), lambda m: (0, m))
    sub = PROJ_ROWS // ROUTE_ROWS
    return pl.pallas_call(
        _outproj_kernel,
        grid=(t // PROJ_ROWS,),
        in_specs=[rowblk(dm), rowblk(width), rowblk(width), const(dm, dm), const(1, width),
                  const(1, dm), const(1, dm), const(dm, LANES), const(1, LANES),
                  const(PROJ_ROWS, PROJ_ROWS)],
        out_specs=[rowblk(dm), rowblk(LANES), rowblk(LANES), rowblk(LANES), colblk, colblk,
                   pl.BlockSpec((None, sub, 1, LANES), lambda m: (m, 0, 0, 0)), const(1, LANES)],
        out_shape=[jax.ShapeDtypeStruct((t, dm), F32), jax.ShapeDtypeStruct((t, LANES), I32),
                   jax.ShapeDtypeStruct((t, LANES), F32), jax.ShapeDtypeStruct((t, LANES), I32),
                   jax.ShapeDtypeStruct((OCTET, t), I32), jax.ShapeDtypeStruct((OCTET, t), I32),
                   jax.ShapeDtypeStruct((t // PROJ_ROWS, sub, 1, LANES), F32),
                   jax.ShapeDtypeStruct((1, LANES), F32)],
        scratch_shapes=[pltpu.VMEM((1, LANES), F32)],
        compiler_params=_params(("arbitrary",)),
        name="outproj_router",
    )(x2, a_n, b2, w_out_b, mgb, g1, b1, wr, br, tri)


def _octet(ref, q):
    return ref.at[pl.ds(pl.multiple_of(q * OCTET, OCTET), OCTET), :]


def _for_octets(n, start):
    def body(i, c):
        start(2 * i, 0)
        start(2 * i + 1, 1)
        return c

    lax.fori_loop(0, n // 2, body, 0)

    @pl.when(n % 2 == 1)
    def _():
        start(n - 1, 0)


def _wait_octets(n, ref, sem):
    for s in (128, 64, 32, 16, 8, 4, 2, 1):
        @pl.when((n & s) != 0)
        def _():
            d = ref.at[pl.ds(0, s * OCTET), :]
            pltpu.make_async_copy(d, d, sem).wait()


def _dispatch_kernel(nq_ref, tail_ref, gq_ref, h_ref, idxt_ref, rankt_ref, tabt_ref, xs_hbm,
                     stage, zero8, sem, zsem):
    m = pl.program_id(0)
    eid = lax.broadcasted_iota(I32, (N_EXPERTS, ROUTE_ROWS), 0)
    pos = lax.broadcasted_iota(I32, (STAGE_ROWS, ROUTE_ROWS), 0).astype(F32)
    sel = None
    for k in range(TOP_K):
        hot = eid == idxt_ref[k:k + 1, :]
        lpos = (jnp.sum(jnp.where(hot, tabt_ref[...], 0.0), axis=0, keepdims=True)
                + rankt_ref[k:k + 1, :].astype(F32))
        hit = pos == lpos
        sel = hit if sel is None else jnp.logical_or(sel, hit)
    slot = m % 2
    stage[slot] = _dot(jnp.where(sel, 1.0, 0.0).astype(BF16), h_ref[...].astype(BF16))

    @pl.when(m == 0)
    def _():
        zero8[...] = jnp.zeros_like(zero8)

        def zstart(i, c):
            pltpu.make_async_copy(zero8, _octet(xs_hbm, tail_ref[0] + i), zsem).start()
            return c

        lax.fori_loop(0, tail_ref[1], zstart, 0)

    nq = nq_ref[m]
    _for_octets(nq, lambda q, pri: pltpu.make_async_copy(
        _octet(stage.at[slot], q), _octet(xs_hbm, gq_ref[0, 0, q]), sem.at[slot]).start(priority=pri))

    @pl.when(m > 0)
    def _():
        _wait_octets(nq_ref[jnp.maximum(m - 1, 0)], xs_hbm, sem.at[1 - slot])

    @pl.when(m == pl.num_programs(0) - 1)
    def _():
        _wait_octets(nq, xs_hbm, sem.at[slot])

        def zwait(i, c):
            pltpu.make_async_copy(zero8, _octet(xs_hbm, 0), zsem).wait()
            return c

        lax.fori_loop(0, tail_ref[1], zwait, 0)


def _dispatch(h, idxt, rankt, tabt, gq3, nq, tail, nrows):
    t, dm = h.shape
    grid_spec = pltpu.PrefetchScalarGridSpec(
        num_scalar_prefetch=2,
        grid=(t // ROUTE_ROWS,),
        in_specs=[pl.BlockSpec((1, 1, STAGE_OCTETS), lambda m, nq, tl: (m, 0, 0), memory_space=pltpu.SMEM),
                  pl.BlockSpec((ROUTE_ROWS, dm), lambda m, nq, tl: (m, 0)),
                  pl.BlockSpec((OCTET, ROUTE_ROWS), lambda m, nq, tl: (0, m)),
                  pl.BlockSpec((OCTET, ROUTE_ROWS), lambda m, nq, tl: (0, m)),
                  pl.BlockSpec((None, N_EXPERTS, ROUTE_ROWS), lambda m, nq, tl: (m, 0, 0))],
        out_specs=pl.BlockSpec(memory_space=pl.ANY),
        scratch_shapes=[pltpu.VMEM((2, STAGE_ROWS, dm), F32), pltpu.VMEM((OCTET, dm), F32),
                        pltpu.SemaphoreType.DMA((2,)), pltpu.SemaphoreType.DMA(())],
    )
    return pl.pallas_call(
        _dispatch_kernel,
        grid_spec=grid_spec,
        out_shape=jax.ShapeDtypeStruct((nrows, dm), F32),
        compiler_params=_params(("arbitrary",)),
        name="moe_dispatch",
    )(nq, tail, gq3, h, idxt, rankt, tabt)


def _moe_kernel(ib_ref, ie_ref, lo_ref, hi_ref, nxt_ref, slot_ref, xs_ref, bg_ref, bu_ref, bd_ref,
                wg_hbm, wu_hbm, wd_hbm, ys_ref, wf, wb, wsem):
    i = pl.program_id(0)
    prev = jnp.maximum(i - 1, 0)
    lo = lo_ref[i]
    hi = hi_ref[i]

    def weight_copies(e, s):
        return [pltpu.make_async_copy(w.at[e], wf.at[s, j], wsem.at[s])
                for j, w in enumerate((wg_hbm, wu_hbm, wd_hbm))]

    @pl.when(jnp.logical_and(lo >= 0, hi > lo))
    def _():
        @pl.when(jnp.logical_or(i == 0, ie_ref[i] != ie_ref[prev]))
        def _():
            s = slot_ref[i]

            @pl.when(i == 0)
            def _():
                for cp in weight_copies(ie_ref[i], s):
                    cp.start()

            for cp in weight_copies(ie_ref[i], s):
                cp.wait()
            for j in range(3):
                wb[j] = wf[s, j].astype(BF16)

            @pl.when(nxt_ref[i] >= 0)
            def _():
                for cp in weight_copies(nxt_ref[i], 1 - s):
                    cp.start()

        def expert(rows):
            x = xs_ref[rows, :].astype(BF16)
            g = jnp.minimum(_dot(x, wb[0]) + bg_ref[...], SWIGLU_LIMIT)
            u = jnp.clip(_dot(x, wb[1]) + bu_ref[...], -SWIGLU_LIMIT, SWIGLU_LIMIT)
            act = (u + 1.0) * (g * jax.nn.sigmoid(SWIGLU_ALPHA * g))
            return _dot(act.astype(BF16), wb[2]) + bd_ref[...]

        whole = jnp.logical_and(lo == 0, hi == MOE_ROWS)

        @pl.when(whole)
        def _():
            ys_ref[...] = expert(slice(None))

        @pl.when(jnp.logical_not(whole))
        def _():
            @pl.when(jnp.logical_or(i == 0, ib_ref[i] != ib_ref[prev]))
            def _():
                ys_ref[...] = jnp.zeros_like(ys_ref)

            for j in range(MOE_ROWS // MOE_SUB):
                rows = slice(j * MOE_SUB, (j + 1) * MOE_SUB)

                @pl.when(jnp.logical_and(lo < (j + 1) * MOE_SUB, hi > j * MOE_SUB))
                def _():
                    rid = lax.broadcasted_iota(I32, (MOE_SUB, ys_ref.shape[1]), 0) + j * MOE_SUB
                    mine = jnp.logical_and(rid >= lo, rid < hi)
                    ys_ref[rows, :] = jnp.where(mine, expert(rows), ys_ref[rows, :])

    @pl.when(lo < 0)
    def _():
        ys_ref[...] = jnp.zeros_like(ys_ref)


def _moe(xs, items, w_gate, b_gate, w_up, b_up, w_down, b_down):
    nrows = xs.shape[0]
    ne, dm, df = w_gate.shape
    assert dm == df
    nitems = items[0].shape[0]
    bspec = lambda c: pl.BlockSpec((None, 1, c), lambda i, ib, ie, *_: (ie[i], 0, 0))
    rspec = pl.BlockSpec((MOE_ROWS, dm), lambda i, ib, *_: (ib[i], 0))
    hbm = pl.BlockSpec(memory_space=pl.ANY)
    grid_spec = pltpu.PrefetchScalarGridSpec(
        num_scalar_prefetch=6,
        grid=(nitems,),
        in_specs=[rspec, bspec(df), bspec(df), bspec(dm), hbm, hbm, hbm],
        out_specs=rspec,
        scratch_shapes=[pltpu.VMEM((2, 3, dm, df), F32), pltpu.VMEM((3, dm, df), BF16),
                        pltpu.SemaphoreType.DMA((2,))],
    )
    return pl.pallas_call(
        _moe_kernel,
        grid_spec=grid_spec,
        out_shape=jax.ShapeDtypeStruct((nrows, dm), F32),
        compiler_params=_params(("arbitrary",)),
        name="moe_experts",
    )(*items, xs, b_gate.reshape(ne, 1, df), b_up.reshape(ne, 1, df), b_down.reshape(ne, 1, dm),
      w_gate, w_up, w_down)


def _work_items(counts, nrows):
    nblk = nrows // MOE_ROWS
    nitems = nblk + N_EXPERTS - 1
    ends = jnp.cumsum(counts)
    starts = ends - counts
    b0 = jnp.arange(nblk, dtype=I32)[:, None] * MOE_ROWS
    lo = jnp.maximum(starts[None, :], b0)
    hi = jnp.minimum(ends[None, :], b0 + MOE_ROWS)
    nonempty = (hi > lo).reshape(-1)
    csum = jnp.cumsum(nonempty.astype(I32))
    j = jnp.arange(nitems, dtype=I32)
    pos = jnp.sum(csum[None, :] <= j[:, None], axis=1).astype(I32)
    used = j < csum[-1]
    pos = jnp.where(used, pos, jnp.max(jnp.where(nonempty, jnp.arange(nonempty.shape[0], dtype=I32), 0)))
    ib = pos // N_EXPERTS
    ie = pos % N_EXPERTS
    ilo = jnp.where(used, lo.reshape(-1)[pos] - ib * MOE_ROWS, 0)
    ihi = jnp.where(used, hi.reshape(-1)[pos] - ib * MOE_ROWS, 0)
    spare_blk = ib + 1 + (j - csum[-1])
    fill = jnp.logical_and(jnp.logical_not(used), spare_blk < nblk)
    ib = jnp.where(used, ib, jnp.minimum(spare_blk, nblk - 1))
    ilo = jnp.where(fill, -1, ilo)
    ordinal = jnp.cumsum(jnp.concatenate([jnp.zeros((1,), I32), (ie[1:] != ie[:-1]).astype(I32)]))
    first_next = jnp.sum(ordinal[None, :] <= ordinal[:, None], axis=1)
    has_next = first_next < nitems
    nxt = jnp.where(has_next, ie[jnp.minimum(first_next, nitems - 1)], -1)
    return (ib.astype(I32), ie.astype(I32), ilo.astype(I32), ihi.astype(I32), nxt.astype(I32),
            (ordinal % 2).astype(I32))


def _combine_kernel(nq_ref, gq_cur, gq_nxt, h_ref, idx_ref, rank_ref, gate_ref, tab_ref, g2_ref, b2_ref, ys_hbm,
                    o_ref, stage, sem):
    m = pl.program_id(0)
    last = pl.num_programs(0) - 1
    slot = m % 2

    def start_fetch(gq_ref, n, s):
        _for_octets(n, lambda q, pri: pltpu.make_async_copy(
            _octet(ys_hbm, gq_ref[0, 0, q]), _octet(stage.at[s], q), sem.at[s]).start(priority=pri))

    @pl.when(m == 0)
    def _():
        stage[...] = jnp.zeros_like(stage)
        start_fetch(gq_cur, nq_ref[0], 0)

    @pl.when(m < last)
    def _():
        start_fetch(gq_nxt, nq_ref[jnp.minimum(m + 1, last)], 1 - slot)

    _wait_octets(nq_ref[m], stage.at[slot], sem.at[slot])

    lane_e = lax.broadcasted_iota(I32, idx_ref.shape, 1)
    lane_p = lax.broadcasted_iota(I32, (ROUTE_ROWS, STAGE_ROWS), 1).astype(F32)
    idx = idx_ref[...]
    rank = rank_ref[...].astype(F32)
    gates = gate_ref[...]
    wsel = jnp.zeros((ROUTE_ROWS, STAGE_ROWS), F32)
    for k in range(TOP_K):
        hot = lane_e == idx[:, k:k + 1]
        lpos = jnp.sum(jnp.where(hot, tab_ref[...], 0.0), axis=-1, keepdims=True) + rank[:, k:k + 1]
        wsel = jnp.where(lane_p == lpos, gates[:, k:k + 1], wsel)
    y = _dot(wsel.astype(BF16), stage[slot].astype(BF16))
    z = DN_ALPHA * h_ref[...] + y
    mu = jnp.mean(z, axis=-1, keepdims=True)
    zc = z - mu
    var = jnp.mean(zc * zc, axis=-1, keepdims=True)
    o_ref[...] = zc * lax.rsqrt(var + LN_EPS) * g2_ref[...] + b2_ref[...]


def _combine(h, ys, idx, rank, gates, tab, gq3, nq, g2, b2):
    t, dm = h.shape
    nt = t // ROUTE_ROWS
    rowblk = lambda w: pl.BlockSpec((ROUTE_ROWS, w), lambda m, nq: (m, 0))
    const = pl.BlockSpec((1, dm), lambda m, nq: (0, 0))
    qspec = lambda f: pl.BlockSpec((1, 1, STAGE_OCTETS), lambda m, nq: (f(m), 0, 0), memory_space=pltpu.SMEM)
    grid_spec = pltpu.PrefetchScalarGridSpec(
        num_scalar_prefetch=1,
        grid=(nt,),
        in_specs=[qspec(lambda m: m), qspec(lambda m: jnp.minimum(m + 1, nt - 1)),
                  rowblk(dm), rowblk(LANES), rowblk(LANES), rowblk(LANES),
                  pl.BlockSpec((None, 1, LANES), lambda m, nq: (m, 0, 0)), const, const,
                  pl.BlockSpec(memory_space=pl.ANY)],
        out_specs=rowblk(dm),
        scratch_shapes=[pltpu.VMEM((2, STAGE_ROWS, dm), F32), pltpu.SemaphoreType.DMA((2,))],
    )
    return pl.pallas_call(
        _combine_kernel,
        grid_spec=grid_spec,
        out_shape=jax.ShapeDtypeStruct((t, dm), F32),
        compiler_params=_params(("arbitrary",)),
        name="combine_ln",
    )(nq, gq3, gq3, h, idx, rank, gates, tab, g2, b2, ys)


def _route_tables(runs, total, nrows):
    nt = runs.shape[0]
    nxt = jnp.concatenate([runs[1:], total[None, :]], axis=0)
    c8 = (nxt - runs + OCTET - 1) // OCTET
    q_end = jnp.cumsum(c8, axis=1)
    q0 = q_end - c8
    nq = q_end[:, -1]
    per_expert = jnp.sum(c8, axis=0)
    start8 = jnp.cumsum(per_expert) - per_expert
    g8 = start8[None, :] + jnp.cumsum(c8, axis=0) - c8
    tab = OCTET * q0 - runs
    q = jnp.arange(STAGE_OCTETS, dtype=I32)
    e_of_q = jnp.minimum(jnp.sum(q[None, :, None] >= q_end[:, None, :], axis=2), N_EXPERTS - 1)
    pick = e_of_q[:, :, None] == jnp.arange(N_EXPERTS, dtype=I32)[None, None, :]
    gq = jnp.sum(jnp.where(pick, (g8 - q0)[:, None, :], 0), axis=2) + q[None, :]
    gq = jnp.where(q[None, :] < nq[:, None], gq, 0).astype(I32)
    tot8 = jnp.sum(per_expert)
    tail = jnp.stack([tot8, nrows // OCTET - tot8]).astype(I32)
    return tab, gq.reshape(nt, 1, STAGE_OCTETS), nq.astype(I32), tail, (per_expert * OCTET).astype(I32)


def _layer(x, w_in, sgu_w, sgu_b, sgu_ln_g, sgu_ln_b, mix_norm_g, w_out, ln1_g, ln1_b,
           w_router, b_router, w_gate, b_gate, w_up, b_up, w_down, b_down, ln2_g, ln2_b):
    batch, seq, dm = x.shape
    width = dm // 2
    t = batch * seq
    assert seq % ATTN_SPAN == 0 and dm % (2 * LANES) == 0 and w_router.shape[-1] == N_EXPERTS
    x2 = x.reshape(t, dm)

    wc = jnp.tril(sgu_w)
    wpair = jnp.concatenate([wc[0::2], wc[1::2]], axis=-1).astype(BF16)
    sbias = jnp.repeat(sgu_b.T, HEAD_DIM, axis=1)
    grp = jnp.arange(width) // HEAD_DIM
    gmat = jnp.where(grp[:, None] == grp[None, :], 1.0 / HEAD_DIM, 0.0).astype(BF16)
    row = lambda v: v.reshape(1, -1)

    a_n, q1, k1, v1, q4, k4, v4, q16, k16, v16 = _project(
        x2, w_in.astype(BF16), gmat, wpair, sbias, row(sgu_ln_g), row(sgu_ln_b),
        row(mix_norm_g[:width]), batch, seq)
    b = _attention((q1, k1, v1), (q4, k4, v4), (q16, k16, v16), batch, seq)

    wr = jnp.pad(w_router, ((0, 0), (0, LANES - N_EXPERTS))).astype(BF16)
    br = jnp.concatenate([b_router.astype(F32), jnp.full((LANES - N_EXPERTS,), -1e30, F32)]).reshape(1, LANES)
    ti = jnp.arange(PROJ_ROWS)
    tri = (ti[None, :] < ti[:, None]).astype(BF16)
    h, idx, gates, rank, idxt, rankt, runs, cnt = _outproj(
        x2, a_n.reshape(t, width), b.reshape(t, width), w_out.astype(BF16),
        row(mix_norm_g[width:]), row(ln1_g), row(ln1_b), wr, br, tri)

    nt = t // ROUTE_ROWS
    nrows = (t * TOP_K + nt * N_EXPERTS * (OCTET - 1) + MOE_ROWS - 1) // MOE_ROWS * MOE_ROWS
    runs = runs.reshape(nt, LANES)[:, :N_EXPERTS].astype(I32)
    total = cnt[0, :N_EXPERTS].astype(I32)
    tab, gq3, nq, tail, rows_e = _route_tables(runs, total, nrows)
    tabf = tab.astype(F32)
    tab_lane = jnp.pad(tabf, ((0, 0), (0, LANES - N_EXPERTS))).reshape(nt, 1, LANES)
    tab_sub = jnp.broadcast_to(tabf[:, :, None], (nt, N_EXPERTS, ROUTE_ROWS))

    xs = _dispatch(h, idxt, rankt, tab_sub, gq3, nq, tail, nrows)
    ys = _moe(xs, _work_items(rows_e, nrows), w_gate, b_gate, w_up, b_up, w_down, b_down)
    out = _combine(h, ys, idx, rank, gates, tab_lane, gq3, nq, row(ln2_g), row(ln2_b))
    return out.reshape(batch, seq, dm)


def kernel(x, w_in, sgu_w, sgu_b, sgu_ln_g, sgu_ln_b, mix_norm_g, w_out, ln1_g, ln1_b, w_router, b_router,
           w_gate, b_gate, w_up, b_up, w_down, b_down, ln2_g, ln2_b):
    assert w_in.shape[0] == DEPTH
    return _layer(x, w_in[0], sgu_w[0], sgu_b[0], sgu_ln_g[0], sgu_ln_b[0], mix_norm_g[0], w_out[0],
                  ln1_g[0], ln1_b[0], w_router[0], b_router[0], w_gate[0], b_gate[0], w_up[0], b_up[0],
                  w_down[0], b_down[0], ln2_g[0], ln2_b[0])
```

```python
import jax
import jax.numpy as jnp
from jax import lax
from jax.experimental import pallas as pl
from jax.experimental.pallas import tpu as pltpu

F32 = jnp.float32
BF16 = jnp.bfloat16
I32 = jnp.int32

LN_EPS = 1e-5
HEAD_DIM = 64
SGU_CHUNK = 128
BAND_BLOCK = 128
DILATIONS = (1, 4, 16)
ATTN_SPAN = BAND_BLOCK * DILATIONS[-1]
N_EXPERTS = 32
TOP_K = 4
SWIGLU_ALPHA = 1.702
SWIGLU_LIMIT = 7.0
DEPTH = 1
DN_ALPHA = (2 * DEPTH) ** 0.25
LOG2E = 1.4426950408889634

LANES = 128
PROJ_ROWS = 512
MOE_ROWS = 512
MOE_SUB = 256
ROUTE_ROWS = 256
OCTET = 8
STAGE_ROWS = ROUTE_ROWS * TOP_K + N_EXPERTS * OCTET
STAGE_OCTETS = STAGE_ROWS // OCTET
VMEM_LIMIT = 56 * 1024 * 1024


def _params(sem):
    return pltpu.CompilerParams(dimension_semantics=sem, vmem_limit_bytes=VMEM_LIMIT)


def _dot(a, b):
    return jnp.dot(a, b, preferred_element_type=F32)


def _proj_kernel(x_ref, w_ref, gmat_ref, wpair_ref, sbias_ref, lng_ref, lnb_ref, mg_ref,
                 a_ref, q1_ref, k1_ref, v1_ref, q4_ref, k4_ref, v4_ref, q16_ref, k16_ref, v16_ref,
                 a_scr, t_scr):
    width = a_ref.shape[-1]
    xb = x_ref[...].astype(BF16)

    def proj(c):
        return _dot(xb, w_ref[:, c * width:(c + 1) * width])

    u = jax.nn.gelu(proj(0))
    v = jax.nn.gelu(proj(1))
    mean = _dot(v.astype(BF16), gmat_ref[...])
    d = v - mean
    var = _dot((d * d).astype(BF16), gmat_ref[...])
    vn = (d * lax.rsqrt(var + LN_EPS) * lng_ref[...] + lnb_ref[...]).astype(BF16)

    lane = lax.broadcasted_iota(I32, (SGU_CHUNK, LANES), 1)
    low = lane < HEAD_DIM
    zero = jnp.zeros((SGU_CHUNK, LANES), BF16)
    for c in range(PROJ_ROWS // SGU_CHUNK):
        rows = slice(c * SGU_CHUNK, (c + 1) * SGU_CHUNK)
        for j in range(width // LANES):
            cols = slice(j * LANES, (j + 1) * LANES)
            vp = vn[rows, cols]
            rhs = jnp.concatenate([jnp.where(low, vp, zero), jnp.where(low, zero, vp)], axis=0)
            gate = _dot(wpair_ref[j], rhs) + sbias_ref[:, cols]
            a_scr[rows, cols] = u[rows, cols] * gate
    a = a_scr[...]
    ms = jnp.mean(a * a, axis=-1, keepdims=True)
    a_ref[...] = (a * lax.rsqrt(ms + LN_EPS) * mg_ref[...]).astype(BF16)

    outs = ((q1_ref, q4_ref, q16_ref), (k1_ref, k4_ref, k16_ref), (v1_ref, v4_ref, v16_ref))
    for c, (o1, o4, o16) in enumerate(outs):
        t = proj(2 + c)
        if c == 0:
            t = t * (HEAD_DIM ** -0.5 * LOG2E)
        o1[...] = t.astype(BF16)
        for j in range(width // LANES):
            cols = slice(j * LANES, (j + 1) * LANES)
            t_scr[j] = t[:, cols]
            for r in range(4):
                o4[r, :, cols] = t_scr[j, pl.ds(r, PROJ_ROWS // 4, stride=4), :].astype(BF16)
            for r in range(16):
                o16[r, :, cols] = t_scr[j, pl.ds(r, PROJ_ROWS // 16, stride=16), :].astype(BF16)


def _project(x2, w_in_b, gmat, wpair, sbias, lng, lnb, mg, batch, seq):
    t, dm = x2.shape
    width = dm // 2
    nt = seq // PROJ_ROWS
    per_span = ATTN_SPAN // PROJ_ROWS
    const = lambda *shape: pl.BlockSpec(shape, lambda b, m: (0,) * len(shape))
    o1 = jax.ShapeDtypeStruct((batch, seq, width), BF16)
    o4 = jax.ShapeDtypeStruct((batch, seq // 512, 4, BAND_BLOCK, width), BF16)
    o16 = jax.ShapeDtypeStruct((batch, seq // ATTN_SPAN, 16, BAND_BLOCK, width), BF16)
    s1 = pl.BlockSpec((None, PROJ_ROWS, width), lambda b, m: (b, m, 0))
    s4 = pl.BlockSpec((None, None, 4, BAND_BLOCK, width), lambda b, m: (b, m, 0, 0, 0))
    s16 = pl.BlockSpec((None, None, 16, PROJ_ROWS // 16, width),
                       lambda b, m: (b, m // per_span, 0, m % per_span, 0))
    return pl.pallas_call(
        _proj_kernel,
        grid=(batch, nt),
        in_specs=[pl.BlockSpec((PROJ_ROWS, dm), lambda b, m: (b * nt + m, 0)),
                  const(*w_in_b.shape), const(*gmat.shape), const(*wpair.shape), const(*sbias.shape),
                  const(1, width), const(1, width), const(1, width)],
        out_specs=[s1] + [s1, s1, s1] + [s4, s4, s4] + [s16, s16, s16],
        out_shape=[o1] + [o1, o1, o1] + [o4, o4, o4] + [o16, o16, o16],
        scratch_shapes=[pltpu.VMEM((PROJ_ROWS, width), F32),
                        pltpu.VMEM((width // LANES, PROJ_ROWS, LANES), F32)],
        compiler_params=_params(("parallel", "parallel")),
        name="proj_sgu",
    )(x2, w_in_b, gmat, wpair, sbias, lng, lnb, mg)


def _attn_kernel(q1, k1, v1, kp1, vp1, q4, k4, v4, kp4, vp4, q16, k16, v16, kp16, vp16,
                 o_ref, acc_o, acc_m, acc_l):
    not_first = pl.program_id(1) > 0
    blk = BAND_BLOCK
    row = lax.broadcasted_iota(I32, (2 * blk, 2 * blk), 0) % blk
    col = lax.broadcasted_iota(I32, (2 * blk, 2 * blk), 1)
    band = jnp.logical_or(jnp.logical_and(col < blk, col >= row), jnp.logical_and(col >= blk, col - blk <= row))
    neg = jnp.where(not_first, 0.0, -jnp.inf).astype(F32)
    first_pen = jnp.where(col < blk, neg, 0.0)
    low = lax.broadcasted_iota(I32, (blk, LANES), 1) < HEAD_DIM
    zero = jnp.zeros((blk, LANES), BF16)
    ones = jnp.ones((2 * blk, LANES), BF16)

    def unit(q, kprev, kcur, vprev, vcur, maybe_first, rows, init):
        kk = jnp.concatenate([kprev, kcur], axis=0)
        vv = jnp.concatenate([vprev, vcur], axis=0)
        qq = jnp.concatenate([jnp.where(low, q, zero), jnp.where(low, zero, q)], axis=0)
        s = lax.dot_general(qq, kk, (((1,), (1,)), ((), ())), preferred_element_type=F32)
        s = jnp.where(band, s, -jnp.inf)
        if maybe_first:
            s = s + first_pen
        m2 = jnp.max(s, axis=-1, keepdims=True)
        p = jnp.exp2(s - m2)
        o2 = _dot(p.astype(BF16), jnp.concatenate([vv, ones], axis=1))
        o = jnp.where(low, o2[:blk, :LANES], o2[blk:, :LANES])
        m = jnp.where(low, m2[:blk], m2[blk:])
        l = jnp.where(low, o2[:blk, LANES:], o2[blk:, LANES:])
        if init:
            acc_o[rows, :] = o
            acc_m[rows, :] = m
            acc_l[rows, :] = l
        else:
            mo = acc_m[rows, :]
            mn = jnp.maximum(mo, m)
            so = jnp.exp2(mo - mn)
            sn = jnp.exp2(m - mn)
            acc_o[rows, :] = acc_o[rows, :] * so + o * sn
            acc_l[rows, :] = acc_l[rows, :] * so + l * sn
            acc_m[rows, :] = mn

    group = 4

    def rows1(base, j):
        return pl.ds(base + j * blk, blk)

    for j in range(group):
        cur = rows1(0, j)
        if j == 0:
            unit(q1[cur, :], kp1[...], k1[cur, :], vp1[...], v1[cur, :], True, cur, True)
        else:
            prev = rows1(0, j - 1)
            unit(q1[cur, :], k1[prev, :], k1[cur, :], v1[prev, :], v1[cur, :], False, cur, True)

    def body1(g, c):
        base = pl.multiple_of(g * (group * blk), group * blk)
        for j in range(group):
            cur = rows1(base, j)
            prev = rows1(base, j - 1)
            unit(q1[cur, :], k1[prev, :], k1[cur, :], v1[prev, :], v1[cur, :], False, cur, True)
        return c

    lax.fori_loop(1, ATTN_SPAN // (group * blk), body1, 0)

    for r in range(4):
        unit(q4[0, r], kp4[r], k4[0, r], vp4[r], v4[0, r], True, pl.ds(r, blk, stride=4), False)

    def body4(s, c):
        for r in range(4):
            unit(q4[s, r], k4[s - 1, r], k4[s, r], v4[s - 1, r], v4[s, r], False,
                 pl.ds(s * (4 * blk) + r, blk, stride=4), False)
        return c

    lax.fori_loop(1, ATTN_SPAN // (4 * blk), body4, 0)

    group16 = 8

    def body16(g, c):
        for j in range(group16):
            r = g * group16 + j
            unit(q16[r], kp16[r], k16[r], vp16[r], v16[r], True, pl.ds(r, blk, stride=16), False)
        return c

    lax.fori_loop(0, 16 // group16, body16, 0)

    o_ref[...] = (acc_o[...] / acc_l[...]).astype(BF16)


def _attention(qkv1, qkv4, qkv16, batch, seq):
    width = qkv1[0].shape[-1]
    nspan = seq // ATTN_SPAN
    blk = BAND_BLOCK
    n1 = ATTN_SPAN // blk
    n4 = ATTN_SPAN // (4 * blk)

    cur1 = pl.BlockSpec((None, ATTN_SPAN, LANES), lambda b, i, p: (b, i, p))
    prev1 = pl.BlockSpec((None, None, blk, LANES), lambda b, i, p: (b, jnp.maximum(i * n1 - 1, 0), 0, p))
    cur4 = pl.BlockSpec((None, n4, 4, blk, LANES), lambda b, i, p: (b, i, 0, 0, p))
    prev4 = pl.BlockSpec((None, None, 4, blk, LANES), lambda b, i, p: (b, jnp.maximum(i * n4 - 1, 0), 0, 0, p))
    cur16 = pl.BlockSpec((None, None, 16, blk, LANES), lambda b, i, p: (b, i, 0, 0, p))
    prev16 = pl.BlockSpec((None, None, 16, blk, LANES), lambda b, i, p: (b, jnp.maximum(i - 1, 0), 0, 0, p))

    q1, k1, v1 = qkv1
    q4, k4, v4 = qkv4
    q16, k16, v16 = qkv16
    k1b = k1.reshape(batch, seq // blk, blk, width)
    v1b = v1.reshape(batch, seq // blk, blk, width)
    return pl.pallas_call(
        _attn_kernel,
        grid=(batch, nspan, width // LANES),
        in_specs=[cur1, cur1, cur1, prev1, prev1,
                  cur4, cur4, cur4, prev4, prev4,
                  cur16, cur16, cur16, prev16, prev16],
        out_specs=pl.BlockSpec((None, ATTN_SPAN, LANES), lambda b, i, p: (b, i, p)),
        out_shape=jax.ShapeDtypeStruct((batch, seq, width), BF16),
        scratch_shapes=[pltpu.VMEM((ATTN_SPAN, LANES), F32)] * 3,
        compiler_params=_params(("parallel", "parallel", "parallel")),
        name="dilated_attn",
    )(q1, k1, v1, k1b, v1b, q4, k4, v4, k4, v4, q16, k16, v16, k16, v16)


def _outproj_kernel(x_ref, a_ref, b_ref, wo_ref, mgb_ref, g1_ref, b1_ref, wr_ref, br_ref, tri_ref,
                    h_ref, idx_ref, gate_ref, rank_ref, idxt_ref, rankt_ref, runs_ref, cnt_ref, run_scr):
    @pl.when(pl.program_id(0) == 0)
    def _():
        run_scr[...] = jnp.zeros_like(run_scr)

    width = a_ref.shape[-1]
    bf = b_ref[...].astype(F32)
    bn = (bf * lax.rsqrt(jnp.mean(bf * bf, axis=-1, keepdims=True) + LN_EPS) * mgb_ref[...]).astype(BF16)
    mixed = _dot(a_ref[...], wo_ref[0:width, :]) + _dot(bn, wo_ref[width:2 * width, :])
    z = DN_ALPHA * x_ref[...] + mixed
    mu = jnp.mean(z, axis=-1, keepdims=True)
    zc = z - mu
    var = jnp.mean(zc * zc, axis=-1, keepdims=True)
    h = zc * lax.rsqrt(var + LN_EPS) * g1_ref[...] + b1_ref[...]
    h_ref[...] = h

    logits = _dot(h.astype(BF16), wr_ref[...]) + br_ref[...]
    lane = lax.broadcasted_iota(I32, logits.shape, 1).astype(F32)
    work = logits
    vals, idxs, hots = [], [], []
    for _ in range(TOP_K):
        mv = jnp.max(work, axis=-1, keepdims=True)
        ix = jnp.min(jnp.where(work == mv, lane, float(LANES)), axis=-1, keepdims=True)
        hot = lane == ix
        work = jnp.where(hot, -jnp.inf, work)
        vals.append(mv)
        idxs.append(ix)
        hots.append(hot)
    exps = [jnp.exp(v - vals[0]) for v in vals]
    den = exps[0] + exps[1] + exps[2] + exps[3]
    gates = [e / den for e in exps]

    member = jnp.zeros(logits.shape, F32)
    for hot in hots:
        member = jnp.where(hot, 1.0, member)
    before = _dot(tri_ref[...], member.astype(BF16)) + run_scr[...]
    ranks = [jnp.sum(jnp.where(hot, before, 0.0), axis=-1, keepdims=True) for hot in hots]
    runs_ref[0] = run_scr[...]
    for j in range(1, PROJ_ROWS // ROUTE_ROWS):
        runs_ref[j] = run_scr[...] + jnp.sum(member[:j * ROUTE_ROWS], axis=0, keepdims=True)
    run_scr[...] = run_scr[...] + jnp.sum(member, axis=0, keepdims=True)
    cnt_ref[...] = run_scr[...]

    def spread(cols):
        out = jnp.zeros(logits.shape, F32)
        for k, cval in enumerate(cols):
            out = jnp.where(lane == float(k), cval, out)
        return out

    idx_all = spread(idxs)
    rank_all = spread(ranks)
    idx_ref[...] = idx_all.astype(I32)
    gate_ref[...] = spread(gates)
    rank_ref[...] = rank_all.astype(I32)
    idxt_ref[...] = idx_all.T[:OCTET].astype(I32)
    rankt_ref[...] = rank_all.T[:OCTET].astype(I32)


def _outproj(x2, a_n, b2, w_out_b, mgb, g1, b1, wr, br, tri):
    t, dm = x2.shape
    width = dm // 2
    const = lambda *shape: pl.BlockSpec(shape, lambda m: (0,) * len(shape))
    rowblk = lambda w: pl.BlockSpec((PROJ_ROWS, w), lambda m: (m, 0))
    colblk = pl.BlockSpec((OCTET, PROJ_ROWS), lambda m: (0, m))
    sub = PROJ_ROWS // ROUTE_ROWS
    return pl.pallas_call(
        _outproj_kernel,
        grid=(t // PROJ_ROWS,),
        in_specs=[rowblk(dm), rowblk(width), rowblk(width), const(dm, dm), const(1, width),
                  const(1, dm), const(1, dm), const(dm, LANES), const(1, LANES),
                  const(PROJ_ROWS, PROJ_ROWS)],
        out_specs=[rowblk(dm), rowblk(LANES), rowblk(LANES), rowblk(LANES), colblk, colblk,
                   pl.BlockSpec((None, sub, 1, LANES), lambda m: (m, 0, 0, 0)), const(1, LANES)],
        out_shape=[jax.ShapeDtypeStruct((t, dm), F32), jax.ShapeDtypeStruct((t, LANES), I32),
                   jax.ShapeDtypeStruct((t, LANES), F32), jax.ShapeDtypeStruct((t, LANES), I32),
                   jax.ShapeDtypeStruct((OCTET, t), I32), jax.ShapeDtypeStruct((OCTET, t), I32),
                   jax.ShapeDtypeStruct((t // PROJ_ROWS, sub, 1, LANES), F32),
                   jax.ShapeDtypeStruct((1, LANES), F32)],
        scratch_shapes=[pltpu.VMEM((1, LANES), F32)],
        compiler_params=_params(("arbitrary",)),
        name="outproj_router",
    )(x2, a_n, b2, w_out_b, mgb, g1, b1, wr, br, tri)


def _octet(ref, q):
    return ref.at[pl.ds(pl.multiple_of(q * OCTET, OCTET), OCTET), :]


def _for_octets(n, start):
    def body(i, c):
        start(2 * i, 0)
        start(2 * i + 1, 1)
        return c

    lax.fori_loop(0, n // 2, body, 0)

    @pl.when(n % 2 == 1)
    def _():
        start(n - 1, 0)


def _wait_octets(n, ref, sem):
    for s in (128, 64, 32, 16, 8, 4, 2, 1):
        @pl.when((n & s) != 0)
        def _():
            d = ref.at[pl.ds(0, s * OCTET), :]
            pltpu.make_async_copy(d, d, sem).wait()


def _dispatch_kernel(nq_ref, tail_ref, gq_ref, h_ref, idxt_ref, rankt_ref, tabt_ref, xs_hbm,
                     stage, zero8, sem, zsem):
    m = pl.program_id(0)
    eid = lax.broadcasted_iota(I32, (N_EXPERTS, ROUTE_ROWS), 0)
    pos = lax.broadcasted_iota(I32, (STAGE_ROWS, ROUTE_ROWS), 0).astype(F32)
    sel = None
    for k in range(TOP_K):
        hot = eid == idxt_ref[k:k + 1, :]
        lpos = (jnp.sum(jnp.where(hot, tabt_ref[...], 0.0), axis=0, keepdims=True)
                + rankt_ref[k:k + 1, :].astype(F32))
        hit = pos == lpos
        sel = hit if sel is None else jnp.logical_or(sel, hit)
    slot = m % 2
    stage[slot] = _dot(jnp.where(sel, 1.0, 0.0).astype(BF16), h_ref[...].astype(BF16))

    @pl.when(m == 0)
    def _():
        zero8[...] = jnp.zeros_like(zero8)

        def zstart(i, c):
            pltpu.make_async_copy(zero8, _octet(xs_hbm, tail_ref[0] + i), zsem).start()
            return c

        lax.fori_loop(0, tail_ref[1], zstart, 0)

    nq = nq_ref[m]
    _for_octets(nq, lambda q, pri: pltpu.make_async_copy(
        _octet(stage.at[slot], q), _octet(xs_hbm, gq_ref[0, 0, q]), sem.at[slot]).start(priority=pri))

    @pl.when(m > 0)
    def _():
        _wait_octets(nq_ref[jnp.maximum(m - 1, 0)], xs_hbm, sem.at[1 - slot])

    @pl.when(m == pl.num_programs(0) - 1)
    def _():
        _wait_octets(nq, xs_hbm, sem.at[slot])

        def zwait(i, c):
            pltpu.make_async_copy(zero8, _octet(xs_hbm, 0), zsem).wait()
            return c

        lax.fori_loop(0, tail_ref[1], zwait, 0)


def _dispatch(h, idxt, rankt, tabt, gq3, nq, tail, nrows):
    t, dm = h.shape
    grid_spec = pltpu.PrefetchScalarGridSpec(
        num_scalar_prefetch=2,
        grid=(t // ROUTE_ROWS,),
        in_specs=[pl.BlockSpec((1, 1, STAGE_OCTETS), lambda m, nq, tl: (m, 0, 0), memory_space=pltpu.SMEM),
                  pl.BlockSpec((ROUTE_ROWS, dm), lambda m, nq, tl: (m, 0)),
                  pl.BlockSpec((OCTET, ROUTE_ROWS), lambda m, nq, tl: (0, m)),
                  pl.BlockSpec((OCTET, ROUTE_ROWS), lambda m, nq, tl: (0, m)),
                  pl.BlockSpec((None, N_EXPERTS, ROUTE_ROWS), lambda m, nq, tl: (m, 0, 0))],
        out_specs=pl.BlockSpec(memory_space=pl.ANY),
        scratch_shapes=[pltpu.VMEM((2, STAGE_ROWS, dm), F32), pltpu.VMEM((OCTET, dm), F32),
                        pltpu.SemaphoreType.DMA((2,)), pltpu.SemaphoreType.DMA(())],
    )
    return pl.pallas_call(
        _dispatch_kernel,
        grid_spec=grid_spec,
        out_shape=jax.ShapeDtypeStruct((nrows, dm), F32),
        compiler_params=_params(("arbitrary",)),
        name="moe_dispatch",
    )(nq, tail, gq3, h, idxt, rankt, tabt)


def _moe_kernel(ib_ref, ie_ref, lo_ref, hi_ref, nxt_ref, slot_ref, xs_ref, bg_ref, bu_ref, bd_ref,
                wg_hbm, wu_hbm, wd_hbm, ys_ref, wf, wb, wsem):
    i = pl.program_id(0)
    prev = jnp.maximum(i - 1, 0)
    lo = lo_ref[i]
    hi = hi_ref[i]

    def weight_copies(e, s):
        return [pltpu.make_async_copy(w.at[e], wf.at[s, j], wsem.at[s])
                for j, w in enumerate((wg_hbm, wu_hbm, wd_hbm))]

    @pl.when(jnp.logical_and(lo >= 0, hi > lo))
    def _():
        @pl.when(jnp.logical_or(i == 0, ie_ref[i] != ie_ref[prev]))
        def _():
            s = slot_ref[i]

            @pl.when(i == 0)
            def _():
                for cp in weight_copies(ie_ref[i], s):
                    cp.start()

            for cp in weight_copies(ie_ref[i], s):
                cp.wait()
            for j in range(3):
                wb[j] = wf[s, j].astype(BF16)

            @pl.when(nxt_ref[i] >= 0)
            def _():
                for cp in weight_copies(nxt_ref[i], 1 - s):
                    cp.start()

        def expert(rows):
            x = xs_ref[rows, :].astype(BF16)
            g = jnp.minimum(_dot(x, wb[0]) + bg_ref[...], SWIGLU_LIMIT)
            u = jnp.clip(_dot(x, wb[1]) + bu_ref[...], -SWIGLU_LIMIT, SWIGLU_LIMIT)
            act = (u + 1.0) * (g * jax.nn.sigmoid(SWIGLU_ALPHA * g))
            return _dot(act.astype(BF16), wb[2]) + bd_ref[...]

        whole = jnp.logical_and(lo == 0, hi == MOE_ROWS)

        @pl.when(whole)
        def _():
            ys_ref[...] = expert(slice(None))

        @pl.when(jnp.logical_not(whole))
        def _():
            @pl.when(jnp.logical_or(i == 0, ib_ref[i] != ib_ref[prev]))
            def _():
                ys_ref[...] = jnp.zeros_like(ys_ref)

            for j in range(MOE_ROWS // MOE_SUB):
                rows = slice(j * MOE_SUB, (j + 1) * MOE_SUB)

                @pl.when(jnp.logical_and(lo < (j + 1) * MOE_SUB, hi > j * MOE_SUB))
                def _():
                    rid = lax.broadcasted_iota(I32, (MOE_SUB, ys_ref.shape[1]), 0) + j * MOE_SUB
                    mine = jnp.logical_and(rid >= lo, rid < hi)
                    ys_ref[rows, :] = jnp.where(mine, expert(rows), ys_ref[rows, :])

    @pl.when(lo < 0)
    def _():
        ys_ref[...] = jnp.zeros_like(ys_ref)


def _moe(xs, items, w_gate, b_gate, w_up, b_up, w_down, b_down):
    nrows = xs.shape[0]
    ne, dm, df = w_gate.shape
    assert dm == df
    nitems = items[0].shape[0]
    bspec = lambda c: pl.BlockSpec((None, 1, c), lambda i, ib, ie, *_: (ie[i], 0, 0))
    rspec = pl.BlockSpec((MOE_ROWS, dm), lambda i, ib, *_: (ib[i], 0))
    hbm = pl.BlockSpec(memory_space=pl.ANY)
    grid_spec = pltpu.PrefetchScalarGridSpec(
        num_scalar_prefetch=6,
        grid=(nitems,),
        in_specs=[rspec, bspec(df), bspec(df), bspec(dm), hbm, hbm, hbm],
        out_specs=rspec,
        scratch_shapes=[pltpu.VMEM((2, 3, dm, df), F32), pltpu.VMEM((3, dm, df), BF16),
                        pltpu.SemaphoreType.DMA((2,))],
    )
    return pl.pallas_call(
        _moe_kernel,
        grid_spec=grid_spec,
        out_shape=jax.ShapeDtypeStruct((nrows, dm), F32),
        compiler_params=_params(("arbitrary",)),
        name="moe_experts",
    )(*items, xs, b_gate.reshape(ne, 1, df), b_up.reshape(ne, 1, df), b_down.reshape(ne, 1, dm),
      w_gate, w_up, w_down)


def _work_items(counts, nrows):
    nblk = nrows // MOE_ROWS
    nitems = nblk + N_EXPERTS - 1
    ends = jnp.cumsum(counts)
    starts = ends - counts
    b0 = jnp.arange(nblk, dtype=I32)[:, None] * MOE_ROWS
    lo = jnp.maximum(starts[None, :], b0)
    hi = jnp.minimum(ends[None, :], b0 + MOE_ROWS)
    nonempty = (hi > lo).reshape(-1)
    csum = jnp.cumsum(nonempty.astype(I32))
    j = jnp.arange(nitems, dtype=I32)
    pos = jnp.sum(csum[None, :] <= j[:, None], axis=1).astype(I32)
    used = j < csum[-1]
    pos = jnp.where(used, pos, jnp.max(jnp.where(nonempty, jnp.arange(nonempty.shape[0], dtype=I32), 0)))
    ib = pos // N_EXPERTS
    ie = pos % N_EXPERTS
    ilo = jnp.where(used, lo.reshape(-1)[pos] - ib * MOE_ROWS, 0)
    ihi = jnp.where(used, hi.reshape(-1)[pos] - ib * MOE_ROWS, 0)
    spare_blk = ib + 1 + (j - csum[-1])
    fill = jnp.logical_and(jnp.logical_not(used), spare_blk < nblk)
    ib = jnp.where(used, ib, jnp.minimum(spare_blk, nblk - 1))
    ilo = jnp.where(fill, -1, ilo)
    ordinal = jnp.cumsum(jnp.concatenate([jnp.zeros((1,), I32), (ie[1:] != ie[:-1]).astype(I32)]))
    first_next = jnp.sum(ordinal[None, :] <= ordinal[:, None], axis=1)
    has_next = first_next < nitems
    nxt = jnp.where(has_next, ie[jnp.minimum(first_next, nitems - 1)], -1)
    return (ib.astype(I32), ie.astype(I32), ilo.astype(I32), ihi.astype(I32), nxt.astype(I32),
            (ordinal % 2).astype(I32))


def _combine_kernel(nq_ref, gq_cur, gq_nxt, h_ref, idx_ref, rank_ref, gate_ref, tab_ref, g2_ref, b2_ref, ys_hbm,
                    o_ref, stage, sem):
    m = pl.program_id(0)
    last = pl.num_programs(0) - 1
    slot = m % 2

    def start_fetch(gq_ref, n, s):
        _for_octets(n, lambda q, pri: pltpu.make_async_copy(
            _octet(ys_hbm, gq_ref[0, 0, q]), _octet(stage.at[s], q), sem.at[s]).start(priority=pri))

    @pl.when(m == 0)
    def _():
        stage[...] = jnp.zeros_like(stage)
        start_fetch(gq_cur, nq_ref[0], 0)

    @pl.when(m < last)
    def _():
        start_fetch(gq_nxt, nq_ref[jnp.minimum(m + 1, last)], 1 - slot)

    _wait_octets(nq_ref[m], stage.at[slot], sem.at[slot])

    lane_e = lax.broadcasted_iota(I32, idx_ref.shape, 1)
    lane_p = lax.broadcasted_iota(I32, (ROUTE_ROWS, STAGE_ROWS), 1).astype(F32)
    idx = idx_ref[...]
    rank = rank_ref[...].astype(F32)
    gates = gate_ref[...]
    wsel = jnp.zeros((ROUTE_ROWS, STAGE_ROWS), F32)
    for k in range(TOP_K):
        hot = lane_e == idx[:, k:k + 1]
        lpos = jnp.sum(jnp.where(hot, tab_ref[...], 0.0), axis=-1, keepdims=True) + rank[:, k:k + 1]
        wsel = jnp.where(lane_p == lpos, gates[:, k:k + 1], wsel)
    y = _dot(wsel.astype(BF16), stage[slot].astype(BF16))
    z = DN_ALPHA * h_ref[...] + y
    mu = jnp.mean(z, axis=-1, keepdims=True)
    zc = z - mu
    var = jnp.mean(zc * zc, axis=-1, keepdims=True)
    o_ref[...] = zc * lax.rsqrt(var + LN_EPS) * g2_ref[...] + b2_ref[...]


def _combine(h, ys, idx, rank, gates, tab, gq3, nq, g2, b2):
    t, dm = h.shape
    nt = t // ROUTE_ROWS
    rowblk = lambda w: pl.BlockSpec((ROUTE_ROWS, w), lambda m, nq: (m, 0))
    const = pl.BlockSpec((1, dm), lambda m, nq: (0, 0))
    qspec = lambda f: pl.BlockSpec((1, 1, STAGE_OCTETS), lambda m, nq: (f(m), 0, 0), memory_space=pltpu.SMEM)
    grid_spec = pltpu.PrefetchScalarGridSpec(
        num_scalar_prefetch=1,
        grid=(nt,),
        in_specs=[qspec(lambda m: m), qspec(lambda m: jnp.minimum(m + 1, nt - 1)),
                  rowblk(dm), rowblk(LANES), rowblk(LANES), rowblk(LANES),
                  pl.BlockSpec((None, 1, LANES), lambda m, nq: (m, 0, 0)), const, const,
                  pl.BlockSpec(memory_space=pl.ANY)],
        out_specs=rowblk(dm),
        scratch_shapes=[pltpu.VMEM((2, STAGE_ROWS, dm), F32), pltpu.SemaphoreType.DMA((2,))],
    )
    return pl.pallas_call(
        _combine_kernel,
        grid_spec=grid_spec,
        out_shape=jax.ShapeDtypeStruct((t, dm), F32),
        compiler_params=_params(("arbitrary",)),
        name="combine_ln",
    )(nq, gq3, gq3, h, idx, rank, gates, tab, g2, b2, ys)


def _route_tables(runs, total, nrows):
    nt = runs.shape[0]
    nxt = jnp.concatenate([runs[1:], total[None, :]], axis=0)
    c8 = (nxt - runs + OCTET - 1) // OCTET
    q_end = jnp.cumsum(c8, axis=1)
    q0 = q_end - c8
    nq = q_end[:, -1]
    per_expert = jnp.sum(c8, axis=0)
    start8 = jnp.cumsum(per_expert) - per_expert
    g8 = start8[None, :] + jnp.cumsum(c8, axis=0) - c8
    tab = OCTET * q0 - runs
    q = jnp.arange(STAGE_OCTETS, dtype=I32)
    e_of_q = jnp.minimum(jnp.sum(q[None, :, None] >= q_end[:, None, :], axis=2), N_EXPERTS - 1)
    pick = e_of_q[:, :, None] == jnp.arange(N_EXPERTS, dtype=I32)[None, None, :]
    gq = jnp.sum(jnp.where(pick, (g8 - q0)[:, None, :], 0), axis=2) + q[None, :]
    gq = jnp.where(q[None, :] < nq[:, None], gq, 0).astype(I32)
    tot8 = jnp.sum(per_expert)
    tail = jnp.stack([tot8, nrows // OCTET - tot8]).astype(I32)
    return tab, gq.reshape(nt, 1, STAGE_OCTETS), nq.astype(I32), tail, (per_expert * OCTET).astype(I32)


def _layer(x, w_in, sgu_w, sgu_b, sgu_ln_g, sgu_ln_b, mix_norm_g, w_out, ln1_g, ln1_b,
           w_router, b_router, w_gate, b_gate, w_up, b_up, w_down, b_down, ln2_g, ln2_b):
    batch, seq, dm = x.shape
    width = dm // 2
    t = batch * seq
    assert seq % ATTN_SPAN == 0 and dm % (2 * LANES) == 0 and w_router.shape[-1] == N_EXPERTS
    x2 = x.reshape(t, dm)

    wc = jnp.tril(sgu_w)
    wpair = jnp.concatenate([wc[0::2], wc[1::2]], axis=-1).astype(BF16)
    sbias = jnp.repeat(sgu_b.T, HEAD_DIM, axis=1)
    grp = jnp.arange(width) // HEAD_DIM
    gmat = jnp.where(grp[:, None] == grp[None, :], 1.0 / HEAD_DIM, 0.0).astype(BF16)
    row = lambda v: v.reshape(1, -1)

    a_n, q1, k1, v1, q4, k4, v4, q16, k16, v16 = _project(
        x2, w_in.astype(BF16), gmat, wpair, sbias, row(sgu_ln_g), row(sgu_ln_b),
        row(mix_norm_g[:width]), batch, seq)
    b = _attention((q1, k1, v1), (q4, k4, v4), (q16, k16, v16), batch, seq)

    wr = jnp.pad(w_router, ((0, 0), (0, LANES - N_EXPERTS))).astype(BF16)
    br = jnp.concatenate([b_router.astype(F32), jnp.full((LANES - N_EXPERTS,), -1e30, F32)]).reshape(1, LANES)
    ti = jnp.arange(PROJ_ROWS)
    tri = (ti[None, :] < ti[:, None]).astype(BF16)
    h, idx, gates, rank, idxt, rankt, runs, cnt = _outproj(
        x2, a_n.reshape(t, width), b.reshape(t, width), w_out.astype(BF16),
        row(mix_norm_g[width:]), row(ln1_g), row(ln1_b), wr, br, tri)

    nt = t // ROUTE_ROWS
    nrows = (t * TOP_K + nt * N_EXPERTS * (OCTET - 1) + MOE_ROWS - 1) // MOE_ROWS * MOE_ROWS
    runs = runs.reshape(nt, LANES)[:, :N_EXPERTS].astype(I32)
    total = cnt[0, :N_EXPERTS].astype(I32)
    tab, gq3, nq, tail, rows_e = _route_tables(runs, total, nrows)
    tabf = tab.astype(F32)
    tab_lane = jnp.pad(tabf, ((0, 0), (0, LANES - N_EXPERTS))).reshape(nt, 1, LANES)
    tab_sub = jnp.broadcast_to(tabf[:, :, None], (nt, N_EXPERTS, ROUTE_ROWS))

    xs = _dispatch(h, idxt, rankt, tab_sub, gq3, nq, tail, nrows)
    ys = _moe(xs, _work_items(rows_e, nrows), w_gate, b_gate, w_up, b_up, w_down, b_down)
    out = _combine(h, ys, idx, rank, gates, tab_lane, gq3, nq, row(ln2_g), row(ln2_b))
    return out.reshape(batch, seq, dm)


def kernel(x, w_in, sgu_w, sgu_b, sgu_ln_g, sgu_ln_b, mix_norm_g, w_out, ln1_g, ln1_b, w_router, b_router,
           w_gate, b_gate, w_up, b_up, w_down, b_down, ln2_g, ln2_b):
    assert w_in.shape[0] == DEPTH
    return _layer(x, w_in[0], sgu_w[0], sgu_b[0], sgu_ln_g[0], sgu_ln_b[0], mix_norm_g[0], w_out[0],
                  ln1_g[0], ln1_b[0], w_router[0], b_router[0], w_gate[0], b_gate[0], w_up[0], b_up[0],
                  w_down[0], b_down[0], ln2_g[0], ln2_b[0])
```

```python
import jax
import jax.numpy as jnp
from jax import lax
from jax.experimental import pallas as pl
from jax.experimental.pallas import tpu as pltpu

F32 = jnp.float32
BF16 = jnp.bfloat16
I32 = jnp.int32

LN_EPS = 1e-5
HEAD_DIM = 64
SGU_CHUNK = 128
BAND_BLOCK = 128
DILATIONS = (1, 4, 16)
ATTN_SPAN = BAND_BLOCK * DILATIONS[-1]
N_EXPERTS = 32
TOP_K = 4
SWIGLU_ALPHA = 1.702
SWIGLU_LIMIT = 7.0
DEPTH = 1
DN_ALPHA = (2 * DEPTH) ** 0.25
LOG2E = 1.4426950408889634

LANES = 128
PROJ_ROWS = 512
MOE_ROWS = 512
MOE_SUB = 256
ROUTE_ROWS = 256
OCTET = 8
STAGE_ROWS = ROUTE_ROWS * TOP_K + N_EXPERTS * OCTET
STAGE_OCTETS = STAGE_ROWS // OCTET
COMBINE_SLOTS = 3
VMEM_LIMIT = 56 * 1024 * 1024


def _params(sem):
    return pltpu.CompilerParams(dimension_semantics=sem, vmem_limit_bytes=VMEM_LIMIT)


def _dot(a, b):
    return jnp.dot(a, b, preferred_element_type=F32)


def _proj_kernel(x_ref, w_ref, gmat_ref, wpair_ref, sbias_ref, lng_ref, lnb_ref, mg_ref,
                 a_ref, q1_ref, k1_ref, v1_ref, q4_ref, k4_ref, v4_ref, q16_ref, k16_ref, v16_ref,
                 a_scr, t_scr, t4_scr):
    width = a_ref.shape[-1]
    xb = x_ref[...].astype(BF16)

    def proj(c):
        return _dot(xb, w_ref[:, c * width:(c + 1) * width])

    u = jax.nn.gelu(proj(0))
    v = jax.nn.gelu(proj(1))
    mean = _dot(v.astype(BF16), gmat_ref[...])
    d = v - mean
    var = _dot((d * d).astype(BF16), gmat_ref[...])
    vn = (d * lax.rsqrt(var + LN_EPS) * lng_ref[...] + lnb_ref[...]).astype(BF16)

    lane = lax.broadcasted_iota(I32, (SGU_CHUNK, LANES), 1)
    low = lane < HEAD_DIM
    zero = jnp.zeros((SGU_CHUNK, LANES), BF16)
    for c in range(PROJ_ROWS // SGU_CHUNK):
        rows = slice(c * SGU_CHUNK, (c + 1) * SGU_CHUNK)
        for j in range(width // LANES):
            cols = slice(j * LANES, (j + 1) * LANES)
            vp = vn[rows, cols]
            rhs = jnp.concatenate([jnp.where(low, vp, zero), jnp.where(low, zero, vp)], axis=0)
            gate = _dot(wpair_ref[j], rhs) + sbias_ref[:, cols]
            a_scr[rows, cols] = u[rows, cols] * gate
    a = a_scr[...]
    ms = jnp.mean(a * a, axis=-1, keepdims=True)
    a_ref[...] = (a * lax.rsqrt(ms + LN_EPS) * mg_ref[...]).astype(BF16)

    outs = ((q1_ref, q4_ref, q16_ref), (k1_ref, k4_ref, k16_ref), (v1_ref, v4_ref, v16_ref))
    for c, (o1, o4, o16) in enumerate(outs):
        t = proj(2 + c)
        if c == 0:
            t = t * (HEAD_DIM ** -0.5 * LOG2E)
        o1[...] = t.astype(BF16)
        for j in range(width // LANES):
            cols = slice(j * LANES, (j + 1) * LANES)
            t_scr[j] = t[:, cols]
            for b in range(4):
                t4 = t_scr[j, pl.ds(b, PROJ_ROWS // 4, stride=4), :]
                o4[b, :, cols] = t4.astype(BF16)
                t4_scr[b] = t4
                for a in range(4):
                    o16[4 * a + b, :, cols] = t4_scr[b, pl.ds(a, PROJ_ROWS // 16, stride=4), :].astype(BF16)


def _project(x2, w_in_b, gmat, wpair, sbias, lng, lnb, mg, batch, seq):
    t, dm = x2.shape
    width = dm // 2
    nt = seq // PROJ_ROWS
    per_span = ATTN_SPAN // PROJ_ROWS
    const = lambda *shape: pl.BlockSpec(shape, lambda b, m: (0,) * len(shape))
    o1 = jax.ShapeDtypeStruct((batch, seq, width), BF16)
    o4 = jax.ShapeDtypeStruct((batch, seq // 512, 4, BAND_BLOCK, width), BF16)
    o16 = jax.ShapeDtypeStruct((batch, seq // ATTN_SPAN, 16, BAND_BLOCK, width), BF16)
    s1 = pl.BlockSpec((None, PROJ_ROWS, width), lambda b, m: (b, m, 0))
    s4 = pl.BlockSpec((None, None, 4, BAND_BLOCK, width), lambda b, m: (b, m, 0, 0, 0))
    s16 = pl.BlockSpec((None, None, 16, PROJ_ROWS // 16, width),
                       lambda b, m: (b, m // per_span, 0, m % per_span, 0))
    return pl.pallas_call(
        _proj_kernel,
        grid=(batch, nt),
        in_specs=[pl.BlockSpec((PROJ_ROWS, dm), lambda b, m: (b * nt + m, 0)),
                  const(*w_in_b.shape), const(*gmat.shape), const(*wpair.shape), const(*sbias.shape),
                  const(1, width), const(1, width), const(1, width)],
        out_specs=[s1] + [s1, s1, s1] + [s4, s4, s4] + [s16, s16, s16],
        out_shape=[o1] + [o1, o1, o1] + [o4, o4, o4] + [o16, o16, o16],
        scratch_shapes=[pltpu.VMEM((PROJ_ROWS, width), F32),
                        pltpu.VMEM((width // LANES, PROJ_ROWS, LANES), F32),
                        pltpu.VMEM((4, PROJ_ROWS // 4, LANES), F32)],
        compiler_params=_params(("parallel", "parallel")),
        name="proj_sgu",
    )(x2, w_in_b, gmat, wpair, sbias, lng, lnb, mg)


def _attn_kernel(q1, k1, v1, kp1, vp1, q4, k4, v4, kp4, vp4, q16, k16, v16, kp16, vp16,
                 o_ref, acc_o, acc_m, acc_l):
    not_first = pl.program_id(1) > 0
    blk = BAND_BLOCK
    row = lax.broadcasted_iota(I32, (2 * blk, 2 * blk), 0) % blk
    col = lax.broadcasted_iota(I32, (2 * blk, 2 * blk), 1)
    band = jnp.logical_or(jnp.logical_and(col < blk, col >= row), jnp.logical_and(col >= blk, col - blk <= row))
    neg = jnp.where(not_first, 0.0, -jnp.inf).astype(F32)
    first_pen = jnp.where(col < blk, neg, 0.0)
    low = lax.broadcasted_iota(I32, (blk, LANES), 1) < HEAD_DIM
    zero = jnp.zeros((blk, LANES), BF16)
    ones = jnp.ones((2 * blk, LANES), BF16)

    def unit(q, kprev, kcur, vprev, vcur, maybe_first, rows, init):
        kk = jnp.concatenate([kprev, kcur], axis=0)
        vv = jnp.concatenate([vprev, vcur], axis=0)
        qq = jnp.concatenate([jnp.where(low, q, zero), jnp.where(low, zero, q)], axis=0)
        s = lax.dot_general(qq, kk, (((1,), (1,)), ((), ())), preferred_element_type=F32)
        s = jnp.where(band, s, -jnp.inf)
        if maybe_first:
            s = s + first_pen
        m2 = jnp.max(s, axis=-1, keepdims=True)
        p = jnp.exp2(s - m2)
        o2 = _dot(p.astype(BF16), jnp.concatenate([vv, ones], axis=1))
        o = jnp.where(low, o2[:blk, :LANES], o2[blk:, :LANES])
        m = jnp.where(low, m2[:blk], m2[blk:])
        l = jnp.where(low, o2[:blk, LANES:], o2[blk:, LANES:])
        if init:
            acc_o[rows, :] = o
            acc_m[rows, :] = m
            acc_l[rows, :] = l
        else:
            mo = acc_m[rows, :]
            mn = jnp.maximum(mo, m)
            so = jnp.exp2(mo - mn)
            sn = jnp.exp2(m - mn)
            acc_o[rows, :] = acc_o[rows, :] * so + o * sn
            acc_l[rows, :] = acc_l[rows, :] * so + l * sn
            acc_m[rows, :] = mn

    group = 4

    def rows1(base, j):
        return pl.ds(base + j * blk, blk)

    for j in range(group):
        cur = rows1(0, j)
        if j == 0:
            unit(q1[cur, :], kp1[...], k1[cur, :], vp1[...], v1[cur, :], True, cur, True)
        else:
            prev = rows1(0, j - 1)
            unit(q1[cur, :], k1[prev, :], k1[cur, :], v1[prev, :], v1[cur, :], False, cur, True)

    def body1(g, c):
        base = pl.multiple_of(g * (group * blk), group * blk)
        for j in range(group):
            cur = rows1(base, j)
            prev = rows1(base, j - 1)
            unit(q1[cur, :], k1[prev, :], k1[cur, :], v1[prev, :], v1[cur, :], False, cur, True)
        return c

    lax.fori_loop(1, ATTN_SPAN // (group * blk), body1, 0)

    for r in range(4):
        unit(q4[0, r], kp4[r], k4[0, r], vp4[r], v4[0, r], True, pl.ds(r, blk, stride=4), False)

    def body4(s, c):
        for r in range(4):
            unit(q4[s, r], k4[s - 1, r], k4[s, r], v4[s - 1, r], v4[s, r], False,
                 pl.ds(s * (4 * blk) + r, blk, stride=4), False)
        return c

    lax.fori_loop(1, ATTN_SPAN // (4 * blk), body4, 0)

    group16 = 8

    def body16(g, c):
        for j in range(group16):
            r = g * group16 + j
            unit(q16[r], kp16[r], k16[r], vp16[r], v16[r], True, pl.ds(r, blk, stride=16), False)
        return c

    lax.fori_loop(0, 16 // group16, body16, 0)

    o_ref[...] = (acc_o[...] / acc_l[...]).astype(BF16)


def _attention(qkv1, qkv4, qkv16, batch, seq):
    width = qkv1[0].shape[-1]
    nspan = seq // ATTN_SPAN
    blk = BAND_BLOCK
    n1 = ATTN_SPAN // blk
    n4 = ATTN_SPAN // (4 * blk)

    cur1 = pl.BlockSpec((None, ATTN_SPAN, LANES), lambda b, i, p: (b, i, p))
    prev1 = pl.BlockSpec((None, None, blk, LANES), lambda b, i, p: (b, jnp.maximum(i * n1 - 1, 0), 0, p))
    cur4 = pl.BlockSpec((None, n4, 4, blk, LANES), lambda b, i, p: (b, i, 0, 0, p))
    prev4 = pl.BlockSpec((None, None, 4, blk, LANES), lambda b, i, p: (b, jnp.maximum(i * n4 - 1, 0), 0, 0, p))
    cur16 = pl.BlockSpec((None, None, 16, blk, LANES), lambda b, i, p: (b, i, 0, 0, p))
    prev16 = pl.BlockSpec((None, None, 16, blk, LANES), lambda b, i, p: (b, jnp.maximum(i - 1, 0), 0, 0, p))

    q1, k1, v1 = qkv1
    q4, k4, v4 = qkv4
    q16, k16, v16 = qkv16
    k1b = k1.reshape(batch, seq // blk, blk, width)
    v1b = v1.reshape(batch, seq // blk, blk, width)
    return pl.pallas_call(
        _attn_kernel,
        grid=(batch, nspan, width // LANES),
        in_specs=[cur1, cur1, cur1, prev1, prev1,
                  cur4, cur4, cur4, prev4, prev4,
                  cur16, cur16, cur16, prev16, prev16],
        out_specs=pl.BlockSpec((None, ATTN_SPAN, LANES), lambda b, i, p: (b, i, p)),
        out_shape=jax.ShapeDtypeStruct((batch, seq, width), BF16),
        scratch_shapes=[pltpu.VMEM((ATTN_SPAN, LANES), F32)] * 3,
        compiler_params=_params(("parallel", "parallel", "parallel")),
        name="dilated_attn",
    )(q1, k1, v1, k1b, v1b, q4, k4, v4, k4, v4, q16, k16, v16, k16, v16)


def _outproj_kernel(x_ref, a_ref, b_ref, wo_ref, mgb_ref, g1_ref, b1_ref, wr_ref, br_ref, tri_ref,
                    h_ref, idx_ref, gate_ref, rank_ref, idxt_ref, rankt_ref, runs_ref, cnt_ref, run_scr):
    @pl.when(pl.program_id(0) == 0)
    def _():
        run_scr[...] = jnp.zeros_like(run_scr)

    width = a_ref.shape[-1]
    bf = b_ref[...].astype(F32)
    bn = (bf * lax.rsqrt(jnp.mean(bf * bf, axis=-1, keepdims=True) + LN_EPS) * mgb_ref[...]).astype(BF16)
    mixed = _dot(a_ref[...], wo_ref[0:width, :]) + _dot(bn, wo_ref[width:2 * width, :])
    z = DN_ALPHA * x_ref[...] + mixed
    mu = jnp.mean(z, axis=-1, keepdims=True)
    zc = z - mu
    var = jnp.mean(zc * zc, axis=-1, keepdims=True)
    h = zc * lax.rsqrt(var + LN_EPS) * g1_ref[...] + b1_ref[...]
    h_ref[...] = h

    logits = _dot(h.astype(BF16), wr_ref[...]) + br_ref[...]
    lane = lax.broadcasted_iota(I32, logits.shape, 1).astype(F32)
    work = logits
    vals, idxs, hots = [], [], []
    for _ in range(TOP_K):
        mv = jnp.max(work, axis=-1, keepdims=True)
        ix = jnp.min(jnp.where(work == mv, lane, float(LANES)), axis=-1, keepdims=True)
        hot = lane == ix
        work = jnp.where(hot, -jnp.inf, work)
        vals.append(mv)
        idxs.append(ix)
        hots.append(hot)
    exps = [jnp.exp(v - vals[0]) for v in vals]
    den = exps[0] + exps[1] + exps[2] + exps[3]
    gates = [e / den for e in exps]

    member = jnp.zeros(logits.shape, F32)
    for hot in hots:
        member = jnp.where(hot, 1.0, member)
    before = _dot(tri_ref[...], member.astype(BF16)) + run_scr[...]
    ranks = [jnp.sum(jnp.where(hot, before, 0.0), axis=-1, keepdims=True) for hot in hots]
    runs_ref[0] = run_scr[...]
    for j in range(1, PROJ_ROWS // ROUTE_ROWS):
        runs_ref[j] = run_scr[...] + jnp.sum(member[:j * ROUTE_ROWS], axis=0, keepdims=True)
    run_scr[...] = run_scr[...] + jnp.sum(member, axis=0, keepdims=True)
    cnt_ref[...] = run_scr[...]

    def spread(cols):
        out = jnp.zeros(logits.shape, F32)
        for k, cval in enumerate(cols):
            out = jnp.where(lane == float(k), cval, out)
        return out

    idx_all = spread(idxs)
    rank_all = spread(ranks)
    idx_ref[...] = idx_all.astype(I32)
    gate_ref[...] = spread(gates)
    rank_ref[...] = rank_all.astype(I32)
    idxt_ref[...] = idx_all.T[:OCTET].astype(I32)
    rankt_ref[...] = rank_all.T[:OCTET].astype(I32)


def _outproj(x2, a_n, b2, w_out_b, mgb, g1, b1, wr, br, tri):
    t, dm = x2.shape
    width = dm // 2
    const = lambda *shape: pl.BlockSpec(shape, lambda m: (0,) * len(shape))
    rowblk = lambda w: pl.BlockSpec((PROJ_ROWS, w), lambda m: (m, 0))
    colblk = pl.BlockSpec((OCTET, PROJ_ROWS), lambda m: (0, m))
    sub = PROJ_ROWS // ROUTE_ROWS
    return pl.pallas_call(
        _outproj_kernel,
        grid=(t // PROJ_ROWS,),
        in_specs=[rowblk(dm), rowblk(width), rowblk(width), const(dm, dm), const(1, width),
                  const(1, dm), const(1, dm), const(dm, LANES), const(1, LANES),
                  const(PROJ_ROWS, PROJ_ROWS)],
        out_specs=[rowblk(dm), rowblk(LANES), rowblk(LANES), rowblk(LANES), colblk, colblk,
                   pl.BlockSpec((None, sub, 1, LANES), lambda m: (m, 0, 0, 0)), const(1, LANES)],
        out_shape=[jax.ShapeDtypeStruct((t, dm), F32), jax.ShapeDtypeStruct((t, LANES), I32),
                   jax.ShapeDtypeStruct((t, LANES), F32), jax.ShapeDtypeStruct((t, LANES), I32),
                   jax.ShapeDtypeStruct((OCTET, t), I32), jax.ShapeDtypeStruct((OCTET, t), I32),
                   jax.ShapeDtypeStruct((t // PROJ_ROWS, sub, 1, LANES), F32),
                   jax.ShapeDtypeStruct((1, LANES), F32)],
        scratch_shapes=[pltpu.VMEM((1, LANES), F32)],
        compiler_params=_params(("arbitrary",)),
        name="outproj_router",
    )(x2, a_n, b2, w_out_b, mgb, g1, b1, wr, br, tri)


def _octet(ref, q):
    return ref.at[pl.ds(pl.multiple_of(q * OCTET, OCTET), OCTET), :]


def _for_octets(n, start):
    def body(i, c):
        start(2 * i, 0)
        start(2 * i + 1, 1)
        return c

    lax.fori_loop(0, n // 2, body, 0)

    @pl.when(n % 2 == 1)
    def _():
        start(n - 1, 0)


def _wait_octets(n, ref, sem):
    for s in (128, 64, 32, 16, 8, 4, 2, 1):
        @pl.when((n & s) != 0)
        def _():
            d = ref.at[pl.ds(0, s * OCTET), :]
            pltpu.make_async_copy(d, d, sem).wait()


def _dispatch_kernel(nq_ref, tail_ref, gq_ref, h_ref, idxt_ref, rankt_ref, tabt_ref, xs_hbm,
                     stage, zero8, sem, zsem):
    m = pl.program_id(0)
    eid = lax.broadcasted_iota(I32, (N_EXPERTS, ROUTE_ROWS), 0)
    pos = lax.broadcasted_iota(I32, (STAGE_ROWS, ROUTE_ROWS), 0).astype(F32)
    sel = None
    for k in range(TOP_K):
        hot = eid == idxt_ref[k:k + 1, :]
        lpos = (jnp.sum(jnp.where(hot, tabt_ref[...], 0.0), axis=0, keepdims=True)
                + rankt_ref[k:k + 1, :].astype(F32))
        hit = pos == lpos
        sel = hit if sel is None else jnp.logical_or(sel, hit)
    slot = m % 2
    stage[slot] = _dot(jnp.where(sel, 1.0, 0.0).astype(BF16), h_ref[...].astype(BF16))

    @pl.when(m == 0)
    def _():
        zero8[...] = jnp.zeros_like(zero8)

        def zstart(i, c):
            pltpu.make_async_copy(zero8, _octet(xs_hbm, tail_ref[0] + i), zsem).start()
            return c

        lax.fori_loop(0, tail_ref[1], zstart, 0)

    nq = nq_ref[m]
    _for_octets(nq, lambda q, pri: pltpu.make_async_copy(
        _octet(stage.at[slot], q), _octet(xs_hbm, gq_ref[0, 0, q]), sem.at[slot]).start(priority=pri))

    @pl.when(m > 0)
    def _():
        _wait_octets(nq_ref[jnp.maximum(m - 1, 0)], xs_hbm, sem.at[1 - slot])

    @pl.when(m == pl.num_programs(0) - 1)
    def _():
        _wait_octets(nq, xs_hbm, sem.at[slot])

        def zwait(i, c):
            pltpu.make_async_copy(zero8, _octet(xs_hbm, 0), zsem).wait()
            return c

        lax.fori_loop(0, tail_ref[1], zwait, 0)


def _dispatch(h, idxt, rankt, tabt, gq3, nq, tail, nrows):
    t, dm = h.shape
    grid_spec = pltpu.PrefetchScalarGridSpec(
        num_scalar_prefetch=2,
        grid=(t // ROUTE_ROWS,),
        in_specs=[pl.BlockSpec((1, 1, STAGE_OCTETS), lambda m, nq, tl: (m, 0, 0), memory_space=pltpu.SMEM),
                  pl.BlockSpec((ROUTE_ROWS, dm), lambda m, nq, tl: (m, 0)),
                  pl.BlockSpec((OCTET, ROUTE_ROWS), lambda m, nq, tl: (0, m)),
                  pl.BlockSpec((OCTET, ROUTE_ROWS), lambda m, nq, tl: (0, m)),
                  pl.BlockSpec((None, N_EXPERTS, ROUTE_ROWS), lambda m, nq, tl: (m, 0, 0))],
        out_specs=pl.BlockSpec(memory_space=pl.ANY),
        scratch_shapes=[pltpu.VMEM((2, STAGE_ROWS, dm), F32), pltpu.VMEM((OCTET, dm), F32),
                        pltpu.SemaphoreType.DMA((2,)), pltpu.SemaphoreType.DMA(())],
    )
    return pl.pallas_call(
        _dispatch_kernel,
        grid_spec=grid_spec,
        out_shape=jax.ShapeDtypeStruct((nrows, dm), F32),
        compiler_params=_params(("arbitrary",)),
        name="moe_dispatch",
    )(nq, tail, gq3, h, idxt, rankt, tabt)


def _moe_kernel(ib_ref, ie_ref, lo_ref, hi_ref, nxt_ref, slot_ref, xs_ref, bg_ref, bu_ref, bd_ref,
                wg_hbm, wu_hbm, wd_hbm, ys_ref, wf, wb, wsem):
    i = pl.program_id(0)
    prev = jnp.maximum(i - 1, 0)
    lo = lo_ref[i]
    hi = hi_ref[i]

    def weight_copies(e, s):
        return [pltpu.make_async_copy(w.at[e], wf.at[s, j], wsem.at[s])
                for j, w in enumerate((wg_hbm, wu_hbm, wd_hbm))]

    @pl.when(jnp.logical_and(lo >= 0, hi > lo))
    def _():
        @pl.when(jnp.logical_or(i == 0, ie_ref[i] != ie_ref[prev]))
        def _():
            s = slot_ref[i]

            @pl.when(i == 0)
            def _():
                for cp in weight_copies(ie_ref[i], s):
                    cp.start()

            for cp in weight_copies(ie_ref[i], s):
                cp.wait()
            for j in range(3):
                wb[j] = wf[s, j].astype(BF16)

            @pl.when(nxt_ref[i] >= 0)
            def _():
                for cp in weight_copies(nxt_ref[i], 1 - s):
                    cp.start()

        def expert(rows):
            x = xs_ref[rows, :].astype(BF16)
            g = jnp.minimum(_dot(x, wb[0]) + bg_ref[...], SWIGLU_LIMIT)
            u = jnp.clip(_dot(x, wb[1]) + bu_ref[...], -SWIGLU_LIMIT, SWIGLU_LIMIT)
            act = (u + 1.0) * (g * jax.nn.sigmoid(SWIGLU_ALPHA * g))
            return _dot(act.astype(BF16), wb[2]) + bd_ref[...]

        whole = jnp.logical_and(lo == 0, hi == MOE_ROWS)

        @pl.when(whole)
        def _():
            ys_ref[...] = expert(slice(None))

        @pl.when(jnp.logical_not(whole))
        def _():
            @pl.when(jnp.logical_or(i == 0, ib_ref[i] != ib_ref[prev]))
            def _():
                ys_ref[...] = jnp.zeros_like(ys_ref)

            for j in range(MOE_ROWS // MOE_SUB):
                rows = slice(j * MOE_SUB, (j + 1) * MOE_SUB)

                @pl.when(jnp.logical_and(lo < (j + 1) * MOE_SUB, hi > j * MOE_SUB))
                def _():
                    rid = lax.broadcasted_iota(I32, (MOE_SUB, ys_ref.shape[1]), 0) + j * MOE_SUB
                    mine = jnp.logical_and(rid >= lo, rid < hi)
                    ys_ref[rows, :] = jnp.where(mine, expert(rows), ys_ref[rows, :])

    @pl.when(lo < 0)
    def _():
        ys_ref[...] = jnp.zeros_like(ys_ref)


def _moe(xs, items, w_gate, b_gate, w_up, b_up, w_down, b_down):
    nrows = xs.shape[0]
    ne, dm, df = w_gate.shape
    assert dm == df
    nitems = items[0].shape[0]
    bspec = lambda c: pl.BlockSpec((None, 1, c), lambda i, ib, ie, *_: (ie[i], 0, 0))
    rspec = pl.BlockSpec((MOE_ROWS, dm), lambda i, ib, *_: (ib[i], 0))
    hbm = pl.BlockSpec(memory_space=pl.ANY)
    grid_spec = pltpu.PrefetchScalarGridSpec(
        num_scalar_prefetch=6,
        grid=(nitems,),
        in_specs=[rspec, bspec(df), bspec(df), bspec(dm), hbm, hbm, hbm],
        out_specs=rspec,
        scratch_shapes=[pltpu.VMEM((2, 3, dm, df), F32), pltpu.VMEM((3, dm, df), BF16),
                        pltpu.SemaphoreType.DMA((2,))],
    )
    return pl.pallas_call(
        _moe_kernel,
        grid_spec=grid_spec,
        out_shape=jax.ShapeDtypeStruct((nrows, dm), F32),
        compiler_params=_params(("arbitrary",)),
        name="moe_experts",
    )(*items, xs, b_gate.reshape(ne, 1, df), b_up.reshape(ne, 1, df), b_down.reshape(ne, 1, dm),
      w_gate, w_up, w_down)


def _work_items(counts, nrows):
    nblk = nrows // MOE_ROWS
    nitems = nblk + N_EXPERTS - 1
    ends = jnp.cumsum(counts)
    starts = ends - counts
    b0 = jnp.arange(nblk, dtype=I32)[:, None] * MOE_ROWS
    lo = jnp.maximum(starts[None, :], b0)
    hi = jnp.minimum(ends[None, :], b0 + MOE_ROWS)
    nonempty = (hi > lo).reshape(-1)
    csum = jnp.cumsum(nonempty.astype(I32))
    j = jnp.arange(nitems, dtype=I32)
    pos = jnp.sum(csum[None, :] <= j[:, None], axis=1).astype(I32)
    used = j < csum[-1]
    pos = jnp.where(used, pos, jnp.max(jnp.where(nonempty, jnp.arange(nonempty.shape[0], dtype=I32), 0)))
    ib = pos // N_EXPERTS
    ie = pos % N_EXPERTS
    ilo = jnp.where(used, lo.reshape(-1)[pos] - ib * MOE_ROWS, 0)
    ihi = jnp.where(used, hi.reshape(-1)[pos] - ib * MOE_ROWS, 0)
    spare_blk = ib + 1 + (j - csum[-1])
    fill = jnp.logical_and(jnp.logical_not(used), spare_blk < nblk)
    ib = jnp.where(used, ib, jnp.minimum(spare_blk, nblk - 1))
    ilo = jnp.where(fill, -1, ilo)
    ordinal = jnp.cumsum(jnp.concatenate([jnp.zeros((1,), I32), (ie[1:] != ie[:-1]).astype(I32)]))
    first_next = jnp.sum(ordinal[None, :] <= ordinal[:, None], axis=1)
    has_next = first_next < nitems
    nxt = jnp.where(has_next, ie[jnp.minimum(first_next, nitems - 1)], -1)
    return (ib.astype(I32), ie.astype(I32), ilo.astype(I32), ihi.astype(I32), nxt.astype(I32),
            (ordinal % 2).astype(I32))


def _combine_kernel(nq_ref, gq0, gq1, gq2, h_ref, idx_ref, rank_ref, gate_ref, tab_ref, g2_ref, b2_ref, ys_hbm,
                    o_ref, stage, sem):
    m = pl.program_id(0)
    last = pl.num_programs(0) - 1
    slot = m % COMBINE_SLOTS

    def start_fetch(gq_ref, tile, s):
        _for_octets(nq_ref[jnp.minimum(tile, last)], lambda q, pri: pltpu.make_async_copy(
            _octet(ys_hbm, gq_ref[0, 0, q]), _octet(stage.at[s], q), sem.at[s]).start(priority=pri))

    @pl.when(m == 0)
    def _():
        stage[...] = jnp.zeros_like(stage)
        start_fetch(gq0, 0, 0)

        @pl.when(last >= 1)
        def _():
            start_fetch(gq1, 1, 1)

    @pl.when(m + 2 <= last)
    def _():
        start_fetch(gq2, m + 2, (m + 2) % COMBINE_SLOTS)

    _wait_octets(nq_ref[m], stage.at[slot], sem.at[slot])

    lane_e = lax.broadcasted_iota(I32, idx_ref.shape, 1)
    lane_p = lax.broadcasted_iota(I32, (ROUTE_ROWS, STAGE_ROWS), 1).astype(F32)
    idx = idx_ref[...]
    rank = rank_ref[...].astype(F32)
    gates = gate_ref[...]
    wsel = jnp.zeros((ROUTE_ROWS, STAGE_ROWS), F32)
    for k in range(TOP_K):
        hot = lane_e == idx[:, k:k + 1]
        lpos = jnp.sum(jnp.where(hot, tab_ref[...], 0.0), axis=-1, keepdims=True) + rank[:, k:k + 1]
        wsel = jnp.where(lane_p == lpos, gates[:, k:k + 1], wsel)
    y = _dot(wsel.astype(BF16), stage[slot].astype(BF16))
    z = DN_ALPHA * h_ref[...] + y
    mu = jnp.mean(z, axis=-1, keepdims=True)
    zc = z - mu
    var = jnp.mean(zc * zc, axis=-1, keepdims=True)
    o_ref[...] = zc * lax.rsqrt(var + LN_EPS) * g2_ref[...] + b2_ref[...]


def _combine(h, ys, idx, rank, gates, tab, gq3, nq, g2, b2):
    t, dm = h.shape
    nt = t // ROUTE_ROWS
    rowblk = lambda w: pl.BlockSpec((ROUTE_ROWS, w), lambda m, nq: (m, 0))
    const = pl.BlockSpec((1, dm), lambda m, nq: (0, 0))
    qspec = lambda f: pl.BlockSpec((1, 1, STAGE_OCTETS), lambda m, nq: (f(m), 0, 0), memory_space=pltpu.SMEM)
    grid_spec = pltpu.PrefetchScalarGridSpec(
        num_scalar_prefetch=1,
        grid=(nt,),
        in_specs=[qspec(lambda m: m), qspec(lambda m: jnp.minimum(m + 1, nt - 1)),
                  qspec(lambda m: jnp.minimum(m + 2, nt - 1)),
                  rowblk(dm), rowblk(LANES), rowblk(LANES), rowblk(LANES),
                  pl.BlockSpec((None, 1, LANES), lambda m, nq: (m, 0, 0)), const, const,
                  pl.BlockSpec(memory_space=pl.ANY)],
        out_specs=rowblk(dm),
        scratch_shapes=[pltpu.VMEM((COMBINE_SLOTS, STAGE_ROWS, dm), F32),
                        pltpu.SemaphoreType.DMA((COMBINE_SLOTS,))],
    )
    return pl.pallas_call(
        _combine_kernel,
        grid_spec=grid_spec,
        out_shape=jax.ShapeDtypeStruct((t, dm), F32),
        compiler_params=_params(("arbitrary",)),
        name="combine_ln",
    )(nq, gq3, gq3, gq3, h, idx, rank, gates, tab, g2, b2, ys)


def _route_tables(runs, total, nrows):
    nt = runs.shape[0]
    nxt = jnp.concatenate([runs[1:], total[None, :]], axis=0)
    c8 = (nxt - runs + OCTET - 1) // OCTET
    q_end = jnp.cumsum(c8, axis=1)
    q0 = q_end - c8
    nq = q_end[:, -1]
    per_expert = jnp.sum(c8, axis=0)
    start8 = jnp.cumsum(per_expert) - per_expert
    g8 = start8[None, :] + jnp.cumsum(c8, axis=0) - c8
    tab = OCTET * q0 - runs
    q = jnp.arange(STAGE_OCTETS, dtype=I32)
    e_of_q = jnp.minimum(jnp.sum(q[None, :, None] >= q_end[:, None, :], axis=2), N_EXPERTS - 1)
    pick = e_of_q[:, :, None] == jnp.arange(N_EXPERTS, dtype=I32)[None, None, :]
    gq = jnp.sum(jnp.where(pick, (g8 - q0)[:, None, :], 0), axis=2) + q[None, :]
    gq = jnp.where(q[None, :] < nq[:, None], gq, 0).astype(I32)
    tot8 = jnp.sum(per_expert)
    tail = jnp.stack([tot8, nrows // OCTET - tot8]).astype(I32)
    return tab, gq.reshape(nt, 1, STAGE_OCTETS), nq.astype(I32), tail, (per_expert * OCTET).astype(I32)


def _layer(x, w_in, sgu_w, sgu_b, sgu_ln_g, sgu_ln_b, mix_norm_g, w_out, ln1_g, ln1_b,
           w_router, b_router, w_gate, b_gate, w_up, b_up, w_down, b_down, ln2_g, ln2_b):
    batch, seq, dm = x.shape
    width = dm // 2
    t = batch * seq
    assert seq % ATTN_SPAN == 0 and dm % (2 * LANES) == 0 and w_router.shape[-1] == N_EXPERTS
    x2 = x.reshape(t, dm)

    wc = jnp.tril(sgu_w)
    wpair = jnp.concatenate([wc[0::2], wc[1::2]], axis=-1).astype(BF16)
    sbias = jnp.repeat(sgu_b.T, HEAD_DIM, axis=1)
    grp = jnp.arange(width) // HEAD_DIM
    gmat = jnp.where(grp[:, None] == grp[None, :], 1.0 / HEAD_DIM, 0.0).astype(BF16)
    row = lambda v: v.reshape(1, -1)

    a_n, q1, k1, v1, q4, k4, v4, q16, k16, v16 = _project(
        x2, w_in.astype(BF16), gmat, wpair, sbias, row(sgu_ln_g), row(sgu_ln_b),
        row(mix_norm_g[:width]), batch, seq)
    b = _attention((q1, k1, v1), (q4, k4, v4), (q16, k16, v16), batch, seq)

    wr = jnp.pad(w_router, ((0, 0), (0, LANES - N_EXPERTS))).astype(BF16)
    br = jnp.concatenate([b_router.astype(F32), jnp.full((LANES - N_EXPERTS,), -1e30, F32)]).reshape(1, LANES)
    ti = jnp.arange(PROJ_ROWS)
    tri = (ti[None, :] < ti[:, None]).astype(BF16)
    h, idx, gates, rank, idxt, rankt, runs, cnt = _outproj(
        x2, a_n.reshape(t, width), b.reshape(t, width), w_out.astype(BF16),
        row(mix_norm_g[width:]), row(ln1_g), row(ln1_b), wr, br, tri)

    nt = t // ROUTE_ROWS
    nrows = (t * TOP_K + nt * N_EXPERTS * (OCTET - 1) + MOE_ROWS - 1) // MOE_ROWS * MOE_ROWS
    runs = runs.reshape(nt, LANES)[:, :N_EXPERTS].astype(I32)
    total = cnt[0, :N_EXPERTS].astype(I32)
    tab, gq3, nq, tail, rows_e = _route_tables(runs, total, nrows)
    tabf = tab.astype(F32)
    tab_lane = jnp.pad(tabf, ((0, 0), (0, LANES - N_EXPERTS))).reshape(nt, 1, LANES)
    tab_sub = jnp.broadcast_to(tabf[:, :, None], (nt, N_EXPERTS, ROUTE_ROWS))

    xs = _dispatch(h, idxt, rankt, tab_sub, gq3, nq, tail, nrows)
    ys = _moe(xs, _work_items(rows_e, nrows), w_gate, b_gate, w_up, b_up, w_down, b_down)
    out = _combine(h, ys, idx, rank, gates, tab_lane, gq3, nq, row(ln2_g), row(ln2_b))
    return out.reshape(batch, seq, dm)


def kernel(x, w_in, sgu_w, sgu_b, sgu_ln_g, sgu_ln_b, mix_norm_g, w_out, ln1_g, ln1_b, w_router, b_router,
           w_gate, b_gate, w_up, b_up, w_down, b_down, ln2_g, ln2_b):
    assert w_in.shape[0] == DEPTH
    return _layer(x, w_in[0], sgu_w[0], sgu_b[0], sgu_ln_g[0], sgu_ln_b[0], mix_norm_g[0], w_out[0],
                  ln1_g[0], ln1_b[0], w_router[0], b_router[0], w_gate[0], b_gate[0], w_up[0], b_up[0],
                  w_down[0], b_down[0], ln2_g[0], ln2_b[0])
```

```python
import jax
import jax.numpy as jnp
from jax import lax
from jax.experimental import pallas as pl
from jax.experimental.pallas import tpu as pltpu

F32 = jnp.float32
BF16 = jnp.bfloat16
I32 = jnp.int32

LN_EPS = 1e-5
HEAD_DIM = 64
SGU_CHUNK = 128
BAND_BLOCK = 128
DILATIONS = (1, 4, 16)
ATTN_SPAN = BAND_BLOCK * DILATIONS[-1]
N_EXPERTS = 32
TOP_K = 4
SWIGLU_ALPHA = 1.702
SWIGLU_LIMIT = 7.0
DEPTH = 1
DN_ALPHA = (2 * DEPTH) ** 0.25
LOG2E = 1.4426950408889634

LANES = 128
PROJ_ROWS = 512
MOE_ROWS = 512
MOE_SUB = 256
ROUTE_ROWS = 256
OCTET = 8
STAGE_ROWS = ROUTE_ROWS * TOP_K + N_EXPERTS * OCTET
STAGE_OCTETS = STAGE_ROWS // OCTET
COMBINE_SLOTS = 3
VMEM_LIMIT = 56 * 1024 * 1024


def _params(sem):
    return pltpu.CompilerParams(dimension_semantics=sem, vmem_limit_bytes=VMEM_LIMIT)


def _dot(a, b):
    return jnp.dot(a, b, preferred_element_type=F32)


def _proj_kernel(x_ref, w_ref, gmat_ref, wpair_ref, sbias_ref, lng_ref, lnb_ref, mg_ref,
                 a_ref, q1_ref, k1_ref, v1_ref, q4_ref, k4_ref, v4_ref, q16_ref, k16_ref, v16_ref,
                 a_scr, t_scr, t4_scr):
    width = a_ref.shape[-1]
    xb = x_ref[...].astype(BF16)

    def proj(c):
        return _dot(xb, w_ref[:, c * width:(c + 1) * width])

    u = jax.nn.gelu(proj(0))
    v = jax.nn.gelu(proj(1))
    mean = _dot(v.astype(BF16), gmat_ref[...])
    d = v - mean
    var = _dot((d * d).astype(BF16), gmat_ref[...])
    vn = (d * lax.rsqrt(var + LN_EPS) * lng_ref[...] + lnb_ref[...]).astype(BF16)

    lane = lax.broadcasted_iota(I32, (SGU_CHUNK, LANES), 1)
    low = lane < HEAD_DIM
    zero = jnp.zeros((SGU_CHUNK, LANES), BF16)
    for c in range(PROJ_ROWS // SGU_CHUNK):
        rows = slice(c * SGU_CHUNK, (c + 1) * SGU_CHUNK)
        for j in range(width // LANES):
            cols = slice(j * LANES, (j + 1) * LANES)
            vp = vn[rows, cols]
            rhs = jnp.concatenate([jnp.where(low, vp, zero), jnp.where(low, zero, vp)], axis=0)
            gate = _dot(wpair_ref[j], rhs) + sbias_ref[:, cols]
            a_scr[rows, cols] = u[rows, cols] * gate
    a = a_scr[...]
    ms = jnp.mean(a * a, axis=-1, keepdims=True)
    a_ref[...] = (a * lax.rsqrt(ms + LN_EPS) * mg_ref[...]).astype(BF16)

    outs = ((q1_ref, q4_ref, q16_ref), (k1_ref, k4_ref, k16_ref), (v1_ref, v4_ref, v16_ref))
    for c, (o1, o4, o16) in enumerate(outs):
        t = proj(2 + c)
        if c == 0:
            t = t * (HEAD_DIM ** -0.5 * LOG2E)
        o1[...] = t.astype(BF16)
        for j in range(width // LANES):
            cols = slice(j * LANES, (j + 1) * LANES)
            t_scr[j] = t[:, cols]
            for b in range(4):
                t4 = t_scr[j, pl.ds(b, PROJ_ROWS // 4, stride=4), :]
                o4[b, :, cols] = t4.astype(BF16)
                t4_scr[b] = t4
                for a in range(4):
                    o16[4 * a + b, :, cols] = t4_scr[b, pl.ds(a, PROJ_ROWS // 16, stride=4), :].astype(BF16)


def _project(x2, w_in_b, gmat, wpair, sbias, lng, lnb, mg, batch, seq):
    t, dm = x2.shape
    width = dm // 2
    nt = seq // PROJ_ROWS
    per_span = ATTN_SPAN // PROJ_ROWS
    const = lambda *shape: pl.BlockSpec(shape, lambda b, m: (0,) * len(shape))
    o1 = jax.ShapeDtypeStruct((batch, seq, width), BF16)
    o4 = jax.ShapeDtypeStruct((batch, seq // 512, 4, BAND_BLOCK, width), BF16)
    o16 = jax.ShapeDtypeStruct((batch, seq // ATTN_SPAN, 16, BAND_BLOCK, width), BF16)
    s1 = pl.BlockSpec((None, PROJ_ROWS, width), lambda b, m: (b, m, 0))
    s4 = pl.BlockSpec((None, None, 4, BAND_BLOCK, width), lambda b, m: (b, m, 0, 0, 0))
    s16 = pl.BlockSpec((None, None, 16, PROJ_ROWS // 16, width),
                       lambda b, m: (b, m // per_span, 0, m % per_span, 0))
    return pl.pallas_call(
        _proj_kernel,
        grid=(batch, nt),
        in_specs=[pl.BlockSpec((PROJ_ROWS, dm), lambda b, m: (b * nt + m, 0)),
                  const(*w_in_b.shape), const(*gmat.shape), const(*wpair.shape), const(*sbias.shape),
                  const(1, width), const(1, width), const(1, width)],
        out_specs=[s1] + [s1, s1, s1] + [s4, s4, s4] + [s16, s16, s16],
        out_shape=[o1] + [o1, o1, o1] + [o4, o4, o4] + [o16, o16, o16],
        scratch_shapes=[pltpu.VMEM((PROJ_ROWS, width), F32),
                        pltpu.VMEM((width // LANES, PROJ_ROWS, LANES), F32),
                        pltpu.VMEM((4, PROJ_ROWS // 4, LANES), F32)],
        compiler_params=_params(("parallel", "parallel")),
        name="proj_sgu",
    )(x2, w_in_b, gmat, wpair, sbias, lng, lnb, mg)


def _attn_kernel(q1, k1, v1, kp1, vp1, q4, k4, v4, kp4, vp4, q16, k16, v16, kp16, vp16,
                 o_ref, acc_o, acc_m, acc_l):
    not_first = pl.program_id(1) > 0
    blk = BAND_BLOCK
    row = lax.broadcasted_iota(I32, (2 * blk, 2 * blk), 0) % blk
    col = lax.broadcasted_iota(I32, (2 * blk, 2 * blk), 1)
    band = jnp.logical_or(jnp.logical_and(col < blk, col >= row), jnp.logical_and(col >= blk, col - blk <= row))
    neg = jnp.where(not_first, 0.0, -jnp.inf).astype(F32)
    first_pen = jnp.where(col < blk, neg, 0.0)
    low = lax.broadcasted_iota(I32, (blk, LANES), 1) < HEAD_DIM
    zero = jnp.zeros((blk, LANES), BF16)
    ones = jnp.ones((2 * blk, LANES), BF16)

    def unit(q, kprev, kcur, vprev, vcur, maybe_first, rows, init):
        kk = jnp.concatenate([kprev, kcur], axis=0)
        vv = jnp.concatenate([vprev, vcur], axis=0)
        qq = jnp.concatenate([jnp.where(low, q, zero), jnp.where(low, zero, q)], axis=0)
        s = lax.dot_general(qq, kk, (((1,), (1,)), ((), ())), preferred_element_type=F32)
        s = jnp.where(band, s, -jnp.inf)
        if maybe_first:
            s = s + first_pen
        m2 = jnp.max(s, axis=-1, keepdims=True)
        p = jnp.exp2(s - m2)
        o2 = _dot(p.astype(BF16), jnp.concatenate([vv, ones], axis=1))
        o = jnp.where(low, o2[:blk, :LANES], o2[blk:, :LANES])
        m = jnp.where(low, m2[:blk], m2[blk:])
        l = jnp.where(low, o2[:blk, LANES:], o2[blk:, LANES:])
        if init:
            acc_o[rows, :] = o
            acc_m[rows, :] = m
            acc_l[rows, :] = l
        else:
            mo = acc_m[rows, :]
            mn = jnp.maximum(mo, m)
            so = jnp.exp2(mo - mn)
            sn = jnp.exp2(m - mn)
            acc_o[rows, :] = acc_o[rows, :] * so + o * sn
            acc_l[rows, :] = acc_l[rows, :] * so + l * sn
            acc_m[rows, :] = mn


    for n in range(ATTN_SPAN // blk):
        cur = pl.ds(n * blk, blk)
        if n == 0:
            unit(q1[cur, :], kp1[...], k1[cur, :], vp1[...], v1[cur, :], True, cur, True)
        else:
            prev = pl.ds((n - 1) * blk, blk)
            unit(q1[cur, :], k1[prev, :], k1[cur, :], v1[prev, :], v1[cur, :], False, cur, True)

    for s in range(ATTN_SPAN // (4 * blk)):
        for r in range(4):
            rows = pl.ds(s * (4 * blk) + r, blk, stride=4)
            if s == 0:
                unit(q4[0, r], kp4[r], k4[0, r], vp4[r], v4[0, r], True, rows, False)
            else:
                unit(q4[s, r], k4[s - 1, r], k4[s, r], v4[s - 1, r], v4[s, r], False, rows, False)

    for r in range(16):
        unit(q16[r], kp16[r], k16[r], vp16[r], v16[r], True, pl.ds(r, blk, stride=16), False)

    o_ref[...] = (acc_o[...] / acc_l[...]).astype(BF16)


def _attention(qkv1, qkv4, qkv16, batch, seq):
    width = qkv1[0].shape[-1]
    nspan = seq // ATTN_SPAN
    blk = BAND_BLOCK
    n1 = ATTN_SPAN // blk
    n4 = ATTN_SPAN // (4 * blk)

    cur1 = pl.BlockSpec((None, ATTN_SPAN, LANES), lambda b, i, p: (b, i, p))
    prev1 = pl.BlockSpec((None, None, blk, LANES), lambda b, i, p: (b, jnp.maximum(i * n1 - 1, 0), 0, p))
    cur4 = pl.BlockSpec((None, n4, 4, blk, LANES), lambda b, i, p: (b, i, 0, 0, p))
    prev4 = pl.BlockSpec((None, None, 4, blk, LANES), lambda b, i, p: (b, jnp.maximum(i * n4 - 1, 0), 0, 0, p))
    cur16 = pl.BlockSpec((None, None, 16, blk, LANES), lambda b, i, p: (b, i, 0, 0, p))
    prev16 = pl.BlockSpec((None, None, 16, blk, LANES), lambda b, i, p: (b, jnp.maximum(i - 1, 0), 0, 0, p))

    q1, k1, v1 = qkv1
    q4, k4, v4 = qkv4
    q16, k16, v16 = qkv16
    k1b = k1.reshape(batch, seq // blk, blk, width)
    v1b = v1.reshape(batch, seq // blk, blk, width)
    return pl.pallas_call(
        _attn_kernel,
        grid=(batch, nspan, width // LANES),
        in_specs=[cur1, cur1, cur1, prev1, prev1,
                  cur4, cur4, cur4, prev4, prev4,
                  cur16, cur16, cur16, prev16, prev16],
        out_specs=pl.BlockSpec((None, ATTN_SPAN, LANES), lambda b, i, p: (b, i, p)),
        out_shape=jax.ShapeDtypeStruct((batch, seq, width), BF16),
        scratch_shapes=[pltpu.VMEM((ATTN_SPAN, LANES), F32)] * 3,
        compiler_params=_params(("parallel", "parallel", "parallel")),
        name="dilated_attn",
    )(q1, k1, v1, k1b, v1b, q4, k4, v4, k4, v4, q16, k16, v16, k16, v16)


def _outproj_kernel(x_ref, a_ref, b_ref, wo_ref, mgb_ref, g1_ref, b1_ref, wr_ref, br_ref, tri_ref,
                    h_ref, idx_ref, gate_ref, rank_ref, idxt_ref, rankt_ref, runs_ref, cnt_ref, run_scr):
    @pl.when(pl.program_id(0) == 0)
    def _():
        run_scr[...] = jnp.zeros_like(run_scr)

    width = a_ref.shape[-1]
    bf = b_ref[...].astype(F32)
    bn = (bf * lax.rsqrt(jnp.mean(bf * bf, axis=-1, keepdims=True) + LN_EPS) * mgb_ref[...]).astype(BF16)
    mixed = _dot(a_ref[...], wo_ref[0:width, :]) + _dot(bn, wo_ref[width:2 * width, :])
    z = DN_ALPHA * x_ref[...] + mixed
    mu = jnp.mean(z, axis=-1, keepdims=True)
    zc = z - mu
    var = jnp.mean(zc * zc, axis=-1, keepdims=True)
    h = zc * lax.rsqrt(var + LN_EPS) * g1_ref[...] + b1_ref[...]
    h_ref[...] = h

    logits = _dot(h.astype(BF16), wr_ref[...]) + br_ref[...]
    lane = lax.broadcasted_iota(I32, logits.shape, 1).astype(F32)
    work = logits
    vals, idxs, hots = [], [], []
    for _ in range(TOP_K):
        mv = jnp.max(work, axis=-1, keepdims=True)
        ix = jnp.min(jnp.where(work == mv, lane, float(LANES)), axis=-1, keepdims=True)
        hot = lane == ix
        work = jnp.where(hot, -jnp.inf, work)
        vals.append(mv)
        idxs.append(ix)
        hots.append(hot)
    exps = [jnp.exp(v - vals[0]) for v in vals]
    den = exps[0] + exps[1] + exps[2] + exps[3]
    gates = [e / den for e in exps]

    member = jnp.zeros(logits.shape, F32)
    for hot in hots:
        member = jnp.where(hot, 1.0, member)
    before = _dot(tri_ref[...], member.astype(BF16)) + run_scr[...]
    ranks = [jnp.sum(jnp.where(hot, before, 0.0), axis=-1, keepdims=True) for hot in hots]
    runs_ref[0] = run_scr[...]
    for j in range(1, PROJ_ROWS // ROUTE_ROWS):
        runs_ref[j] = run_scr[...] + jnp.sum(member[:j * ROUTE_ROWS], axis=0, keepdims=True)
    run_scr[...] = run_scr[...] + jnp.sum(member, axis=0, keepdims=True)
    cnt_ref[...] = run_scr[...]

    def spread(cols):
        out = jnp.zeros(logits.shape, F32)
        for k, cval in enumerate(cols):
            out = jnp.where(lane == float(k), cval, out)
        return out

    idx_all = spread(idxs)
    rank_all = spread(ranks)
    idx_ref[...] = idx_all.astype(I32)
    gate_ref[...] = spread(gates)
    rank_ref[...] = rank_all.astype(I32)
    idxt_ref[...] = idx_all.T[:OCTET].astype(I32)
    rankt_ref[...] = rank_all.T[:OCTET].astype(I32)


def _outproj(x2, a_n, b2, w_out_b, mgb, g1, b1, wr, br, tri):
    t, dm = x2.shape
    width = dm // 2
    const = lambda *shape: pl.BlockSpec(shape, lambda m: (0,) * len(shape))
    rowblk = lambda w: pl.BlockSpec((PROJ_ROWS, w), lambda m: (m, 0))
    colblk = pl.BlockSpec((OCTET, PROJ_ROWS), lambda m: (0, m))
    sub = PROJ_ROWS // ROUTE_ROWS
    return pl.pallas_call(
        _outproj_kernel,
        grid=(t // PROJ_ROWS,),
        in_specs=[rowblk(dm), rowblk(width), rowblk(width), const(dm, dm), const(1, width),
                  const(1, dm), const(1, dm), const(dm, LANES), const(1, LANES),
                  const(PROJ_ROWS, PROJ_ROWS)],
        out_specs=[rowblk(dm), rowblk(LANES), rowblk(LANES), rowblk(LANES), colblk, colblk,
                   pl.BlockSpec((None, sub, 1, LANES), lambda m: (m, 0, 0, 0)), const(1, LANES)],
        out_shape=[jax.ShapeDtypeStruct((t, dm), F32), jax.ShapeDtypeStruct((t, LANES), I32),
                   jax.ShapeDtypeStruct((t, LANES), F32), jax.ShapeDtypeStruct((t, LANES), I32),
                   jax.ShapeDtypeStruct((OCTET, t), I32), jax.ShapeDtypeStruct((OCTET, t), I32),
                   jax.ShapeDtypeStruct((t // PROJ_ROWS, sub, 1, LANES), F32),
                   jax.ShapeDtypeStruct((1, LANES), F32)],
        scratch_shapes=[pltpu.VMEM((1, LANES), F32)],
        compiler_params=_params(("arbitrary",)),
        name="outproj_router",
    )(x2, a_n, b2, w_out_b, mgb, g1, b1, wr, br, tri)


def _octet(ref, q):
    return ref.at[pl.ds(pl.multiple_of(q * OCTET, OCTET), OCTET), :]


def _for_octets(n, start):
    def body(i, c):
        start(2 * i, 0)
        start(2 * i + 1, 1)
        return c

    lax.fori_loop(0, n // 2, body, 0)

    @pl.when(n % 2 == 1)
    def _():
        start(n - 1, 0)


def _wait_octets(n, ref, sem):
    for s in (128, 64, 32, 16, 8, 4, 2, 1):
        @pl.when((n & s) != 0)
        def _():
            d = ref.at[pl.ds(0, s * OCTET), :]
            pltpu.make_async_copy(d, d, sem).wait()


def _dispatch_kernel(nq_ref, tail_ref, gq_ref, h_ref, idxt_ref, rankt_ref, tabt_ref, xs_hbm,
                     stage, zero8, sem, zsem):
    m = pl.program_id(0)
    eid = lax.broadcasted_iota(I32, (N_EXPERTS, ROUTE_ROWS), 0)
    pos = lax.broadcasted_iota(I32, (STAGE_ROWS, ROUTE_ROWS), 0).astype(F32)
    sel = None
    for k in range(TOP_K):
        hot = eid == idxt_ref[k:k + 1, :]
        lpos = (jnp.sum(jnp.where(hot, tabt_ref[...], 0.0), axis=0, keepdims=True)
                + rankt_ref[k:k + 1, :].astype(F32))
        hit = pos == lpos
        sel = hit if sel is None else jnp.logical_or(sel, hit)
    slot = m % 2
    stage[slot] = _dot(jnp.where(sel, 1.0, 0.0).astype(BF16), h_ref[...].astype(BF16))

    @pl.when(m == 0)
    def _():
        zero8[...] = jnp.zeros_like(zero8)

        def zstart(i, c):
            pltpu.make_async_copy(zero8, _octet(xs_hbm, tail_ref[0] + i), zsem).start()
            return c

        lax.fori_loop(0, tail_ref[1], zstart, 0)

    nq = nq_ref[m]
    _for_octets(nq, lambda q, pri: pltpu.make_async_copy(
        _octet(stage.at[slot], q), _octet(xs_hbm, gq_ref[0, 0, q]), sem.at[slot]).start(priority=pri))

    @pl.when(m > 0)
    def _():
        _wait_octets(nq_ref[jnp.maximum(m - 1, 0)], xs_hbm, sem.at[1 - slot])

    @pl.when(m == pl.num_programs(0) - 1)
    def _():
        _wait_octets(nq, xs_hbm, sem.at[slot])

        def zwait(i, c):
            pltpu.make_async_copy(zero8, _octet(xs_hbm, 0), zsem).wait()
            return c

        lax.fori_loop(0, tail_ref[1], zwait, 0)


def _dispatch(h, idxt, rankt, tabt, gq3, nq, tail, nrows):
    t, dm = h.shape
    grid_spec = pltpu.PrefetchScalarGridSpec(
        num_scalar_prefetch=2,
        grid=(t // ROUTE_ROWS,),
        in_specs=[pl.BlockSpec((1, 1, STAGE_OCTETS), lambda m, nq, tl: (m, 0, 0), memory_space=pltpu.SMEM),
                  pl.BlockSpec((ROUTE_ROWS, dm), lambda m, nq, tl: (m, 0)),
                  pl.BlockSpec((OCTET, ROUTE_ROWS), lambda m, nq, tl: (0, m)),
                  pl.BlockSpec((OCTET, ROUTE_ROWS), lambda m, nq, tl: (0, m)),
                  pl.BlockSpec((None, N_EXPERTS, ROUTE_ROWS), lambda m, nq, tl: (m, 0, 0))],
        out_specs=pl.BlockSpec(memory_space=pl.ANY),
        scratch_shapes=[pltpu.VMEM((2, STAGE_ROWS, dm), F32), pltpu.VMEM((OCTET, dm), F32),
                        pltpu.SemaphoreType.DMA((2,)), pltpu.SemaphoreType.DMA(())],
    )
    return pl.pallas_call(
        _dispatch_kernel,
        grid_spec=grid_spec,
        out_shape=jax.ShapeDtypeStruct((nrows, dm), F32),
        compiler_params=_params(("arbitrary",)),
        name="moe_dispatch",
    )(nq, tail, gq3, h, idxt, rankt, tabt)


def _moe_kernel(ib_ref, ie_ref, lo_ref, hi_ref, nxt_ref, slot_ref, xs_ref, bg_ref, bu_ref, bd_ref,
                wg_hbm, wu_hbm, wd_hbm, ys_ref, wf, wb, wsem):
    i = pl.program_id(0)
    prev = jnp.maximum(i - 1, 0)
    lo = lo_ref[i]
    hi = hi_ref[i]

    def weight_copies(e, s):
        return [pltpu.make_async_copy(w.at[e], wf.at[s, j], wsem.at[s])
                for j, w in enumerate((wg_hbm, wu_hbm, wd_hbm))]

    @pl.when(jnp.logical_and(lo >= 0, hi > lo))
    def _():
        @pl.when(jnp.logical_or(i == 0, ie_ref[i] != ie_ref[prev]))
        def _():
            s = slot_ref[i]

            @pl.when(i == 0)
            def _():
                for cp in weight_copies(ie_ref[i], s):
                    cp.start()

            for cp in weight_copies(ie_ref[i], s):
                cp.wait()
            for j in range(3):
                wb[j] = wf[s, j].astype(BF16)

            @pl.when(nxt_ref[i] >= 0)
            def _():
                for cp in weight_copies(nxt_ref[i], 1 - s):
                    cp.start()

        def expert(rows):
            x = xs_ref[rows, :].astype(BF16)
            g = jnp.minimum(_dot(x, wb[0]) + bg_ref[...], SWIGLU_LIMIT)
            u = jnp.clip(_dot(x, wb[1]) + bu_ref[...], -SWIGLU_LIMIT, SWIGLU_LIMIT)
            act = (u + 1.0) * (g * jax.nn.sigmoid(SWIGLU_ALPHA * g))
            return _dot(act.astype(BF16), wb[2]) + bd_ref[...]

        whole = jnp.logical_and(lo == 0, hi == MOE_ROWS)

        @pl.when(whole)
        def _():
            ys_ref[...] = expert(slice(None))

        @pl.when(jnp.logical_not(whole))
        def _():
            @pl.when(jnp.logical_or(i == 0, ib_ref[i] != ib_ref[prev]))
            def _():
                ys_ref[...] = jnp.zeros_like(ys_ref)

            for j in range(MOE_ROWS // MOE_SUB):
                rows = slice(j * MOE_SUB, (j + 1) * MOE_SUB)

                @pl.when(jnp.logical_and(lo < (j + 1) * MOE_SUB, hi > j * MOE_SUB))
                def _():
                    rid = lax.broadcasted_iota(I32, (MOE_SUB, ys_ref.shape[1]), 0) + j * MOE_SUB
                    mine = jnp.logical_and(rid >= lo, rid < hi)
                    ys_ref[rows, :] = jnp.where(mine, expert(rows), ys_ref[rows, :])

    @pl.when(lo < 0)
    def _():
        ys_ref[...] = jnp.zeros_like(ys_ref)


def _moe(xs, items, w_gate, b_gate, w_up, b_up, w_down, b_down):
    nrows = xs.shape[0]
    ne, dm, df = w_gate.shape
    assert dm == df
    nitems = items[0].shape[0]
    bspec = lambda c: pl.BlockSpec((None, 1, c), lambda i, ib, ie, *_: (ie[i], 0, 0))
    rspec = pl.BlockSpec((MOE_ROWS, dm), lambda i, ib, *_: (ib[i], 0))
    hbm = pl.BlockSpec(memory_space=pl.ANY)
    grid_spec = pltpu.PrefetchScalarGridSpec(
        num_scalar_prefetch=6,
        grid=(nitems,),
        in_specs=[rspec, bspec(df), bspec(df), bspec(dm), hbm, hbm, hbm],
        out_specs=rspec,
        scratch_shapes=[pltpu.VMEM((2, 3, dm, df), F32), pltpu.VMEM((3, dm, df), BF16),
                        pltpu.SemaphoreType.DMA((2,))],
    )
    return pl.pallas_call(
        _moe_kernel,
        grid_spec=grid_spec,
        out_shape=jax.ShapeDtypeStruct((nrows, dm), F32),
        compiler_params=_params(("arbitrary",)),
        name="moe_experts",
    )(*items, xs, b_gate.reshape(ne, 1, df), b_up.reshape(ne, 1, df), b_down.reshape(ne, 1, dm),
      w_gate, w_up, w_down)


def _work_items(counts, nrows):
    nblk = nrows // MOE_ROWS
    nitems = nblk + N_EXPERTS - 1
    ends = jnp.cumsum(counts)
    starts = ends - counts
    b0 = jnp.arange(nblk, dtype=I32)[:, None] * MOE_ROWS
    lo = jnp.maximum(starts[None, :], b0)
    hi = jnp.minimum(ends[None, :], b0 + MOE_ROWS)
    nonempty = (hi > lo).reshape(-1)
    csum = jnp.cumsum(nonempty.astype(I32))
    j = jnp.arange(nitems, dtype=I32)
    pos = jnp.sum(csum[None, :] <= j[:, None], axis=1).astype(I32)
    used = j < csum[-1]
    pos = jnp.where(used, pos, jnp.max(jnp.where(nonempty, jnp.arange(nonempty.shape[0], dtype=I32), 0)))
    ib = pos // N_EXPERTS
    ie = pos % N_EXPERTS
    ilo = jnp.where(used, lo.reshape(-1)[pos] - ib * MOE_ROWS, 0)
    ihi = jnp.where(used, hi.reshape(-1)[pos] - ib * MOE_ROWS, 0)
    spare_blk = ib + 1 + (j - csum[-1])
    fill = jnp.logical_and(jnp.logical_not(used), spare_blk < nblk)
    ib = jnp.where(used, ib, jnp.minimum(spare_blk, nblk - 1))
    ilo = jnp.where(fill, -1, ilo)
    ordinal = jnp.cumsum(jnp.concatenate([jnp.zeros((1,), I32), (ie[1:] != ie[:-1]).astype(I32)]))
    first_next = jnp.sum(ordinal[None, :] <= ordinal[:, None], axis=1)
    has_next = first_next < nitems
    nxt = jnp.where(has_next, ie[jnp.minimum(first_next, nitems - 1)], -1)
    return (ib.astype(I32), ie.astype(I32), ilo.astype(I32), ihi.astype(I32), nxt.astype(I32),
            (ordinal % 2).astype(I32))


def _combine_kernel(nq_ref, gq0, gq1, gq2, h_ref, idx_ref, rank_ref, gate_ref, tab_ref, g2_ref, b2_ref, ys_hbm,
                    o_ref, stage, sem):
    m = pl.program_id(0)
    last = pl.num_programs(0) - 1
    slot = m % COMBINE_SLOTS

    def start_fetch(gq_ref, tile, s):
        _for_octets(nq_ref[jnp.minimum(tile, last)], lambda q, pri: pltpu.make_async_copy(
            _octet(ys_hbm, gq_ref[0, 0, q]), _octet(stage.at[s], q), sem.at[s]).start(priority=pri))

    @pl.when(m == 0)
    def _():
        stage[...] = jnp.zeros_like(stage)
        start_fetch(gq0, 0, 0)

        @pl.when(last >= 1)
        def _():
            start_fetch(gq1, 1, 1)

    @pl.when(m + 2 <= last)
    def _():
        start_fetch(gq2, m + 2, (m + 2) % COMBINE_SLOTS)

    _wait_octets(nq_ref[m], stage.at[slot], sem.at[slot])

    lane_e = lax.broadcasted_iota(I32, idx_ref.shape, 1)
    lane_p = lax.broadcasted_iota(I32, (ROUTE_ROWS, STAGE_ROWS), 1).astype(F32)
    idx = idx_ref[...]
    rank = rank_ref[...].astype(F32)
    gates = gate_ref[...]
    wsel = jnp.zeros((ROUTE_ROWS, STAGE_ROWS), F32)
    for k in range(TOP_K):
        hot = lane_e == idx[:, k:k + 1]
        lpos = jnp.sum(jnp.where(hot, tab_ref[...], 0.0), axis=-1, keepdims=True) + rank[:, k:k + 1]
        wsel = jnp.where(lane_p == lpos, gates[:, k:k + 1], wsel)
    y = _dot(wsel.astype(BF16), stage[slot].astype(BF16))
    z = DN_ALPHA * h_ref[...] + y
    mu = jnp.mean(z, axis=-1, keepdims=True)
    zc = z - mu
    var = jnp.mean(zc * zc, axis=-1, keepdims=True)
    o_ref[...] = zc * lax.rsqrt(var + LN_EPS) * g2_ref[...] + b2_ref[...]


def _combine(h, ys, idx, rank, gates, tab, gq3, nq, g2, b2):
    t, dm = h.shape
    nt = t // ROUTE_ROWS
    rowblk = lambda w: pl.BlockSpec((ROUTE_ROWS, w), lambda m, nq: (m, 0))
    const = pl.BlockSpec((1, dm), lambda m, nq: (0, 0))
    qspec = lambda f: pl.BlockSpec((1, 1, STAGE_OCTETS), lambda m, nq: (f(m), 0, 0), memory_space=pltpu.SMEM)
    grid_spec = pltpu.PrefetchScalarGridSpec(
        num_scalar_prefetch=1,
        grid=(nt,),
        in_specs=[qspec(lambda m: m), qspec(lambda m: jnp.minimum(m + 1, nt - 1)),
                  qspec(lambda m: jnp.minimum(m + 2, nt - 1)),
                  rowblk(dm), rowblk(LANES), rowblk(LANES), rowblk(LANES),
                  pl.BlockSpec((None, 1, LANES), lambda m, nq: (m, 0, 0)), const, const,
                  pl.BlockSpec(memory_space=pl.ANY)],
        out_specs=rowblk(dm),
        scratch_shapes=[pltpu.VMEM((COMBINE_SLOTS, STAGE_ROWS, dm), F32),
                        pltpu.SemaphoreType.DMA((COMBINE_SLOTS,))],
    )
    return pl.pallas_call(
        _combine_kernel,
        grid_spec=grid_spec,
        out_shape=jax.ShapeDtypeStruct((t, dm), F32),
        compiler_params=_params(("arbitrary",)),
        name="combine_ln",
    )(nq, gq3, gq3, gq3, h, idx, rank, gates, tab, g2, b2, ys)


def _route_tables(runs, total, nrows):
    nt = runs.shape[0]
    nxt = jnp.concatenate([runs[1:], total[None, :]], axis=0)
    c8 = (nxt - runs + OCTET - 1) // OCTET
    q_end = jnp.cumsum(c8, axis=1)
    q0 = q_end - c8
    nq = q_end[:, -1]
    per_expert = jnp.sum(c8, axis=0)
    start8 = jnp.cumsum(per_expert) - per_expert
    g8 = start8[None, :] + jnp.cumsum(c8, axis=0) - c8
    tab = OCTET * q0 - runs
    q = jnp.arange(STAGE_OCTETS, dtype=I32)
    e_of_q = jnp.minimum(jnp.sum(q[None, :, None] >= q_end[:, None, :], axis=2), N_EXPERTS - 1)
    pick = e_of_q[:, :, None] == jnp.arange(N_EXPERTS, dtype=I32)[None, None, :]
    gq = jnp.sum(jnp.where(pick, (g8 - q0)[:, None, :], 0), axis=2) + q[None, :]
    gq = jnp.where(q[None, :] < nq[:, None], gq, 0).astype(I32)
    tot8 = jnp.sum(per_expert)
    tail = jnp.stack([tot8, nrows // OCTET - tot8]).astype(I32)
    return tab, gq.reshape(nt, 1, STAGE_OCTETS), nq.astype(I32), tail, (per_expert * OCTET).astype(I32)


def _layer(x, w_in, sgu_w, sgu_b, sgu_ln_g, sgu_ln_b, mix_norm_g, w_out, ln1_g, ln1_b,
           w_router, b_router, w_gate, b_gate, w_up, b_up, w_down, b_down, ln2_g, ln2_b):
    batch, seq, dm = x.shape
    width = dm // 2
    t = batch * seq
    assert seq % ATTN_SPAN == 0 and dm % (2 * LANES) == 0 and w_router.shape[-1] == N_EXPERTS
    x2 = x.reshape(t, dm)

    wc = jnp.tril(sgu_w)
    wpair = jnp.concatenate([wc[0::2], wc[1::2]], axis=-1).astype(BF16)
    sbias = jnp.repeat(sgu_b.T, HEAD_DIM, axis=1)
    grp = jnp.arange(width) // HEAD_DIM
    gmat = jnp.where(grp[:, None] == grp[None, :], 1.0 / HEAD_DIM, 0.0).astype(BF16)
    row = lambda v: v.reshape(1, -1)

    a_n, q1, k1, v1, q4, k4, v4, q16, k16, v16 = _project(
        x2, w_in.astype(BF16), gmat, wpair, sbias, row(sgu_ln_g), row(sgu_ln_b),
        row(mix_norm_g[:width]), batch, seq)
    b = _attention((q1, k1, v1), (q4, k4, v4), (q16, k16, v16), batch, seq)

    wr = jnp.pad(w_router, ((0, 0), (0, LANES - N_EXPERTS))).astype(BF16)
    br = jnp.concatenate([b_router.astype(F32), jnp.full((LANES - N_EXPERTS,), -1e30, F32)]).reshape(1, LANES)
    ti = jnp.arange(PROJ_ROWS)
    tri = (ti[None, :] < ti[:, None]).astype(BF16)
    h, idx, gates, rank, idxt, rankt, runs, cnt = _outproj(
        x2, a_n.reshape(t, width), b.reshape(t, width), w_out.astype(BF16),
        row(mix_norm_g[width:]), row(ln1_g), row(ln1_b), wr, br, tri)

    nt = t // ROUTE_ROWS
    nrows = (t * TOP_K + nt * N_EXPERTS * (OCTET - 1) + MOE_ROWS - 1) // MOE_ROWS * MOE_ROWS
    runs = runs.reshape(nt, LANES)[:, :N_EXPERTS].astype(I32)
    total = cnt[0, :N_EXPERTS].astype(I32)
    tab, gq3, nq, tail, rows_e = _route_tables(runs, total, nrows)
    tabf = tab.astype(F32)
    tab_lane = jnp.pad(tabf, ((0, 0), (0, LANES - N_EXPERTS))).reshape(nt, 1, LANES)
    tab_sub = jnp.broadcast_to(tabf[:, :, None], (nt, N_EXPERTS, ROUTE_ROWS))

    xs = _dispatch(h, idxt, rankt, tab_sub, gq3, nq, tail, nrows)
    ys = _moe(xs, _work_items(rows_e, nrows), w_gate, b_gate, w_up, b_up, w_down, b_down)
    out = _combine(h, ys, idx, rank, gates, tab_lane, gq3, nq, row(ln2_g), row(ln2_b))
    return out.reshape(batch, seq, dm)


def kernel(x, w_in, sgu_w, sgu_b, sgu_ln_g, sgu_ln_b, mix_norm_g, w_out, ln1_g, ln1_b, w_router, b_router,
           w_gate, b_gate, w_up, b_up, w_down, b_down, ln2_g, ln2_b):
    assert w_in.shape[0] == DEPTH
    return _layer(x, w_in[0], sgu_w[0], sgu_b[0], sgu_ln_g[0], sgu_ln_b[0], mix_norm_g[0], w_out[0],
                  ln1_g[0], ln1_b[0], w_router[0], b_router[0], w_gate[0], b_gate[0], w_up[0], b_up[0],
                  w_down[0], b_down[0], ln2_g[0], ln2_b[0])
```

```python
import jax
import jax.numpy as jnp
from jax import lax
from jax.experimental import pallas as pl
from jax.experimental.pallas import tpu as pltpu

F32 = jnp.float32
BF16 = jnp.bfloat16
I32 = jnp.int32

LN_EPS = 1e-5
HEAD_DIM = 64
SGU_CHUNK = 128
BAND_BLOCK = 128
DILATIONS = (1, 4, 16)
ATTN_SPAN = BAND_BLOCK * DILATIONS[-1]
N_EXPERTS = 32
TOP_K = 4
SWIGLU_ALPHA = 1.702
SWIGLU_LIMIT = 7.0
DEPTH = 1
DN_ALPHA = (2 * DEPTH) ** 0.25
LOG2E = 1.4426950408889634

LANES = 128
PROJ_ROWS = 512
MOE_ROWS = 512
MOE_SUB = 256
ROUTE_ROWS = 256
OCTET = 8
STAGE_ROWS = ROUTE_ROWS * TOP_K + N_EXPERTS * OCTET
STAGE_OCTETS = STAGE_ROWS // OCTET
COMBINE_SLOTS = 3
VMEM_LIMIT = 56 * 1024 * 1024


def _params(sem):
    return pltpu.CompilerParams(dimension_semantics=sem, vmem_limit_bytes=VMEM_LIMIT)


def _dot(a, b):
    return jnp.dot(a, b, preferred_element_type=F32)


def _proj_kernel(x_ref, w_ref, gmat_ref, wpair_ref, sbias_ref, lng_ref, lnb_ref, mg_ref,
                 a_ref, q1_ref, k1_ref, v1_ref, q4_ref, k4_ref, v4_ref, q16_ref, k16_ref, v16_ref,
                 a_scr, t_scr, t4_scr):
    width = a_ref.shape[-1]
    xb = x_ref[...].astype(BF16)

    def proj(c):
        return _dot(xb, w_ref[:, c * width:(c + 1) * width])

    u = jax.nn.gelu(proj(0))
    v = jax.nn.gelu(proj(1))
    mean = _dot(v.astype(BF16), gmat_ref[...])
    d = v - mean
    var = _dot((d * d).astype(BF16), gmat_ref[...])
    vn = (d * lax.rsqrt(var + LN_EPS) * lng_ref[...] + lnb_ref[...]).astype(BF16)

    lane = lax.broadcasted_iota(I32, (SGU_CHUNK, LANES), 1)
    low = lane < HEAD_DIM
    zero = jnp.zeros((SGU_CHUNK, LANES), BF16)
    for c in range(PROJ_ROWS // SGU_CHUNK):
        rows = slice(c * SGU_CHUNK, (c + 1) * SGU_CHUNK)
        for j in range(width // LANES):
            cols = slice(j * LANES, (j + 1) * LANES)
            vp = vn[rows, cols]
            rhs = jnp.concatenate([jnp.where(low, vp, zero), jnp.where(low, zero, vp)], axis=0)
            gate = _dot(wpair_ref[j], rhs) + sbias_ref[:, cols]
            a_scr[rows, cols] = u[rows, cols] * gate
    a = a_scr[...]
    ms = jnp.mean(a * a, axis=-1, keepdims=True)
    a_ref[...] = (a * lax.rsqrt(ms + LN_EPS) * mg_ref[...]).astype(BF16)

    outs = ((q1_ref, q4_ref, q16_ref), (k1_ref, k4_ref, k16_ref), (v1_ref, v4_ref, v16_ref))
    for c, (o1, o4, o16) in enumerate(outs):
        t = proj(2 + c)
        if c == 0:
            t = t * (HEAD_DIM ** -0.5 * LOG2E)
        o1[...] = t.astype(BF16)
        for j in range(width // LANES):
            cols = slice(j * LANES, (j + 1) * LANES)
            t_scr[j] = t[:, cols]
            for b in range(4):
                t4 = t_scr[j, pl.ds(b, PROJ_ROWS // 4, stride=4), :]
                o4[b, :, cols] = t4.astype(BF16)
                t4_scr[b] = t4
                for a in range(4):
                    o16[4 * a + b, :, cols] = t4_scr[b, pl.ds(a, PROJ_ROWS // 16, stride=4), :].astype(BF16)


def _project(x2, w_in_b, gmat, wpair, sbias, lng, lnb, mg, batch, seq):
    t, dm = x2.shape
    width = dm // 2
    nt = seq // PROJ_ROWS
    per_span = ATTN_SPAN // PROJ_ROWS
    const = lambda *shape: pl.BlockSpec(shape, lambda b, m: (0,) * len(shape))
    o1 = jax.ShapeDtypeStruct((batch, seq, width), BF16)
    o4 = jax.ShapeDtypeStruct((batch, seq // 512, 4, BAND_BLOCK, width), BF16)
    o16 = jax.ShapeDtypeStruct((batch, seq // ATTN_SPAN, 16, BAND_BLOCK, width), BF16)
    s1 = pl.BlockSpec((None, PROJ_ROWS, width), lambda b, m: (b, m, 0))
    s4 = pl.BlockSpec((None, None, 4, BAND_BLOCK, width), lambda b, m: (b, m, 0, 0, 0))
    s16 = pl.BlockSpec((None, None, 16, PROJ_ROWS // 16, width),
                       lambda b, m: (b, m // per_span, 0, m % per_span, 0))
    return pl.pallas_call(
        _proj_kernel,
        grid=(batch, nt),
        in_specs=[pl.BlockSpec((PROJ_ROWS, dm), lambda b, m: (b * nt + m, 0)),
                  const(*w_in_b.shape), const(*gmat.shape), const(*wpair.shape), const(*sbias.shape),
                  const(1, width), const(1, width), const(1, width)],
        out_specs=[s1] + [s1, s1, s1] + [s4, s4, s4] + [s16, s16, s16],
        out_shape=[o1] + [o1, o1, o1] + [o4, o4, o4] + [o16, o16, o16],
        scratch_shapes=[pltpu.VMEM((PROJ_ROWS, width), F32),
                        pltpu.VMEM((width // LANES, PROJ_ROWS, LANES), F32),
                        pltpu.VMEM((4, PROJ_ROWS // 4, LANES), F32)],
        compiler_params=_params(("parallel", "parallel")),
        name="proj_sgu",
    )(x2, w_in_b, gmat, wpair, sbias, lng, lnb, mg)


def _attn_kernel(q1, k1, v1, kp1, vp1, q4, k4, v4, kp4, vp4, q16, k16, v16, kp16, vp16,
                 o_ref, acc_o, acc_m, acc_l):
    not_first = pl.program_id(1) > 0
    blk = BAND_BLOCK
    row = lax.broadcasted_iota(I32, (2 * blk, 2 * blk), 0) % blk
    col = lax.broadcasted_iota(I32, (2 * blk, 2 * blk), 1)
    band = jnp.logical_or(jnp.logical_and(col < blk, col >= row), jnp.logical_and(col >= blk, col - blk <= row))
    neg = jnp.where(not_first, 0.0, -jnp.inf).astype(F32)
    first_pen = jnp.where(col < blk, neg, 0.0)
    low = lax.broadcasted_iota(I32, (blk, LANES), 1) < HEAD_DIM
    zero = jnp.zeros((blk, LANES), BF16)
    ones = jnp.ones((2 * blk, LANES), BF16)

    def unit(q, kprev, kcur, vprev, vcur, maybe_first, rows, init):
        kk = jnp.concatenate([kprev, kcur], axis=0)
        vv = jnp.concatenate([vprev, vcur], axis=0)
        qq = jnp.concatenate([jnp.where(low, q, zero), jnp.where(low, zero, q)], axis=0)
        s = lax.dot_general(qq, kk, (((1,), (1,)), ((), ())), preferred_element_type=F32)
        s = jnp.where(band, s, -jnp.inf)
        if maybe_first:
            s = s + first_pen
        m2 = jnp.max(s, axis=-1, keepdims=True)
        p = jnp.exp2(s - m2)
        o2 = _dot(p.astype(BF16), jnp.concatenate([vv, ones], axis=1))
        o = jnp.where(low, o2[:blk, :LANES], o2[blk:, :LANES])
        m = jnp.where(low, m2[:blk], m2[blk:])
        l = jnp.where(low, o2[:blk, LANES:], o2[blk:, LANES:])
        if init:
            acc_o[rows, :] = o
            acc_m[rows, :] = m
            acc_l[rows, :] = l
        else:
            mo = acc_m[rows, :]
            mn = jnp.maximum(mo, m)
            so = jnp.exp2(mo - mn)
            sn = jnp.exp2(m - mn)
            acc_o[rows, :] = acc_o[rows, :] * so + o * sn
            acc_l[rows, :] = acc_l[rows, :] * so + l * sn
            acc_m[rows, :] = mn


    for n in range(ATTN_SPAN // blk):
        cur = pl.ds(n * blk, blk)
        if n == 0:
            unit(q1[cur, :], kp1[...], k1[cur, :], vp1[...], v1[cur, :], True, cur, True)
        else:
            prev = pl.ds((n - 1) * blk, blk)
            unit(q1[cur, :], k1[prev, :], k1[cur, :], v1[prev, :], v1[cur, :], False, cur, True)

    for s in range(ATTN_SPAN // (4 * blk)):
        for r in range(4):
            rows = pl.ds(s * (4 * blk) + r, blk, stride=4)
            if s == 0:
                unit(q4[0, r], kp4[r], k4[0, r], vp4[r], v4[0, r], True, rows, False)
            else:
                unit(q4[s, r], k4[s - 1, r], k4[s, r], v4[s - 1, r], v4[s, r], False, rows, False)

    for r in range(16):
        unit(q16[r], kp16[r], k16[r], vp16[r], v16[r], True, pl.ds(r, blk, stride=16), False)

    o_ref[...] = (acc_o[...] / acc_l[...]).astype(BF16)


def _attention(qkv1, qkv4, qkv16, batch, seq):
    width = qkv1[0].shape[-1]
    nspan = seq // ATTN_SPAN
    blk = BAND_BLOCK
    n1 = ATTN_SPAN // blk
    n4 = ATTN_SPAN // (4 * blk)

    cur1 = pl.BlockSpec((None, ATTN_SPAN, LANES), lambda b, i, p: (b, i, p))
    prev1 = pl.BlockSpec((None, None, blk, LANES), lambda b, i, p: (b, jnp.maximum(i * n1 - 1, 0), 0, p))
    cur4 = pl.BlockSpec((None, n4, 4, blk, LANES), lambda b, i, p: (b, i, 0, 0, p))
    prev4 = pl.BlockSpec((None, None, 4, blk, LANES), lambda b, i, p: (b, jnp.maximum(i * n4 - 1, 0), 0, 0, p))
    cur16 = pl.BlockSpec((None, None, 16, blk, LANES), lambda b, i, p: (b, i, 0, 0, p))
    prev16 = pl.BlockSpec((None, None, 16, blk, LANES), lambda b, i, p: (b, jnp.maximum(i - 1, 0), 0, 0, p))

    q1, k1, v1 = qkv1
    q4, k4, v4 = qkv4
    q16, k16, v16 = qkv16
    k1b = k1.reshape(batch, seq // blk, blk, width)
    v1b = v1.reshape(batch, seq // blk, blk, width)
    return pl.pallas_call(
        _attn_kernel,
        grid=(batch, nspan, width // LANES),
        in_specs=[cur1, cur1, cur1, prev1, prev1,
                  cur4, cur4, cur4, prev4, prev4,
                  cur16, cur16, cur16, prev16, prev16],
        out_specs=pl.BlockSpec((None, ATTN_SPAN, LANES), lambda b, i, p: (b, i, p)),
        out_shape=jax.ShapeDtypeStruct((batch, seq, width), BF16),
        scratch_shapes=[pltpu.VMEM((ATTN_SPAN, LANES), F32)] * 3,
        compiler_params=_params(("parallel", "parallel", "parallel")),
        name="dilated_attn",
    )(q1, k1, v1, k1b, v1b, q4, k4, v4, k4, v4, q16, k16, v16, k16, v16)


def _outproj_kernel(x_ref, a_ref, b_ref, wo_ref, mgb_ref, g1_ref, b1_ref, wr_ref, br_ref, tri_ref,
                    h_ref, idx_ref, gate_ref, rank_ref, idxt_ref, rankt_ref, runs_ref, cnt_ref, run_scr):
    @pl.when(pl.program_id(0) == 0)
    def _():
        run_scr[...] = jnp.zeros_like(run_scr)

    width = a_ref.shape[-1]
    bf = b_ref[...].astype(F32)
    bn = (bf * lax.rsqrt(jnp.mean(bf * bf, axis=-1, keepdims=True) + LN_EPS) * mgb_ref[...]).astype(BF16)
    mixed = _dot(a_ref[...], wo_ref[0:width, :]) + _dot(bn, wo_ref[width:2 * width, :])
    z = DN_ALPHA * x_ref[...] + mixed
    mu = jnp.mean(z, axis=-1, keepdims=True)
    zc = z - mu
    var = jnp.mean(zc * zc, axis=-1, keepdims=True)
    h = zc * lax.rsqrt(var + LN_EPS) * g1_ref[...] + b1_ref[...]
    h_ref[...] = h

    logits = _dot(h.astype(BF16), wr_ref[...]) + br_ref[...]
    lane = lax.broadcasted_iota(I32, logits.shape, 1).astype(F32)
    work = logits
    vals, idxs, hots = [], [], []
    for _ in range(TOP_K):
        mv = jnp.max(work, axis=-1, keepdims=True)
        ix = jnp.min(jnp.where(work == mv, lane, float(LANES)), axis=-1, keepdims=True)
        hot = lane == ix
        work = jnp.where(hot, -jnp.inf, work)
        vals.append(mv)
        idxs.append(ix)
        hots.append(hot)
    exps = [jnp.exp(v - vals[0]) for v in vals]
    den = exps[0] + exps[1] + exps[2] + exps[3]
    gates = [e / den for e in exps]

    member = jnp.zeros(logits.shape, F32)
    for hot in hots:
        member = jnp.where(hot, 1.0, member)
    before = _dot(tri_ref[...], member.astype(BF16)) + run_scr[...]
    ranks = [jnp.sum(jnp.where(hot, before, 0.0), axis=-1, keepdims=True) for hot in hots]
    runs_ref[0] = run_scr[...]
    for j in range(1, PROJ_ROWS // ROUTE_ROWS):
        runs_ref[j] = run_scr[...] + jnp.sum(member[:j * ROUTE_ROWS], axis=0, keepdims=True)
    run_scr[...] = run_scr[...] + jnp.sum(member, axis=0, keepdims=True)
    cnt_ref[...] = run_scr[...]

    def spread(cols):
        out = jnp.zeros(logits.shape, F32)
        for k, cval in enumerate(cols):
            out = jnp.where(lane == float(k), cval, out)
        return out

    idx_all = spread(idxs)
    rank_all = spread(ranks)
    idx_ref[...] = idx_all.astype(I32)
    gate_ref[...] = spread(gates)
    rank_ref[...] = rank_all.astype(I32)
    idxt_ref[...] = idx_all.T[:OCTET].astype(I32)
    rankt_ref[...] = rank_all.T[:OCTET].astype(I32)


def _outproj(x2, a_n, b2, w_out_b, mgb, g1, b1, wr, br, tri):
    t, dm = x2.shape
    width = dm // 2
    const = lambda *shape: pl.BlockSpec(shape, lambda m: (0,) * len(shape))
    rowblk = lambda w: pl.BlockSpec((PROJ_ROWS, w), lambda m: (m, 0))
    colblk = pl.BlockSpec((OCTET, PROJ_ROWS), lambda m: (0, m))
    sub = PROJ_ROWS // ROUTE_ROWS
    return pl.pallas_call(
        _outproj_kernel,
        grid=(t // PROJ_ROWS,),
        in_specs=[rowblk(dm), rowblk(width), rowblk(width), const(dm, dm), const(1, width),
                  const(1, dm), const(1, dm), const(dm, LANES), const(1, LANES),
                  const(PROJ_ROWS, PROJ_ROWS)],
        out_specs=[rowblk(dm), rowblk(LANES), rowblk(LANES), rowblk(LANES), colblk, colblk,
                   pl.BlockSpec((None, sub, 1, LANES), lambda m: (m, 0, 0, 0)), const(1, LANES)],
        out_shape=[jax.ShapeDtypeStruct((t, dm), F32), jax.ShapeDtypeStruct((t, LANES), I32),
                   jax.ShapeDtypeStruct((t, LANES), F32), jax.ShapeDtypeStruct((t, LANES), I32),
                   jax.ShapeDtypeStruct((OCTET, t), I32), jax.ShapeDtypeStruct((OCTET, t), I32),
                   jax.ShapeDtypeStruct((t // PROJ_ROWS, sub, 1, LANES), F32),
                   jax.ShapeDtypeStruct((1, LANES), F32)],
        scratch_shapes=[pltpu.VMEM((1, LANES), F32)],
        compiler_params=_params(("arbitrary",)),
        name="outproj_router",
    )(x2, a_n, b2, w_out_b, mgb, g1, b1, wr, br, tri)


def _octet(ref, q):
    return ref.at[pl.ds(pl.multiple_of(q * OCTET, OCTET), OCTET), :]


def _for_octets(n, start):
    def body(i, c):
        for j in range(4):
            start(4 * i + j, j % 2)
        return c

    lax.fori_loop(0, n // 4, body, 0)
    done = (n // 4) * 4

    @pl.when((n & 2) != 0)
    def _():
        start(done, 0)
        start(done + 1, 1)

    @pl.when((n & 1) != 0)
    def _():
        start(n - 1, 0)


def _wait_octets(n, ref, sem):
    for s in (128, 64, 32, 16, 8, 4, 2, 1):
        @pl.when((n & s) != 0)
        def _():
            d = ref.at[pl.ds(0, s * OCTET), :]
            pltpu.make_async_copy(d, d, sem).wait()


def _dispatch_kernel(nq_ref, fstart_ref, fcount_ref, gq_ref, h_ref, idxt_ref, rankt_ref, tabt_ref, xs_hbm,
                     stage, zero8, sem, zsem):
    m = pl.program_id(0)
    eid = lax.broadcasted_iota(I32, (N_EXPERTS, ROUTE_ROWS), 0)
    pos = lax.broadcasted_iota(I32, (STAGE_ROWS, ROUTE_ROWS), 0).astype(F32)
    sel = None
    for k in range(TOP_K):
        hot = eid == idxt_ref[k:k + 1, :]
        lpos = (jnp.sum(jnp.where(hot, tabt_ref[...], 0.0), axis=0, keepdims=True)
                + rankt_ref[k:k + 1, :].astype(F32))
        hit = pos == lpos
        sel = hit if sel is None else jnp.logical_or(sel, hit)
    slot = m % 2
    stage[slot] = _dot(jnp.where(sel, 1.0, 0.0).astype(BF16), h_ref[...].astype(BF16))

    @pl.when(m == 0)
    def _():
        zero8[...] = jnp.zeros_like(zero8)
        for g in range(N_EXPERTS + 1):
            def zstart(i, c, g=g):
                pltpu.make_async_copy(zero8, _octet(xs_hbm, fstart_ref[g] + i), zsem).start()
                return c

            lax.fori_loop(0, fcount_ref[g], zstart, 0)

    nq = nq_ref[m]
    _for_octets(nq, lambda q, pri: pltpu.make_async_copy(
        _octet(stage.at[slot], q), _octet(xs_hbm, gq_ref[0, 0, q]), sem.at[slot]).start(priority=pri))

    @pl.when(m > 0)
    def _():
        _wait_octets(nq_ref[jnp.maximum(m - 1, 0)], xs_hbm, sem.at[1 - slot])

    @pl.when(m == pl.num_programs(0) - 1)
    def _():
        _wait_octets(nq, xs_hbm, sem.at[slot])

        def zwait(i, c):
            pltpu.make_async_copy(zero8, _octet(xs_hbm, 0), zsem).wait()
            return c

        for g in range(N_EXPERTS + 1):
            lax.fori_loop(0, fcount_ref[g], zwait, 0)


def _dispatch(h, idxt, rankt, tabt, gq3, nq, fill_start, fill_count, nrows):
    t, dm = h.shape
    grid_spec = pltpu.PrefetchScalarGridSpec(
        num_scalar_prefetch=3,
        grid=(t // ROUTE_ROWS,),
        in_specs=[pl.BlockSpec((1, 1, STAGE_OCTETS), lambda m, *_: (m, 0, 0), memory_space=pltpu.SMEM),
                  pl.BlockSpec((ROUTE_ROWS, dm), lambda m, *_: (m, 0)),
                  pl.BlockSpec((OCTET, ROUTE_ROWS), lambda m, *_: (0, m)),
                  pl.BlockSpec((OCTET, ROUTE_ROWS), lambda m, *_: (0, m)),
                  pl.BlockSpec((None, N_EXPERTS, ROUTE_ROWS), lambda m, *_: (m, 0, 0))],
        out_specs=pl.BlockSpec(memory_space=pl.ANY),
        scratch_shapes=[pltpu.VMEM((2, STAGE_ROWS, dm), F32), pltpu.VMEM((OCTET, dm), F32),
                        pltpu.SemaphoreType.DMA((2,)), pltpu.SemaphoreType.DMA(())],
    )
    return pl.pallas_call(
        _dispatch_kernel,
        grid_spec=grid_spec,
        out_shape=jax.ShapeDtypeStruct((nrows, dm), F32),
        compiler_params=_params(("arbitrary",)),
        name="moe_dispatch",
    )(nq, fill_start, fill_count, gq3, h, idxt, rankt, tabt)


def _moe_kernel(ib_ref, ie_ref, lo_ref, hi_ref, nxt_ref, slot_ref, xs_ref, bg_ref, bu_ref, bd_ref,
                wg_hbm, wu_hbm, wd_hbm, ys_ref, wf, wb, wsem):
    i = pl.program_id(0)
    prev = jnp.maximum(i - 1, 0)
    lo = lo_ref[i]
    hi = hi_ref[i]

    def weight_copies(e, s):
        return [pltpu.make_async_copy(w.at[e], wf.at[s, j], wsem.at[s])
                for j, w in enumerate((wg_hbm, wu_hbm, wd_hbm))]

    @pl.when(jnp.logical_and(lo >= 0, hi > lo))
    def _():
        @pl.when(jnp.logical_or(i == 0, ie_ref[i] != ie_ref[prev]))
        def _():
            s = slot_ref[i]

            @pl.when(i == 0)
            def _():
                for cp in weight_copies(ie_ref[i], s):
                    cp.start()

            for cp in weight_copies(ie_ref[i], s):
                cp.wait()
            for j in range(3):
                wb[j] = wf[s, j].astype(BF16)

            @pl.when(nxt_ref[i] >= 0)
            def _():
                for cp in weight_copies(nxt_ref[i], 1 - s):
                    cp.start()

        def expert(rows):
            x = xs_ref[rows, :].astype(BF16)
            g = jnp.minimum(_dot(x, wb[0]) + bg_ref[...], SWIGLU_LIMIT)
            u = jnp.clip(_dot(x, wb[1]) + bu_ref[...], -SWIGLU_LIMIT, SWIGLU_LIMIT)
            act = (u + 1.0) * (g * jax.nn.sigmoid(SWIGLU_ALPHA * g))
            return _dot(act.astype(BF16), wb[2]) + bd_ref[...]

        whole = jnp.logical_and(lo == 0, hi == MOE_ROWS)

        @pl.when(whole)
        def _():
            ys_ref[...] = expert(slice(None))

        @pl.when(jnp.logical_not(whole))
        def _():
            @pl.when(jnp.logical_or(i == 0, ib_ref[i] != ib_ref[prev]))
            def _():
                ys_ref[...] = jnp.zeros_like(ys_ref)

            for j in range(MOE_ROWS // MOE_SUB):
                rows = slice(j * MOE_SUB, (j + 1) * MOE_SUB)

                @pl.when(jnp.logical_and(lo < (j + 1) * MOE_SUB, hi > j * MOE_SUB))
                def _():
                    ys_ref[rows, :] = expert(rows)

    @pl.when(lo < 0)
    def _():
        ys_ref[...] = jnp.zeros_like(ys_ref)


def _moe(xs, items, w_gate, b_gate, w_up, b_up, w_down, b_down):
    nrows = xs.shape[0]
    ne, dm, df = w_gate.shape
    assert dm == df
    nitems = items[0].shape[0]
    bspec = lambda c: pl.BlockSpec((None, 1, c), lambda i, ib, ie, *_: (ie[i], 0, 0))
    rspec = pl.BlockSpec((MOE_ROWS, dm), lambda i, ib, *_: (ib[i], 0))
    hbm = pl.BlockSpec(memory_space=pl.ANY)
    grid_spec = pltpu.PrefetchScalarGridSpec(
        num_scalar_prefetch=6,
        grid=(nitems,),
        in_specs=[rspec, bspec(df), bspec(df), bspec(dm), hbm, hbm, hbm],
        out_specs=rspec,
        scratch_shapes=[pltpu.VMEM((2, 3, dm, df), F32), pltpu.VMEM((3, dm, df), BF16),
                        pltpu.SemaphoreType.DMA((2,))],
    )
    return pl.pallas_call(
        _moe_kernel,
        grid_spec=grid_spec,
        out_shape=jax.ShapeDtypeStruct((nrows, dm), F32),
        compiler_params=_params(("arbitrary",)),
        name="moe_experts",
    )(*items, xs, b_gate.reshape(ne, 1, df), b_up.reshape(ne, 1, df), b_down.reshape(ne, 1, dm),
      w_gate, w_up, w_down)


def _work_items(counts, nrows):
    nblk = nrows // MOE_ROWS
    nitems = nblk + N_EXPERTS - 1
    ends = jnp.cumsum(counts)
    starts = ends - counts
    b0 = jnp.arange(nblk, dtype=I32)[:, None] * MOE_ROWS
    lo = jnp.maximum(starts[None, :], b0)
    hi = jnp.minimum(ends[None, :], b0 + MOE_ROWS)
    nonempty = (hi > lo).reshape(-1)
    csum = jnp.cumsum(nonempty.astype(I32))
    j = jnp.arange(nitems, dtype=I32)
    pos = jnp.sum(csum[None, :] <= j[:, None], axis=1).astype(I32)
    used = j < csum[-1]
    pos = jnp.where(used, pos, jnp.max(jnp.where(nonempty, jnp.arange(nonempty.shape[0], dtype=I32), 0)))
    ib = pos // N_EXPERTS
    ie = pos % N_EXPERTS
    ilo = jnp.where(used, lo.reshape(-1)[pos] - ib * MOE_ROWS, 0)
    ihi = jnp.where(used, hi.reshape(-1)[pos] - ib * MOE_ROWS, 0)
    spare_blk = ib + 1 + (j - csum[-1])
    fill = jnp.logical_and(jnp.logical_not(used), spare_blk < nblk)
    ib = jnp.where(used, ib, jnp.minimum(spare_blk, nblk - 1))
    ilo = jnp.where(fill, -1, ilo)
    ordinal = jnp.cumsum(jnp.concatenate([jnp.zeros((1,), I32), (ie[1:] != ie[:-1]).astype(I32)]))
    first_next = jnp.sum(ordinal[None, :] <= ordinal[:, None], axis=1)
    has_next = first_next < nitems
    nxt = jnp.where(has_next, ie[jnp.minimum(first_next, nitems - 1)], -1)
    return (ib.astype(I32), ie.astype(I32), ilo.astype(I32), ihi.astype(I32), nxt.astype(I32),
            (ordinal % 2).astype(I32))


def _combine_kernel(nq_ref, gq0, gq1, gq2, h_ref, idx_ref, rank_ref, gate_ref, tab_ref, g2_ref, b2_ref, ys_hbm,
                    o_ref, stage, sem):
    m = pl.program_id(0)
    last = pl.num_programs(0) - 1
    slot = m % COMBINE_SLOTS

    def start_fetch(gq_ref, tile, s):
        _for_octets(nq_ref[jnp.minimum(tile, last)], lambda q, pri: pltpu.make_async_copy(
            _octet(ys_hbm, gq_ref[0, 0, q]), _octet(stage.at[s], q), sem.at[s]).start(priority=pri))

    @pl.when(m == 0)
    def _():
        stage[...] = jnp.zeros_like(stage)
        start_fetch(gq0, 0, 0)

        @pl.when(last >= 1)
        def _():
            start_fetch(gq1, 1, 1)

    @pl.when(m + 2 <= last)
    def _():
        start_fetch(gq2, m + 2, (m + 2) % COMBINE_SLOTS)

    _wait_octets(nq_ref[m], stage.at[slot], sem.at[slot])

    lane_e = lax.broadcasted_iota(I32, idx_ref.shape, 1)
    lane_p = lax.broadcasted_iota(I32, (ROUTE_ROWS, STAGE_ROWS), 1).astype(F32)
    idx = idx_ref[...]
    rank = rank_ref[...].astype(F32)
    gates = gate_ref[...]
    wsel = jnp.zeros((ROUTE_ROWS, STAGE_ROWS), F32)
    for k in range(TOP_K):
        hot = lane_e == idx[:, k:k + 1]
        lpos = jnp.sum(jnp.where(hot, tab_ref[...], 0.0), axis=-1, keepdims=True) + rank[:, k:k + 1]
        wsel = jnp.where(lane_p == lpos, gates[:, k:k + 1], wsel)
    y = _dot(wsel.astype(BF16), stage[slot].astype(BF16))
    z = DN_ALPHA * h_ref[...] + y
    mu = jnp.mean(z, axis=-1, keepdims=True)
    zc = z - mu
    var = jnp.mean(zc * zc, axis=-1, keepdims=True)
    o_ref[...] = zc * lax.rsqrt(var + LN_EPS) * g2_ref[...] + b2_ref[...]


def _combine(h, ys, idx, rank, gates, tab, gq3, nq, g2, b2):
    t, dm = h.shape
    nt = t // ROUTE_ROWS
    rowblk = lambda w: pl.BlockSpec((ROUTE_ROWS, w), lambda m, nq: (m, 0))
    const = pl.BlockSpec((1, dm), lambda m, nq: (0, 0))
    qspec = lambda f: pl.BlockSpec((1, 1, STAGE_OCTETS), lambda m, nq: (f(m), 0, 0), memory_space=pltpu.SMEM)
    grid_spec = pltpu.PrefetchScalarGridSpec(
        num_scalar_prefetch=1,
        grid=(nt,),
        in_specs=[qspec(lambda m: m), qspec(lambda m: jnp.minimum(m + 1, nt - 1)),
                  qspec(lambda m: jnp.minimum(m + 2, nt - 1)),
                  rowblk(dm), rowblk(LANES), rowblk(LANES), rowblk(LANES),
                  pl.BlockSpec((None, 1, LANES), lambda m, nq: (m, 0, 0)), const, const,
                  pl.BlockSpec(memory_space=pl.ANY)],
        out_specs=rowblk(dm),
        scratch_shapes=[pltpu.VMEM((COMBINE_SLOTS, STAGE_ROWS, dm), F32),
                        pltpu.SemaphoreType.DMA((COMBINE_SLOTS,))],
    )
    return pl.pallas_call(
        _combine_kernel,
        grid_spec=grid_spec,
        out_shape=jax.ShapeDtypeStruct((t, dm), F32),
        compiler_params=_params(("arbitrary",)),
        name="combine_ln",
    )(nq, gq3, gq3, gq3, h, idx, rank, gates, tab, g2, b2, ys)


def _route_tables(runs, total, nrows):
    nt = runs.shape[0]
    nxt = jnp.concatenate([runs[1:], total[None, :]], axis=0)
    c8 = (nxt - runs + OCTET - 1) // OCTET
    q_end = jnp.cumsum(c8, axis=1)
    q0 = q_end - c8
    nq = q_end[:, -1]
    per_expert = jnp.sum(c8, axis=0)
    sub8 = MOE_SUB // OCTET
    per_pad = (per_expert + sub8 - 1) // sub8 * sub8
    start8 = jnp.cumsum(per_pad) - per_pad
    g8 = start8[None, :] + jnp.cumsum(c8, axis=0) - c8
    tab = OCTET * q0 - runs
    q = jnp.arange(STAGE_OCTETS, dtype=I32)
    e_of_q = jnp.minimum(jnp.sum(q[None, :, None] >= q_end[:, None, :], axis=2), N_EXPERTS - 1)
    pick = e_of_q[:, :, None] == jnp.arange(N_EXPERTS, dtype=I32)[None, None, :]
    gq = jnp.sum(jnp.where(pick, (g8 - q0)[:, None, :], 0), axis=2) + q[None, :]
    gq = jnp.where(q[None, :] < nq[:, None], gq, 0).astype(I32)
    tot8 = jnp.sum(per_pad)
    fill_start = jnp.concatenate([start8 + per_expert, tot8[None]]).astype(I32)
    fill_count = jnp.concatenate([per_pad - per_expert, (nrows // OCTET - tot8)[None]]).astype(I32)
    return (tab, gq.reshape(nt, 1, STAGE_OCTETS), nq.astype(I32), fill_start, fill_count,
            (per_pad * OCTET).astype(I32))


def _layer(x, w_in, sgu_w, sgu_b, sgu_ln_g, sgu_ln_b, mix_norm_g, w_out, ln1_g, ln1_b,
           w_router, b_router, w_gate, b_gate, w_up, b_up, w_down, b_down, ln2_g, ln2_b):
    batch, seq, dm = x.shape
    width = dm // 2
    t = batch * seq
    assert seq % ATTN_SPAN == 0 and dm % (2 * LANES) == 0 and w_router.shape[-1] == N_EXPERTS
    x2 = x.reshape(t, dm)

    wc = jnp.tril(sgu_w)
    wpair = jnp.concatenate([wc[0::2], wc[1::2]], axis=-1).astype(BF16)
    sbias = jnp.repeat(sgu_b.T, HEAD_DIM, axis=1)
    grp = jnp.arange(width) // HEAD_DIM
    gmat = jnp.where(grp[:, None] == grp[None, :], 1.0 / HEAD_DIM, 0.0).astype(BF16)
    row = lambda v: v.reshape(1, -1)

    a_n, q1, k1, v1, q4, k4, v4, q16, k16, v16 = _project(
        x2, w_in.astype(BF16), gmat, wpair, sbias, row(sgu_ln_g), row(sgu_ln_b),
        row(mix_norm_g[:width]), batch, seq)
    b = _attention((q1, k1, v1), (q4, k4, v4), (q16, k16, v16), batch, seq)

    wr = jnp.pad(w_router, ((0, 0), (0, LANES - N_EXPERTS))).astype(BF16)
    br = jnp.concatenate([b_router.astype(F32), jnp.full((LANES - N_EXPERTS,), -1e30, F32)]).reshape(1, LANES)
    ti = jnp.arange(PROJ_ROWS)
    tri = (ti[None, :] < ti[:, None]).astype(BF16)
    h, idx, gates, rank, idxt, rankt, runs, cnt = _outproj(
        x2, a_n.reshape(t, width), b.reshape(t, width), w_out.astype(BF16),
        row(mix_norm_g[width:]), row(ln1_g), row(ln1_b), wr, br, tri)

    nt = t // ROUTE_ROWS
    worst = t * TOP_K + nt * N_EXPERTS * (OCTET - 1) + N_EXPERTS * (MOE_SUB - OCTET)
    nrows = (worst + MOE_ROWS - 1) // MOE_ROWS * MOE_ROWS
    runs = runs.reshape(nt, LANES)[:, :N_EXPERTS].astype(I32)
    total = cnt[0, :N_EXPERTS].astype(I32)
    tab, gq3, nq, fill_start, fill_count, rows_e = _route_tables(runs, total, nrows)
    tabf = tab.astype(F32)
    tab_lane = jnp.pad(tabf, ((0, 0), (0, LANES - N_EXPERTS))).reshape(nt, 1, LANES)
    tab_sub = jnp.broadcast_to(tabf[:, :, None], (nt, N_EXPERTS, ROUTE_ROWS))

    xs = _dispatch(h, idxt, rankt, tab_sub, gq3, nq, fill_start, fill_count, nrows)
    ys = _moe(xs, _work_items(rows_e, nrows), w_gate, b_gate, w_up, b_up, w_down, b_down)
    out = _combine(h, ys, idx, rank, gates, tab_lane, gq3, nq, row(ln2_g), row(ln2_b))
    return out.reshape(batch, seq, dm)


def kernel(x, w_in, sgu_w, sgu_b, sgu_ln_g, sgu_ln_b, mix_norm_g, w_out, ln1_g, ln1_b, w_router, b_router,
           w_gate, b_gate, w_up, b_up, w_down, b_down, ln2_g, ln2_b):
    assert w_in.shape[0] == DEPTH
    return _layer(x, w_in[0], sgu_w[0], sgu_b[0], sgu_ln_g[0], sgu_ln_b[0], mix_norm_g[0], w_out[0],
                  ln1_g[0], ln1_b[0], w_router[0], b_router[0], w_gate[0], b_gate[0], w_up[0], b_up[0],
                  w_down[0], b_down[0], ln2_g[0], ln2_b[0])
```

```python
import jax
import jax.numpy as jnp
from jax import lax
from jax.experimental import pallas as pl
from jax.experimental.pallas import tpu as pltpu

F32 = jnp.float32
BF16 = jnp.bfloat16
I32 = jnp.int32

LN_EPS = 1e-5
HEAD_DIM = 64
SGU_CHUNK = 128
BAND_BLOCK = 128
DILATIONS = (1, 4, 16)
ATTN_SPAN = BAND_BLOCK * DILATIONS[-1]
N_EXPERTS = 32
TOP_K = 4
SWIGLU_ALPHA = 1.702
SWIGLU_LIMIT = 7.0
DEPTH = 1
DN_ALPHA = (2 * DEPTH) ** 0.25
LOG2E = 1.4426950408889634

LANES = 128
PROJ_ROWS = 512
MOE_ROWS = 512
MOE_SUB = 256
ROUTE_ROWS = 256
OCTET = 8
STAGE_ROWS = ROUTE_ROWS * TOP_K + N_EXPERTS * OCTET
STAGE_OCTETS = STAGE_ROWS // OCTET
PAIR = 2
VMEM_LIMIT = 56 * 1024 * 1024


def _params(sem):
    return pltpu.CompilerParams(dimension_semantics=sem, vmem_limit_bytes=VMEM_LIMIT)


def _dot(a, b):
    return jnp.dot(a, b, preferred_element_type=F32)


def _proj_kernel(x_ref, w_ref, gmat_ref, wpair_ref, sbias_ref, lng_ref, lnb_ref, mg_ref,
                 a_ref, q1_ref, k1_ref, v1_ref, q4_ref, k4_ref, v4_ref, q16_ref, k16_ref, v16_ref,
                 a_scr, t_scr, t4_scr):
    width = a_ref.shape[-1]
    xb = x_ref[...].astype(BF16)

    def proj(c):
        return _dot(xb, w_ref[:, c * width:(c + 1) * width])

    u = jax.nn.gelu(proj(0))
    v = jax.nn.gelu(proj(1))
    mean = _dot(v.astype(BF16), gmat_ref[...])
    d = v - mean
    var = _dot((d * d).astype(BF16), gmat_ref[...])
    vn = (d * lax.rsqrt(var + LN_EPS) * lng_ref[...] + lnb_ref[...]).astype(BF16)

    lane = lax.broadcasted_iota(I32, (SGU_CHUNK, LANES), 1)
    low = lane < HEAD_DIM
    zero = jnp.zeros((SGU_CHUNK, LANES), BF16)
    for c in range(PROJ_ROWS // SGU_CHUNK):
        rows = slice(c * SGU_CHUNK, (c + 1) * SGU_CHUNK)
        for j in range(width // LANES):
            cols = slice(j * LANES, (j + 1) * LANES)
            vp = vn[rows, cols]
            rhs = jnp.concatenate([jnp.where(low, vp, zero), jnp.where(low, zero, vp)], axis=0)
            gate = _dot(wpair_ref[j], rhs) + sbias_ref[:, cols]
            a_scr[rows, cols] = u[rows, cols] * gate
    a = a_scr[...]
    ms = jnp.mean(a * a, axis=-1, keepdims=True)
    a_ref[...] = (a * lax.rsqrt(ms + LN_EPS) * mg_ref[...]).astype(BF16)

    outs = ((q1_ref, q4_ref, q16_ref), (k1_ref, k4_ref, k16_ref), (v1_ref, v4_ref, v16_ref))
    for c, (o1, o4, o16) in enumerate(outs):
        t = proj(2 + c)
        if c == 0:
            t = t * (HEAD_DIM ** -0.5 * LOG2E)
        o1[...] = t.astype(BF16)
        for j in range(width // LANES):
            cols = slice(j * LANES, (j + 1) * LANES)
            t_scr[j] = t[:, cols]
            for b in range(4):
                t4 = t_scr[j, pl.ds(b, PROJ_ROWS // 4, stride=4), :]
                o4[b, :, cols] = t4.astype(BF16)
                t4_scr[b] = t4
                for a in range(4):
                    o16[4 * a + b, :, cols] = t4_scr[b, pl.ds(a, PROJ_ROWS // 16, stride=4), :].astype(BF16)


def _project(x2, w_in_b, gmat, wpair, sbias, lng, lnb, mg, batch, seq):
    t, dm = x2.shape
    width = dm // 2
    nt = seq // PROJ_ROWS
    per_span = ATTN_SPAN // PROJ_ROWS
    const = lambda *shape: pl.BlockSpec(shape, lambda b, m: (0,) * len(shape))
    o1 = jax.ShapeDtypeStruct((batch, seq, width), BF16)
    o4 = jax.ShapeDtypeStruct((batch, seq // 512, 4, BAND_BLOCK, width), BF16)
    o16 = jax.ShapeDtypeStruct((batch, seq // ATTN_SPAN, 16, BAND_BLOCK, width), BF16)
    s1 = pl.BlockSpec((None, PROJ_ROWS, width), lambda b, m: (b, m, 0))
    s4 = pl.BlockSpec((None, None, 4, BAND_BLOCK, width), lambda b, m: (b, m, 0, 0, 0))
    s16 = pl.BlockSpec((None, None, 16, PROJ_ROWS // 16, width),
                       lambda b, m: (b, m // per_span, 0, m % per_span, 0))
    return pl.pallas_call(
        _proj_kernel,
        grid=(batch, nt),
        in_specs=[pl.BlockSpec((PROJ_ROWS, dm), lambda b, m: (b * nt + m, 0)),
                  const(*w_in_b.shape), const(*gmat.shape), const(*wpair.shape), const(*sbias.shape),
                  const(1, width), const(1, width), const(1, width)],
        out_specs=[s1] + [s1, s1, s1] + [s4, s4, s4] + [s16, s16, s16],
        out_shape=[o1] + [o1, o1, o1] + [o4, o4, o4] + [o16, o16, o16],
        scratch_shapes=[pltpu.VMEM((PROJ_ROWS, width), F32),
                        pltpu.VMEM((width // LANES, PROJ_ROWS, LANES), F32),
                        pltpu.VMEM((4, PROJ_ROWS // 4, LANES), F32)],
        compiler_params=_params(("parallel", "parallel")),
        name="proj_sgu",
    )(x2, w_in_b, gmat, wpair, sbias, lng, lnb, mg)


def _attn_kernel(q1, k1, v1, kp1, vp1, q4, k4, v4, kp4, vp4, q16, k16, v16, kp16, vp16,
                 o_ref, acc_o, acc_m, acc_l):
    not_first = pl.program_id(1) > 0
    blk = BAND_BLOCK
    row = lax.broadcasted_iota(I32, (2 * blk, 2 * blk), 0) % blk
    col = lax.broadcasted_iota(I32, (2 * blk, 2 * blk), 1)
    band = jnp.logical_or(jnp.logical_and(col < blk, col >= row), jnp.logical_and(col >= blk, col - blk <= row))
    neg = jnp.where(not_first, 0.0, -jnp.inf).astype(F32)
    first_pen = jnp.where(col < blk, neg, 0.0)
    low = lax.broadcasted_iota(I32, (blk, LANES), 1) < HEAD_DIM
    zero = jnp.zeros((blk, LANES), BF16)
    ones = jnp.ones((2 * blk, LANES), BF16)

    def unit(q, kprev, kcur, vprev, vcur, maybe_first, rows, init):
        kk = jnp.concatenate([kprev, kcur], axis=0)
        vv = jnp.concatenate([vprev, vcur], axis=0)
        qq = jnp.concatenate([jnp.where(low, q, zero), jnp.where(low, zero, q)], axis=0)
        s = lax.dot_general(qq, kk, (((1,), (1,)), ((), ())), preferred_element_type=F32)
        s = jnp.where(band, s, -jnp.inf)
        if maybe_first:
            s = s + first_pen
        m2 = jnp.max(s, axis=-1, keepdims=True)
        p = jnp.exp2(s - m2)
        o2 = _dot(p.astype(BF16), jnp.concatenate([vv, ones], axis=1))
        o = jnp.where(low, o2[:blk, :LANES], o2[blk:, :LANES])
        m = jnp.where(low, m2[:blk], m2[blk:])
        l = jnp.where(low, o2[:blk, LANES:], o2[blk:, LANES:])
        if init:
            acc_o[rows, :] = o
            acc_m[rows, :] = m
            acc_l[rows, :] = l
        else:
            mo = acc_m[rows, :]
            mn = jnp.maximum(mo, m)
            so = jnp.exp2(mo - mn)
            sn = jnp.exp2(m - mn)
            acc_o[rows, :] = acc_o[rows, :] * so + o * sn
            acc_l[rows, :] = acc_l[rows, :] * so + l * sn
            acc_m[rows, :] = mn


    for n in range(ATTN_SPAN // blk):
        cur = pl.ds(n * blk, blk)
        if n == 0:
            unit(q1[cur, :], kp1[...], k1[cur, :], vp1[...], v1[cur, :], True, cur, True)
        else:
            prev = pl.ds((n - 1) * blk, blk)
            unit(q1[cur, :], k1[prev, :], k1[cur, :], v1[prev, :], v1[cur, :], False, cur, True)

    for s in range(ATTN_SPAN // (4 * blk)):
        for r in range(4):
            rows = pl.ds(s * (4 * blk) + r, blk, stride=4)
            if s == 0:
                unit(q4[0, r], kp4[r], k4[0, r], vp4[r], v4[0, r], True, rows, False)
            else:
                unit(q4[s, r], k4[s - 1, r], k4[s, r], v4[s - 1, r], v4[s, r], False, rows, False)

    for r in range(16):
        unit(q16[r], kp16[r], k16[r], vp16[r], v16[r], True, pl.ds(r, blk, stride=16), False)

    o_ref[...] = (acc_o[...] / acc_l[...]).astype(BF16)


def _attention(qkv1, qkv4, qkv16, batch, seq):
    width = qkv1[0].shape[-1]
    nspan = seq // ATTN_SPAN
    blk = BAND_BLOCK
    n1 = ATTN_SPAN // blk
    n4 = ATTN_SPAN // (4 * blk)

    cur1 = pl.BlockSpec((None, ATTN_SPAN, LANES), lambda b, i, p: (b, i, p))
    prev1 = pl.BlockSpec((None, None, blk, LANES), lambda b, i, p: (b, jnp.maximum(i * n1 - 1, 0), 0, p))
    cur4 = pl.BlockSpec((None, n4, 4, blk, LANES), lambda b, i, p: (b, i, 0, 0, p))
    prev4 = pl.BlockSpec((None, None, 4, blk, LANES), lambda b, i, p: (b, jnp.maximum(i * n4 - 1, 0), 0, 0, p))
    cur16 = pl.BlockSpec((None, None, 16, blk, LANES), lambda b, i, p: (b, i, 0, 0, p))
    prev16 = pl.BlockSpec((None, None, 16, blk, LANES), lambda b, i, p: (b, jnp.maximum(i - 1, 0), 0, 0, p))

    q1, k1, v1 = qkv1
    q4, k4, v4 = qkv4
    q16, k16, v16 = qkv16
    k1b = k1.reshape(batch, seq // blk, blk, width)
    v1b = v1.reshape(batch, seq // blk, blk, width)
    return pl.pallas_call(
        _attn_kernel,
        grid=(batch, nspan, width // LANES),
        in_specs=[cur1, cur1, cur1, prev1, prev1,
                  cur4, cur4, cur4, prev4, prev4,
                  cur16, cur16, cur16, prev16, prev16],
        out_specs=pl.BlockSpec((None, ATTN_SPAN, LANES), lambda b, i, p: (b, i, p)),
        out_shape=jax.ShapeDtypeStruct((batch, seq, width), BF16),
        scratch_shapes=[pltpu.VMEM((ATTN_SPAN, LANES), F32)] * 3,
        compiler_params=_params(("parallel", "parallel", "parallel")),
        name="dilated_attn",
    )(q1, k1, v1, k1b, v1b, q4, k4, v4, k4, v4, q16, k16, v16, k16, v16)


def _outproj_kernel(x_ref, a_ref, b_ref, wo_ref, mgb_ref, g1_ref, b1_ref, wr_ref, br_ref, tri_ref,
                    h_ref, idx_ref, gate_ref, rank_ref, idxt_ref, rankt_ref, runs_ref, cnt_ref, run_scr):
    @pl.when(pl.program_id(0) == 0)
    def _():
        run_scr[...] = jnp.zeros_like(run_scr)

    width = a_ref.shape[-1]
    lane = lax.broadcasted_iota(I32, (ROUTE_ROWS, LANES), 1).astype(F32)

    def spread(cols):
        out = jnp.zeros((ROUTE_ROWS, LANES), F32)
        for k, cval in enumerate(cols):
            out = jnp.where(lane == float(k), cval, out)
        return out

    subs = [slice(j * ROUTE_ROWS, (j + 1) * ROUTE_ROWS) for j in range(PROJ_ROWS // ROUTE_ROWS)]
    works = []
    for rows in subs:
        bf = b_ref[rows, :].astype(F32)
        bn = (bf * lax.rsqrt(jnp.mean(bf * bf, axis=-1, keepdims=True) + LN_EPS) * mgb_ref[...]).astype(BF16)
        mixed = _dot(a_ref[rows, :], wo_ref[0:width, :]) + _dot(bn, wo_ref[width:2 * width, :])
        z = DN_ALPHA * x_ref[rows, :] + mixed
        mu = jnp.mean(z, axis=-1, keepdims=True)
        zc = z - mu
        var = jnp.mean(zc * zc, axis=-1, keepdims=True)
        h = zc * lax.rsqrt(var + LN_EPS) * g1_ref[...] + b1_ref[...]
        h_ref[rows, :] = h
        works.append(_dot(h.astype(BF16), wr_ref[...]) + br_ref[...])

    vals = [[] for _ in subs]
    idxs = [[] for _ in subs]
    hots = [[] for _ in subs]
    for _ in range(TOP_K):
        for j in range(len(subs)):
            mv = jnp.max(works[j], axis=-1, keepdims=True)
            ix = jnp.min(jnp.where(works[j] == mv, lane, float(LANES)), axis=-1, keepdims=True)
            hot = lane == ix
            works[j] = jnp.where(hot, -jnp.inf, works[j])
            vals[j].append(mv)
            idxs[j].append(ix)
            hots[j].append(hot)

    members = []
    for j in range(len(subs)):
        member = jnp.zeros((ROUTE_ROWS, LANES), F32)
        for hot in hots[j]:
            member = jnp.where(hot, 1.0, member)
        members.append(member)
    withins = [_dot(tri_ref[...], member.astype(BF16)) for member in members]
    run = run_scr[...]
    for j, rows in enumerate(subs):
        runs_ref[j] = run
        before = withins[j] + run
        ranks = [jnp.sum(jnp.where(hot, before, 0.0), axis=-1, keepdims=True) for hot in hots[j]]
        run = run + jnp.sum(members[j], axis=0, keepdims=True)
        exps = [jnp.exp(v - vals[j][0]) for v in vals[j]]
        den = exps[0] + exps[1] + exps[2] + exps[3]
        idx_all = spread(idxs[j])
        rank_all = spread(ranks)
        idx_ref[rows, :] = idx_all.astype(I32)
        gate_ref[rows, :] = spread([e / den for e in exps])
        rank_ref[rows, :] = rank_all.astype(I32)
        idxt_ref[:, rows] = idx_all.T[:OCTET].astype(I32)
        rankt_ref[:, rows] = rank_all.T[:OCTET].astype(I32)
    run_scr[...] = run
    cnt_ref[...] = run


def _outproj(x2, a_n, b2, w_out_b, mgb, g1, b1, wr, br, tri):
    t, dm = x2.shape
    width = dm // 2
    const = lambda *shape: pl.BlockSpec(shape, lambda m: (0,) * len(shape))
    rowblk = lambda w: pl.BlockSpec((PROJ_ROWS, w), lambda m: (m, 0))
    colblk = pl.BlockSpec((OCTET, PROJ_ROWS), lambda m: (0, m))
    sub = PROJ_ROWS // ROUTE_ROWS
    return pl.pallas_call(
        _outproj_kernel,
        grid=(t // PROJ_ROWS,),
        in_specs=[rowblk(dm), rowblk(width), rowblk(width), const(dm, dm), const(1, width),
                  const(1, dm), const(1, dm), const(dm, LANES), const(1, LANES),
                  const(ROUTE_ROWS, ROUTE_ROWS)],
        out_specs=[rowblk(dm), rowblk(LANES), rowblk(LANES), rowblk(LANES), colblk, colblk,
                   pl.BlockSpec((None, sub, 1, LANES), lambda m: (m, 0, 0, 0)), const(1, LANES)],
        out_shape=[jax.ShapeDtypeStruct((t, dm), F32), jax.ShapeDtypeStruct((t, LANES), I32),
                   jax.ShapeDtypeStruct((t, LANES), F32), jax.ShapeDtypeStruct((t, LANES), I32),
                   jax.ShapeDtypeStruct((OCTET, t), I32), jax.ShapeDtypeStruct((OCTET, t), I32),
                   jax.ShapeDtypeStruct((t // PROJ_ROWS, sub, 1, LANES), F32),
                   jax.ShapeDtypeStruct((1, LANES), F32)],
        scratch_shapes=[pltpu.VMEM((1, LANES), F32)],
        compiler_params=_params(("arbitrary",)),
        name="outproj_router",
    )(x2, a_n, b2, w_out_b, mgb, g1, b1, wr, br, tri)


def _octet(ref, q):
    return ref.at[pl.ds(pl.multiple_of(q * OCTET, OCTET), OCTET), :]


def _sub_block(ref, b):
    return ref.at[pl.ds(pl.multiple_of(b * MOE_SUB, MOE_SUB), MOE_SUB), :]


def _for_octets(n, start):
    def body(i, c):
        for j in range(4):
            start(4 * i + j, j % 2)
        return c

    lax.fori_loop(0, n // 4, body, 0)
    done = (n // 4) * 4

    @pl.when((n & 2) != 0)
    def _():
        start(done, 0)
        start(done + 1, 1)

    @pl.when((n & 1) != 0)
    def _():
        start(n - 1, 0)


def _wait_octets(n, ref, sem):
    for s in (128, 64, 32, 16, 8, 4, 2, 1):
        @pl.when((n & s) != 0)
        def _():
            d = ref.at[pl.ds(0, s * OCTET), :]
            pltpu.make_async_copy(d, d, sem).wait()


def _dispatch_kernel(nq_ref, fstart_ref, fcount_ref, gq_ref, h_ref, idxt_ref, rankt_ref, tabt_ref, xs_hbm,
                     stage, zeros, sem, zsem):
    s = pl.program_id(0)
    base = (s % 2) * PAIR
    eid = lax.broadcasted_iota(I32, (N_EXPERTS, ROUTE_ROWS), 0)
    pos = lax.broadcasted_iota(I32, (STAGE_ROWS, ROUTE_ROWS), 0).astype(F32)
    sels = []
    for j in range(PAIR):
        cols = slice(j * ROUTE_ROWS, (j + 1) * ROUTE_ROWS)
        sel = None
        for k in range(TOP_K):
            hot = eid == idxt_ref[k:k + 1, cols]
            lpos = (jnp.sum(jnp.where(hot, tabt_ref[j], 0.0), axis=0, keepdims=True)
                    + rankt_ref[k:k + 1, cols].astype(F32))
            hit = pos == lpos
            sel = hit if sel is None else jnp.logical_or(sel, hit)
        sels.append(jnp.where(sel, 1.0, 0.0).astype(BF16))
    for j in range(PAIR):
        rows = slice(j * ROUTE_ROWS, (j + 1) * ROUTE_ROWS)
        stage[base + j] = _dot(sels[j], h_ref[rows, :].astype(BF16))

    @pl.when(s == 0)
    def _():
        zeros[...] = jnp.zeros_like(zeros)
        for g in range(N_EXPERTS):
            def zstart(i, c, g=g):
                pltpu.make_async_copy(_octet(zeros, 0), _octet(xs_hbm, fstart_ref[g] + i), zsem.at[0]).start()
                return c

            lax.fori_loop(0, fcount_ref[g], zstart, 0)

        def tstart(i, c):
            pltpu.make_async_copy(zeros, _sub_block(xs_hbm, fstart_ref[N_EXPERTS] + i), zsem.at[1]).start()
            return c

        lax.fori_loop(0, fcount_ref[N_EXPERTS], tstart, 0)

    for j in range(PAIR):
        _for_octets(nq_ref[s * PAIR + j], lambda q, pri, j=j: pltpu.make_async_copy(
            _octet(stage.at[base + j], q), _octet(xs_hbm, gq_ref[j, 0, q]), sem.at[base + j]).start(priority=pri))

    @pl.when(s > 0)
    def _():
        for j in range(PAIR):
            _wait_octets(nq_ref[jnp.maximum(s - 1, 0) * PAIR + j], xs_hbm, sem.at[PAIR - base + j])

    @pl.when(s == pl.num_programs(0) - 1)
    def _():
        for j in range(PAIR):
            _wait_octets(nq_ref[s * PAIR + j], xs_hbm, sem.at[base + j])

        def zwait(i, c):
            pltpu.make_async_copy(_octet(zeros, 0), _octet(xs_hbm, 0), zsem.at[0]).wait()
            return c

        for g in range(N_EXPERTS):
            lax.fori_loop(0, fcount_ref[g], zwait, 0)

        def twait(i, c):
            pltpu.make_async_copy(zeros, _sub_block(xs_hbm, 0), zsem.at[1]).wait()
            return c

        lax.fori_loop(0, fcount_ref[N_EXPERTS], twait, 0)


def _dispatch(h, idxt, rankt, tabt, gq3, nq, fill_start, fill_count, nrows):
    t, dm = h.shape
    step = PAIR * ROUTE_ROWS
    grid_spec = pltpu.PrefetchScalarGridSpec(
        num_scalar_prefetch=3,
        grid=(t // step,),
        in_specs=[pl.BlockSpec((PAIR, 1, STAGE_OCTETS), lambda m, *_: (m, 0, 0), memory_space=pltpu.SMEM),
                  pl.BlockSpec((step, dm), lambda m, *_: (m, 0)),
                  pl.BlockSpec((OCTET, step), lambda m, *_: (0, m)),
                  pl.BlockSpec((OCTET, step), lambda m, *_: (0, m)),
                  pl.BlockSpec((PAIR, N_EXPERTS, ROUTE_ROWS), lambda m, *_: (m, 0, 0))],
        out_specs=pl.BlockSpec(memory_space=pl.ANY),
        scratch_shapes=[pltpu.VMEM((2 * PAIR, STAGE_ROWS, dm), F32), pltpu.VMEM((MOE_SUB, dm), F32),
                        pltpu.SemaphoreType.DMA((2 * PAIR,)), pltpu.SemaphoreType.DMA((2,))],
    )
    return pl.pallas_call(
        _dispatch_kernel,
        grid_spec=grid_spec,
        out_shape=jax.ShapeDtypeStruct((nrows, dm), F32),
        compiler_params=_params(("arbitrary",)),
        name="moe_dispatch",
    )(nq, fill_start, fill_count, gq3, h, idxt, rankt, tabt)


def _moe_kernel(ib_ref, ie_ref, lo_ref, hi_ref, nxt_ref, slot_ref, xs_ref, bg_ref, bu_ref, bd_ref,
                wg_hbm, wu_hbm, wd_hbm, ys_ref, wf, wb, wsem):
    i = pl.program_id(0)
    prev = jnp.maximum(i - 1, 0)
    lo = lo_ref[i]
    hi = hi_ref[i]

    def weight_copies(e, s):
        return [pltpu.make_async_copy(w.at[e], wf.at[s, j], wsem.at[s])
                for j, w in enumerate((wg_hbm, wu_hbm, wd_hbm))]

    @pl.when(jnp.logical_and(lo >= 0, hi > lo))
    def _():
        @pl.when(jnp.logical_or(i == 0, ie_ref[i] != ie_ref[prev]))
        def _():
            s = slot_ref[i]

            @pl.when(i == 0)
            def _():
                for cp in weight_copies(ie_ref[i], s):
                    cp.start()

            for cp in weight_copies(ie_ref[i], s):
                cp.wait()
            for j in range(3):
                wb[j] = wf[s, j].astype(BF16)

            @pl.when(nxt_ref[i] >= 0)
            def _():
                for cp in weight_copies(nxt_ref[i], 1 - s):
                    cp.start()

        def expert(rows):
            x = xs_ref[rows, :].astype(BF16)
            g = jnp.minimum(_dot(x, wb[0]) + bg_ref[...], SWIGLU_LIMIT)
            u = jnp.clip(_dot(x, wb[1]) + bu_ref[...], -SWIGLU_LIMIT, SWIGLU_LIMIT)
            act = (u + 1.0) * (g * jax.nn.sigmoid(SWIGLU_ALPHA * g))
            return _dot(act.astype(BF16), wb[2]) + bd_ref[...]

        whole = jnp.logical_and(lo == 0, hi == MOE_ROWS)

        @pl.when(whole)
        def _():
            ys_ref[...] = expert(slice(None))

        @pl.when(jnp.logical_not(whole))
        def _():
            @pl.when(jnp.logical_or(i == 0, ib_ref[i] != ib_ref[prev]))
            def _():
                ys_ref[...] = jnp.zeros_like(ys_ref)

            for j in range(MOE_ROWS // MOE_SUB):
                rows = slice(j * MOE_SUB, (j + 1) * MOE_SUB)

                @pl.when(jnp.logical_and(lo < (j + 1) * MOE_SUB, hi > j * MOE_SUB))
                def _():
                    ys_ref[rows, :] = expert(rows)

    @pl.when(lo < 0)
    def _():
        ys_ref[...] = jnp.zeros_like(ys_ref)


def _moe(xs, items, w_gate, b_gate, w_up, b_up, w_down, b_down):
    nrows = xs.shape[0]
    ne, dm, df = w_gate.shape
    assert dm == df
    nitems = items[0].shape[0]
    bspec = lambda c: pl.BlockSpec((None, 1, c), lambda i, ib, ie, *_: (ie[i], 0, 0))
    rspec = pl.BlockSpec((MOE_ROWS, dm), lambda i, ib, *_: (ib[i], 0))
    hbm = pl.BlockSpec(memory_space=pl.ANY)
    grid_spec = pltpu.PrefetchScalarGridSpec(
        num_scalar_prefetch=6,
        grid=(nitems,),
        in_specs=[rspec, bspec(df), bspec(df), bspec(dm), hbm, hbm, hbm],
        out_specs=rspec,
        scratch_shapes=[pltpu.VMEM((2, 3, dm, df), F32), pltpu.VMEM((3, dm, df), BF16),
                        pltpu.SemaphoreType.DMA((2,))],
    )
    return pl.pallas_call(
        _moe_kernel,
        grid_spec=grid_spec,
        out_shape=jax.ShapeDtypeStruct((nrows, dm), F32),
        compiler_params=_params(("arbitrary",)),
        name="moe_experts",
    )(*items, xs, b_gate.reshape(ne, 1, df), b_up.reshape(ne, 1, df), b_down.reshape(ne, 1, dm),
      w_gate, w_up, w_down)


def _work_items(counts, nrows):
    nblk = nrows // MOE_ROWS
    nitems = nblk + N_EXPERTS - 1
    ends = jnp.cumsum(counts)
    starts = ends - counts
    b0 = jnp.arange(nblk, dtype=I32)[:, None] * MOE_ROWS
    lo = jnp.maximum(starts[None, :], b0)
    hi = jnp.minimum(ends[None, :], b0 + MOE_ROWS)
    nonempty = (hi > lo).reshape(-1)
    csum = jnp.cumsum(nonempty.astype(I32))
    j = jnp.arange(nitems, dtype=I32)
    pos = jnp.sum(csum[None, :] <= j[:, None], axis=1).astype(I32)
    used = j < csum[-1]
    pos = jnp.where(used, pos, jnp.max(jnp.where(nonempty, jnp.arange(nonempty.shape[0], dtype=I32), 0)))
    ib = pos // N_EXPERTS
    ie = pos % N_EXPERTS
    ilo = jnp.where(used, lo.reshape(-1)[pos] - ib * MOE_ROWS, 0)
    ihi = jnp.where(used, hi.reshape(-1)[pos] - ib * MOE_ROWS, 0)
    spare_blk = ib + 1 + (j - csum[-1])
    fill = jnp.logical_and(jnp.logical_not(used), spare_blk < nblk)
    ib = jnp.where(used, ib, jnp.minimum(spare_blk, nblk - 1))
    ilo = jnp.where(fill, -1, ilo)
    ordinal = jnp.cumsum(jnp.concatenate([jnp.zeros((1,), I32), (ie[1:] != ie[:-1]).astype(I32)]))
    first_next = jnp.sum(ordinal[None, :] <= ordinal[:, None], axis=1)
    has_next = first_next < nitems
    nxt = jnp.where(has_next, ie[jnp.minimum(first_next, nitems - 1)], -1)
    return (ib.astype(I32), ie.astype(I32), ilo.astype(I32), ihi.astype(I32), nxt.astype(I32),
            (ordinal % 2).astype(I32))


def _combine_kernel(nq_ref, gq_cur, gq_nxt, h_ref, idx_ref, rank_ref, gate_ref, tab_ref, g2_ref, b2_ref, ys_hbm,
                    o_ref, stage, sem):
    s = pl.program_id(0)
    last = pl.num_programs(0) - 1
    base = (s % 2) * PAIR

    def start_fetch(gq_ref, step, slot0):
        for j in range(PAIR):
            _for_octets(nq_ref[jnp.minimum(step, last) * PAIR + j], lambda q, pri, j=j: pltpu.make_async_copy(
                _octet(ys_hbm, gq_ref[j, 0, q]), _octet(stage.at[slot0 + j], q),
                sem.at[slot0 + j]).start(priority=pri))

    @pl.when(s == 0)
    def _():
        stage[...] = jnp.zeros_like(stage)
        start_fetch(gq_cur, 0, 0)

    @pl.when(s < last)
    def _():
        start_fetch(gq_nxt, s + 1, PAIR - base)

    for j in range(PAIR):
        _wait_octets(nq_ref[s * PAIR + j], stage.at[base + j], sem.at[base + j])

    lane_e = lax.broadcasted_iota(I32, (ROUTE_ROWS, LANES), 1)
    lane_p = lax.broadcasted_iota(I32, (ROUTE_ROWS, STAGE_ROWS), 1).astype(F32)
    wsels = []
    for j in range(PAIR):
        rows = slice(j * ROUTE_ROWS, (j + 1) * ROUTE_ROWS)
        idx = idx_ref[rows, :]
        rank = rank_ref[rows, :].astype(F32)
        gates = gate_ref[rows, :]
        wsel = jnp.zeros((ROUTE_ROWS, STAGE_ROWS), F32)
        for k in range(TOP_K):
            hot = lane_e == idx[:, k:k + 1]
            lpos = jnp.sum(jnp.where(hot, tab_ref[j], 0.0), axis=-1, keepdims=True) + rank[:, k:k + 1]
            wsel = jnp.where(lane_p == lpos, gates[:, k:k + 1], wsel)
        wsels.append(wsel.astype(BF16))
    ys = [_dot(wsels[j], stage[base + j].astype(BF16)) for j in range(PAIR)]
    for j in range(PAIR):
        rows = slice(j * ROUTE_ROWS, (j + 1) * ROUTE_ROWS)
        z = DN_ALPHA * h_ref[rows, :] + ys[j]
        mu = jnp.mean(z, axis=-1, keepdims=True)
        zc = z - mu
        var = jnp.mean(zc * zc, axis=-1, keepdims=True)
        o_ref[rows, :] = zc * lax.rsqrt(var + LN_EPS) * g2_ref[...] + b2_ref[...]


def _combine(h, ys, idx, rank, gates, tab, gq3, nq, g2, b2):
    t, dm = h.shape
    step = PAIR * ROUTE_ROWS
    ns = t // step
    rowblk = lambda w: pl.BlockSpec((step, w), lambda m, nq: (m, 0))
    const = pl.BlockSpec((1, dm), lambda m, nq: (0, 0))
    qspec = lambda f: pl.BlockSpec((PAIR, 1, STAGE_OCTETS), lambda m, nq: (f(m), 0, 0), memory_space=pltpu.SMEM)
    grid_spec = pltpu.PrefetchScalarGridSpec(
        num_scalar_prefetch=1,
        grid=(ns,),
        in_specs=[qspec(lambda m: m), qspec(lambda m: jnp.minimum(m + 1, ns - 1)),
                  rowblk(dm), rowblk(LANES), rowblk(LANES), rowblk(LANES),
                  pl.BlockSpec((PAIR, 1, LANES), lambda m, nq: (m, 0, 0)), const, const,
                  pl.BlockSpec(memory_space=pl.ANY)],
        out_specs=rowblk(dm),
        scratch_shapes=[pltpu.VMEM((2 * PAIR, STAGE_ROWS, dm), F32), pltpu.SemaphoreType.DMA((2 * PAIR,))],
    )
    return pl.pallas_call(
        _combine_kernel,
        grid_spec=grid_spec,
        out_shape=jax.ShapeDtypeStruct((t, dm), F32),
        compiler_params=_params(("arbitrary",)),
        name="combine_ln",
    )(nq, gq3, gq3, h, idx, rank, gates, tab, g2, b2, ys)


def _route_tables(runs, total, nrows):
    nt = runs.shape[0]
    nxt = jnp.concatenate([runs[1:], total[None, :]], axis=0)
    c8 = (nxt - runs + OCTET - 1) // OCTET
    q_end = jnp.cumsum(c8, axis=1)
    q0 = q_end - c8
    nq = q_end[:, -1]
    per_expert = jnp.sum(c8, axis=0)
    sub8 = MOE_SUB // OCTET
    per_pad = (per_expert + sub8 - 1) // sub8 * sub8
    start8 = jnp.cumsum(per_pad) - per_pad
    g8 = start8[None, :] + jnp.cumsum(c8, axis=0) - c8
    tab = OCTET * q0 - runs
    q = jnp.arange(STAGE_OCTETS, dtype=I32)
    e_of_q = jnp.minimum(jnp.sum(q[None, :, None] >= q_end[:, None, :], axis=2), N_EXPERTS - 1)
    pick = e_of_q[:, :, None] == jnp.arange(N_EXPERTS, dtype=I32)[None, None, :]
    gq = jnp.sum(jnp.where(pick, (g8 - q0)[:, None, :], 0), axis=2) + q[None, :]
    gq = jnp.where(q[None, :] < nq[:, None], gq, 0).astype(I32)
    tot8 = jnp.sum(per_pad)
    fill_start = jnp.concatenate([start8 + per_expert, (tot8 // sub8)[None]]).astype(I32)
    fill_count = jnp.concatenate([per_pad - per_expert, (nrows // MOE_SUB - tot8 // sub8)[None]]).astype(I32)
    return (tab, gq.reshape(nt, 1, STAGE_OCTETS), nq.astype(I32), fill_start, fill_count,
            (per_pad * OCTET).astype(I32))


def _layer(x, w_in, sgu_w, sgu_b, sgu_ln_g, sgu_ln_b, mix_norm_g, w_out, ln1_g, ln1_b,
           w_router, b_router, w_gate, b_gate, w_up, b_up, w_down, b_down, ln2_g, ln2_b):
    batch, seq, dm = x.shape
    width = dm // 2
    t = batch * seq
    assert seq % ATTN_SPAN == 0 and dm % (2 * LANES) == 0 and w_router.shape[-1] == N_EXPERTS
    x2 = x.reshape(t, dm)

    wc = jnp.tril(sgu_w)
    wpair = jnp.concatenate([wc[0::2], wc[1::2]], axis=-1).astype(BF16)
    sbias = jnp.repeat(sgu_b.T, HEAD_DIM, axis=1)
    grp = jnp.arange(width) // HEAD_DIM
    gmat = jnp.where(grp[:, None] == grp[None, :], 1.0 / HEAD_DIM, 0.0).astype(BF16)
    row = lambda v: v.reshape(1, -1)

    a_n, q1, k1, v1, q4, k4, v4, q16, k16, v16 = _project(
        x2, w_in.astype(BF16), gmat, wpair, sbias, row(sgu_ln_g), row(sgu_ln_b),
        row(mix_norm_g[:width]), batch, seq)
    b = _attention((q1, k1, v1), (q4, k4, v4), (q16, k16, v16), batch, seq)

    wr = jnp.pad(w_router, ((0, 0), (0, LANES - N_EXPERTS))).astype(BF16)
    br = jnp.concatenate([b_router.astype(F32), jnp.full((LANES - N_EXPERTS,), -1e30, F32)]).reshape(1, LANES)
    ti = jnp.arange(ROUTE_ROWS)
    tri = (ti[None, :] < ti[:, None]).astype(BF16)
    h, idx, gates, rank, idxt, rankt, runs, cnt = _outproj(
        x2, a_n.reshape(t, width), b.reshape(t, width), w_out.astype(BF16),
        row(mix_norm_g[width:]), row(ln1_g), row(ln1_b), wr, br, tri)

    nt = t // ROUTE_ROWS
    worst = t * TOP_K + nt * N_EXPERTS * (OCTET - 1) + N_EXPERTS * (MOE_SUB - OCTET)
    nrows = (worst + MOE_ROWS - 1) // MOE_ROWS * MOE_ROWS
    runs = runs.reshape(nt, LANES)[:, :N_EXPERTS].astype(I32)
    total = cnt[0, :N_EXPERTS].astype(I32)
    tab, gq3, nq, fill_start, fill_count, rows_e = _route_tables(runs, total, nrows)
    tabf = tab.astype(F32)
    tab_lane = jnp.pad(tabf, ((0, 0), (0, LANES - N_EXPERTS))).reshape(nt, 1, LANES)
    tab_sub = jnp.broadcast_to(tabf[:, :, None], (nt, N_EXPERTS, ROUTE_ROWS))

    xs = _dispatch(h, idxt, rankt, tab_sub, gq3, nq, fill_start, fill_count, nrows)
    ys = _moe(xs, _work_items(rows_e, nrows), w_gate, b_gate, w_up, b_up, w_down, b_down)
    out = _combine(h, ys, idx, rank, gates, tab_lane, gq3, nq, row(ln2_g), row(ln2_b))
    return out.reshape(batch, seq, dm)


def kernel(x, w_in, sgu_w, sgu_b, sgu_ln_g, sgu_ln_b, mix_norm_g, w_out, ln1_g, ln1_b, w_router, b_router,
           w_gate, b_gate, w_up, b_up, w_down, b_down, ln2_g, ln2_b):
    assert w_in.shape[0] == DEPTH
    return _layer(x, w_in[0], sgu_w[0], sgu_b[0], sgu_ln_g[0], sgu_ln_b[0], mix_norm_g[0], w_out[0],
                  ln1_g[0], ln1_b[0], w_router[0], b_router[0], w_gate[0], b_gate[0], w_up[0], b_up[0],
                  w_down[0], b_down[0], ln2_g[0], ln2_b[0])
```

```python
import jax
import jax.numpy as jnp
from jax import lax
from jax.experimental import pallas as pl
from jax.experimental.pallas import tpu as pltpu

F32 = jnp.float32
BF16 = jnp.bfloat16
I32 = jnp.int32

LN_EPS = 1e-5
HEAD_DIM = 64
SGU_CHUNK = 128
BAND_BLOCK = 128
DILATIONS = (1, 4, 16)
ATTN_SPAN = BAND_BLOCK * DILATIONS[-1]
N_EXPERTS = 32
TOP_K = 4
SWIGLU_ALPHA = 1.702
SWIGLU_LIMIT = 7.0
DEPTH = 1
DN_ALPHA = (2 * DEPTH) ** 0.25
LOG2E = 1.4426950408889634

LANES = 128
PROJ_ROWS = 512
MOE_ROWS = 1024
MOE_CHUNK = 512
MOE_SUB = 256
ROUTE_ROWS = 256
OCTET = 8
STAGE_ROWS = ROUTE_ROWS * TOP_K + N_EXPERTS * OCTET
STAGE_OCTETS = STAGE_ROWS // OCTET
PAIR = 2
VMEM_LIMIT = 56 * 1024 * 1024


def _params(sem):
    return pltpu.CompilerParams(dimension_semantics=sem, vmem_limit_bytes=VMEM_LIMIT)


def _dot(a, b):
    return jnp.dot(a, b, preferred_element_type=F32)


def _proj_kernel(x_ref, w_ref, gmat_ref, wpair_ref, sbias_ref, lng_ref, lnb_ref, mg_ref,
                 a_ref, q1_ref, k1_ref, v1_ref, q4_ref, k4_ref, v4_ref, q16_ref, k16_ref, v16_ref,
                 a_scr, t_scr, t4_scr):
    width = a_ref.shape[-1]
    xb = x_ref[...].astype(BF16)

    def proj(c):
        return _dot(xb, w_ref[:, c * width:(c + 1) * width])

    u = jax.nn.gelu(proj(0))
    v = jax.nn.gelu(proj(1))
    mean = _dot(v.astype(BF16), gmat_ref[...])
    d = v - mean
    var = _dot((d * d).astype(BF16), gmat_ref[...])
    vn = (d * lax.rsqrt(var + LN_EPS) * lng_ref[...] + lnb_ref[...]).astype(BF16)

    lane = lax.broadcasted_iota(I32, (SGU_CHUNK, LANES), 1)
    low = lane < HEAD_DIM
    zero = jnp.zeros((SGU_CHUNK, LANES), BF16)
    for c in range(PROJ_ROWS // SGU_CHUNK):
        rows = slice(c * SGU_CHUNK, (c + 1) * SGU_CHUNK)
        for j in range(width // LANES):
            cols = slice(j * LANES, (j + 1) * LANES)
            vp = vn[rows, cols]
            rhs = jnp.concatenate([jnp.where(low, vp, zero), jnp.where(low, zero, vp)], axis=0)
            gate = _dot(wpair_ref[j], rhs) + sbias_ref[:, cols]
            a_scr[rows, cols] = u[rows, cols] * gate
    a = a_scr[...]
    ms = jnp.mean(a * a, axis=-1, keepdims=True)
    a_ref[...] = (a * lax.rsqrt(ms + LN_EPS) * mg_ref[...]).astype(BF16)

    outs = ((q1_ref, q4_ref, q16_ref), (k1_ref, k4_ref, k16_ref), (v1_ref, v4_ref, v16_ref))
    for c, (o1, o4, o16) in enumerate(outs):
        t = proj(2 + c)
        if c == 0:
            t = t * (HEAD_DIM ** -0.5 * LOG2E)
        o1[...] = t.astype(BF16)
        for j in range(width // LANES):
            cols = slice(j * LANES, (j + 1) * LANES)
            t_scr[j] = t[:, cols]
            for b in range(4):
                t4 = t_scr[j, pl.ds(b, PROJ_ROWS // 4, stride=4), :]
                o4[b, :, cols] = t4.astype(BF16)
                t4_scr[b] = t4
                for a in range(4):
                    o16[4 * a + b, :, cols] = t4_scr[b, pl.ds(a, PROJ_ROWS // 16, stride=4), :].astype(BF16)


def _project(x2, w_in_b, gmat, wpair, sbias, lng, lnb, mg, batch, seq):
    t, dm = x2.shape
    width = dm // 2
    nt = seq // PROJ_ROWS
    per_span = ATTN_SPAN // PROJ_ROWS
    const = lambda *shape: pl.BlockSpec(shape, lambda b, m: (0,) * len(shape))
    o1 = jax.ShapeDtypeStruct((batch, seq, width), BF16)
    o4 = jax.ShapeDtypeStruct((batch, seq // 512, 4, BAND_BLOCK, width), BF16)
    o16 = jax.ShapeDtypeStruct((batch, seq // ATTN_SPAN, 16, BAND_BLOCK, width), BF16)
    s1 = pl.BlockSpec((None, PROJ_ROWS, width), lambda b, m: (b, m, 0))
    s4 = pl.BlockSpec((None, None, 4, BAND_BLOCK, width), lambda b, m: (b, m, 0, 0, 0))
    s16 = pl.BlockSpec((None, None, 16, PROJ_ROWS // 16, width),
                       lambda b, m: (b, m // per_span, 0, m % per_span, 0))
    return pl.pallas_call(
        _proj_kernel,
        grid=(batch, nt),
        in_specs=[pl.BlockSpec((PROJ_ROWS, dm), lambda b, m: (b * nt + m, 0)),
                  const(*w_in_b.shape), const(*gmat.shape), const(*wpair.shape), const(*sbias.shape),
                  const(1, width), const(1, width), const(1, width)],
        out_specs=[s1] + [s1, s1, s1] + [s4, s4, s4] + [s16, s16, s16],
        out_shape=[o1] + [o1, o1, o1] + [o4, o4, o4] + [o16, o16, o16],
        scratch_shapes=[pltpu.VMEM((PROJ_ROWS, width), F32),
                        pltpu.VMEM((width // LANES, PROJ_ROWS, LANES), F32),
                        pltpu.VMEM((4, PROJ_ROWS // 4, LANES), F32)],
        compiler_params=_params(("parallel", "parallel")),
        name="proj_sgu",
    )(x2, w_in_b, gmat, wpair, sbias, lng, lnb, mg)


def _attn_kernel(q1, k1, v1, kp1, vp1, q4, k4, v4, kp4, vp4, q16, k16, v16, kp16, vp16,
                 o_ref, acc_o, acc_m, acc_l):
    not_first = pl.program_id(1) > 0
    blk = BAND_BLOCK
    row = lax.broadcasted_iota(I32, (2 * blk, 2 * blk), 0) % blk
    col = lax.broadcasted_iota(I32, (2 * blk, 2 * blk), 1)
    band = jnp.logical_or(jnp.logical_and(col < blk, col >= row), jnp.logical_and(col >= blk, col - blk <= row))
    neg = jnp.where(not_first, 0.0, -jnp.inf).astype(F32)
    first_pen = jnp.where(col < blk, neg, 0.0)
    low = lax.broadcasted_iota(I32, (blk, LANES), 1) < HEAD_DIM
    zero = jnp.zeros((blk, LANES), BF16)
    ones = jnp.ones((2 * blk, LANES), BF16)

    def unit(q, kprev, kcur, vprev, vcur, maybe_first, rows, init):
        kk = jnp.concatenate([kprev, kcur], axis=0)
        vv = jnp.concatenate([vprev, vcur], axis=0)
        qq = jnp.concatenate([jnp.where(low, q, zero), jnp.where(low, zero, q)], axis=0)
        s = lax.dot_general(qq, kk, (((1,), (1,)), ((), ())), preferred_element_type=F32)
        s = jnp.where(band, s, -jnp.inf)
        if maybe_first:
            s = s + first_pen
        m2 = jnp.max(s, axis=-1, keepdims=True)
        p = jnp.exp2(s - m2)
        o2 = _dot(p.astype(BF16), jnp.concatenate([vv, ones], axis=1))
        o = jnp.where(low, o2[:blk, :LANES], o2[blk:, :LANES])
        m = jnp.where(low, m2[:blk], m2[blk:])
        l = jnp.where(low, o2[:blk, LANES:], o2[blk:, LANES:])
        if init:
            acc_o[rows, :] = o
            acc_m[rows, :] = m
            acc_l[rows, :] = l
        else:
            mo = acc_m[rows, :]
            mn = jnp.maximum(mo, m)
            so = jnp.exp2(mo - mn)
            sn = jnp.exp2(m - mn)
            acc_o[rows, :] = acc_o[rows, :] * so + o * sn
            acc_l[rows, :] = acc_l[rows, :] * so + l * sn
            acc_m[rows, :] = mn


    for n in range(ATTN_SPAN // blk):
        cur = pl.ds(n * blk, blk)
        if n == 0:
            unit(q1[cur, :], kp1[...], k1[cur, :], vp1[...], v1[cur, :], True, cur, True)
        else:
            prev = pl.ds((n - 1) * blk, blk)
            unit(q1[cur, :], k1[prev, :], k1[cur, :], v1[prev, :], v1[cur, :], False, cur, True)

    for s in range(ATTN_SPAN // (4 * blk)):
        for r in range(4):
            rows = pl.ds(s * (4 * blk) + r, blk, stride=4)
            if s == 0:
                unit(q4[0, r], kp4[r], k4[0, r], vp4[r], v4[0, r], True, rows, False)
            else:
                unit(q4[s, r], k4[s - 1, r], k4[s, r], v4[s - 1, r], v4[s, r], False, rows, False)

    for r in range(16):
        unit(q16[r], kp16[r], k16[r], vp16[r], v16[r], True, pl.ds(r, blk, stride=16), False)

    o_ref[...] = (acc_o[...] / acc_l[...]).astype(BF16)


def _attention(qkv1, qkv4, qkv16, batch, seq):
    width = qkv1[0].shape[-1]
    nspan = seq // ATTN_SPAN
    blk = BAND_BLOCK
    n1 = ATTN_SPAN // blk
    n4 = ATTN_SPAN // (4 * blk)

    cur1 = pl.BlockSpec((None, ATTN_SPAN, LANES), lambda b, i, p: (b, i, p))
    prev1 = pl.BlockSpec((None, None, blk, LANES), lambda b, i, p: (b, jnp.maximum(i * n1 - 1, 0), 0, p))
    cur4 = pl.BlockSpec((None, n4, 4, blk, LANES), lambda b, i, p: (b, i, 0, 0, p))
    prev4 = pl.BlockSpec((None, None, 4, blk, LANES), lambda b, i, p: (b, jnp.maximum(i * n4 - 1, 0), 0, 0, p))
    cur16 = pl.BlockSpec((None, None, 16, blk, LANES), lambda b, i, p: (b, i, 0, 0, p))
    prev16 = pl.BlockSpec((None, None, 16, blk, LANES), lambda b, i, p: (b, jnp.maximum(i - 1, 0), 0, 0, p))

    q1, k1, v1 = qkv1
    q4, k4, v4 = qkv4
    q16, k16, v16 = qkv16
    k1b = k1.reshape(batch, seq // blk, blk, width)
    v1b = v1.reshape(batch, seq // blk, blk, width)
    return pl.pallas_call(
        _attn_kernel,
        grid=(batch, nspan, width // LANES),
        in_specs=[cur1, cur1, cur1, prev1, prev1,
                  cur4, cur4, cur4, prev4, prev4,
                  cur16, cur16, cur16, prev16, prev16],
        out_specs=pl.BlockSpec((None, ATTN_SPAN, LANES), lambda b, i, p: (b, i, p)),
        out_shape=jax.ShapeDtypeStruct((batch, seq, width), BF16),
        scratch_shapes=[pltpu.VMEM((ATTN_SPAN, LANES), F32)] * 3,
        compiler_params=_params(("parallel", "parallel", "parallel")),
        name="dilated_attn",
    )(q1, k1, v1, k1b, v1b, q4, k4, v4, k4, v4, q16, k16, v16, k16, v16)


def _outproj_kernel(x_ref, a_ref, b_ref, wo_ref, mgb_ref, g1_ref, b1_ref, wr_ref, br_ref, tri_ref,
                    h_ref, idx_ref, gate_ref, rank_ref, idxt_ref, rankt_ref, runs_ref, cnt_ref, run_scr):
    @pl.when(pl.program_id(0) == 0)
    def _():
        run_scr[...] = jnp.zeros_like(run_scr)

    width = a_ref.shape[-1]
    lane = lax.broadcasted_iota(I32, (ROUTE_ROWS, LANES), 1).astype(F32)

    def spread(cols):
        out = jnp.zeros((ROUTE_ROWS, LANES), F32)
        for k, cval in enumerate(cols):
            out = jnp.where(lane == float(k), cval, out)
        return out

    subs = [slice(j * ROUTE_ROWS, (j + 1) * ROUTE_ROWS) for j in range(PROJ_ROWS // ROUTE_ROWS)]
    works = []
    for rows in subs:
        bf = b_ref[rows, :].astype(F32)
        bn = (bf * lax.rsqrt(jnp.mean(bf * bf, axis=-1, keepdims=True) + LN_EPS) * mgb_ref[...]).astype(BF16)
        mixed = _dot(a_ref[rows, :], wo_ref[0:width, :]) + _dot(bn, wo_ref[width:2 * width, :])
        z = DN_ALPHA * x_ref[rows, :] + mixed
        mu = jnp.mean(z, axis=-1, keepdims=True)
        zc = z - mu
        var = jnp.mean(zc * zc, axis=-1, keepdims=True)
        h = zc * lax.rsqrt(var + LN_EPS) * g1_ref[...] + b1_ref[...]
        h_ref[rows, :] = h
        works.append(_dot(h.astype(BF16), wr_ref[...]) + br_ref[...])

    vals = [[] for _ in subs]
    idxs = [[] for _ in subs]
    hots = [[] for _ in subs]
    for _ in range(TOP_K):
        for j in range(len(subs)):
            mv = jnp.max(works[j], axis=-1, keepdims=True)
            ix = jnp.min(jnp.where(works[j] == mv, lane, float(LANES)), axis=-1, keepdims=True)
            hot = lane == ix
            works[j] = jnp.where(hot, -jnp.inf, works[j])
            vals[j].append(mv)
            idxs[j].append(ix)
            hots[j].append(hot)

    members = []
    for j in range(len(subs)):
        member = jnp.zeros((ROUTE_ROWS, LANES), F32)
        for hot in hots[j]:
            member = jnp.where(hot, 1.0, member)
        members.append(member)
    withins = [_dot(tri_ref[...], member.astype(BF16)) for member in members]
    run = run_scr[...]
    for j, rows in enumerate(subs):
        runs_ref[j] = run
        before = withins[j] + run
        ranks = [jnp.sum(jnp.where(hot, before, 0.0), axis=-1, keepdims=True) for hot in hots[j]]
        run = run + jnp.sum(members[j], axis=0, keepdims=True)
        exps = [jnp.exp(v - vals[j][0]) for v in vals[j]]
        den = exps[0] + exps[1] + exps[2] + exps[3]
        idx_all = spread(idxs[j])
        rank_all = spread(ranks)
        idx_ref[rows, :] = idx_all.astype(I32)
        gate_ref[rows, :] = spread([e / den for e in exps])
        rank_ref[rows, :] = rank_all.astype(I32)
        idxt_ref[:, rows] = idx_all.T[:OCTET].astype(I32)
        rankt_ref[:, rows] = rank_all.T[:OCTET].astype(I32)
    run_scr[...] = run
    cnt_ref[...] = run


def _outproj(x2, a_n, b2, w_out_b, mgb, g1, b1, wr, br, tri):
    t, dm = x2.shape
    width = dm // 2
    const = lambda *shape: pl.BlockSpec(shape, lambda m: (0,) * len(shape))
    rowblk = lambda w: pl.BlockSpec((PROJ_ROWS, w), lambda m: (m, 0))
    colblk = pl.BlockSpec((OCTET, PROJ_ROWS), lambda m: (0, m))
    sub = PROJ_ROWS // ROUTE_ROWS
    return pl.pallas_call(
        _outproj_kernel,
        grid=(t // PROJ_ROWS,),
        in_specs=[rowblk(dm), rowblk(width), rowblk(width), const(dm, dm), const(1, width),
                  const(1, dm), const(1, dm), const(dm, LANES), const(1, LANES),
                  const(ROUTE_ROWS, ROUTE_ROWS)],
        out_specs=[rowblk(dm), rowblk(LANES), rowblk(LANES), rowblk(LANES), colblk, colblk,
                   pl.BlockSpec((None, sub, 1, LANES), lambda m: (m, 0, 0, 0)), const(1, LANES)],
        out_shape=[jax.ShapeDtypeStruct((t, dm), F32), jax.ShapeDtypeStruct((t, LANES), I32),
                   jax.ShapeDtypeStruct((t, LANES), F32), jax.ShapeDtypeStruct((t, LANES), I32),
                   jax.ShapeDtypeStruct((OCTET, t), I32), jax.ShapeDtypeStruct((OCTET, t), I32),
                   jax.ShapeDtypeStruct((t // PROJ_ROWS, sub, 1, LANES), F32),
                   jax.ShapeDtypeStruct((1, LANES), F32)],
        scratch_shapes=[pltpu.VMEM((1, LANES), F32)],
        compiler_params=_params(("arbitrary",)),
        name="outproj_router",
    )(x2, a_n, b2, w_out_b, mgb, g1, b1, wr, br, tri)


def _octet(ref, q):
    return ref.at[pl.ds(pl.multiple_of(q * OCTET, OCTET), OCTET), :]


def _sub_block(ref, b):
    return ref.at[pl.ds(pl.multiple_of(b * MOE_SUB, MOE_SUB), MOE_SUB), :]


def _for_octets(n, start):
    def body(i, c):
        for j in range(4):
            start(4 * i + j, j % 2)
        return c

    lax.fori_loop(0, n // 4, body, 0)
    done = (n // 4) * 4

    @pl.when((n & 2) != 0)
    def _():
        start(done, 0)
        start(done + 1, 1)

    @pl.when((n & 1) != 0)
    def _():
        start(n - 1, 0)


def _wait_octets(n, ref, sem):
    for s in (128, 64, 32, 16, 8, 4, 2, 1):
        @pl.when((n & s) != 0)
        def _():
            d = ref.at[pl.ds(0, s * OCTET), :]
            pltpu.make_async_copy(d, d, sem).wait()


def _dispatch_kernel(nq_ref, fstart_ref, fcount_ref, gq_ref, h_ref, idxt_ref, rankt_ref, tabt_ref, xs_hbm,
                     stage, zeros, sem, zsem):
    s = pl.program_id(0)
    base = (s % 2) * PAIR
    eid = lax.broadcasted_iota(I32, (N_EXPERTS, ROUTE_ROWS), 0)
    pos = lax.broadcasted_iota(I32, (STAGE_ROWS, ROUTE_ROWS), 0).astype(F32)
    sels = []
    for j in range(PAIR):
        cols = slice(j * ROUTE_ROWS, (j + 1) * ROUTE_ROWS)
        sel = None
        for k in range(TOP_K):
            hot = eid == idxt_ref[k:k + 1, cols]
            lpos = (jnp.sum(jnp.where(hot, tabt_ref[j], 0.0), axis=0, keepdims=True)
                    + rankt_ref[k:k + 1, cols].astype(F32))
            hit = pos == lpos
            sel = hit if sel is None else jnp.logical_or(sel, hit)
        sels.append(jnp.where(sel, 1.0, 0.0).astype(BF16))
    for j in range(PAIR):
        rows = slice(j * ROUTE_ROWS, (j + 1) * ROUTE_ROWS)
        stage[base + j] = _dot(sels[j], h_ref[rows, :].astype(BF16))

    @pl.when(s == 0)
    def _():
        zeros[...] = jnp.zeros_like(zeros)
        for g in range(N_EXPERTS):
            def zstart(i, c, g=g):
                pltpu.make_async_copy(_octet(zeros, 0), _octet(xs_hbm, fstart_ref[g] + i), zsem.at[0]).start()
                return c

            lax.fori_loop(0, fcount_ref[g], zstart, 0)

        def tstart(i, c):
            pltpu.make_async_copy(zeros, _sub_block(xs_hbm, fstart_ref[N_EXPERTS] + i), zsem.at[1]).start()
            return c

        lax.fori_loop(0, fcount_ref[N_EXPERTS], tstart, 0)

    for j in range(PAIR):
        _for_octets(nq_ref[s * PAIR + j], lambda q, pri, j=j: pltpu.make_async_copy(
            _octet(stage.at[base + j], q), _octet(xs_hbm, gq_ref[j, 0, q]), sem.at[base + j]).start(priority=pri))

    @pl.when(s > 0)
    def _():
        for j in range(PAIR):
            _wait_octets(nq_ref[jnp.maximum(s - 1, 0) * PAIR + j], xs_hbm, sem.at[PAIR - base + j])

    @pl.when(s == pl.num_programs(0) - 1)
    def _():
        for j in range(PAIR):
            _wait_octets(nq_ref[s * PAIR + j], xs_hbm, sem.at[base + j])

        def zwait(i, c):
            pltpu.make_async_copy(_octet(zeros, 0), _octet(xs_hbm, 0), zsem.at[0]).wait()
            return c

        for g in range(N_EXPERTS):
            lax.fori_loop(0, fcount_ref[g], zwait, 0)

        def twait(i, c):
            pltpu.make_async_copy(zeros, _sub_block(xs_hbm, 0), zsem.at[1]).wait()
            return c

        lax.fori_loop(0, fcount_ref[N_EXPERTS], twait, 0)


def _dispatch(h, idxt, rankt, tabt, gq3, nq, fill_start, fill_count, nrows):
    t, dm = h.shape
    step = PAIR * ROUTE_ROWS
    grid_spec = pltpu.PrefetchScalarGridSpec(
        num_scalar_prefetch=3,
        grid=(t // step,),
        in_specs=[pl.BlockSpec((PAIR, 1, STAGE_OCTETS), lambda m, *_: (m, 0, 0), memory_space=pltpu.SMEM),
                  pl.BlockSpec((step, dm), lambda m, *_: (m, 0)),
                  pl.BlockSpec((OCTET, step), lambda m, *_: (0, m)),
                  pl.BlockSpec((OCTET, step), lambda m, *_: (0, m)),
                  pl.BlockSpec((PAIR, N_EXPERTS, ROUTE_ROWS), lambda m, *_: (m, 0, 0))],
        out_specs=pl.BlockSpec(memory_space=pl.ANY),
        scratch_shapes=[pltpu.VMEM((2 * PAIR, STAGE_ROWS, dm), F32), pltpu.VMEM((MOE_SUB, dm), F32),
                        pltpu.SemaphoreType.DMA((2 * PAIR,)), pltpu.SemaphoreType.DMA((2,))],
    )
    return pl.pallas_call(
        _dispatch_kernel,
        grid_spec=grid_spec,
        out_shape=jax.ShapeDtypeStruct((nrows, dm), F32),
        compiler_params=_params(("arbitrary",)),
        name="moe_dispatch",
    )(nq, fill_start, fill_count, gq3, h, idxt, rankt, tabt)


def _moe_kernel(ib_ref, ie_ref, lo_ref, hi_ref, nxt_ref, xs_ref, bg_ref, bu_ref, bd_ref,
                wg_hbm, wu_hbm, wd_hbm, ys_ref, wf, wb, wsem):
    i = pl.program_id(0)
    prev = jnp.maximum(i - 1, 0)
    lo = lo_ref[i]
    hi = hi_ref[i]

    def weight_copies(e):
        return [pltpu.make_async_copy(w.at[e], wf.at[j], wsem) for j, w in enumerate((wg_hbm, wu_hbm, wd_hbm))]

    @pl.when(jnp.logical_and(lo >= 0, hi > lo))
    def _():
        @pl.when(jnp.logical_or(i == 0, ie_ref[i] != ie_ref[prev]))
        def _():
            @pl.when(i == 0)
            def _():
                for cp in weight_copies(ie_ref[i]):
                    cp.start()

            for cp in weight_copies(ie_ref[i]):
                cp.wait()
            for j in range(3):
                wb[j] = wf[j].astype(BF16)

            @pl.when(nxt_ref[i] >= 0)
            def _():
                for cp in weight_copies(nxt_ref[i]):
                    cp.start()

        def expert(rows):
            x = xs_ref[rows, :].astype(BF16)
            g = jnp.minimum(_dot(x, wb[0]) + bg_ref[...], SWIGLU_LIMIT)
            u = jnp.clip(_dot(x, wb[1]) + bu_ref[...], -SWIGLU_LIMIT, SWIGLU_LIMIT)
            act = (u + 1.0) * (g * jax.nn.sigmoid(SWIGLU_ALPHA * g))
            return _dot(act.astype(BF16), wb[2]) + bd_ref[...]

        whole = jnp.logical_and(lo == 0, hi == MOE_ROWS)

        @pl.when(whole)
        def _():
            for j in range(MOE_ROWS // MOE_CHUNK):
                rows = slice(j * MOE_CHUNK, (j + 1) * MOE_CHUNK)
                ys_ref[rows, :] = expert(rows)

        @pl.when(jnp.logical_not(whole))
        def _():
            @pl.when(jnp.logical_or(i == 0, ib_ref[i] != ib_ref[prev]))
            def _():
                ys_ref[...] = jnp.zeros_like(ys_ref)

            for j in range(MOE_ROWS // MOE_SUB):
                rows = slice(j * MOE_SUB, (j + 1) * MOE_SUB)

                @pl.when(jnp.logical_and(lo < (j + 1) * MOE_SUB, hi > j * MOE_SUB))
                def _():
                    ys_ref[rows, :] = expert(rows)

    @pl.when(lo < 0)
    def _():
        ys_ref[...] = jnp.zeros_like(ys_ref)


def _moe(xs, items, w_gate, b_gate, w_up, b_up, w_down, b_down):
    nrows = xs.shape[0]
    ne, dm, df = w_gate.shape
    assert dm == df
    nitems = items[0].shape[0]
    bspec = lambda c: pl.BlockSpec((None, 1, c), lambda i, ib, ie, *_: (ie[i], 0, 0))
    rspec = pl.BlockSpec((MOE_ROWS, dm), lambda i, ib, *_: (ib[i], 0))
    hbm = pl.BlockSpec(memory_space=pl.ANY)
    grid_spec = pltpu.PrefetchScalarGridSpec(
        num_scalar_prefetch=5,
        grid=(nitems,),
        in_specs=[rspec, bspec(df), bspec(df), bspec(dm), hbm, hbm, hbm],
        out_specs=rspec,
        scratch_shapes=[pltpu.VMEM((3, dm, df), F32), pltpu.VMEM((3, dm, df), BF16),
                        pltpu.SemaphoreType.DMA(())],
    )
    return pl.pallas_call(
        _moe_kernel,
        grid_spec=grid_spec,
        out_shape=jax.ShapeDtypeStruct((nrows, dm), F32),
        compiler_params=_params(("arbitrary",)),
        name="moe_experts",
    )(*items, xs, b_gate.reshape(ne, 1, df), b_up.reshape(ne, 1, df), b_down.reshape(ne, 1, dm),
      w_gate, w_up, w_down)


def _work_items(counts, nrows):
    nblk = nrows // MOE_ROWS
    nitems = nblk + N_EXPERTS - 1
    ends = jnp.cumsum(counts)
    starts = ends - counts
    b0 = jnp.arange(nblk, dtype=I32)[:, None] * MOE_ROWS
    lo = jnp.maximum(starts[None, :], b0)
    hi = jnp.minimum(ends[None, :], b0 + MOE_ROWS)
    nonempty = (hi > lo).reshape(-1)
    csum = jnp.cumsum(nonempty.astype(I32))
    j = jnp.arange(nitems, dtype=I32)
    pos = jnp.sum(csum[None, :] <= j[:, None], axis=1).astype(I32)
    used = j < csum[-1]
    pos = jnp.where(used, pos, jnp.max(jnp.where(nonempty, jnp.arange(nonempty.shape[0], dtype=I32), 0)))
    ib = pos // N_EXPERTS
    ie = pos % N_EXPERTS
    ilo = jnp.where(used, lo.reshape(-1)[pos] - ib * MOE_ROWS, 0)
    ihi = jnp.where(used, hi.reshape(-1)[pos] - ib * MOE_ROWS, 0)
    spare_blk = ib + 1 + (j - csum[-1])
    fill = jnp.logical_and(jnp.logical_not(used), spare_blk < nblk)
    ib = jnp.where(used, ib, jnp.minimum(spare_blk, nblk - 1))
    ilo = jnp.where(fill, -1, ilo)
    ordinal = jnp.cumsum(jnp.concatenate([jnp.zeros((1,), I32), (ie[1:] != ie[:-1]).astype(I32)]))
    first_next = jnp.sum(ordinal[None, :] <= ordinal[:, None], axis=1)
    has_next = first_next < nitems
    nxt = jnp.where(has_next, ie[jnp.minimum(first_next, nitems - 1)], -1)
    return ib.astype(I32), ie.astype(I32), ilo.astype(I32), ihi.astype(I32), nxt.astype(I32)


def _combine_kernel(nq_ref, gq_cur, gq_nxt, h_ref, idx_ref, rank_ref, gate_ref, tab_ref, g2_ref, b2_ref, ys_hbm,
                    o_ref, stage, sem):
    s = pl.program_id(0)
    last = pl.num_programs(0) - 1
    base = (s % 2) * PAIR

    def start_fetch(gq_ref, step, slot0):
        for j in range(PAIR):
            _for_octets(nq_ref[jnp.minimum(step, last) * PAIR + j], lambda q, pri, j=j: pltpu.make_async_copy(
                _octet(ys_hbm, gq_ref[j, 0, q]), _octet(stage.at[slot0 + j], q),
                sem.at[slot0 + j]).start(priority=pri))

    @pl.when(s == 0)
    def _():
        stage[...] = jnp.zeros_like(stage)
        start_fetch(gq_cur, 0, 0)

    @pl.when(s < last)
    def _():
        start_fetch(gq_nxt, s + 1, PAIR - base)

    for j in range(PAIR):
        _wait_octets(nq_ref[s * PAIR + j], stage.at[base + j], sem.at[base + j])

    lane_e = lax.broadcasted_iota(I32, (ROUTE_ROWS, LANES), 1)
    lane_p = lax.broadcasted_iota(I32, (ROUTE_ROWS, STAGE_ROWS), 1).astype(F32)
    wsels = []
    for j in range(PAIR):
        rows = slice(j * ROUTE_ROWS, (j + 1) * ROUTE_ROWS)
        idx = idx_ref[rows, :]
        rank = rank_ref[rows, :].astype(F32)
        gates = gate_ref[rows, :]
        wsel = jnp.zeros((ROUTE_ROWS, STAGE_ROWS), F32)
        for k in range(TOP_K):
            hot = lane_e == idx[:, k:k + 1]
            lpos = jnp.sum(jnp.where(hot, tab_ref[j], 0.0), axis=-1, keepdims=True) + rank[:, k:k + 1]
            wsel = jnp.where(lane_p == lpos, gates[:, k:k + 1], wsel)
        wsels.append(wsel.astype(BF16))
    ys = [_dot(wsels[j], stage[base + j].astype(BF16)) for j in range(PAIR)]
    for j in range(PAIR):
        rows = slice(j * ROUTE_ROWS, (j + 1) * ROUTE_ROWS)
        z = DN_ALPHA * h_ref[rows, :] + ys[j]
        mu = jnp.mean(z, axis=-1, keepdims=True)
        zc = z - mu
        var = jnp.mean(zc * zc, axis=-1, keepdims=True)
        o_ref[rows, :] = zc * lax.rsqrt(var + LN_EPS) * g2_ref[...] + b2_ref[...]


def _combine(h, ys, idx, rank, gates, tab, gq3, nq, g2, b2):
    t, dm = h.shape
    step = PAIR * ROUTE_ROWS
    ns = t // step
    rowblk = lambda w: pl.BlockSpec((step, w), lambda m, nq: (m, 0))
    const = pl.BlockSpec((1, dm), lambda m, nq: (0, 0))
    qspec = lambda f: pl.BlockSpec((PAIR, 1, STAGE_OCTETS), lambda m, nq: (f(m), 0, 0), memory_space=pltpu.SMEM)
    grid_spec = pltpu.PrefetchScalarGridSpec(
        num_scalar_prefetch=1,
        grid=(ns,),
        in_specs=[qspec(lambda m: m), qspec(lambda m: jnp.minimum(m + 1, ns - 1)),
                  rowblk(dm), rowblk(LANES), rowblk(LANES), rowblk(LANES),
                  pl.BlockSpec((PAIR, 1, LANES), lambda m, nq: (m, 0, 0)), const, const,
                  pl.BlockSpec(memory_space=pl.ANY)],
        out_specs=rowblk(dm),
        scratch_shapes=[pltpu.VMEM((2 * PAIR, STAGE_ROWS, dm), F32), pltpu.SemaphoreType.DMA((2 * PAIR,))],
    )
    return pl.pallas_call(
        _combine_kernel,
        grid_spec=grid_spec,
        out_shape=jax.ShapeDtypeStruct((t, dm), F32),
        compiler_params=_params(("arbitrary",)),
        name="combine_ln",
    )(nq, gq3, gq3, h, idx, rank, gates, tab, g2, b2, ys)


def _route_tables(runs, total, nrows):
    nt = runs.shape[0]
    nxt = jnp.concatenate([runs[1:], total[None, :]], axis=0)
    c8 = (nxt - runs + OCTET - 1) // OCTET
    q_end = jnp.cumsum(c8, axis=1)
    q0 = q_end - c8
    nq = q_end[:, -1]
    per_expert = jnp.sum(c8, axis=0)
    sub8 = MOE_SUB // OCTET
    per_pad = (per_expert + sub8 - 1) // sub8 * sub8
    start8 = jnp.cumsum(per_pad) - per_pad
    g8 = start8[None, :] + jnp.cumsum(c8, axis=0) - c8
    tab = OCTET * q0 - runs
    q = jnp.arange(STAGE_OCTETS, dtype=I32)
    e_of_q = jnp.minimum(jnp.sum(q[None, :, None] >= q_end[:, None, :], axis=2), N_EXPERTS - 1)
    pick = e_of_q[:, :, None] == jnp.arange(N_EXPERTS, dtype=I32)[None, None, :]
    gq = jnp.sum(jnp.where(pick, (g8 - q0)[:, None, :], 0), axis=2) + q[None, :]
    gq = jnp.where(q[None, :] < nq[:, None], gq, 0).astype(I32)
    tot8 = jnp.sum(per_pad)
    fill_start = jnp.concatenate([start8 + per_expert, (tot8 // sub8)[None]]).astype(I32)
    fill_count = jnp.concatenate([per_pad - per_expert, (nrows // MOE_SUB - tot8 // sub8)[None]]).astype(I32)
    return (tab, gq.reshape(nt, 1, STAGE_OCTETS), nq.astype(I32), fill_start, fill_count,
            (per_pad * OCTET).astype(I32))


def _layer(x, w_in, sgu_w, sgu_b, sgu_ln_g, sgu_ln_b, mix_norm_g, w_out, ln1_g, ln1_b,
           w_router, b_router, w_gate, b_gate, w_up, b_up, w_down, b_down, ln2_g, ln2_b):
    batch, seq, dm = x.shape
    width = dm // 2
    t = batch * seq
    assert seq % ATTN_SPAN == 0 and dm % (2 * LANES) == 0 and w_router.shape[-1] == N_EXPERTS
    x2 = x.reshape(t, dm)

    wc = jnp.tril(sgu_w)
    wpair = jnp.concatenate([wc[0::2], wc[1::2]], axis=-1).astype(BF16)
    sbias = jnp.repeat(sgu_b.T, HEAD_DIM, axis=1)
    grp = jnp.arange(width) // HEAD_DIM
    gmat = jnp.where(grp[:, None] == grp[None, :], 1.0 / HEAD_DIM, 0.0).astype(BF16)
    row = lambda v: v.reshape(1, -1)

    a_n, q1, k1, v1, q4, k4, v4, q16, k16, v16 = _project(
        x2, w_in.astype(BF16), gmat, wpair, sbias, row(sgu_ln_g), row(sgu_ln_b),
        row(mix_norm_g[:width]), batch, seq)
    b = _attention((q1, k1, v1), (q4, k4, v4), (q16, k16, v16), batch, seq)

    wr = jnp.pad(w_router, ((0, 0), (0, LANES - N_EXPERTS))).astype(BF16)
    br = jnp.concatenate([b_router.astype(F32), jnp.full((LANES - N_EXPERTS,), -1e30, F32)]).reshape(1, LANES)
    ti = jnp.arange(ROUTE_ROWS)
    tri = (ti[None, :] < ti[:, None]).astype(BF16)
    h, idx, gates, rank, idxt, rankt, runs, cnt = _outproj(
        x2, a_n.reshape(t, width), b.reshape(t, width), w_out.astype(BF16),
        row(mix_norm_g[width:]), row(ln1_g), row(ln1_b), wr, br, tri)

    nt = t // ROUTE_ROWS
    worst = t * TOP_K + nt * N_EXPERTS * (OCTET - 1) + N_EXPERTS * (MOE_SUB - OCTET)
    nrows = (worst + MOE_ROWS - 1) // MOE_ROWS * MOE_ROWS
    runs = runs.reshape(nt, LANES)[:, :N_EXPERTS].astype(I32)
    total = cnt[0, :N_EXPERTS].astype(I32)
    tab, gq3, nq, fill_start, fill_count, rows_e = _route_tables(runs, total, nrows)
    tabf = tab.astype(F32)
    tab_lane = jnp.pad(tabf, ((0, 0), (0, LANES - N_EXPERTS))).reshape(nt, 1, LANES)
    tab_sub = jnp.broadcast_to(tabf[:, :, None], (nt, N_EXPERTS, ROUTE_ROWS))

    xs = _dispatch(h, idxt, rankt, tab_sub, gq3, nq, fill_start, fill_count, nrows)
    ys = _moe(xs, _work_items(rows_e, nrows), w_gate, b_gate, w_up, b_up, w_down, b_down)
    out = _combine(h, ys, idx, rank, gates, tab_lane, gq3, nq, row(ln2_g), row(ln2_b))
    return out.reshape(batch, seq, dm)


def kernel(x, w_in, sgu_w, sgu_b, sgu_ln_g, sgu_ln_b, mix_norm_g, w_out, ln1_g, ln1_b, w_router, b_router,
           w_gate, b_gate, w_up, b_up, w_down, b_down, ln2_g, ln2_b):
    assert w_in.shape[0] == DEPTH
    return _layer(x, w_in[0], sgu_w[0], sgu_b[0], sgu_ln_g[0], sgu_ln_b[0], mix_norm_g[0], w_out[0],
                  ln1_g[0], ln1_b[0], w_router[0], b_router[0], w_gate[0], b_gate[0], w_up[0], b_up[0],
                  w_down[0], b_down[0], ln2_g[0], ln2_b[0])
```

```python
import jax
import jax.numpy as jnp
from jax import lax
from jax.experimental import pallas as pl
from jax.experimental.pallas import tpu as pltpu

F32 = jnp.float32
BF16 = jnp.bfloat16
I32 = jnp.int32

LN_EPS = 1e-5
HEAD_DIM = 64
SGU_CHUNK = 128
BAND_BLOCK = 128
DILATIONS = (1, 4, 16)
ATTN_SPAN = BAND_BLOCK * DILATIONS[-1]
N_EXPERTS = 32
TOP_K = 4
SWIGLU_ALPHA = 1.702
SWIGLU_LIMIT = 7.0
DEPTH = 1
DN_ALPHA = (2 * DEPTH) ** 0.25
LOG2E = 1.4426950408889634

LANES = 128
PROJ_ROWS = 512
OUT_ROWS = 1024
MOE_ROWS = 1024
MOE_CHUNK = 512
MOE_SUB = 256
ROUTE_ROWS = 256
OCTET = 8
STAGE_ROWS = ROUTE_ROWS * TOP_K + N_EXPERTS * OCTET
STAGE_OCTETS = STAGE_ROWS // OCTET
PAIR = 2
VMEM_LIMIT = 56 * 1024 * 1024


def _params(sem):
    return pltpu.CompilerParams(dimension_semantics=sem, vmem_limit_bytes=VMEM_LIMIT)


def _dot(a, b):
    return jnp.dot(a, b, preferred_element_type=F32)


def _proj_kernel(x_ref, w_ref, gmat_ref, wpair_ref, sbias_ref, lng_ref, lnb_ref, mg_ref,
                 a_ref, q1_ref, k1_ref, v1_ref, q4_ref, k4_ref, v4_ref, q16_ref, k16_ref, v16_ref,
                 a_scr, t_scr, t4_scr, wb_scr):
    width = a_ref.shape[-1]

    @pl.when(jnp.logical_and(pl.program_id(0) == 0, pl.program_id(1) == 0))
    def _():
        wb_scr[...] = w_ref[...].astype(BF16)

    xb = x_ref[...].astype(BF16)

    def proj(c):
        return _dot(xb, wb_scr[:, c * width:(c + 1) * width])

    u = jax.nn.gelu(proj(0))
    v = jax.nn.gelu(proj(1))
    mean = _dot(v.astype(BF16), gmat_ref[...])
    d = v - mean
    var = _dot((d * d).astype(BF16), gmat_ref[...])
    vn = (d * lax.rsqrt(var + LN_EPS) * lng_ref[...] + lnb_ref[...]).astype(BF16)

    lane = lax.broadcasted_iota(I32, (SGU_CHUNK, LANES), 1)
    low = lane < HEAD_DIM
    zero = jnp.zeros((SGU_CHUNK, LANES), BF16)
    for c in range(PROJ_ROWS // SGU_CHUNK):
        rows = slice(c * SGU_CHUNK, (c + 1) * SGU_CHUNK)
        for j in range(width // LANES):
            cols = slice(j * LANES, (j + 1) * LANES)
            vp = vn[rows, cols]
            rhs = jnp.concatenate([jnp.where(low, vp, zero), jnp.where(low, zero, vp)], axis=0)
            gate = _dot(wpair_ref[j], rhs) + sbias_ref[:, cols]
            a_scr[rows, cols] = u[rows, cols] * gate
    a = a_scr[...]
    ms = jnp.mean(a * a, axis=-1, keepdims=True)
    a_ref[...] = (a * lax.rsqrt(ms + LN_EPS) * mg_ref[...]).astype(BF16)

    outs = ((q1_ref, q4_ref, q16_ref), (k1_ref, k4_ref, k16_ref), (v1_ref, v4_ref, v16_ref))
    for c, (o1, o4, o16) in enumerate(outs):
        t = proj(2 + c)
        if c == 0:
            t = t * (HEAD_DIM ** -0.5 * LOG2E)
        o1[...] = t.astype(BF16)
        for j in range(width // LANES):
            cols = slice(j * LANES, (j + 1) * LANES)
            t_scr[j] = t[:, cols]
            for b in range(4):
                t4 = t_scr[j, pl.ds(b, PROJ_ROWS // 4, stride=4), :]
                o4[b, :, cols] = t4.astype(BF16)
                t4_scr[b] = t4
                for a in range(4):
                    o16[4 * a + b, :, cols] = t4_scr[b, pl.ds(a, PROJ_ROWS // 16, stride=4), :].astype(BF16)


def _project(x2, w_in, gmat, wpair, sbias, lng, lnb, mg, batch, seq):
    t, dm = x2.shape
    width = dm // 2
    nt = seq // PROJ_ROWS
    per_span = ATTN_SPAN // PROJ_ROWS
    const = lambda *shape: pl.BlockSpec(shape, lambda b, m: (0,) * len(shape))
    o1 = jax.ShapeDtypeStruct((batch, seq, width), BF16)
    o4 = jax.ShapeDtypeStruct((batch, seq // 512, 4, BAND_BLOCK, width), BF16)
    o16 = jax.ShapeDtypeStruct((batch, seq // ATTN_SPAN, 16, BAND_BLOCK, width), BF16)
    s1 = pl.BlockSpec((None, PROJ_ROWS, width), lambda b, m: (b, m, 0))
    s4 = pl.BlockSpec((None, None, 4, BAND_BLOCK, width), lambda b, m: (b, m, 0, 0, 0))
    s16 = pl.BlockSpec((None, None, 16, PROJ_ROWS // 16, width),
                       lambda b, m: (b, m // per_span, 0, m % per_span, 0))
    return pl.pallas_call(
        _proj_kernel,
        grid=(batch, nt),
        in_specs=[pl.BlockSpec((PROJ_ROWS, dm), lambda b, m: (b * nt + m, 0)),
                  pl.BlockSpec(w_in.shape, lambda b, m: (0, 0), pipeline_mode=pl.Buffered(1)), const(*gmat.shape), const(*wpair.shape), const(*sbias.shape),
                  const(1, width), const(1, width), const(1, width)],
        out_specs=[s1] + [s1, s1, s1] + [s4, s4, s4] + [s16, s16, s16],
        out_shape=[o1] + [o1, o1, o1] + [o4, o4, o4] + [o16, o16, o16],
        scratch_shapes=[pltpu.VMEM((PROJ_ROWS, width), F32),
                        pltpu.VMEM((width // LANES, PROJ_ROWS, LANES), F32),
                        pltpu.VMEM((4, PROJ_ROWS // 4, LANES), F32),
                        pltpu.VMEM(w_in.shape, BF16)],
        compiler_params=_params(("arbitrary", "arbitrary")),
        name="proj_sgu",
    )(x2, w_in, gmat, wpair, sbias, lng, lnb, mg)


def _attn_kernel(q1, k1, v1, kp1, vp1, q4, k4, v4, kp4, vp4, q16, k16, v16, kp16, vp16,
                 o_ref, acc_o, acc_m, acc_l):
    not_first = pl.program_id(1) > 0
    blk = BAND_BLOCK
    row = lax.broadcasted_iota(I32, (2 * blk, 2 * blk), 0) % blk
    col = lax.broadcasted_iota(I32, (2 * blk, 2 * blk), 1)
    band = jnp.logical_or(jnp.logical_and(col < blk, col >= row), jnp.logical_and(col >= blk, col - blk <= row))
    neg = jnp.where(not_first, 0.0, -jnp.inf).astype(F32)
    first_pen = jnp.where(col < blk, neg, 0.0)
    low = lax.broadcasted_iota(I32, (blk, LANES), 1) < HEAD_DIM
    zero = jnp.zeros((blk, LANES), BF16)
    ones = jnp.ones((2 * blk, LANES), BF16)

    def unit(q, kprev, kcur, vprev, vcur, maybe_first, rows, init):
        kk = jnp.concatenate([kprev, kcur], axis=0)
        vv = jnp.concatenate([vprev, vcur], axis=0)
        qq = jnp.concatenate([jnp.where(low, q, zero), jnp.where(low, zero, q)], axis=0)
        s = lax.dot_general(qq, kk, (((1,), (1,)), ((), ())), preferred_element_type=F32)
        s = jnp.where(band, s, -jnp.inf)
        if maybe_first:
            s = s + first_pen
        m2 = jnp.max(s, axis=-1, keepdims=True)
        p = jnp.exp2(s - m2)
        o2 = _dot(p.astype(BF16), jnp.concatenate([vv, ones], axis=1))
        o = jnp.where(low, o2[:blk, :LANES], o2[blk:, :LANES])
        m = jnp.where(low, m2[:blk], m2[blk:])
        l = jnp.where(low, o2[:blk, LANES:], o2[blk:, LANES:])
        if init:
            acc_o[rows, :] = o
            acc_m[rows, :] = m
            acc_l[rows, :] = l
        else:
            mo = acc_m[rows, :]
            mn = jnp.maximum(mo, m)
            so = jnp.exp2(mo - mn)
            sn = jnp.exp2(m - mn)
            acc_o[rows, :] = acc_o[rows, :] * so + o * sn
            acc_l[rows, :] = acc_l[rows, :] * so + l * sn
            acc_m[rows, :] = mn


    for n in range(ATTN_SPAN // blk):
        cur = pl.ds(n * blk, blk)
        if n == 0:
            unit(q1[cur, :], kp1[...], k1[cur, :], vp1[...], v1[cur, :], True, cur, True)
        else:
            prev = pl.ds((n - 1) * blk, blk)
            unit(q1[cur, :], k1[prev, :], k1[cur, :], v1[prev, :], v1[cur, :], False, cur, True)

    for s in range(ATTN_SPAN // (4 * blk)):
        for r in range(4):
            rows = pl.ds(s * (4 * blk) + r, blk, stride=4)
            if s == 0:
                unit(q4[0, r], kp4[r], k4[0, r], vp4[r], v4[0, r], True, rows, False)
            else:
                unit(q4[s, r], k4[s - 1, r], k4[s, r], v4[s - 1, r], v4[s, r], False, rows, False)

    for r in range(16):
        unit(q16[r], kp16[r], k16[r], vp16[r], v16[r], True, pl.ds(r, blk, stride=16), False)

    o_ref[...] = (acc_o[...] / acc_l[...]).astype(BF16)


def _attention(qkv1, qkv4, qkv16, batch, seq):
    width = qkv1[0].shape[-1]
    nspan = seq // ATTN_SPAN
    blk = BAND_BLOCK
    n1 = ATTN_SPAN // blk
    n4 = ATTN_SPAN // (4 * blk)

    cur1 = pl.BlockSpec((None, ATTN_SPAN, LANES), lambda b, i, p: (b, i, p))
    prev1 = pl.BlockSpec((None, None, blk, LANES), lambda b, i, p: (b, jnp.maximum(i * n1 - 1, 0), 0, p))
    cur4 = pl.BlockSpec((None, n4, 4, blk, LANES), lambda b, i, p: (b, i, 0, 0, p))
    prev4 = pl.BlockSpec((None, None, 4, blk, LANES), lambda b, i, p: (b, jnp.maximum(i * n4 - 1, 0), 0, 0, p))
    cur16 = pl.BlockSpec((None, None, 16, blk, LANES), lambda b, i, p: (b, i, 0, 0, p))
    prev16 = pl.BlockSpec((None, None, 16, blk, LANES), lambda b, i, p: (b, jnp.maximum(i - 1, 0), 0, 0, p))

    q1, k1, v1 = qkv1
    q4, k4, v4 = qkv4
    q16, k16, v16 = qkv16
    k1b = k1.reshape(batch, seq // blk, blk, width)
    v1b = v1.reshape(batch, seq // blk, blk, width)
    return pl.pallas_call(
        _attn_kernel,
        grid=(batch, nspan, width // LANES),
        in_specs=[cur1, cur1, cur1, prev1, prev1,
                  cur4, cur4, cur4, prev4, prev4,
                  cur16, cur16, cur16, prev16, prev16],
        out_specs=pl.BlockSpec((None, ATTN_SPAN, LANES), lambda b, i, p: (b, i, p)),
        out_shape=jax.ShapeDtypeStruct((batch, seq, width), BF16),
        scratch_shapes=[pltpu.VMEM((ATTN_SPAN, LANES), F32)] * 3,
        compiler_params=_params(("parallel", "parallel", "parallel")),
        name="dilated_attn",
    )(q1, k1, v1, k1b, v1b, q4, k4, v4, k4, v4, q16, k16, v16, k16, v16)


def _outproj_kernel(x_ref, a_ref, b_ref, wo_ref, mgb_ref, g1_ref, b1_ref, wr_ref, br_ref, tri_ref,
                    h_ref, idx_ref, gate_ref, rank_ref, idxt_ref, rankt_ref, runs_ref, cnt_ref, run_scr, wob_scr):
    @pl.when(pl.program_id(0) == 0)
    def _():
        run_scr[...] = jnp.zeros_like(run_scr)
        wob_scr[...] = wo_ref[...].astype(BF16)

    width = a_ref.shape[-1]
    lane = lax.broadcasted_iota(I32, (ROUTE_ROWS, LANES), 1).astype(F32)

    def spread(cols):
        out = jnp.zeros((ROUTE_ROWS, LANES), F32)
        for k, cval in enumerate(cols):
            out = jnp.where(lane == float(k), cval, out)
        return out

    subs = [slice(j * ROUTE_ROWS, (j + 1) * ROUTE_ROWS) for j in range(OUT_ROWS // ROUTE_ROWS)]
    works = []
    for rows in subs:
        bf = b_ref[rows, :].astype(F32)
        bn = (bf * lax.rsqrt(jnp.mean(bf * bf, axis=-1, keepdims=True) + LN_EPS) * mgb_ref[...]).astype(BF16)
        mixed = _dot(a_ref[rows, :], wob_scr[0:width, :]) + _dot(bn, wob_scr[width:2 * width, :])
        z = DN_ALPHA * x_ref[rows, :] + mixed
        mu = jnp.mean(z, axis=-1, keepdims=True)
        zc = z - mu
        var = jnp.mean(zc * zc, axis=-1, keepdims=True)
        h = zc * lax.rsqrt(var + LN_EPS) * g1_ref[...] + b1_ref[...]
        h_ref[rows, :] = h
        works.append(_dot(h.astype(BF16), wr_ref[...]) + br_ref[...])

    vals = [[] for _ in subs]
    idxs = [[] for _ in subs]
    hots = [[] for _ in subs]
    for _ in range(TOP_K):
        for j in range(len(subs)):
            mv = jnp.max(works[j], axis=-1, keepdims=True)
            ix = jnp.min(jnp.where(works[j] == mv, lane, float(LANES)), axis=-1, keepdims=True)
            hot = lane == ix
            works[j] = jnp.where(hot, -jnp.inf, works[j])
            vals[j].append(mv)
            idxs[j].append(ix)
            hots[j].append(hot)

    members = []
    for j in range(len(subs)):
        member = jnp.zeros((ROUTE_ROWS, LANES), F32)
        for hot in hots[j]:
            member = jnp.where(hot, 1.0, member)
        members.append(member)
    withins = [_dot(tri_ref[...], member.astype(BF16)) for member in members]
    run = run_scr[...]
    for j, rows in enumerate(subs):
        runs_ref[j] = run
        before = withins[j] + run
        ranks = [jnp.sum(jnp.where(hot, before, 0.0), axis=-1, keepdims=True) for hot in hots[j]]
        run = run + jnp.sum(members[j], axis=0, keepdims=True)
        exps = [jnp.exp(v - vals[j][0]) for v in vals[j]]
        den = exps[0] + exps[1] + exps[2] + exps[3]
        idx_all = spread(idxs[j])
        rank_all = spread(ranks)
        idx_ref[rows, :] = idx_all.astype(I32)
        gate_ref[rows, :] = spread([e / den for e in exps])
        rank_ref[rows, :] = rank_all.astype(I32)
        idxt_ref[:, rows] = idx_all.T[:OCTET].astype(I32)
        rankt_ref[:, rows] = rank_all.T[:OCTET].astype(I32)
    run_scr[...] = run
    cnt_ref[...] = run


def _outproj(x2, a_n, b2, w_out, mgb, g1, b1, wr, br, tri):
    t, dm = x2.shape
    width = dm // 2
    const = lambda *shape: pl.BlockSpec(shape, lambda m: (0,) * len(shape))
    rowblk = lambda w: pl.BlockSpec((OUT_ROWS, w), lambda m: (m, 0))
    colblk = pl.BlockSpec((OCTET, OUT_ROWS), lambda m: (0, m))
    sub = OUT_ROWS // ROUTE_ROWS
    return pl.pallas_call(
        _outproj_kernel,
        grid=(t // OUT_ROWS,),
        in_specs=[rowblk(dm), rowblk(width), rowblk(width),
                  pl.BlockSpec((dm, dm), lambda m: (0, 0), pipeline_mode=pl.Buffered(1)), const(1, width),
                  const(1, dm), const(1, dm), const(dm, LANES), const(1, LANES),
                  const(ROUTE_ROWS, ROUTE_ROWS)],
        out_specs=[rowblk(dm), rowblk(LANES), rowblk(LANES), rowblk(LANES), colblk, colblk,
                   pl.BlockSpec((None, sub, 1, LANES), lambda m: (m, 0, 0, 0)), const(1, LANES)],
        out_shape=[jax.ShapeDtypeStruct((t, dm), F32), jax.ShapeDtypeStruct((t, LANES), I32),
                   jax.ShapeDtypeStruct((t, LANES), F32), jax.ShapeDtypeStruct((t, LANES), I32),
                   jax.ShapeDtypeStruct((OCTET, t), I32), jax.ShapeDtypeStruct((OCTET, t), I32),
                   jax.ShapeDtypeStruct((t // OUT_ROWS, sub, 1, LANES), F32),
                   jax.ShapeDtypeStruct((1, LANES), F32)],
        scratch_shapes=[pltpu.VMEM((1, LANES), F32), pltpu.VMEM((dm, dm), BF16)],
        compiler_params=_params(("arbitrary",)),
        name="outproj_router",
    )(x2, a_n, b2, w_out, mgb, g1, b1, wr, br, tri)


def _octet(ref, q):
    return ref.at[pl.ds(pl.multiple_of(q * OCTET, OCTET), OCTET), :]


def _sub_block(ref, b):
    return ref.at[pl.ds(pl.multiple_of(b * MOE_SUB, MOE_SUB), MOE_SUB), :]


def _for_octets(n, start):
    def body(i, c):
        for j in range(4):
            start(4 * i + j, j % 2)
        return c

    lax.fori_loop(0, n // 4, body, 0)
    done = (n // 4) * 4

    @pl.when((n & 2) != 0)
    def _():
        start(done, 0)
        start(done + 1, 1)

    @pl.when((n & 1) != 0)
    def _():
        start(n - 1, 0)


def _wait_octets(n, ref, sem):
    for s in (128, 64, 32, 16, 8, 4, 2, 1):
        @pl.when((n & s) != 0)
        def _():
            d = ref.at[pl.ds(0, s * OCTET), :]
            pltpu.make_async_copy(d, d, sem).wait()


def _dispatch_kernel(nq_ref, fstart_ref, fcount_ref, gq_ref, h_ref, idxt_ref, rankt_ref, tabt_ref, xs_hbm,
                     stage, zeros, sem, zsem):
    s = pl.program_id(0)
    base = (s % 2) * PAIR
    eid = lax.broadcasted_iota(I32, (N_EXPERTS, ROUTE_ROWS), 0)
    pos = lax.broadcasted_iota(I32, (STAGE_ROWS, ROUTE_ROWS), 0).astype(F32)
    sels = []
    for j in range(PAIR):
        cols = slice(j * ROUTE_ROWS, (j + 1) * ROUTE_ROWS)
        sel = None
        for k in range(TOP_K):
            hot = eid == idxt_ref[k:k + 1, cols]
            lpos = (jnp.sum(jnp.where(hot, tabt_ref[j], 0.0), axis=0, keepdims=True)
                    + rankt_ref[k:k + 1, cols].astype(F32))
            hit = pos == lpos
            sel = hit if sel is None else jnp.logical_or(sel, hit)
        sels.append(jnp.where(sel, 1.0, 0.0).astype(BF16))
    for j in range(PAIR):
        rows = slice(j * ROUTE_ROWS, (j + 1) * ROUTE_ROWS)
        stage[base + j] = _dot(sels[j], h_ref[rows, :].astype(BF16))

    @pl.when(s == 0)
    def _():
        zeros[...] = jnp.zeros_like(zeros)
        for g in range(N_EXPERTS):
            def zstart(i, c, g=g):
                pltpu.make_async_copy(_octet(zeros, 0), _octet(xs_hbm, fstart_ref[g] + i), zsem.at[0]).start()
                return c

            lax.fori_loop(0, fcount_ref[g], zstart, 0)

        def tstart(i, c):
            pltpu.make_async_copy(zeros, _sub_block(xs_hbm, fstart_ref[N_EXPERTS] + i), zsem.at[1]).start()
            return c

        lax.fori_loop(0, fcount_ref[N_EXPERTS], tstart, 0)

    for j in range(PAIR):
        _for_octets(nq_ref[s * PAIR + j], lambda q, pri, j=j: pltpu.make_async_copy(
            _octet(stage.at[base + j], q), _octet(xs_hbm, gq_ref[j, 0, q]), sem.at[base + j]).start(priority=pri))

    @pl.when(s > 0)
    def _():
        for j in range(PAIR):
            _wait_octets(nq_ref[jnp.maximum(s - 1, 0) * PAIR + j], xs_hbm, sem.at[PAIR - base + j])

    @pl.when(s == pl.num_programs(0) - 1)
    def _():
        for j in range(PAIR):
            _wait_octets(nq_ref[s * PAIR + j], xs_hbm, sem.at[base + j])

        def zwait(i, c):
            pltpu.make_async_copy(_octet(zeros, 0), _octet(xs_hbm, 0), zsem.at[0]).wait()
            return c

        for g in range(N_EXPERTS):
            lax.fori_loop(0, fcount_ref[g], zwait, 0)

        def twait(i, c):
            pltpu.make_async_copy(zeros, _sub_block(xs_hbm, 0), zsem.at[1]).wait()
            return c

        lax.fori_loop(0, fcount_ref[N_EXPERTS], twait, 0)


def _dispatch(h, idxt, rankt, tabt, gq3, nq, fill_start, fill_count, nrows):
    t, dm = h.shape
    step = PAIR * ROUTE_ROWS
    grid_spec = pltpu.PrefetchScalarGridSpec(
        num_scalar_prefetch=3,
        grid=(t // step,),
        in_specs=[pl.BlockSpec((PAIR, 1, STAGE_OCTETS), lambda m, *_: (m, 0, 0), memory_space=pltpu.SMEM),
                  pl.BlockSpec((step, dm), lambda m, *_: (m, 0)),
                  pl.BlockSpec((OCTET, step), lambda m, *_: (0, m)),
                  pl.BlockSpec((OCTET, step), lambda m, *_: (0, m)),
                  pl.BlockSpec((PAIR, N_EXPERTS, ROUTE_ROWS), lambda m, *_: (m, 0, 0))],
        out_specs=pl.BlockSpec(memory_space=pl.ANY),
        scratch_shapes=[pltpu.VMEM((2 * PAIR, STAGE_ROWS, dm), F32), pltpu.VMEM((MOE_SUB, dm), F32),
                        pltpu.SemaphoreType.DMA((2 * PAIR,)), pltpu.SemaphoreType.DMA((2,))],
    )
    return pl.pallas_call(
        _dispatch_kernel,
        grid_spec=grid_spec,
        out_shape=jax.ShapeDtypeStruct((nrows, dm), F32),
        compiler_params=_params(("arbitrary",)),
        name="moe_dispatch",
    )(nq, fill_start, fill_count, gq3, h, idxt, rankt, tabt)


def _moe_kernel(ib_ref, ie_ref, lo_ref, hi_ref, nxt_ref, xs_ref, bg_ref, bu_ref, bd_ref,
                wg_hbm, wu_hbm, wd_hbm, ys_ref, wf, wb, wsem):
    i = pl.program_id(0)
    prev = jnp.maximum(i - 1, 0)
    lo = lo_ref[i]
    hi = hi_ref[i]

    def weight_copies(e):
        return [pltpu.make_async_copy(w.at[e], wf.at[j], wsem) for j, w in enumerate((wg_hbm, wu_hbm, wd_hbm))]

    @pl.when(jnp.logical_and(lo >= 0, hi > lo))
    def _():
        @pl.when(jnp.logical_or(i == 0, ie_ref[i] != ie_ref[prev]))
        def _():
            @pl.when(i == 0)
            def _():
                for cp in weight_copies(ie_ref[i]):
                    cp.start()

            for cp in weight_copies(ie_ref[i]):
                cp.wait()
            for j in range(3):
                wb[j] = wf[j].astype(BF16)

            @pl.when(nxt_ref[i] >= 0)
            def _():
                for cp in weight_copies(nxt_ref[i]):
                    cp.start()

        def expert(rows):
            x = xs_ref[rows, :].astype(BF16)
            g = jnp.minimum(_dot(x, wb[0]) + bg_ref[...], SWIGLU_LIMIT)
            u = jnp.clip(_dot(x, wb[1]) + bu_ref[...], -SWIGLU_LIMIT, SWIGLU_LIMIT)
            act = (u + 1.0) * (g * jax.nn.sigmoid(SWIGLU_ALPHA * g))
            return _dot(act.astype(BF16), wb[2]) + bd_ref[...]

        whole = jnp.logical_and(lo == 0, hi == MOE_ROWS)

        @pl.when(whole)
        def _():
            for j in range(MOE_ROWS // MOE_CHUNK):
                rows = slice(j * MOE_CHUNK, (j + 1) * MOE_CHUNK)
                ys_ref[rows, :] = expert(rows)

        @pl.when(jnp.logical_not(whole))
        def _():
            @pl.when(jnp.logical_or(i == 0, ib_ref[i] != ib_ref[prev]))
            def _():
                ys_ref[...] = jnp.zeros_like(ys_ref)

            for j in range(MOE_ROWS // MOE_SUB):
                rows = slice(j * MOE_SUB, (j + 1) * MOE_SUB)

                @pl.when(jnp.logical_and(lo < (j + 1) * MOE_SUB, hi > j * MOE_SUB))
                def _():
                    ys_ref[rows, :] = expert(rows)

    @pl.when(lo < 0)
    def _():
        ys_ref[...] = jnp.zeros_like(ys_ref)


def _moe(xs, items, w_gate, b_gate, w_up, b_up, w_down, b_down):
    nrows = xs.shape[0]
    ne, dm, df = w_gate.shape
    assert dm == df
    nitems = items[0].shape[0]
    bspec = lambda c: pl.BlockSpec((None, 1, c), lambda i, ib, ie, *_: (ie[i], 0, 0))
    rspec = pl.BlockSpec((MOE_ROWS, dm), lambda i, ib, *_: (ib[i], 0))
    hbm = pl.BlockSpec(memory_space=pl.ANY)
    grid_spec = pltpu.PrefetchScalarGridSpec(
        num_scalar_prefetch=5,
        grid=(nitems,),
        in_specs=[rspec, bspec(df), bspec(df), bspec(dm), hbm, hbm, hbm],
        out_specs=rspec,
        scratch_shapes=[pltpu.VMEM((3, dm, df), F32), pltpu.VMEM((3, dm, df), BF16),
                        pltpu.SemaphoreType.DMA(())],
    )
    return pl.pallas_call(
        _moe_kernel,
        grid_spec=grid_spec,
        out_shape=jax.ShapeDtypeStruct((nrows, dm), F32),
        compiler_params=_params(("arbitrary",)),
        name="moe_experts",
    )(*items, xs, b_gate.reshape(ne, 1, df), b_up.reshape(ne, 1, df), b_down.reshape(ne, 1, dm),
      w_gate, w_up, w_down)


def _work_items(counts, nrows):
    nblk = nrows // MOE_ROWS
    nitems = nblk + N_EXPERTS - 1
    ends = jnp.cumsum(counts)
    starts = ends - counts
    b0 = jnp.arange(nblk, dtype=I32)[:, None] * MOE_ROWS
    lo = jnp.maximum(starts[None, :], b0)
    hi = jnp.minimum(ends[None, :], b0 + MOE_ROWS)
    nonempty = (hi > lo).reshape(-1)
    csum = jnp.cumsum(nonempty.astype(I32))
    j = jnp.arange(nitems, dtype=I32)
    pos = jnp.sum(csum[None, :] <= j[:, None], axis=1).astype(I32)
    used = j < csum[-1]
    pos = jnp.where(used, pos, jnp.max(jnp.where(nonempty, jnp.arange(nonempty.shape[0], dtype=I32), 0)))
    ib = pos // N_EXPERTS
    ie = pos % N_EXPERTS
    ilo = jnp.where(used, lo.reshape(-1)[pos] - ib * MOE_ROWS, 0)
    ihi = jnp.where(used, hi.reshape(-1)[pos] - ib * MOE_ROWS, 0)
    spare_blk = ib + 1 + (j - csum[-1])
    fill = jnp.logical_and(jnp.logical_not(used), spare_blk < nblk)
    ib = jnp.where(used, ib, jnp.minimum(spare_blk, nblk - 1))
    ilo = jnp.where(fill, -1, ilo)
    ordinal = jnp.cumsum(jnp.concatenate([jnp.zeros((1,), I32), (ie[1:] != ie[:-1]).astype(I32)]))
    first_next = jnp.sum(ordinal[None, :] <= ordinal[:, None], axis=1)
    has_next = first_next < nitems
    nxt = jnp.where(has_next, ie[jnp.minimum(first_next, nitems - 1)], -1)
    return ib.astype(I32), ie.astype(I32), ilo.astype(I32), ihi.astype(I32), nxt.astype(I32)


def _combine_kernel(nq_ref, gq_cur, gq_nxt, h_ref, idx_ref, rank_ref, gate_ref, tab_ref, g2_ref, b2_ref, ys_hbm,
                    o_ref, stage, sem):
    s = pl.program_id(0)
    last = pl.num_programs(0) - 1
    base = (s % 2) * PAIR

    def start_fetch(gq_ref, step, slot0):
        for j in range(PAIR):
            _for_octets(nq_ref[jnp.minimum(step, last) * PAIR + j], lambda q, pri, j=j: pltpu.make_async_copy(
                _octet(ys_hbm, gq_ref[j, 0, q]), _octet(stage.at[slot0 + j], q),
                sem.at[slot0 + j]).start(priority=pri))

    @pl.when(s == 0)
    def _():
        stage[...] = jnp.zeros_like(stage)
        start_fetch(gq_cur, 0, 0)

    @pl.when(s < last)
    def _():
        start_fetch(gq_nxt, s + 1, PAIR - base)

    for j in range(PAIR):
        _wait_octets(nq_ref[s * PAIR + j], stage.at[base + j], sem.at[base + j])

    lane_e = lax.broadcasted_iota(I32, (ROUTE_ROWS, LANES), 1)
    lane_p = lax.broadcasted_iota(I32, (ROUTE_ROWS, STAGE_ROWS), 1).astype(F32)
    wsels = []
    for j in range(PAIR):
        rows = slice(j * ROUTE_ROWS, (j + 1) * ROUTE_ROWS)
        idx = idx_ref[rows, :]
        rank = rank_ref[rows, :].astype(F32)
        gates = gate_ref[rows, :]
        wsel = jnp.zeros((ROUTE_ROWS, STAGE_ROWS), F32)
        for k in range(TOP_K):
            hot = lane_e == idx[:, k:k + 1]
            lpos = jnp.sum(jnp.where(hot, tab_ref[j], 0.0), axis=-1, keepdims=True) + rank[:, k:k + 1]
            wsel = jnp.where(lane_p == lpos, gates[:, k:k + 1], wsel)
        wsels.append(wsel.astype(BF16))
    ys = [_dot(wsels[j], stage[base + j].astype(BF16)) for j in range(PAIR)]
    for j in range(PAIR):
        rows = slice(j * ROUTE_ROWS, (j + 1) * ROUTE_ROWS)
        z = DN_ALPHA * h_ref[rows, :] + ys[j]
        mu = jnp.mean(z, axis=-1, keepdims=True)
        zc = z - mu
        var = jnp.mean(zc * zc, axis=-1, keepdims=True)
        o_ref[rows, :] = zc * lax.rsqrt(var + LN_EPS) * g2_ref[...] + b2_ref[...]


def _combine(h, ys, idx, rank, gates, tab, gq3, nq, g2, b2):
    t, dm = h.shape
    step = PAIR * ROUTE_ROWS
    ns = t // step
    rowblk = lambda w: pl.BlockSpec((step, w), lambda m, nq: (m, 0))
    const = pl.BlockSpec((1, dm), lambda m, nq: (0, 0))
    qspec = lambda f: pl.BlockSpec((PAIR, 1, STAGE_OCTETS), lambda m, nq: (f(m), 0, 0), memory_space=pltpu.SMEM)
    grid_spec = pltpu.PrefetchScalarGridSpec(
        num_scalar_prefetch=1,
        grid=(ns,),
        in_specs=[qspec(lambda m: m), qspec(lambda m: jnp.minimum(m + 1, ns - 1)),
                  rowblk(dm), rowblk(LANES), rowblk(LANES), rowblk(LANES),
                  pl.BlockSpec((PAIR, 1, LANES), lambda m, nq: (m, 0, 0)), const, const,
                  pl.BlockSpec(memory_space=pl.ANY)],
        out_specs=rowblk(dm),
        scratch_shapes=[pltpu.VMEM((2 * PAIR, STAGE_ROWS, dm), F32), pltpu.SemaphoreType.DMA((2 * PAIR,))],
    )
    return pl.pallas_call(
        _combine_kernel,
        grid_spec=grid_spec,
        out_shape=jax.ShapeDtypeStruct((t, dm), F32),
        compiler_params=_params(("arbitrary",)),
        name="combine_ln",
    )(nq, gq3, gq3, h, idx, rank, gates, tab, g2, b2, ys)


def _route_tables(runs, total, nrows):
    nt = runs.shape[0]
    nxt = jnp.concatenate([runs[1:], total[None, :]], axis=0)
    c8 = (nxt - runs + OCTET - 1) // OCTET
    q_end = jnp.cumsum(c8, axis=1)
    q0 = q_end - c8
    nq = q_end[:, -1]
    per_expert = jnp.sum(c8, axis=0)
    sub8 = MOE_SUB // OCTET
    per_pad = (per_expert + sub8 - 1) // sub8 * sub8
    start8 = jnp.cumsum(per_pad) - per_pad
    g8 = start8[None, :] + jnp.cumsum(c8, axis=0) - c8
    tab = OCTET * q0 - runs
    q = jnp.arange(STAGE_OCTETS, dtype=I32)
    e_of_q = jnp.minimum(jnp.sum(q[None, :, None] >= q_end[:, None, :], axis=2), N_EXPERTS - 1)
    pick = e_of_q[:, :, None] == jnp.arange(N_EXPERTS, dtype=I32)[None, None, :]
    gq = jnp.sum(jnp.where(pick, (g8 - q0)[:, None, :], 0), axis=2) + q[None, :]
    gq = jnp.where(q[None, :] < nq[:, None], gq, 0).astype(I32)
    tot8 = jnp.sum(per_pad)
    fill_start = jnp.concatenate([start8 + per_expert, (tot8 // sub8)[None]]).astype(I32)
    fill_count = jnp.concatenate([per_pad - per_expert, (nrows // MOE_SUB - tot8 // sub8)[None]]).astype(I32)
    return (tab, gq.reshape(nt, 1, STAGE_OCTETS), nq.astype(I32), fill_start, fill_count,
            (per_pad * OCTET).astype(I32))


def _layer(x, w_in, sgu_w, sgu_b, sgu_ln_g, sgu_ln_b, mix_norm_g, w_out, ln1_g, ln1_b,
           w_router, b_router, w_gate, b_gate, w_up, b_up, w_down, b_down, ln2_g, ln2_b):
    batch, seq, dm = x.shape
    width = dm // 2
    t = batch * seq
    assert seq % ATTN_SPAN == 0 and dm % (2 * LANES) == 0 and w_router.shape[-1] == N_EXPERTS
    x2 = x.reshape(t, dm)

    wc = jnp.tril(sgu_w)
    wpair = jnp.concatenate([wc[0::2], wc[1::2]], axis=-1).astype(BF16)
    sbias = jnp.repeat(sgu_b.T, HEAD_DIM, axis=1)
    grp = jnp.arange(width) // HEAD_DIM
    gmat = jnp.where(grp[:, None] == grp[None, :], 1.0 / HEAD_DIM, 0.0).astype(BF16)
    row = lambda v: v.reshape(1, -1)

    a_n, q1, k1, v1, q4, k4, v4, q16, k16, v16 = _project(
        x2, w_in, gmat, wpair, sbias, row(sgu_ln_g), row(sgu_ln_b),
        row(mix_norm_g[:width]), batch, seq)
    b = _attention((q1, k1, v1), (q4, k4, v4), (q16, k16, v16), batch, seq)

    wr = jnp.pad(w_router, ((0, 0), (0, LANES - N_EXPERTS))).astype(BF16)
    br = jnp.concatenate([b_router.astype(F32), jnp.full((LANES - N_EXPERTS,), -1e30, F32)]).reshape(1, LANES)
    ti = jnp.arange(ROUTE_ROWS)
    tri = (ti[None, :] < ti[:, None]).astype(BF16)
    h, idx, gates, rank, idxt, rankt, runs, cnt = _outproj(
        x2, a_n.reshape(t, width), b.reshape(t, width), w_out,
        row(mix_norm_g[width:]), row(ln1_g), row(ln1_b), wr, br, tri)

    nt = t // ROUTE_ROWS
    worst = t * TOP_K + nt * N_EXPERTS * (OCTET - 1) + N_EXPERTS * (MOE_SUB - OCTET)
    nrows = (worst + MOE_ROWS - 1) // MOE_ROWS * MOE_ROWS
    runs = runs.reshape(nt, LANES)[:, :N_EXPERTS].astype(I32)
    total = cnt[0, :N_EXPERTS].astype(I32)
    tab, gq3, nq, fill_start, fill_count, rows_e = _route_tables(runs, total, nrows)
    tabf = tab.astype(F32)
    tab_lane = jnp.pad(tabf, ((0, 0), (0, LANES - N_EXPERTS))).reshape(nt, 1, LANES)
    tab_sub = jnp.broadcast_to(tabf[:, :, None], (nt, N_EXPERTS, ROUTE_ROWS))

    xs = _dispatch(h, idxt, rankt, tab_sub, gq3, nq, fill_start, fill_count, nrows)
    ys = _moe(xs, _work_items(rows_e, nrows), w_gate, b_gate, w_up, b_up, w_down, b_down)
    out = _combine(h, ys, idx, rank, gates, tab_lane, gq3, nq, row(ln2_g), row(ln2_b))
    return out.reshape(batch, seq, dm)


def kernel(x, w_in, sgu_w, sgu_b, sgu_ln_g, sgu_ln_b, mix_norm_g, w_out, ln1_g, ln1_b, w_router, b_router,
           w_gate, b_gate, w_up, b_up, w_down, b_down, ln2_g, ln2_b):
    assert w_in.shape[0] == DEPTH
    return _layer(x, w_in[0], sgu_w[0], sgu_b[0], sgu_ln_g[0], sgu_ln_b[0], mix_norm_g[0], w_out[0],
                  ln1_g[0], ln1_b[0], w_router[0], b_router[0], w_gate[0], b_gate[0], w_up[0], b_up[0],
                  w_down[0], b_down[0], ln2_g[0], ln2_b[0])
```

```python
import jax
import jax.numpy as jnp
from jax import lax
from jax.experimental import pallas as pl
from jax.experimental.pallas import tpu as pltpu

F32 = jnp.float32
BF16 = jnp.bfloat16
I32 = jnp.int32

LN_EPS = 1e-5
HEAD_DIM = 64
SGU_CHUNK = 128
BAND_BLOCK = 128
DILATIONS = (1, 4, 16)
ATTN_SPAN = BAND_BLOCK * DILATIONS[-1]
N_EXPERTS = 32
TOP_K = 4
SWIGLU_ALPHA = 1.702
SWIGLU_LIMIT = 7.0
DEPTH = 1
DN_ALPHA = (2 * DEPTH) ** 0.25
LOG2E = 1.4426950408889634

LANES = 128
PROJ_ROWS = 512
OUT_ROWS = 1024
MOE_ROWS = 1024
MOE_CHUNK = 512
MOE_SUB = 256
ROUTE_ROWS = 256
OCTET = 8
STAGE_ROWS = ROUTE_ROWS * TOP_K + N_EXPERTS * OCTET
STAGE_OCTETS = STAGE_ROWS // OCTET
PAIR = 2
VMEM_LIMIT = 56 * 1024 * 1024


def _params(sem):
    return pltpu.CompilerParams(dimension_semantics=sem, vmem_limit_bytes=VMEM_LIMIT)


def _dot(a, b):
    return jnp.dot(a, b, preferred_element_type=F32)


def _proj_kernel(x_ref, w_ref, gmat_ref, wpair_ref, sbias_ref, lng_ref, lnb_ref, mg_ref,
                 a_ref, q1_ref, k1_ref, v1_ref, q4_ref, k4_ref, v4_ref, q16_ref, k16_ref, v16_ref,
                 a_scr, t_scr, t4_scr, wb_scr):
    width = a_ref.shape[-1]

    @pl.when(jnp.logical_and(pl.program_id(0) == 0, pl.program_id(1) == 0))
    def _():
        wb_scr[...] = w_ref[...].astype(BF16)

    xb = x_ref[...].astype(BF16)

    def proj(c):
        return _dot(xb, wb_scr[:, c * width:(c + 1) * width])

    u = jax.nn.gelu(proj(0))
    v = jax.nn.gelu(proj(1))
    mean = _dot(v.astype(BF16), gmat_ref[...])
    d = v - mean
    var = _dot((d * d).astype(BF16), gmat_ref[...])
    vn = (d * lax.rsqrt(var + LN_EPS) * lng_ref[...] + lnb_ref[...]).astype(BF16)

    lane = lax.broadcasted_iota(I32, (SGU_CHUNK, LANES), 1)
    low = lane < HEAD_DIM
    zero = jnp.zeros((SGU_CHUNK, LANES), BF16)
    for c in range(PROJ_ROWS // SGU_CHUNK):
        rows = slice(c * SGU_CHUNK, (c + 1) * SGU_CHUNK)
        for j in range(width // LANES):
            cols = slice(j * LANES, (j + 1) * LANES)
            vp = vn[rows, cols]
            rhs = jnp.concatenate([jnp.where(low, vp, zero), jnp.where(low, zero, vp)], axis=0)
            gate = _dot(wpair_ref[j], rhs) + sbias_ref[:, cols]
            a_scr[rows, cols] = u[rows, cols] * gate
    a = a_scr[...]
    ms = jnp.mean(a * a, axis=-1, keepdims=True)
    a_ref[...] = (a * lax.rsqrt(ms + LN_EPS) * mg_ref[...]).astype(BF16)

    outs = ((q1_ref, q4_ref, q16_ref), (k1_ref, k4_ref, k16_ref), (v1_ref, v4_ref, v16_ref))
    for c, (o1, o4, o16) in enumerate(outs):
        t = proj(2 + c)
        if c == 0:
            t = t * (HEAD_DIM ** -0.5 * LOG2E)
        o1[...] = t.astype(BF16)
        for j in range(width // LANES):
            cols = slice(j * LANES, (j + 1) * LANES)
            t_scr[j] = t[:, cols]
            for b in range(4):
                t4 = t_scr[j, pl.ds(b, PROJ_ROWS // 4, stride=4), :]
                o4[b, :, cols] = t4.astype(BF16)
                t4_scr[b] = t4
                for a in range(4):
                    o16[4 * a + b, :, cols] = t4_scr[b, pl.ds(a, PROJ_ROWS // 16, stride=4), :].astype(BF16)


def _project(x2, w_in, gmat, wpair, sbias, lng, lnb, mg, batch, seq):
    t, dm = x2.shape
    width = dm // 2
    nt = seq // PROJ_ROWS
    per_span = ATTN_SPAN // PROJ_ROWS
    const = lambda *shape: pl.BlockSpec(shape, lambda b, m: (0,) * len(shape))
    o1 = jax.ShapeDtypeStruct((batch, seq, width), BF16)
    o4 = jax.ShapeDtypeStruct((batch, seq // 512, 4, BAND_BLOCK, width), BF16)
    o16 = jax.ShapeDtypeStruct((batch, seq // ATTN_SPAN, 16, BAND_BLOCK, width), BF16)
    s1 = pl.BlockSpec((None, PROJ_ROWS, width), lambda b, m: (b, m, 0))
    s4 = pl.BlockSpec((None, None, 4, BAND_BLOCK, width), lambda b, m: (b, m, 0, 0, 0))
    s16 = pl.BlockSpec((None, None, 16, PROJ_ROWS // 16, width),
                       lambda b, m: (b, m // per_span, 0, m % per_span, 0))
    return pl.pallas_call(
        _proj_kernel,
        grid=(batch, nt),
        in_specs=[pl.BlockSpec((PROJ_ROWS, dm), lambda b, m: (b * nt + m, 0)),
                  pl.BlockSpec(w_in.shape, lambda b, m: (0, 0), pipeline_mode=pl.Buffered(1)),
                  const(*gmat.shape), const(*wpair.shape), const(*sbias.shape),
                  const(1, width), const(1, width), const(1, width)],
        out_specs=[s1] + [s1, s1, s1] + [s4, s4, s4] + [s16, s16, s16],
        out_shape=[o1] + [o1, o1, o1] + [o4, o4, o4] + [o16, o16, o16],
        scratch_shapes=[pltpu.VMEM((PROJ_ROWS, width), F32),
                        pltpu.VMEM((width // LANES, PROJ_ROWS, LANES), F32),
                        pltpu.VMEM((4, PROJ_ROWS // 4, LANES), F32),
                        pltpu.VMEM(w_in.shape, BF16)],
        compiler_params=_params(("arbitrary", "arbitrary")),
        name="proj_sgu",
    )(x2, w_in, gmat, wpair, sbias, lng, lnb, mg)


def _attn_kernel(q1, k1, v1, kp1, vp1, q4, k4, v4, kp4, vp4, q16, k16, v16, kp16, vp16,
                 o_ref, acc_o, acc_m, acc_l):
    not_first = pl.program_id(1) > 0
    blk = BAND_BLOCK
    row = lax.broadcasted_iota(I32, (2 * blk, 2 * blk), 0) % blk
    col = lax.broadcasted_iota(I32, (2 * blk, 2 * blk), 1)
    band = jnp.logical_or(jnp.logical_and(col < blk, col >= row), jnp.logical_and(col >= blk, col - blk <= row))
    neg = jnp.where(not_first, 0.0, -jnp.inf).astype(F32)
    first_pen = jnp.where(col < blk, neg, 0.0)
    low = lax.broadcasted_iota(I32, (blk, LANES), 1) < HEAD_DIM
    zero = jnp.zeros((blk, LANES), BF16)
    ones = jnp.ones((2 * blk, LANES), BF16)

    def unit(q, kprev, kcur, vprev, vcur, maybe_first, rows, init):
        kk = jnp.concatenate([kprev, kcur], axis=0)
        vv = jnp.concatenate([vprev, vcur], axis=0)
        qq = jnp.concatenate([jnp.where(low, q, zero), jnp.where(low, zero, q)], axis=0)
        s = lax.dot_general(qq, kk, (((1,), (1,)), ((), ())), preferred_element_type=F32)
        s = jnp.where(band, s, -jnp.inf)
        if maybe_first:
            s = s + first_pen
        m2 = jnp.max(s, axis=-1, keepdims=True)
        p = jnp.exp2(s - m2)
        o2 = _dot(p.astype(BF16), jnp.concatenate([vv, ones], axis=1))
        o = jnp.where(low, o2[:blk, :LANES], o2[blk:, :LANES])
        m = jnp.where(low, m2[:blk], m2[blk:])
        l = jnp.where(low, o2[:blk, LANES:], o2[blk:, LANES:])
        if init:
            acc_o[rows, :] = o
            acc_m[rows, :] = m
            acc_l[rows, :] = l
        else:
            mo = acc_m[rows, :]
            mn = jnp.maximum(mo, m)
            so = jnp.exp2(mo - mn)
            sn = jnp.exp2(m - mn)
            acc_o[rows, :] = acc_o[rows, :] * so + o * sn
            acc_l[rows, :] = acc_l[rows, :] * so + l * sn
            acc_m[rows, :] = mn


    for n in range(ATTN_SPAN // blk):
        cur = pl.ds(n * blk, blk)
        if n == 0:
            unit(q1[cur, :], kp1[...], k1[cur, :], vp1[...], v1[cur, :], True, cur, True)
        else:
            prev = pl.ds((n - 1) * blk, blk)
            unit(q1[cur, :], k1[prev, :], k1[cur, :], v1[prev, :], v1[cur, :], False, cur, True)

    for s in range(ATTN_SPAN // (4 * blk)):
        for r in range(4):
            rows = pl.ds(s * (4 * blk) + r, blk, stride=4)
            if s == 0:
                unit(q4[0, r], kp4[r], k4[0, r], vp4[r], v4[0, r], True, rows, False)
            else:
                unit(q4[s, r], k4[s - 1, r], k4[s, r], v4[s - 1, r], v4[s, r], False, rows, False)

    for r in range(16):
        unit(q16[r], kp16[r], k16[r], vp16[r], v16[r], True, pl.ds(r, blk, stride=16), False)

    o_ref[...] = (acc_o[...] / acc_l[...]).astype(BF16)


def _attention(qkv1, qkv4, qkv16, batch, seq):
    width = qkv1[0].shape[-1]
    nspan = seq // ATTN_SPAN
    blk = BAND_BLOCK
    n1 = ATTN_SPAN // blk
    n4 = ATTN_SPAN // (4 * blk)

    cur1 = pl.BlockSpec((None, ATTN_SPAN, LANES), lambda b, i, p: (b, i, p))
    prev1 = pl.BlockSpec((None, None, blk, LANES), lambda b, i, p: (b, jnp.maximum(i * n1 - 1, 0), 0, p))
    cur4 = pl.BlockSpec((None, n4, 4, blk, LANES), lambda b, i, p: (b, i, 0, 0, p))
    prev4 = pl.BlockSpec((None, None, 4, blk, LANES), lambda b, i, p: (b, jnp.maximum(i * n4 - 1, 0), 0, 0, p))
    cur16 = pl.BlockSpec((None, None, 16, blk, LANES), lambda b, i, p: (b, i, 0, 0, p))
    prev16 = pl.BlockSpec((None, None, 16, blk, LANES), lambda b, i, p: (b, jnp.maximum(i - 1, 0), 0, 0, p))

    q1, k1, v1 = qkv1
    q4, k4, v4 = qkv4
    q16, k16, v16 = qkv16
    k1b = k1.reshape(batch, seq // blk, blk, width)
    v1b = v1.reshape(batch, seq // blk, blk, width)
    return pl.pallas_call(
        _attn_kernel,
        grid=(batch, nspan, width // LANES),
        in_specs=[cur1, cur1, cur1, prev1, prev1,
                  cur4, cur4, cur4, prev4, prev4,
                  cur16, cur16, cur16, prev16, prev16],
        out_specs=pl.BlockSpec((None, ATTN_SPAN, LANES), lambda b, i, p: (b, i, p)),
        out_shape=jax.ShapeDtypeStruct((batch, seq, width), BF16),
        scratch_shapes=[pltpu.VMEM((ATTN_SPAN, LANES), F32)] * 3,
        compiler_params=_params(("parallel", "parallel", "parallel")),
        name="dilated_attn",
    )(q1, k1, v1, k1b, v1b, q4, k4, v4, k4, v4, q16, k16, v16, k16, v16)


def _outproj_kernel(x_ref, a_ref, b_ref, wo_ref, mgb_ref, g1_ref, b1_ref, wr_ref, br_ref, tri_ref,
                    h_ref, idx_ref, gate_ref, rank_ref, idxt_ref, rankt_ref, runs_ref, cnt_ref, run_scr, wob_scr):
    @pl.when(pl.program_id(0) == 0)
    def _():
        run_scr[...] = jnp.zeros_like(run_scr)
        wob_scr[...] = wo_ref[...].astype(BF16)

    width = a_ref.shape[-1]
    lane = lax.broadcasted_iota(I32, (ROUTE_ROWS, LANES), 1).astype(F32)

    def spread(cols):
        out = jnp.zeros((ROUTE_ROWS, LANES), F32)
        for k, cval in enumerate(cols):
            out = jnp.where(lane == float(k), cval, out)
        return out

    subs = [slice(j * ROUTE_ROWS, (j + 1) * ROUTE_ROWS) for j in range(OUT_ROWS // ROUTE_ROWS)]
    works = []
    for rows in subs:
        bf = b_ref[rows, :].astype(F32)
        bn = (bf * lax.rsqrt(jnp.mean(bf * bf, axis=-1, keepdims=True) + LN_EPS) * mgb_ref[...]).astype(BF16)
        mixed = _dot(a_ref[rows, :], wob_scr[0:width, :]) + _dot(bn, wob_scr[width:2 * width, :])
        z = DN_ALPHA * x_ref[rows, :] + mixed
        mu = jnp.mean(z, axis=-1, keepdims=True)
        zc = z - mu
        var = jnp.mean(zc * zc, axis=-1, keepdims=True)
        h = zc * lax.rsqrt(var + LN_EPS) * g1_ref[...] + b1_ref[...]
        h_ref[rows, :] = h
        works.append(_dot(h.astype(BF16), wr_ref[...]) + br_ref[...])

    vals = [[] for _ in subs]
    idxs = [[] for _ in subs]
    hots = [[] for _ in subs]
    for _ in range(TOP_K):
        for j in range(len(subs)):
            mv = jnp.max(works[j], axis=-1, keepdims=True)
            ix = jnp.min(jnp.where(works[j] == mv, lane, float(LANES)), axis=-1, keepdims=True)
            hot = lane == ix
            works[j] = jnp.where(hot, -jnp.inf, works[j])
            vals[j].append(mv)
            idxs[j].append(ix)
            hots[j].append(hot)

    members = []
    for j in range(len(subs)):
        member = jnp.zeros((ROUTE_ROWS, LANES), F32)
        for hot in hots[j]:
            member = jnp.where(hot, 1.0, member)
        members.append(member)
    withins = [_dot(tri_ref[...], member.astype(BF16)) for member in members]
    run = run_scr[...]
    for j, rows in enumerate(subs):
        runs_ref[j] = run
        before = withins[j] + run
        ranks = [jnp.sum(jnp.where(hot, before, 0.0), axis=-1, keepdims=True) for hot in hots[j]]
        run = run + jnp.sum(members[j], axis=0, keepdims=True)
        exps = [jnp.exp(v - vals[j][0]) for v in vals[j]]
        den = exps[0] + exps[1] + exps[2] + exps[3]
        idx_all = spread(idxs[j])
        rank_all = spread(ranks)
        idx_ref[rows, :] = idx_all.astype(I32)
        gate_ref[rows, :] = spread([e / den for e in exps])
        rank_ref[rows, :] = rank_all.astype(I32)
        idxt_ref[:, rows] = idx_all.T[:OCTET].astype(I32)
        rankt_ref[:, rows] = rank_all.T[:OCTET].astype(I32)
    run_scr[...] = run
    cnt_ref[...] = run


def _outproj(x2, a_n, b2, w_out, mgb, g1, b1, wr, br, tri):
    t, dm = x2.shape
    width = dm // 2
    const = lambda *shape: pl.BlockSpec(shape, lambda m: (0,) * len(shape))
    rowblk = lambda w: pl.BlockSpec((OUT_ROWS, w), lambda m: (m, 0))
    colblk = pl.BlockSpec((OCTET, OUT_ROWS), lambda m: (0, m))
    sub = OUT_ROWS // ROUTE_ROWS
    return pl.pallas_call(
        _outproj_kernel,
        grid=(t // OUT_ROWS,),
        in_specs=[rowblk(dm), rowblk(width), rowblk(width),
                  pl.BlockSpec((dm, dm), lambda m: (0, 0), pipeline_mode=pl.Buffered(1)), const(1, width),
                  const(1, dm), const(1, dm), const(dm, LANES), const(1, LANES),
                  const(ROUTE_ROWS, ROUTE_ROWS)],
        out_specs=[rowblk(dm), rowblk(LANES), rowblk(LANES), rowblk(LANES), colblk, colblk,
                   pl.BlockSpec((None, sub, 1, LANES), lambda m: (m, 0, 0, 0)), const(1, LANES)],
        out_shape=[jax.ShapeDtypeStruct((t, dm), F32), jax.ShapeDtypeStruct((t, LANES), I32),
                   jax.ShapeDtypeStruct((t, LANES), F32), jax.ShapeDtypeStruct((t, LANES), I32),
                   jax.ShapeDtypeStruct((OCTET, t), I32), jax.ShapeDtypeStruct((OCTET, t), I32),
                   jax.ShapeDtypeStruct((t // OUT_ROWS, sub, 1, LANES), F32),
                   jax.ShapeDtypeStruct((1, LANES), F32)],
        scratch_shapes=[pltpu.VMEM((1, LANES), F32), pltpu.VMEM((dm, dm), BF16)],
        compiler_params=_params(("arbitrary",)),
        name="outproj_router",
    )(x2, a_n, b2, w_out, mgb, g1, b1, wr, br, tri)


def _octet(ref, q):
    return ref.at[pl.ds(pl.multiple_of(q * OCTET, OCTET), OCTET), :]


def _sub_block(ref, b):
    return ref.at[pl.ds(pl.multiple_of(b * MOE_SUB, MOE_SUB), MOE_SUB), :]


def _for_octets(n, start):
    def body(i, c):
        for j in range(4):
            start(4 * i + j, j % 2)
        return c

    lax.fori_loop(0, n // 4, body, 0)
    done = (n // 4) * 4

    @pl.when((n & 2) != 0)
    def _():
        start(done, 0)
        start(done + 1, 1)

    @pl.when((n & 1) != 0)
    def _():
        start(n - 1, 0)


def _wait_octets(n, ref, sem):
    for s in (128, 64, 32, 16, 8, 4, 2, 1):
        @pl.when((n & s) != 0)
        def _():
            d = ref.at[pl.ds(0, s * OCTET), :]
            pltpu.make_async_copy(d, d, sem).wait()


def _dispatch_kernel(nq_ref, fstart_ref, fcount_ref, gq_ref, h_ref, idxt_ref, rankt_ref, tabt_ref, xs_hbm,
                     stage, zeros, sem, zsem):
    s = pl.program_id(0)
    base = (s % 2) * PAIR
    eid = lax.broadcasted_iota(I32, (N_EXPERTS, ROUTE_ROWS), 0)
    pos = lax.broadcasted_iota(I32, (STAGE_ROWS, ROUTE_ROWS), 0).astype(F32)
    sels = []
    for j in range(PAIR):
        cols = slice(j * ROUTE_ROWS, (j + 1) * ROUTE_ROWS)
        sel = None
        for k in range(TOP_K):
            hot = eid == idxt_ref[k:k + 1, cols]
            lpos = (jnp.sum(jnp.where(hot, tabt_ref[j], 0.0), axis=0, keepdims=True)
                    + rankt_ref[k:k + 1, cols].astype(F32))
            hit = pos == lpos
            sel = hit if sel is None else jnp.logical_or(sel, hit)
        sels.append(jnp.where(sel, 1.0, 0.0).astype(BF16))
    for j in range(PAIR):
        rows = slice(j * ROUTE_ROWS, (j + 1) * ROUTE_ROWS)
        stage[base + j] = _dot(sels[j], h_ref[rows, :].astype(BF16))

    @pl.when(s == 0)
    def _():
        zeros[...] = jnp.zeros_like(zeros)
        for g in range(N_EXPERTS):
            def zstart(i, c, g=g):
                pltpu.make_async_copy(_octet(zeros, 0), _octet(xs_hbm, fstart_ref[g] + i), zsem.at[0]).start()
                return c

            lax.fori_loop(0, fcount_ref[g], zstart, 0)

        def tstart(i, c):
            pltpu.make_async_copy(zeros, _sub_block(xs_hbm, fstart_ref[N_EXPERTS] + i), zsem.at[1]).start()
            return c

        lax.fori_loop(0, fcount_ref[N_EXPERTS], tstart, 0)

    for j in range(PAIR):
        _for_octets(nq_ref[s * PAIR + j], lambda q, pri, j=j: pltpu.make_async_copy(
            _octet(stage.at[base + j], q), _octet(xs_hbm, gq_ref[j, 0, q]), sem.at[base + j]).start(priority=pri))

    @pl.when(s > 0)
    def _():
        for j in range(PAIR):
            _wait_octets(nq_ref[jnp.maximum(s - 1, 0) * PAIR + j], xs_hbm, sem.at[PAIR - base + j])

    @pl.when(s == pl.num_programs(0) - 1)
    def _():
        for j in range(PAIR):
            _wait_octets(nq_ref[s * PAIR + j], xs_hbm, sem.at[base + j])

        def zwait(i, c):
            pltpu.make_async_copy(_octet(zeros, 0), _octet(xs_hbm, 0), zsem.at[0]).wait()
            return c

        for g in range(N_EXPERTS):
            lax.fori_loop(0, fcount_ref[g], zwait, 0)

        def twait(i, c):
            pltpu.make_async_copy(zeros, _sub_block(xs_hbm, 0), zsem.at[1]).wait()
            return c

        lax.fori_loop(0, fcount_ref[N_EXPERTS], twait, 0)


def _dispatch(h, idxt, rankt, tabt, gq3, nq, fill_start, fill_count, nrows):
    t, dm = h.shape
    step = PAIR * ROUTE_ROWS
    grid_spec = pltpu.PrefetchScalarGridSpec(
        num_scalar_prefetch=3,
        grid=(t // step,),
        in_specs=[pl.BlockSpec((PAIR, 1, STAGE_OCTETS), lambda m, *_: (m, 0, 0), memory_space=pltpu.SMEM),
                  pl.BlockSpec((step, dm), lambda m, *_: (m, 0)),
                  pl.BlockSpec((OCTET, step), lambda m, *_: (0, m)),
                  pl.BlockSpec((OCTET, step), lambda m, *_: (0, m)),
                  pl.BlockSpec((PAIR, N_EXPERTS, ROUTE_ROWS), lambda m, *_: (m, 0, 0))],
        out_specs=pl.BlockSpec(memory_space=pl.ANY),
        scratch_shapes=[pltpu.VMEM((2 * PAIR, STAGE_ROWS, dm), F32), pltpu.VMEM((MOE_SUB, dm), F32),
                        pltpu.SemaphoreType.DMA((2 * PAIR,)), pltpu.SemaphoreType.DMA((2,))],
    )
    return pl.pallas_call(
        _dispatch_kernel,
        grid_spec=grid_spec,
        out_shape=jax.ShapeDtypeStruct((nrows, dm), F32),
        compiler_params=_params(("arbitrary",)),
        name="moe_dispatch",
    )(nq, fill_start, fill_count, gq3, h, idxt, rankt, tabt)


def _moe_kernel(ib_ref, ie_ref, lo_ref, hi_ref, nxt_ref, xs_ref, bg_ref, bu_ref, bd_ref,
                wg_hbm, wu_hbm, wd_hbm, ys_ref, wf, wb, wsem):
    i = pl.program_id(0)
    prev = jnp.maximum(i - 1, 0)
    lo = lo_ref[i]
    hi = hi_ref[i]

    def weight_copies(e):
        return [pltpu.make_async_copy(w.at[e], wf.at[j], wsem) for j, w in enumerate((wg_hbm, wu_hbm, wd_hbm))]

    @pl.when(jnp.logical_and(lo >= 0, hi > lo))
    def _():
        @pl.when(jnp.logical_or(i == 0, ie_ref[i] != ie_ref[prev]))
        def _():
            @pl.when(i == 0)
            def _():
                for cp in weight_copies(ie_ref[i]):
                    cp.start()

            for cp in weight_copies(ie_ref[i]):
                cp.wait()
            for j in range(3):
                wb[j] = wf[j].astype(BF16)

            @pl.when(nxt_ref[i] >= 0)
            def _():
                for cp in weight_copies(nxt_ref[i]):
                    cp.start()

        def expert(rows):
            x = xs_ref[rows, :].astype(BF16)
            g = jnp.minimum(_dot(x, wb[0]) + bg_ref[...], SWIGLU_LIMIT)
            u = jnp.clip(_dot(x, wb[1]) + bu_ref[...], -SWIGLU_LIMIT, SWIGLU_LIMIT)
            act = (u + 1.0) * (g * jax.nn.sigmoid(SWIGLU_ALPHA * g))
            return _dot(act.astype(BF16), wb[2]) + bd_ref[...]

        whole = jnp.logical_and(lo == 0, hi == MOE_ROWS)

        @pl.when(whole)
        def _():
            for j in range(MOE_ROWS // MOE_CHUNK):
                rows = slice(j * MOE_CHUNK, (j + 1) * MOE_CHUNK)
                ys_ref[rows, :] = expert(rows)

        @pl.when(jnp.logical_not(whole))
        def _():
            @pl.when(jnp.logical_or(i == 0, ib_ref[i] != ib_ref[prev]))
            def _():
                ys_ref[...] = jnp.zeros_like(ys_ref)

            mine = [jnp.logical_and(lo < (j + 1) * MOE_SUB, hi > j * MOE_SUB) for j in range(MOE_ROWS // MOE_SUB)]
            for j in range(0, MOE_ROWS // MOE_SUB, 2):
                for first, second in ((True, True), (True, False), (False, True)):
                    rows = slice((j if first else j + 1) * MOE_SUB, (j + 2 if second else j + 1) * MOE_SUB)
                    cond = jnp.logical_and(mine[j] if first else jnp.logical_not(mine[j]),
                                           mine[j + 1] if second else jnp.logical_not(mine[j + 1]))

                    @pl.when(cond)
                    def _(rows=rows):
                        ys_ref[rows, :] = expert(rows)

    @pl.when(lo < 0)
    def _():
        ys_ref[...] = jnp.zeros_like(ys_ref)


def _moe(xs, items, w_gate, b_gate, w_up, b_up, w_down, b_down):
    nrows = xs.shape[0]
    ne, dm, df = w_gate.shape
    assert dm == df
    nitems = items[0].shape[0]
    bspec = lambda c: pl.BlockSpec((None, 1, c), lambda i, ib, ie, *_: (ie[i], 0, 0))
    rspec = pl.BlockSpec((MOE_ROWS, dm), lambda i, ib, *_: (ib[i], 0))
    hbm = pl.BlockSpec(memory_space=pl.ANY)
    grid_spec = pltpu.PrefetchScalarGridSpec(
        num_scalar_prefetch=5,
        grid=(nitems,),
        in_specs=[rspec, bspec(df), bspec(df), bspec(dm), hbm, hbm, hbm],
        out_specs=rspec,
        scratch_shapes=[pltpu.VMEM((3, dm, df), F32), pltpu.VMEM((3, dm, df), BF16),
                        pltpu.SemaphoreType.DMA(())],
    )
    return pl.pallas_call(
        _moe_kernel,
        grid_spec=grid_spec,
        out_shape=jax.ShapeDtypeStruct((nrows, dm), F32),
        compiler_params=_params(("arbitrary",)),
        name="moe_experts",
    )(*items, xs, b_gate.reshape(ne, 1, df), b_up.reshape(ne, 1, df), b_down.reshape(ne, 1, dm),
      w_gate, w_up, w_down)


def _work_items(counts, nrows):
    nblk = nrows // MOE_ROWS
    nitems = nblk + N_EXPERTS - 1
    ends = jnp.cumsum(counts)
    starts = ends - counts
    b0 = jnp.arange(nblk, dtype=I32)[:, None] * MOE_ROWS
    lo = jnp.maximum(starts[None, :], b0)
    hi = jnp.minimum(ends[None, :], b0 + MOE_ROWS)
    nonempty = (hi > lo).reshape(-1)
    csum = jnp.cumsum(nonempty.astype(I32))
    j = jnp.arange(nitems, dtype=I32)
    pos = jnp.sum(csum[None, :] <= j[:, None], axis=1).astype(I32)
    used = j < csum[-1]
    pos = jnp.where(used, pos, jnp.max(jnp.where(nonempty, jnp.arange(nonempty.shape[0], dtype=I32), 0)))
    ib = pos // N_EXPERTS
    ie = pos % N_EXPERTS
    ilo = jnp.where(used, lo.reshape(-1)[pos] - ib * MOE_ROWS, 0)
    ihi = jnp.where(used, hi.reshape(-1)[pos] - ib * MOE_ROWS, 0)
    spare_blk = ib + 1 + (j - csum[-1])
    fill = jnp.logical_and(jnp.logical_not(used), spare_blk < nblk)
    ib = jnp.where(used, ib, jnp.minimum(spare_blk, nblk - 1))
    ilo = jnp.where(fill, -1, ilo)
    ordinal = jnp.cumsum(jnp.concatenate([jnp.zeros((1,), I32), (ie[1:] != ie[:-1]).astype(I32)]))
    first_next = jnp.sum(ordinal[None, :] <= ordinal[:, None], axis=1)
    has_next = first_next < nitems
    nxt = jnp.where(has_next, ie[jnp.minimum(first_next, nitems - 1)], -1)
    return ib.astype(I32), ie.astype(I32), ilo.astype(I32), ihi.astype(I32), nxt.astype(I32)


def _combine_kernel(nq_ref, gq_cur, gq_nxt, h_ref, idx_ref, rank_ref, gate_ref, tab_ref, g2_ref, b2_ref, ys_hbm,
                    o_ref, stage, sem):
    s = pl.program_id(0)
    last = pl.num_programs(0) - 1
    base = (s % 2) * PAIR

    def start_fetch(gq_ref, step, slot0):
        for j in range(PAIR):
            _for_octets(nq_ref[jnp.minimum(step, last) * PAIR + j], lambda q, pri, j=j: pltpu.make_async_copy(
                _octet(ys_hbm, gq_ref[j, 0, q]), _octet(stage.at[slot0 + j], q),
                sem.at[slot0 + j]).start(priority=pri))

    @pl.when(s == 0)
    def _():
        stage[...] = jnp.zeros_like(stage)
        start_fetch(gq_cur, 0, 0)

    @pl.when(s < last)
    def _():
        start_fetch(gq_nxt, s + 1, PAIR - base)

    for j in range(PAIR):
        _wait_octets(nq_ref[s * PAIR + j], stage.at[base + j], sem.at[base + j])

    lane_e = lax.broadcasted_iota(I32, (ROUTE_ROWS, LANES), 1)
    lane_p = lax.broadcasted_iota(I32, (ROUTE_ROWS, STAGE_ROWS), 1).astype(F32)
    wsels = []
    for j in range(PAIR):
        rows = slice(j * ROUTE_ROWS, (j + 1) * ROUTE_ROWS)
        idx = idx_ref[rows, :]
        rank = rank_ref[rows, :].astype(F32)
        gates = gate_ref[rows, :]
        wsel = jnp.zeros((ROUTE_ROWS, STAGE_ROWS), F32)
        for k in range(TOP_K):
            hot = lane_e == idx[:, k:k + 1]
            lpos = jnp.sum(jnp.where(hot, tab_ref[j], 0.0), axis=-1, keepdims=True) + rank[:, k:k + 1]
            wsel = jnp.where(lane_p == lpos, gates[:, k:k + 1], wsel)
        wsels.append(wsel.astype(BF16))
    ys = [_dot(wsels[j], stage[base + j].astype(BF16)) for j in range(PAIR)]
    for j in range(PAIR):
        rows = slice(j * ROUTE_ROWS, (j + 1) * ROUTE_ROWS)
        z = DN_ALPHA * h_ref[rows, :] + ys[j]
        mu = jnp.mean(z, axis=-1, keepdims=True)
        zc = z - mu
        var = jnp.mean(zc * zc, axis=-1, keepdims=True)
        o_ref[rows, :] = zc * lax.rsqrt(var + LN_EPS) * g2_ref[...] + b2_ref[...]


def _combine(h, ys, idx, rank, gates, tab, gq3, nq, g2, b2):
    t, dm = h.shape
    step = PAIR * ROUTE_ROWS
    ns = t // step
    rowblk = lambda w: pl.BlockSpec((step, w), lambda m, nq: (m, 0))
    const = pl.BlockSpec((1, dm), lambda m, nq: (0, 0))
    qspec = lambda f: pl.BlockSpec((PAIR, 1, STAGE_OCTETS), lambda m, nq: (f(m), 0, 0), memory_space=pltpu.SMEM)
    grid_spec = pltpu.PrefetchScalarGridSpec(
        num_scalar_prefetch=1,
        grid=(ns,),
        in_specs=[qspec(lambda m: m), qspec(lambda m: jnp.minimum(m + 1, ns - 1)),
                  rowblk(dm), rowblk(LANES), rowblk(LANES), rowblk(LANES),
                  pl.BlockSpec((PAIR, 1, LANES), lambda m, nq: (m, 0, 0)), const, const,
                  pl.BlockSpec(memory_space=pl.ANY)],
        out_specs=rowblk(dm),
        scratch_shapes=[pltpu.VMEM((2 * PAIR, STAGE_ROWS, dm), F32), pltpu.SemaphoreType.DMA((2 * PAIR,))],
    )
    return pl.pallas_call(
        _combine_kernel,
        grid_spec=grid_spec,
        out_shape=jax.ShapeDtypeStruct((t, dm), F32),
        compiler_params=_params(("arbitrary",)),
        name="combine_ln",
    )(nq, gq3, gq3, h, idx, rank, gates, tab, g2, b2, ys)


def _route_tables(runs, total, nrows):
    nt = runs.shape[0]
    nxt = jnp.concatenate([runs[1:], total[None, :]], axis=0)
    c8 = (nxt - runs + OCTET - 1) // OCTET
    q_end = jnp.cumsum(c8, axis=1)
    q0 = q_end - c8
    nq = q_end[:, -1]
    per_expert = jnp.sum(c8, axis=0)
    sub8 = MOE_SUB // OCTET
    per_pad = (per_expert + sub8 - 1) // sub8 * sub8
    start8 = jnp.cumsum(per_pad) - per_pad
    g8 = start8[None, :] + jnp.cumsum(c8, axis=0) - c8
    tab = OCTET * q0 - runs
    q = jnp.arange(STAGE_OCTETS, dtype=I32)
    e_of_q = jnp.minimum(jnp.sum(q[None, :, None] >= q_end[:, None, :], axis=2), N_EXPERTS - 1)
    pick = e_of_q[:, :, None] == jnp.arange(N_EXPERTS, dtype=I32)[None, None, :]
    gq = jnp.sum(jnp.where(pick, (g8 - q0)[:, None, :], 0), axis=2) + q[None, :]
    gq = jnp.where(q[None, :] < nq[:, None], gq, 0).astype(I32)
    tot8 = jnp.sum(per_pad)
    fill_start = jnp.concatenate([start8 + per_expert, (tot8 // sub8)[None]]).astype(I32)
    fill_count = jnp.concatenate([per_pad - per_expert, (nrows // MOE_SUB - tot8 // sub8)[None]]).astype(I32)
    return (tab, gq.reshape(nt, 1, STAGE_OCTETS), nq.astype(I32), fill_start, fill_count,
            (per_pad * OCTET).astype(I32))


def _layer(x, w_in, sgu_w, sgu_b, sgu_ln_g, sgu_ln_b, mix_norm_g, w_out, ln1_g, ln1_b,
           w_router, b_router, w_gate, b_gate, w_up, b_up, w_down, b_down, ln2_g, ln2_b):
    batch, seq, dm = x.shape
    width = dm // 2
    t = batch * seq
    assert seq % ATTN_SPAN == 0 and dm % (2 * LANES) == 0 and w_router.shape[-1] == N_EXPERTS
    x2 = x.reshape(t, dm)

    wc = jnp.tril(sgu_w)
    wpair = jnp.concatenate([wc[0::2], wc[1::2]], axis=-1).astype(BF16)
    sbias = jnp.repeat(sgu_b.T, HEAD_DIM, axis=1)
    grp = jnp.arange(width) // HEAD_DIM
    gmat = jnp.where(grp[:, None] == grp[None, :], 1.0 / HEAD_DIM, 0.0).astype(BF16)
    row = lambda v: v.reshape(1, -1)

    a_n, q1, k1, v1, q4, k4, v4, q16, k16, v16 = _project(
        x2, w_in, gmat, wpair, sbias, row(sgu_ln_g), row(sgu_ln_b),
        row(mix_norm_g[:width]), batch, seq)
    b = _attention((q1, k1, v1), (q4, k4, v4), (q16, k16, v16), batch, seq)

    wr = jnp.pad(w_router, ((0, 0), (0, LANES - N_EXPERTS))).astype(BF16)
    br = jnp.concatenate([b_router.astype(F32), jnp.full((LANES - N_EXPERTS,), -1e30, F32)]).reshape(1, LANES)
    ti = jnp.arange(ROUTE_ROWS)
    tri = (ti[None, :] < ti[:, None]).astype(BF16)
    h, idx, gates, rank, idxt, rankt, runs, cnt = _outproj(
        x2, a_n.reshape(t, width), b.reshape(t, width), w_out,
        row(mix_norm_g[width:]), row(ln1_g), row(ln1_b), wr, br, tri)

    nt = t // ROUTE_ROWS
    worst = t * TOP_K + nt * N_EXPERTS * (OCTET - 1) + N_EXPERTS * (MOE_SUB - OCTET)
    nrows = (worst + MOE_ROWS - 1) // MOE_ROWS * MOE_ROWS
    runs = runs.reshape(nt, LANES)[:, :N_EXPERTS].astype(I32)
    total = cnt[0, :N_EXPERTS].astype(I32)
    tab, gq3, nq, fill_start, fill_count, rows_e = _route_tables(runs, total, nrows)
    tabf = tab.astype(F32)
    tab_lane = jnp.pad(tabf, ((0, 0), (0, LANES - N_EXPERTS))).reshape(nt, 1, LANES)
    tab_sub = jnp.broadcast_to(tabf[:, :, None], (nt, N_EXPERTS, ROUTE_ROWS))

    xs = _dispatch(h, idxt, rankt, tab_sub, gq3, nq, fill_start, fill_count, nrows)
    ys = _moe(xs, _work_items(rows_e, nrows), w_gate, b_gate, w_up, b_up, w_down, b_down)
    out = _combine(h, ys, idx, rank, gates, tab_lane, gq3, nq, row(ln2_g), row(ln2_b))
    return out.reshape(batch, seq, dm)


def kernel(x, w_in, sgu_w, sgu_b, sgu_ln_g, sgu_ln_b, mix_norm_g, w_out, ln1_g, ln1_b, w_router, b_router,
           w_gate, b_gate, w_up, b_up, w_down, b_down, ln2_g, ln2_b):
    assert w_in.shape[0] == DEPTH
    return _layer(x, w_in[0], sgu_w[0], sgu_b[0], sgu_ln_g[0], sgu_ln_b[0], mix_norm_g[0], w_out[0],
                  ln1_g[0], ln1_b[0], w_router[0], b_router[0], w_gate[0], b_gate[0], w_up[0], b_up[0],
                  w_down[0], b_down[0], ln2_g[0], ln2_b[0])
```

```python
import jax
import jax.numpy as jnp
from jax import lax
from jax.experimental import pallas as pl
from jax.experimental.pallas import tpu as pltpu

F32 = jnp.float32
BF16 = jnp.bfloat16
I32 = jnp.int32

LN_EPS = 1e-5
HEAD_DIM = 64
SGU_CHUNK = 128
BAND_BLOCK = 128
DILATIONS = (1, 4, 16)
ATTN_SPAN = BAND_BLOCK * DILATIONS[-1]
N_EXPERTS = 32
TOP_K = 4
SWIGLU_ALPHA = 1.702
SWIGLU_LIMIT = 7.0
DEPTH = 1
DN_ALPHA = (2 * DEPTH) ** 0.25
LOG2E = 1.4426950408889634

LANES = 128
PROJ_ROWS = 512
OUT_ROWS = 1024
MOE_ROWS = 1024
MOE_CHUNK = 512
MOE_SUB = 256
ROUTE_ROWS = 256
OCTET = 8
STAGE_ROWS = ROUTE_ROWS * TOP_K + N_EXPERTS * OCTET
STAGE_OCTETS = STAGE_ROWS // OCTET
PAIR = 2
VMEM_LIMIT = 56 * 1024 * 1024


def _params(sem):
    return pltpu.CompilerParams(dimension_semantics=sem, vmem_limit_bytes=VMEM_LIMIT)


def _dot(a, b):
    return jnp.dot(a, b, preferred_element_type=F32)


def _proj_kernel(x_ref, w_ref, gmat_ref, wpair_ref, sbias_ref, lng_ref, lnb_ref, mg_ref,
                 a_ref, q1_ref, k1_ref, v1_ref, q4_ref, k4_ref, v4_ref, q16_ref, k16_ref, v16_ref,
                 a_scr, t_scr, t4_scr, wb_scr):
    width = a_ref.shape[-1]

    @pl.when(jnp.logical_and(pl.program_id(0) == 0, pl.program_id(1) == 0))
    def _():
        wb_scr[...] = w_ref[...].astype(BF16)

    xb = x_ref[...].astype(BF16)

    def proj(c):
        return _dot(xb, wb_scr[:, c * width:(c + 1) * width])

    u = jax.nn.gelu(proj(0))
    v = jax.nn.gelu(proj(1))
    mean = _dot(v.astype(BF16), gmat_ref[...])
    d = v - mean
    var = _dot((d * d).astype(BF16), gmat_ref[...])
    vn = (d * lax.rsqrt(var + LN_EPS) * lng_ref[...] + lnb_ref[...]).astype(BF16)

    lane = lax.broadcasted_iota(I32, (SGU_CHUNK, LANES), 1)
    low = lane < HEAD_DIM
    zero = jnp.zeros((SGU_CHUNK, LANES), BF16)
    for c in range(PROJ_ROWS // SGU_CHUNK):
        rows = slice(c * SGU_CHUNK, (c + 1) * SGU_CHUNK)
        for j in range(width // LANES):
            cols = slice(j * LANES, (j + 1) * LANES)
            vp = vn[rows, cols]
            rhs = jnp.concatenate([jnp.where(low, vp, zero), jnp.where(low, zero, vp)], axis=0)
            gate = _dot(wpair_ref[j], rhs) + sbias_ref[:, cols]
            a_scr[rows, cols] = u[rows, cols] * gate
    a = a_scr[...]
    ms = jnp.mean(a * a, axis=-1, keepdims=True)
    a_ref[...] = (a * lax.rsqrt(ms + LN_EPS) * mg_ref[...]).astype(BF16)

    outs = ((q1_ref, q4_ref, q16_ref), (k1_ref, k4_ref, k16_ref), (v1_ref, v4_ref, v16_ref))
    for c, (o1, o4, o16) in enumerate(outs):
        t = proj(2 + c)
        if c == 0:
            t = t * (HEAD_DIM ** -0.5 * LOG2E)
        o1[...] = t.astype(BF16)
        for j in range(width // LANES):
            cols = slice(j * LANES, (j + 1) * LANES)
            t_scr[j] = t[:, cols]
            for b in range(4):
                t4 = t_scr[j, pl.ds(b, PROJ_ROWS // 4, stride=4), :]
                o4[b, :, cols] = t4.astype(BF16)
                t4_scr[b] = t4
                for a in range(4):
                    o16[4 * a + b, :, cols] = t4_scr[b, pl.ds(a, PROJ_ROWS // 16, stride=4), :].astype(BF16)


def _project(x2, w_in, gmat, wpair, sbias, lng, lnb, mg, batch, seq):
    t, dm = x2.shape
    width = dm // 2
    nt = seq // PROJ_ROWS
    per_span = ATTN_SPAN // PROJ_ROWS
    const = lambda *shape: pl.BlockSpec(shape, lambda b, m: (0,) * len(shape))
    o1 = jax.ShapeDtypeStruct((batch, seq, width), BF16)
    o4 = jax.ShapeDtypeStruct((batch, seq // 512, 4, BAND_BLOCK, width), BF16)
    o16 = jax.ShapeDtypeStruct((batch, seq // ATTN_SPAN, 16, BAND_BLOCK, width), BF16)
    s1 = pl.BlockSpec((None, PROJ_ROWS, width), lambda b, m: (b, m, 0))
    s4 = pl.BlockSpec((None, None, 4, BAND_BLOCK, width), lambda b, m: (b, m, 0, 0, 0))
    s16 = pl.BlockSpec((None, None, 16, PROJ_ROWS // 16, width),
                       lambda b, m: (b, m // per_span, 0, m % per_span, 0))
    return pl.pallas_call(
        _proj_kernel,
        grid=(batch, nt),
        in_specs=[pl.BlockSpec((PROJ_ROWS, dm), lambda b, m: (b * nt + m, 0)),
                  pl.BlockSpec(w_in.shape, lambda b, m: (0, 0), pipeline_mode=pl.Buffered(1)),
                  const(*gmat.shape), const(*wpair.shape), const(*sbias.shape),
                  const(1, width), const(1, width), const(1, width)],
        out_specs=[s1] + [s1, s1, s1] + [s4, s4, s4] + [s16, s16, s16],
        out_shape=[o1] + [o1, o1, o1] + [o4, o4, o4] + [o16, o16, o16],
        scratch_shapes=[pltpu.VMEM((PROJ_ROWS, width), F32),
                        pltpu.VMEM((width // LANES, PROJ_ROWS, LANES), F32),
                        pltpu.VMEM((4, PROJ_ROWS // 4, LANES), F32),
                        pltpu.VMEM(w_in.shape, BF16)],
        compiler_params=_params(("arbitrary", "arbitrary")),
        name="proj_sgu",
    )(x2, w_in, gmat, wpair, sbias, lng, lnb, mg)


def _attn_kernel(q1, k1, v1, kp1, vp1, q4, k4, v4, kp4, vp4, q16, k16, v16, kp16, vp16,
                 o_ref, buf1, buf4, out_scr):
    not_first = pl.program_id(1) > 0
    blk = BAND_BLOCK
    row = lax.broadcasted_iota(I32, (2 * blk, 2 * blk), 0) % blk
    col = lax.broadcasted_iota(I32, (2 * blk, 2 * blk), 1)
    band = jnp.logical_or(jnp.logical_and(col < blk, col >= row), jnp.logical_and(col >= blk, col - blk <= row))
    neg = jnp.where(not_first, 0.0, -jnp.inf).astype(F32)
    first_pen = jnp.where(col < blk, neg, 0.0)
    low = lax.broadcasted_iota(I32, (blk, LANES), 1) < HEAD_DIM
    zero = jnp.zeros((blk, LANES), BF16)
    ones = jnp.ones((2 * blk, LANES), BF16)

    def unit(q, kprev, kcur, vprev, vcur, maybe_first):
        kk = jnp.concatenate([kprev, kcur], axis=0)
        vv = jnp.concatenate([vprev, vcur], axis=0)
        qq = jnp.concatenate([jnp.where(low, q, zero), jnp.where(low, zero, q)], axis=0)
        s = lax.dot_general(qq, kk, (((1,), (1,)), ((), ())), preferred_element_type=F32)
        s = jnp.where(band, s, -jnp.inf)
        if maybe_first:
            s = s + first_pen
        m2 = jnp.max(s, axis=-1, keepdims=True)
        p = jnp.exp2(s - m2)
        o2 = _dot(p.astype(BF16), jnp.concatenate([vv, ones], axis=1))
        o = jnp.where(low, o2[:blk, :LANES], o2[blk:, :LANES])
        m = jnp.where(low, m2[:blk], m2[blk:])
        l = jnp.where(low, o2[:blk, LANES:], o2[blk:, LANES:])
        return o, jnp.broadcast_to(m, o.shape), l

    def put(buf, rows, oml):
        for a in range(3):
            buf[a, rows, :] = oml[a]


    for n in range(ATTN_SPAN // blk):
        cur = pl.ds(n * blk, blk)
        if n == 0:
            put(buf1, cur, unit(q1[cur, :], kp1[...], k1[cur, :], vp1[...], v1[cur, :], True))
        else:
            prev = pl.ds((n - 1) * blk, blk)
            put(buf1, cur, unit(q1[cur, :], k1[prev, :], k1[cur, :], v1[prev, :], v1[cur, :], False))

    for s in range(ATTN_SPAN // (4 * blk)):
        for r in range(4):
            rows = pl.ds((4 * s + r) * blk, blk)
            if s == 0:
                put(buf4, rows, unit(q4[0, r], kp4[r], k4[0, r], vp4[r], v4[0, r], True))
            else:
                put(buf4, rows, unit(q4[s, r], k4[s - 1, r], k4[s, r], v4[s - 1, r], v4[s, r], False))

    per = blk // (ATTN_SPAN // (4 * blk))
    for r in range(16):
        o16, m16, l16 = unit(q16[r], kp16[r], k16[r], vp16[r], v16[r], True)
        tok = pl.ds(r, blk, stride=16)
        o1, m1, l1 = (buf1[a, tok, :] for a in range(3))
        o4, m4, l4 = (jnp.concatenate(
            [buf4[a, pl.ds((4 * s + r % 4) * blk + r // 4, per, stride=4), :]
             for s in range(ATTN_SPAN // (4 * blk))], axis=0) for a in range(3))
        mn = jnp.maximum(jnp.maximum(m1, m4), m16)
        e1 = jnp.exp2(m1 - mn)
        e4 = jnp.exp2(m4 - mn)
        e16 = jnp.exp2(m16 - mn)
        num = o1 * e1 + o4 * e4 + o16 * e16
        den = l1 * e1 + l4 * e4 + l16 * e16
        out_scr[tok, :] = num / den

    o_ref[...] = out_scr[...].astype(BF16)


def _attention(qkv1, qkv4, qkv16, batch, seq):
    width = qkv1[0].shape[-1]
    nspan = seq // ATTN_SPAN
    blk = BAND_BLOCK
    n1 = ATTN_SPAN // blk
    n4 = ATTN_SPAN // (4 * blk)

    cur1 = pl.BlockSpec((None, ATTN_SPAN, LANES), lambda b, i, p: (b, i, p))
    prev1 = pl.BlockSpec((None, None, blk, LANES), lambda b, i, p: (b, jnp.maximum(i * n1 - 1, 0), 0, p))
    cur4 = pl.BlockSpec((None, n4, 4, blk, LANES), lambda b, i, p: (b, i, 0, 0, p))
    prev4 = pl.BlockSpec((None, None, 4, blk, LANES), lambda b, i, p: (b, jnp.maximum(i * n4 - 1, 0), 0, 0, p))
    cur16 = pl.BlockSpec((None, None, 16, blk, LANES), lambda b, i, p: (b, i, 0, 0, p))
    prev16 = pl.BlockSpec((None, None, 16, blk, LANES), lambda b, i, p: (b, jnp.maximum(i - 1, 0), 0, 0, p))

    q1, k1, v1 = qkv1
    q4, k4, v4 = qkv4
    q16, k16, v16 = qkv16
    k1b = k1.reshape(batch, seq // blk, blk, width)
    v1b = v1.reshape(batch, seq // blk, blk, width)
    return pl.pallas_call(
        _attn_kernel,
        grid=(batch, nspan, width // LANES),
        in_specs=[cur1, cur1, cur1, prev1, prev1,
                  cur4, cur4, cur4, prev4, prev4,
                  cur16, cur16, cur16, prev16, prev16],
        out_specs=pl.BlockSpec((None, ATTN_SPAN, LANES), lambda b, i, p: (b, i, p)),
        out_shape=jax.ShapeDtypeStruct((batch, seq, width), BF16),
        scratch_shapes=[pltpu.VMEM((3, ATTN_SPAN, LANES), F32), pltpu.VMEM((3, ATTN_SPAN, LANES), F32),
                        pltpu.VMEM((ATTN_SPAN, LANES), F32)],
        compiler_params=_params(("parallel", "parallel", "parallel")),
        name="dilated_attn",
    )(q1, k1, v1, k1b, v1b, q4, k4, v4, k4, v4, q16, k16, v16, k16, v16)


def _outproj_kernel(x_ref, a_ref, b_ref, wo_ref, mgb_ref, g1_ref, b1_ref, wr_ref, br_ref, tri_ref,
                    h_ref, idx_ref, gate_ref, rank_ref, idxt_ref, rankt_ref, runs_ref, cnt_ref, run_scr, wob_scr):
    @pl.when(pl.program_id(0) == 0)
    def _():
        run_scr[...] = jnp.zeros_like(run_scr)
        wob_scr[...] = wo_ref[...].astype(BF16)

    width = a_ref.shape[-1]
    lane = lax.broadcasted_iota(I32, (ROUTE_ROWS, LANES), 1).astype(F32)

    def spread(cols):
        out = jnp.zeros((ROUTE_ROWS, LANES), F32)
        for k, cval in enumerate(cols):
            out = jnp.where(lane == float(k), cval, out)
        return out

    subs = [slice(j * ROUTE_ROWS, (j + 1) * ROUTE_ROWS) for j in range(OUT_ROWS // ROUTE_ROWS)]
    works = []
    for rows in subs:
        bf = b_ref[rows, :].astype(F32)
        bn = (bf * lax.rsqrt(jnp.mean(bf * bf, axis=-1, keepdims=True) + LN_EPS) * mgb_ref[...]).astype(BF16)
        mixed = _dot(a_ref[rows, :], wob_scr[0:width, :]) + _dot(bn, wob_scr[width:2 * width, :])
        z = DN_ALPHA * x_ref[rows, :] + mixed
        mu = jnp.mean(z, axis=-1, keepdims=True)
        zc = z - mu
        var = jnp.mean(zc * zc, axis=-1, keepdims=True)
        h = zc * lax.rsqrt(var + LN_EPS) * g1_ref[...] + b1_ref[...]
        h_ref[rows, :] = h
        works.append(_dot(h.astype(BF16), wr_ref[...]) + br_ref[...])

    vals = [[] for _ in subs]
    idxs = [[] for _ in subs]
    hots = [[] for _ in subs]
    for _ in range(TOP_K):
        for j in range(len(subs)):
            mv = jnp.max(works[j], axis=-1, keepdims=True)
            ix = jnp.min(jnp.where(works[j] == mv, lane, float(LANES)), axis=-1, keepdims=True)
            hot = lane == ix
            works[j] = jnp.where(hot, -jnp.inf, works[j])
            vals[j].append(mv)
            idxs[j].append(ix)
            hots[j].append(hot)

    members = []
    for j in range(len(subs)):
        member = jnp.zeros((ROUTE_ROWS, LANES), F32)
        for hot in hots[j]:
            member = jnp.where(hot, 1.0, member)
        members.append(member)
    withins = [_dot(tri_ref[...], member.astype(BF16)) for member in members]
    run = run_scr[...]
    for j, rows in enumerate(subs):
        runs_ref[j] = run
        before = withins[j] + run
        ranks = [jnp.sum(jnp.where(hot, before, 0.0), axis=-1, keepdims=True) for hot in hots[j]]
        run = run + jnp.sum(members[j], axis=0, keepdims=True)
        exps = [jnp.exp(v - vals[j][0]) for v in vals[j]]
        den = exps[0] + exps[1] + exps[2] + exps[3]
        idx_all = spread(idxs[j])
        rank_all = spread(ranks)
        idx_ref[rows, :] = idx_all.astype(I32)
        gate_ref[rows, :] = spread([e / den for e in exps])
        rank_ref[rows, :] = rank_all.astype(I32)
        idxt_ref[:, rows] = idx_all.T[:OCTET].astype(I32)
        rankt_ref[:, rows] = rank_all.T[:OCTET].astype(I32)
    run_scr[...] = run
    cnt_ref[...] = run


def _outproj(x2, a_n, b2, w_out, mgb, g1, b1, wr, br, tri):
    t, dm = x2.shape
    width = dm // 2
    const = lambda *shape: pl.BlockSpec(shape, lambda m: (0,) * len(shape))
    rowblk = lambda w: pl.BlockSpec((OUT_ROWS, w), lambda m: (m, 0))
    colblk = pl.BlockSpec((OCTET, OUT_ROWS), lambda m: (0, m))
    sub = OUT_ROWS // ROUTE_ROWS
    return pl.pallas_call(
        _outproj_kernel,
        grid=(t // OUT_ROWS,),
        in_specs=[rowblk(dm), rowblk(width), rowblk(width),
                  pl.BlockSpec((dm, dm), lambda m: (0, 0), pipeline_mode=pl.Buffered(1)), const(1, width),
                  const(1, dm), const(1, dm), const(dm, LANES), const(1, LANES),
                  const(ROUTE_ROWS, ROUTE_ROWS)],
        out_specs=[rowblk(dm), rowblk(LANES), rowblk(LANES), rowblk(LANES), colblk, colblk,
                   pl.BlockSpec((None, sub, 1, LANES), lambda m: (m, 0, 0, 0)), const(1, LANES)],
        out_shape=[jax.ShapeDtypeStruct((t, dm), F32), jax.ShapeDtypeStruct((t, LANES), I32),
                   jax.ShapeDtypeStruct((t, LANES), F32), jax.ShapeDtypeStruct((t, LANES), I32),
                   jax.ShapeDtypeStruct((OCTET, t), I32), jax.ShapeDtypeStruct((OCTET, t), I32),
                   jax.ShapeDtypeStruct((t // OUT_ROWS, sub, 1, LANES), F32),
                   jax.ShapeDtypeStruct((1, LANES), F32)],
        scratch_shapes=[pltpu.VMEM((1, LANES), F32), pltpu.VMEM((dm, dm), BF16)],
        compiler_params=_params(("arbitrary",)),
        name="outproj_router",
    )(x2, a_n, b2, w_out, mgb, g1, b1, wr, br, tri)


def _octet(ref, q):
    return ref.at[pl.ds(pl.multiple_of(q * OCTET, OCTET), OCTET), :]


def _sub_block(ref, b):
    return ref.at[pl.ds(pl.multiple_of(b * MOE_SUB, MOE_SUB), MOE_SUB), :]


def _for_octets(n, start):
    def body(i, c):
        for j in range(4):
            start(4 * i + j, j % 2)
        return c

    lax.fori_loop(0, n // 4, body, 0)
    done = (n // 4) * 4

    @pl.when((n & 2) != 0)
    def _():
        start(done, 0)
        start(done + 1, 1)

    @pl.when((n & 1) != 0)
    def _():
        start(n - 1, 0)


def _wait_octets(n, ref, sem):
    for s in (128, 64, 32, 16, 8, 4, 2, 1):
        @pl.when((n & s) != 0)
        def _():
            d = ref.at[pl.ds(0, s * OCTET), :]
            pltpu.make_async_copy(d, d, sem).wait()


def _dispatch_kernel(nq_ref, fstart_ref, fcount_ref, gq_ref, h_ref, idxt_ref, rankt_ref, tabt_ref, xs_hbm,
                     stage, zeros, sem, zsem):
    s = pl.program_id(0)
    base = (s % 2) * PAIR
    eid = lax.broadcasted_iota(I32, (N_EXPERTS, ROUTE_ROWS), 0)
    pos = lax.broadcasted_iota(I32, (STAGE_ROWS, ROUTE_ROWS), 0).astype(F32)
    sels = []
    for j in range(PAIR):
        cols = slice(j * ROUTE_ROWS, (j + 1) * ROUTE_ROWS)
        sel = None
        for k in range(TOP_K):
            hot = eid == idxt_ref[k:k + 1, cols]
            lpos = (jnp.sum(jnp.where(hot, tabt_ref[j], 0.0), axis=0, keepdims=True)
                    + rankt_ref[k:k + 1, cols].astype(F32))
            hit = pos == lpos
            sel = hit if sel is None else jnp.logical_or(sel, hit)
        sels.append(jnp.where(sel, 1.0, 0.0).astype(BF16))
    for j in range(PAIR):
        rows = slice(j * ROUTE_ROWS, (j + 1) * ROUTE_ROWS)
        stage[base + j] = _dot(sels[j], h_ref[rows, :].astype(BF16))

    @pl.when(s == 0)
    def _():
        zeros[...] = jnp.zeros_like(zeros)
        for g in range(N_EXPERTS):
            def zstart(i, c, g=g):
                pltpu.make_async_copy(_octet(zeros, 0), _octet(xs_hbm, fstart_ref[g] + i), zsem.at[0]).start()
                return c

            lax.fori_loop(0, fcount_ref[g], zstart, 0)

        def tstart(i, c):
            pltpu.make_async_copy(zeros, _sub_block(xs_hbm, fstart_ref[N_EXPERTS] + i), zsem.at[1]).start()
            return c

        lax.fori_loop(0, fcount_ref[N_EXPERTS], tstart, 0)

    for j in range(PAIR):
        _for_octets(nq_ref[s * PAIR + j], lambda q, pri, j=j: pltpu.make_async_copy(
            _octet(stage.at[base + j], q), _octet(xs_hbm, gq_ref[j, 0, q]), sem.at[base + j]).start(priority=pri))

    @pl.when(s > 0)
    def _():
        for j in range(PAIR):
            _wait_octets(nq_ref[jnp.maximum(s - 1, 0) * PAIR + j], xs_hbm, sem.at[PAIR - base + j])

    @pl.when(s == pl.num_programs(0) - 1)
    def _():
        for j in range(PAIR):
            _wait_octets(nq_ref[s * PAIR + j], xs_hbm, sem.at[base + j])

        def zwait(i, c):
            pltpu.make_async_copy(_octet(zeros, 0), _octet(xs_hbm, 0), zsem.at[0]).wait()
            return c

        for g in range(N_EXPERTS):
            lax.fori_loop(0, fcount_ref[g], zwait, 0)

        def twait(i, c):
            pltpu.make_async_copy(zeros, _sub_block(xs_hbm, 0), zsem.at[1]).wait()
            return c

        lax.fori_loop(0, fcount_ref[N_EXPERTS], twait, 0)


def _dispatch(h, idxt, rankt, tabt, gq3, nq, fill_start, fill_count, nrows):
    t, dm = h.shape
    step = PAIR * ROUTE_ROWS
    grid_spec = pltpu.PrefetchScalarGridSpec(
        num_scalar_prefetch=3,
        grid=(t // step,),
        in_specs=[pl.BlockSpec((PAIR, 1, STAGE_OCTETS), lambda m, *_: (m, 0, 0), memory_space=pltpu.SMEM),
                  pl.BlockSpec((step, dm), lambda m, *_: (m, 0)),
                  pl.BlockSpec((OCTET, step), lambda m, *_: (0, m)),
                  pl.BlockSpec((OCTET, step), lambda m, *_: (0, m)),
                  pl.BlockSpec((PAIR, N_EXPERTS, ROUTE_ROWS), lambda m, *_: (m, 0, 0))],
        out_specs=pl.BlockSpec(memory_space=pl.ANY),
        scratch_shapes=[pltpu.VMEM((2 * PAIR, STAGE_ROWS, dm), F32), pltpu.VMEM((MOE_SUB, dm), F32),
                        pltpu.SemaphoreType.DMA((2 * PAIR,)), pltpu.SemaphoreType.DMA((2,))],
    )
    return pl.pallas_call(
        _dispatch_kernel,
        grid_spec=grid_spec,
        out_shape=jax.ShapeDtypeStruct((nrows, dm), F32),
        compiler_params=_params(("arbitrary",)),
        name="moe_dispatch",
    )(nq, fill_start, fill_count, gq3, h, idxt, rankt, tabt)


def _moe_kernel(ib_ref, ie_ref, lo_ref, hi_ref, nxt_ref, xs_ref, bg_ref, bu_ref, bd_ref,
                wg_hbm, wu_hbm, wd_hbm, ys_ref, wf, wb, wsem):
    i = pl.program_id(0)
    prev = jnp.maximum(i - 1, 0)
    lo = lo_ref[i]
    hi = hi_ref[i]

    def weight_copies(e):
        return [pltpu.make_async_copy(w.at[e], wf.at[j], wsem) for j, w in enumerate((wg_hbm, wu_hbm, wd_hbm))]

    @pl.when(jnp.logical_and(lo >= 0, hi > lo))
    def _():
        @pl.when(jnp.logical_or(i == 0, ie_ref[i] != ie_ref[prev]))
        def _():
            @pl.when(i == 0)
            def _():
                for cp in weight_copies(ie_ref[i]):
                    cp.start()

            for cp in weight_copies(ie_ref[i]):
                cp.wait()
            for j in range(3):
                wb[j] = wf[j].astype(BF16)

            @pl.when(nxt_ref[i] >= 0)
            def _():
                for cp in weight_copies(nxt_ref[i]):
                    cp.start()

        def expert(rows):
            x = xs_ref[rows, :].astype(BF16)
            g = jnp.minimum(_dot(x, wb[0]) + bg_ref[...], SWIGLU_LIMIT)
            u = jnp.clip(_dot(x, wb[1]) + bu_ref[...], -SWIGLU_LIMIT, SWIGLU_LIMIT)
            act = (u + 1.0) * (g * jax.nn.sigmoid(SWIGLU_ALPHA * g))
            return _dot(act.astype(BF16), wb[2]) + bd_ref[...]

        whole = jnp.logical_and(lo == 0, hi == MOE_ROWS)

        @pl.when(whole)
        def _():
            for j in range(MOE_ROWS // MOE_CHUNK):
                rows = slice(j * MOE_CHUNK, (j + 1) * MOE_CHUNK)
                ys_ref[rows, :] = expert(rows)

        @pl.when(jnp.logical_not(whole))
        def _():
            @pl.when(jnp.logical_or(i == 0, ib_ref[i] != ib_ref[prev]))
            def _():
                ys_ref[...] = jnp.zeros_like(ys_ref)

            mine = [jnp.logical_and(lo < (j + 1) * MOE_SUB, hi > j * MOE_SUB) for j in range(MOE_ROWS // MOE_SUB)]
            for j in range(0, MOE_ROWS // MOE_SUB, 2):
                for first, second in ((True, True), (True, False), (False, True)):
                    rows = slice((j if first else j + 1) * MOE_SUB, (j + 2 if second else j + 1) * MOE_SUB)
                    cond = jnp.logical_and(mine[j] if first else jnp.logical_not(mine[j]),
                                           mine[j + 1] if second else jnp.logical_not(mine[j + 1]))

                    @pl.when(cond)
                    def _(rows=rows):
                        ys_ref[rows, :] = expert(rows)

    @pl.when(lo < 0)
    def _():
        ys_ref[...] = jnp.zeros_like(ys_ref)


def _moe(xs, items, w_gate, b_gate, w_up, b_up, w_down, b_down):
    nrows = xs.shape[0]
    ne, dm, df = w_gate.shape
    assert dm == df
    nitems = items[0].shape[0]
    bspec = lambda c: pl.BlockSpec((None, 1, c), lambda i, ib, ie, *_: (ie[i], 0, 0))
    rspec = pl.BlockSpec((MOE_ROWS, dm), lambda i, ib, *_: (ib[i], 0))
    hbm = pl.BlockSpec(memory_space=pl.ANY)
    grid_spec = pltpu.PrefetchScalarGridSpec(
        num_scalar_prefetch=5,
        grid=(nitems,),
        in_specs=[rspec, bspec(df), bspec(df), bspec(dm), hbm, hbm, hbm],
        out_specs=rspec,
        scratch_shapes=[pltpu.VMEM((3, dm, df), F32), pltpu.VMEM((3, dm, df), BF16),
                        pltpu.SemaphoreType.DMA(())],
    )
    return pl.pallas_call(
        _moe_kernel,
        grid_spec=grid_spec,
        out_shape=jax.ShapeDtypeStruct((nrows, dm), F32),
        compiler_params=_params(("arbitrary",)),
        name="moe_experts",
    )(*items, xs, b_gate.reshape(ne, 1, df), b_up.reshape(ne, 1, df), b_down.reshape(ne, 1, dm),
      w_gate, w_up, w_down)


def _work_items(counts, nrows):
    nblk = nrows // MOE_ROWS
    nitems = nblk + N_EXPERTS - 1
    ends = jnp.cumsum(counts)
    starts = ends - counts
    b0 = jnp.arange(nblk, dtype=I32)[:, None] * MOE_ROWS
    lo = jnp.maximum(starts[None, :], b0)
    hi = jnp.minimum(ends[None, :], b0 + MOE_ROWS)
    nonempty = (hi > lo).reshape(-1)
    csum = jnp.cumsum(nonempty.astype(I32))
    j = jnp.arange(nitems, dtype=I32)
    pos = jnp.sum(csum[None, :] <= j[:, None], axis=1).astype(I32)
    used = j < csum[-1]
    pos = jnp.where(used, pos, jnp.max(jnp.where(nonempty, jnp.arange(nonempty.shape[0], dtype=I32), 0)))
    ib = pos // N_EXPERTS
    ie = pos % N_EXPERTS
    ilo = jnp.where(used, lo.reshape(-1)[pos] - ib * MOE_ROWS, 0)
    ihi = jnp.where(used, hi.reshape(-1)[pos] - ib * MOE_ROWS, 0)
    spare_blk = ib + 1 + (j - csum[-1])
    fill = jnp.logical_and(jnp.logical_not(used), spare_blk < nblk)
    ib = jnp.where(used, ib, jnp.minimum(spare_blk, nblk - 1))
    ilo = jnp.where(fill, -1, ilo)
    ordinal = jnp.cumsum(jnp.concatenate([jnp.zeros((1,), I32), (ie[1:] != ie[:-1]).astype(I32)]))
    first_next = jnp.sum(ordinal[None, :] <= ordinal[:, None], axis=1)
    has_next = first_next < nitems
    nxt = jnp.where(has_next, ie[jnp.minimum(first_next, nitems - 1)], -1)
    return ib.astype(I32), ie.astype(I32), ilo.astype(I32), ihi.astype(I32), nxt.astype(I32)


def _combine_kernel(nq_ref, gq_cur, gq_nxt, h_ref, idx_ref, rank_ref, gate_ref, tab_ref, g2_ref, b2_ref, ys_hbm,
                    o_ref, stage, sem):
    s = pl.program_id(0)
    last = pl.num_programs(0) - 1
    base = (s % 2) * PAIR

    def start_fetch(gq_ref, step, slot0):
        for j in range(PAIR):
            _for_octets(nq_ref[jnp.minimum(step, last) * PAIR + j], lambda q, pri, j=j: pltpu.make_async_copy(
                _octet(ys_hbm, gq_ref[j, 0, q]), _octet(stage.at[slot0 + j], q),
                sem.at[slot0 + j]).start(priority=pri))

    @pl.when(s == 0)
    def _():
        stage[...] = jnp.zeros_like(stage)
        start_fetch(gq_cur, 0, 0)

    @pl.when(s < last)
    def _():
        start_fetch(gq_nxt, s + 1, PAIR - base)

    for j in range(PAIR):
        _wait_octets(nq_ref[s * PAIR + j], stage.at[base + j], sem.at[base + j])

    lane_e = lax.broadcasted_iota(I32, (ROUTE_ROWS, LANES), 1)
    lane_p = lax.broadcasted_iota(I32, (ROUTE_ROWS, STAGE_ROWS), 1).astype(F32)
    wsels = []
    for j in range(PAIR):
        rows = slice(j * ROUTE_ROWS, (j + 1) * ROUTE_ROWS)
        idx = idx_ref[rows, :]
        rank = rank_ref[rows, :].astype(F32)
        gates = gate_ref[rows, :]
        wsel = jnp.zeros((ROUTE_ROWS, STAGE_ROWS), F32)
        for k in range(TOP_K):
            hot = lane_e == idx[:, k:k + 1]
            lpos = jnp.sum(jnp.where(hot, tab_ref[j], 0.0), axis=-1, keepdims=True) + rank[:, k:k + 1]
            wsel = jnp.where(lane_p == lpos, gates[:, k:k + 1], wsel)
        wsels.append(wsel.astype(BF16))
    ys = [_dot(wsels[j], stage[base + j].astype(BF16)) for j in range(PAIR)]
    for j in range(PAIR):
        rows = slice(j * ROUTE_ROWS, (j + 1) * ROUTE_ROWS)
        z = DN_ALPHA * h_ref[rows, :] + ys[j]
        mu = jnp.mean(z, axis=-1, keepdims=True)
        zc = z - mu
        var = jnp.mean(zc * zc, axis=-1, keepdims=True)
        o_ref[rows, :] = zc * lax.rsqrt(var + LN_EPS) * g2_ref[...] + b2_ref[...]


def _combine(h, ys, idx, rank, gates, tab, gq3, nq, g2, b2):
    t, dm = h.shape
    step = PAIR * ROUTE_ROWS
    ns = t // step
    rowblk = lambda w: pl.BlockSpec((step, w), lambda m, nq: (m, 0))
    const = pl.BlockSpec((1, dm), lambda m, nq: (0, 0))
    qspec = lambda f: pl.BlockSpec((PAIR, 1, STAGE_OCTETS), lambda m, nq: (f(m), 0, 0), memory_space=pltpu.SMEM)
    grid_spec = pltpu.PrefetchScalarGridSpec(
        num_scalar_prefetch=1,
        grid=(ns,),
        in_specs=[qspec(lambda m: m), qspec(lambda m: jnp.minimum(m + 1, ns - 1)),
                  rowblk(dm), rowblk(LANES), rowblk(LANES), rowblk(LANES),
                  pl.BlockSpec((PAIR, 1, LANES), lambda m, nq: (m, 0, 0)), const, const,
                  pl.BlockSpec(memory_space=pl.ANY)],
        out_specs=rowblk(dm),
        scratch_shapes=[pltpu.VMEM((2 * PAIR, STAGE_ROWS, dm), F32), pltpu.SemaphoreType.DMA((2 * PAIR,))],
    )
    return pl.pallas_call(
        _combine_kernel,
        grid_spec=grid_spec,
        out_shape=jax.ShapeDtypeStruct((t, dm), F32),
        compiler_params=_params(("arbitrary",)),
        name="combine_ln",
    )(nq, gq3, gq3, h, idx, rank, gates, tab, g2, b2, ys)


def _route_tables(runs, total, nrows):
    nt = runs.shape[0]
    nxt = jnp.concatenate([runs[1:], total[None, :]], axis=0)
    c8 = (nxt - runs + OCTET - 1) // OCTET
    q_end = jnp.cumsum(c8, axis=1)
    q0 = q_end - c8
    nq = q_end[:, -1]
    per_expert = jnp.sum(c8, axis=0)
    sub8 = MOE_SUB // OCTET
    per_pad = (per_expert + sub8 - 1) // sub8 * sub8
    start8 = jnp.cumsum(per_pad) - per_pad
    g8 = start8[None, :] + jnp.cumsum(c8, axis=0) - c8
    tab = OCTET * q0 - runs
    q = jnp.arange(STAGE_OCTETS, dtype=I32)
    e_of_q = jnp.minimum(jnp.sum(q[None, :, None] >= q_end[:, None, :], axis=2), N_EXPERTS - 1)
    pick = e_of_q[:, :, None] == jnp.arange(N_EXPERTS, dtype=I32)[None, None, :]
    gq = jnp.sum(jnp.where(pick, (g8 - q0)[:, None, :], 0), axis=2) + q[None, :]
    gq = jnp.where(q[None, :] < nq[:, None], gq, 0).astype(I32)
    tot8 = jnp.sum(per_pad)
    fill_start = jnp.concatenate([start8 + per_expert, (tot8 // sub8)[None]]).astype(I32)
    fill_count = jnp.concatenate([per_pad - per_expert, (nrows // MOE_SUB - tot8 // sub8)[None]]).astype(I32)
    return (tab, gq.reshape(nt, 1, STAGE_OCTETS), nq.astype(I32), fill_start, fill_count,
            (per_pad * OCTET).astype(I32))


def _layer(x, w_in, sgu_w, sgu_b, sgu_ln_g, sgu_ln_b, mix_norm_g, w_out, ln1_g, ln1_b,
           w_router, b_router, w_gate, b_gate, w_up, b_up, w_down, b_down, ln2_g, ln2_b):
    batch, seq, dm = x.shape
    width = dm // 2
    t = batch * seq
    assert seq % ATTN_SPAN == 0 and dm % (2 * LANES) == 0 and w_router.shape[-1] == N_EXPERTS
    x2 = x.reshape(t, dm)

    wc = jnp.tril(sgu_w)
    wpair = jnp.concatenate([wc[0::2], wc[1::2]], axis=-1).astype(BF16)
    sbias = jnp.repeat(sgu_b.T, HEAD_DIM, axis=1)
    grp = jnp.arange(width) // HEAD_DIM
    gmat = jnp.where(grp[:, None] == grp[None, :], 1.0 / HEAD_DIM, 0.0).astype(BF16)
    row = lambda v: v.reshape(1, -1)

    a_n, q1, k1, v1, q4, k4, v4, q16, k16, v16 = _project(
        x2, w_in, gmat, wpair, sbias, row(sgu_ln_g), row(sgu_ln_b),
        row(mix_norm_g[:width]), batch, seq)
    b = _attention((q1, k1, v1), (q4, k4, v4), (q16, k16, v16), batch, seq)

    wr = jnp.pad(w_router, ((0, 0), (0, LANES - N_EXPERTS))).astype(BF16)
    br = jnp.concatenate([b_router.astype(F32), jnp.full((LANES - N_EXPERTS,), -1e30, F32)]).reshape(1, LANES)
    ti = jnp.arange(ROUTE_ROWS)
    tri = (ti[None, :] < ti[:, None]).astype(BF16)
    h, idx, gates, rank, idxt, rankt, runs, cnt = _outproj(
        x2, a_n.reshape(t, width), b.reshape(t, width), w_out,
        row(mix_norm_g[width:]), row(ln1_g), row(ln1_b), wr, br, tri)

    nt = t // ROUTE_ROWS
    worst = t * TOP_K + nt * N_EXPERTS * (OCTET - 1) + N_EXPERTS * (MOE_SUB - OCTET)
    nrows = (worst + MOE_ROWS - 1) // MOE_ROWS * MOE_ROWS
    runs = runs.reshape(nt, LANES)[:, :N_EXPERTS].astype(I32)
    total = cnt[0, :N_EXPERTS].astype(I32)
    tab, gq3, nq, fill_start, fill_count, rows_e = _route_tables(runs, total, nrows)
    tabf = tab.astype(F32)
    tab_lane = jnp.pad(tabf, ((0, 0), (0, LANES - N_EXPERTS))).reshape(nt, 1, LANES)
    tab_sub = jnp.broadcast_to(tabf[:, :, None], (nt, N_EXPERTS, ROUTE_ROWS))

    xs = _dispatch(h, idxt, rankt, tab_sub, gq3, nq, fill_start, fill_count, nrows)
    ys = _moe(xs, _work_items(rows_e, nrows), w_gate, b_gate, w_up, b_up, w_down, b_down)
    out = _combine(h, ys, idx, rank, gates, tab_lane, gq3, nq, row(ln2_g), row(ln2_b))
    return out.reshape(batch, seq, dm)


def kernel(x, w_in, sgu_w, sgu_b, sgu_ln_g, sgu_ln_b, mix_norm_g, w_out, ln1_g, ln1_b, w_router, b_router,
           w_gate, b_gate, w_up, b_up, w_down, b_down, ln2_g, ln2_b):
    assert w_in.shape[0] == DEPTH
    return _layer(x, w_in[0], sgu_w[0], sgu_b[0], sgu_ln_g[0], sgu_ln_b[0], mix_norm_g[0], w_out[0],
                  ln1_g[0], ln1_b[0], w_router[0], b_router[0], w_gate[0], b_gate[0], w_up[0], b_up[0],
                  w_down[0], b_down[0], ln2_g[0], ln2_b[0])
```

```python
import jax
import jax.numpy as jnp
from jax import lax
from jax.experimental import pallas as pl
from jax.experimental.pallas import tpu as pltpu

F32 = jnp.float32
BF16 = jnp.bfloat16
I32 = jnp.int32

LN_EPS = 1e-5
HEAD_DIM = 64
SGU_CHUNK = 128
BAND_BLOCK = 128
DILATIONS = (1, 4, 16)
ATTN_SPAN = BAND_BLOCK * DILATIONS[-1]
N_EXPERTS = 32
TOP_K = 4
SWIGLU_ALPHA = 1.702
SWIGLU_LIMIT = 7.0
DEPTH = 1
DN_ALPHA = (2 * DEPTH) ** 0.25
LOG2E = 1.4426950408889634

LANES = 128
PROJ_ROWS = 512
OUT_ROWS = 1024
MOE_ROWS = 1024
MOE_CHUNK = 512
MOE_SUB = 256
ROUTE_ROWS = 256
OCTET = 8
STAGE_ROWS = ROUTE_ROWS * TOP_K + N_EXPERTS * OCTET
STAGE_OCTETS = STAGE_ROWS // OCTET
PAIR = 2
VMEM_LIMIT = 56 * 1024 * 1024


def _params(sem):
    return pltpu.CompilerParams(dimension_semantics=sem, vmem_limit_bytes=VMEM_LIMIT)


def _dot(a, b):
    return jnp.dot(a, b, preferred_element_type=F32)


def _proj_kernel(x_ref, w_ref, gmat_ref, wpair_ref, sbias_ref, lng_ref, lnb_ref, mg_ref,
                 a_ref, q1_ref, k1_ref, v1_ref, q4_ref, k4_ref, v4_ref, q16_ref, k16_ref, v16_ref,
                 a_scr, t_scr, t4_scr, wb_scr):
    width = a_ref.shape[-1]

    @pl.when(jnp.logical_and(pl.program_id(0) == 0, pl.program_id(1) == 0))
    def _():
        wb_scr[...] = w_ref[...].astype(BF16)

    xb = x_ref[...].astype(BF16)

    def proj(c):
        return _dot(xb, wb_scr[:, c * width:(c + 1) * width])

    u = jax.nn.gelu(proj(0))
    v = jax.nn.gelu(proj(1))
    mean = _dot(v.astype(BF16), gmat_ref[...])
    d = v - mean
    var = _dot((d * d).astype(BF16), gmat_ref[...])
    vn = (d * lax.rsqrt(var + LN_EPS) * lng_ref[...] + lnb_ref[...]).astype(BF16)

    lane = lax.broadcasted_iota(I32, (SGU_CHUNK, LANES), 1)
    low = lane < HEAD_DIM
    zero = jnp.zeros((SGU_CHUNK, LANES), BF16)
    for c in range(PROJ_ROWS // SGU_CHUNK):
        rows = slice(c * SGU_CHUNK, (c + 1) * SGU_CHUNK)
        for j in range(width // LANES):
            cols = slice(j * LANES, (j + 1) * LANES)
            vp = vn[rows, cols]
            rhs = jnp.concatenate([jnp.where(low, vp, zero), jnp.where(low, zero, vp)], axis=0)
            gate = _dot(wpair_ref[j], rhs) + sbias_ref[:, cols]
            a_scr[rows, cols] = u[rows, cols] * gate
    a = a_scr[...]
    ms = jnp.mean(a * a, axis=-1, keepdims=True)
    a_ref[...] = (a * lax.rsqrt(ms + LN_EPS) * mg_ref[...]).astype(BF16)

    outs = ((q1_ref, q4_ref, q16_ref), (k1_ref, k4_ref, k16_ref), (v1_ref, v4_ref, v16_ref))
    for c, (o1, o4, o16) in enumerate(outs):
        t = proj(2 + c)
        if c == 0:
            t = t * (HEAD_DIM ** -0.5 * LOG2E)
        o1[...] = t.astype(BF16)
        for j in range(width // LANES):
            cols = slice(j * LANES, (j + 1) * LANES)
            t_scr[j] = t[:, cols]
            for b in range(4):
                t4 = t_scr[j, pl.ds(b, PROJ_ROWS // 4, stride=4), :]
                o4[b, :, cols] = t4.astype(BF16)
                t4_scr[b] = t4
                for a in range(4):
                    o16[4 * a + b, :, cols] = t4_scr[b, pl.ds(a, PROJ_ROWS // 16, stride=4), :].astype(BF16)


def _project(x2, w_in, gmat, wpair, sbias, lng, lnb, mg, batch, seq):
    t, dm = x2.shape
    width = dm // 2
    nt = seq // PROJ_ROWS
    per_span = ATTN_SPAN // PROJ_ROWS
    const = lambda *shape: pl.BlockSpec(shape, lambda b, m: (0,) * len(shape))
    o1 = jax.ShapeDtypeStruct((batch, seq, width), BF16)
    o4 = jax.ShapeDtypeStruct((batch, seq // 512, 4, BAND_BLOCK, width), BF16)
    o16 = jax.ShapeDtypeStruct((batch, seq // ATTN_SPAN, 16, BAND_BLOCK, width), BF16)
    s1 = pl.BlockSpec((None, PROJ_ROWS, width), lambda b, m: (b, m, 0))
    s4 = pl.BlockSpec((None, None, 4, BAND_BLOCK, width), lambda b, m: (b, m, 0, 0, 0))
    s16 = pl.BlockSpec((None, None, 16, PROJ_ROWS // 16, width),
                       lambda b, m: (b, m // per_span, 0, m % per_span, 0))
    return pl.pallas_call(
        _proj_kernel,
        grid=(batch, nt),
        in_specs=[pl.BlockSpec((PROJ_ROWS, dm), lambda b, m: (b * nt + m, 0)),
                  pl.BlockSpec(w_in.shape, lambda b, m: (0, 0), pipeline_mode=pl.Buffered(1)),
                  const(*gmat.shape), const(*wpair.shape), const(*sbias.shape),
                  const(1, width), const(1, width), const(1, width)],
        out_specs=[s1] + [s1, s1, s1] + [s4, s4, s4] + [s16, s16, s16],
        out_shape=[o1] + [o1, o1, o1] + [o4, o4, o4] + [o16, o16, o16],
        scratch_shapes=[pltpu.VMEM((PROJ_ROWS, width), F32),
                        pltpu.VMEM((width // LANES, PROJ_ROWS, LANES), F32),
                        pltpu.VMEM((4, PROJ_ROWS // 4, LANES), F32),
                        pltpu.VMEM(w_in.shape, BF16)],
        compiler_params=_params(("arbitrary", "arbitrary")),
        name="proj_sgu",
    )(x2, w_in, gmat, wpair, sbias, lng, lnb, mg)


def _attn_kernel(q1, k1, v1, kp1, vp1, q4, k4, v4, kp4, vp4, q16, k16, v16, kp16, vp16,
                 o_ref, buf1, buf4, out_scr):
    not_first = pl.program_id(1) > 0
    blk = BAND_BLOCK
    row = lax.broadcasted_iota(I32, (2 * blk, 2 * blk), 0) % blk
    col = lax.broadcasted_iota(I32, (2 * blk, 2 * blk), 1)
    band = jnp.logical_or(jnp.logical_and(col < blk, col >= row), jnp.logical_and(col >= blk, col - blk <= row))
    neg = jnp.where(not_first, 0.0, -jnp.inf).astype(F32)
    first_pen = jnp.where(col < blk, neg, 0.0)
    low = lax.broadcasted_iota(I32, (blk, LANES), 1) < HEAD_DIM
    zero = jnp.zeros((blk, LANES), BF16)
    ones = jnp.ones((2 * blk, LANES), BF16)

    def unit(q, kprev, kcur, vprev, vcur, maybe_first):
        kk = jnp.concatenate([kprev, kcur], axis=0)
        vv = jnp.concatenate([vprev, vcur], axis=0)
        qq = jnp.concatenate([jnp.where(low, q, zero), jnp.where(low, zero, q)], axis=0)
        s = lax.dot_general(qq, kk, (((1,), (1,)), ((), ())), preferred_element_type=F32)
        s = jnp.where(band, s, -jnp.inf)
        if maybe_first:
            s = s + first_pen
        m2 = jnp.max(s, axis=-1, keepdims=True)
        p = jnp.exp2(s - m2)
        o2 = _dot(p.astype(BF16), jnp.concatenate([vv, ones], axis=1))
        o = jnp.where(low, o2[:blk, :LANES], o2[blk:, :LANES])
        m = jnp.where(low, m2[:blk], m2[blk:])
        l = jnp.where(low, o2[:blk, LANES:], o2[blk:, LANES:])
        return o, jnp.broadcast_to(m, o.shape), l

    def put(buf, rows, oml):
        for a in range(3):
            buf[a, rows, :] = oml[a]


    for n in range(ATTN_SPAN // blk):
        cur = pl.ds(n * blk, blk)
        if n == 0:
            put(buf1, cur, unit(q1[cur, :], kp1[...], k1[cur, :], vp1[...], v1[cur, :], True))
        else:
            prev = pl.ds((n - 1) * blk, blk)
            put(buf1, cur, unit(q1[cur, :], k1[prev, :], k1[cur, :], v1[prev, :], v1[cur, :], False))

    for s in range(ATTN_SPAN // (4 * blk)):
        for r in range(4):
            rows = pl.ds((4 * s + r) * blk, blk)
            if s == 0:
                put(buf4, rows, unit(q4[0, r], kp4[r], k4[0, r], vp4[r], v4[0, r], True))
            else:
                put(buf4, rows, unit(q4[s, r], k4[s - 1, r], k4[s, r], v4[s - 1, r], v4[s, r], False))

    per = blk // (ATTN_SPAN // (4 * blk))
    for r in range(16):
        o16, m16, l16 = unit(q16[r], kp16[r], k16[r], vp16[r], v16[r], True)
        tok = pl.ds(r, blk, stride=16)
        o1, m1, l1 = (buf1[a, tok, :] for a in range(3))
        o4, m4, l4 = (jnp.concatenate(
            [buf4[a, pl.ds((4 * s + r % 4) * blk + r // 4, per, stride=4), :]
             for s in range(ATTN_SPAN // (4 * blk))], axis=0) for a in range(3))
        mn = jnp.maximum(jnp.maximum(m1, m4), m16)
        e1 = jnp.exp2(m1 - mn)
        e4 = jnp.exp2(m4 - mn)
        e16 = jnp.exp2(m16 - mn)
        num = o1 * e1 + o4 * e4 + o16 * e16
        den = l1 * e1 + l4 * e4 + l16 * e16
        out_scr[tok, :] = num / den

    o_ref[...] = out_scr[...].astype(BF16)


def _attention(qkv1, qkv4, qkv16, batch, seq):
    width = qkv1[0].shape[-1]
    nspan = seq // ATTN_SPAN
    blk = BAND_BLOCK
    n1 = ATTN_SPAN // blk
    n4 = ATTN_SPAN // (4 * blk)

    cur1 = pl.BlockSpec((None, ATTN_SPAN, LANES), lambda b, i, p: (b, i, p))
    prev1 = pl.BlockSpec((None, None, blk, LANES), lambda b, i, p: (b, jnp.maximum(i * n1 - 1, 0), 0, p))
    cur4 = pl.BlockSpec((None, n4, 4, blk, LANES), lambda b, i, p: (b, i, 0, 0, p))
    prev4 = pl.BlockSpec((None, None, 4, blk, LANES), lambda b, i, p: (b, jnp.maximum(i * n4 - 1, 0), 0, 0, p))
    cur16 = pl.BlockSpec((None, None, 16, blk, LANES), lambda b, i, p: (b, i, 0, 0, p))
    prev16 = pl.BlockSpec((None, None, 16, blk, LANES), lambda b, i, p: (b, jnp.maximum(i - 1, 0), 0, 0, p))

    q1, k1, v1 = qkv1
    q4, k4, v4 = qkv4
    q16, k16, v16 = qkv16
    k1b = k1.reshape(batch, seq // blk, blk, width)
    v1b = v1.reshape(batch, seq // blk, blk, width)
    return pl.pallas_call(
        _attn_kernel,
        grid=(batch, nspan, width // LANES),
        in_specs=[cur1, cur1, cur1, prev1, prev1,
                  cur4, cur4, cur4, prev4, prev4,
                  cur16, cur16, cur16, prev16, prev16],
        out_specs=pl.BlockSpec((None, ATTN_SPAN, LANES), lambda b, i, p: (b, i, p)),
        out_shape=jax.ShapeDtypeStruct((batch, seq, width), BF16),
        scratch_shapes=[pltpu.VMEM((3, ATTN_SPAN, LANES), F32), pltpu.VMEM((3, ATTN_SPAN, LANES), F32),
                        pltpu.VMEM((ATTN_SPAN, LANES), F32)],
        compiler_params=_params(("parallel", "parallel", "parallel")),
        name="dilated_attn",
    )(q1, k1, v1, k1b, v1b, q4, k4, v4, k4, v4, q16, k16, v16, k16, v16)


def _outproj_kernel(x_ref, a_ref, b_ref, wo_ref, mgb_ref, g1_ref, b1_ref, wr_ref, br_ref, tri_ref,
                    h_ref, hb_ref, idx_ref, gate_ref, rank_ref, idxt_ref, rankt_ref, runs_ref, cnt_ref,
                    run_scr, wob_scr):
    @pl.when(pl.program_id(0) == 0)
    def _():
        run_scr[...] = jnp.zeros_like(run_scr)
        wob_scr[...] = wo_ref[...].astype(BF16)

    width = a_ref.shape[-1]
    lane = lax.broadcasted_iota(I32, (ROUTE_ROWS, LANES), 1).astype(F32)

    def spread(cols):
        out = jnp.zeros((ROUTE_ROWS, LANES), F32)
        for k, cval in enumerate(cols):
            out = jnp.where(lane == float(k), cval, out)
        return out

    subs = [slice(j * ROUTE_ROWS, (j + 1) * ROUTE_ROWS) for j in range(OUT_ROWS // ROUTE_ROWS)]
    works = []
    for rows in subs:
        bf = b_ref[rows, :].astype(F32)
        bn = (bf * lax.rsqrt(jnp.mean(bf * bf, axis=-1, keepdims=True) + LN_EPS) * mgb_ref[...]).astype(BF16)
        mixed = _dot(a_ref[rows, :], wob_scr[0:width, :]) + _dot(bn, wob_scr[width:2 * width, :])
        z = DN_ALPHA * x_ref[rows, :] + mixed
        mu = jnp.mean(z, axis=-1, keepdims=True)
        zc = z - mu
        var = jnp.mean(zc * zc, axis=-1, keepdims=True)
        h = zc * lax.rsqrt(var + LN_EPS) * g1_ref[...] + b1_ref[...]
        h_ref[rows, :] = h
        hb = h.astype(BF16)
        hb_ref[rows, :] = hb
        works.append(_dot(hb, wr_ref[...]) + br_ref[...])

    vals = [[] for _ in subs]
    idxs = [[] for _ in subs]
    hots = [[] for _ in subs]
    for _ in range(TOP_K):
        for j in range(len(subs)):
            mv = jnp.max(works[j], axis=-1, keepdims=True)
            ix = jnp.min(jnp.where(works[j] == mv, lane, float(LANES)), axis=-1, keepdims=True)
            hot = lane == ix
            works[j] = jnp.where(hot, -jnp.inf, works[j])
            vals[j].append(mv)
            idxs[j].append(ix)
            hots[j].append(hot)

    members = []
    for j in range(len(subs)):
        member = jnp.zeros((ROUTE_ROWS, LANES), F32)
        for hot in hots[j]:
            member = jnp.where(hot, 1.0, member)
        members.append(member)
    withins = [_dot(tri_ref[...], member.astype(BF16)) for member in members]
    run = run_scr[...]
    for j, rows in enumerate(subs):
        runs_ref[j] = run
        before = withins[j] + run
        ranks = [jnp.sum(jnp.where(hot, before, 0.0), axis=-1, keepdims=True) for hot in hots[j]]
        run = run + jnp.sum(members[j], axis=0, keepdims=True)
        exps = [jnp.exp(v - vals[j][0]) for v in vals[j]]
        den = exps[0] + exps[1] + exps[2] + exps[3]
        idx_all = spread(idxs[j])
        rank_all = spread(ranks)
        idx_ref[rows, :] = idx_all.astype(I32)
        gate_ref[rows, :] = spread([e / den for e in exps])
        rank_ref[rows, :] = rank_all.astype(I32)
        idxt_ref[:, rows] = idx_all.T[:OCTET].astype(I32)
        rankt_ref[:, rows] = rank_all.T[:OCTET].astype(I32)
    run_scr[...] = run
    cnt_ref[...] = run


def _outproj(x2, a_n, b2, w_out, mgb, g1, b1, wr, br, tri):
    t, dm = x2.shape
    width = dm // 2
    const = lambda *shape: pl.BlockSpec(shape, lambda m: (0,) * len(shape))
    rowblk = lambda w: pl.BlockSpec((OUT_ROWS, w), lambda m: (m, 0))
    colblk = pl.BlockSpec((OCTET, OUT_ROWS), lambda m: (0, m))
    sub = OUT_ROWS // ROUTE_ROWS
    return pl.pallas_call(
        _outproj_kernel,
        grid=(t // OUT_ROWS,),
        in_specs=[rowblk(dm), rowblk(width), rowblk(width),
                  pl.BlockSpec((dm, dm), lambda m: (0, 0), pipeline_mode=pl.Buffered(1)), const(1, width),
                  const(1, dm), const(1, dm), const(dm, LANES), const(1, LANES),
                  const(ROUTE_ROWS, ROUTE_ROWS)],
        out_specs=[rowblk(dm), rowblk(dm), rowblk(LANES), rowblk(LANES), rowblk(LANES), colblk, colblk,
                   pl.BlockSpec((None, sub, 1, LANES), lambda m: (m, 0, 0, 0)), const(1, LANES)],
        out_shape=[jax.ShapeDtypeStruct((t, dm), F32), jax.ShapeDtypeStruct((t, dm), BF16),
                   jax.ShapeDtypeStruct((t, LANES), I32),
                   jax.ShapeDtypeStruct((t, LANES), F32), jax.ShapeDtypeStruct((t, LANES), I32),
                   jax.ShapeDtypeStruct((OCTET, t), I32), jax.ShapeDtypeStruct((OCTET, t), I32),
                   jax.ShapeDtypeStruct((t // OUT_ROWS, sub, 1, LANES), F32),
                   jax.ShapeDtypeStruct((1, LANES), F32)],
        scratch_shapes=[pltpu.VMEM((1, LANES), F32), pltpu.VMEM((dm, dm), BF16)],
        compiler_params=_params(("arbitrary",)),
        name="outproj_router",
    )(x2, a_n, b2, w_out, mgb, g1, b1, wr, br, tri)


def _octet(ref, q):
    return ref.at[pl.ds(pl.multiple_of(q * OCTET, OCTET), OCTET), :]


def _sub_block(ref, b):
    return ref.at[pl.ds(pl.multiple_of(b * MOE_SUB, MOE_SUB), MOE_SUB), :]


def _for_octets(n, start):
    def body(i, c):
        for j in range(4):
            start(4 * i + j, j % 2)
        return c

    lax.fori_loop(0, n // 4, body, 0)
    done = (n // 4) * 4

    @pl.when((n & 2) != 0)
    def _():
        start(done, 0)
        start(done + 1, 1)

    @pl.when((n & 1) != 0)
    def _():
        start(n - 1, 0)


def _wait_octets(n, ref, sem):
    for s in (128, 64, 32, 16, 8, 4, 2, 1):
        @pl.when((n & s) != 0)
        def _():
            d = ref.at[pl.ds(0, s * OCTET), :]
            pltpu.make_async_copy(d, d, sem).wait()


def _dispatch_kernel(nq_ref, fstart_ref, fcount_ref, gq_ref, h_ref, idxt_ref, rankt_ref, tabt_ref, xs_hbm,
                     stage, zeros, sem, zsem):
    s = pl.program_id(0)
    base = (s % 2) * PAIR
    eid = lax.broadcasted_iota(I32, (N_EXPERTS, ROUTE_ROWS), 0)
    pos = lax.broadcasted_iota(I32, (STAGE_ROWS, ROUTE_ROWS), 0).astype(F32)
    sels = []
    for j in range(PAIR):
        cols = slice(j * ROUTE_ROWS, (j + 1) * ROUTE_ROWS)
        sel = None
        for k in range(TOP_K):
            hot = eid == idxt_ref[k:k + 1, cols]
            lpos = (jnp.sum(jnp.where(hot, tabt_ref[j], 0.0), axis=0, keepdims=True)
                    + rankt_ref[k:k + 1, cols].astype(F32))
            hit = pos == lpos
            sel = hit if sel is None else jnp.logical_or(sel, hit)
        sels.append(jnp.where(sel, 1.0, 0.0).astype(BF16))
    for j in range(PAIR):
        rows = slice(j * ROUTE_ROWS, (j + 1) * ROUTE_ROWS)
        stage[base + j] = _dot(sels[j], h_ref[rows, :])

    @pl.when(s == 0)
    def _():
        zeros[...] = jnp.zeros_like(zeros)
        for g in range(N_EXPERTS):
            def zstart(i, c, g=g):
                pltpu.make_async_copy(_octet(zeros, 0), _octet(xs_hbm, fstart_ref[g] + i), zsem.at[0]).start()
                return c

            lax.fori_loop(0, fcount_ref[g], zstart, 0)

        def tstart(i, c):
            pltpu.make_async_copy(zeros, _sub_block(xs_hbm, fstart_ref[N_EXPERTS] + i), zsem.at[1]).start()
            return c

        lax.fori_loop(0, fcount_ref[N_EXPERTS], tstart, 0)

    for j in range(PAIR):
        _for_octets(nq_ref[s * PAIR + j], lambda q, pri, j=j: pltpu.make_async_copy(
            _octet(stage.at[base + j], q), _octet(xs_hbm, gq_ref[j, 0, q]), sem.at[base + j]).start(priority=pri))

    @pl.when(s > 0)
    def _():
        for j in range(PAIR):
            _wait_octets(nq_ref[jnp.maximum(s - 1, 0) * PAIR + j], xs_hbm, sem.at[PAIR - base + j])

    @pl.when(s == pl.num_programs(0) - 1)
    def _():
        for j in range(PAIR):
            _wait_octets(nq_ref[s * PAIR + j], xs_hbm, sem.at[base + j])

        def zwait(i, c):
            pltpu.make_async_copy(_octet(zeros, 0), _octet(xs_hbm, 0), zsem.at[0]).wait()
            return c

        for g in range(N_EXPERTS):
            lax.fori_loop(0, fcount_ref[g], zwait, 0)

        def twait(i, c):
            pltpu.make_async_copy(zeros, _sub_block(xs_hbm, 0), zsem.at[1]).wait()
            return c

        lax.fori_loop(0, fcount_ref[N_EXPERTS], twait, 0)


def _dispatch(h, idxt, rankt, tabt, gq3, nq, fill_start, fill_count, nrows):
    t, dm = h.shape
    step = PAIR * ROUTE_ROWS
    grid_spec = pltpu.PrefetchScalarGridSpec(
        num_scalar_prefetch=3,
        grid=(t // step,),
        in_specs=[pl.BlockSpec((PAIR, 1, STAGE_OCTETS), lambda m, *_: (m, 0, 0), memory_space=pltpu.SMEM),
                  pl.BlockSpec((step, dm), lambda m, *_: (m, 0)),
                  pl.BlockSpec((OCTET, step), lambda m, *_: (0, m)),
                  pl.BlockSpec((OCTET, step), lambda m, *_: (0, m)),
                  pl.BlockSpec((PAIR, N_EXPERTS, ROUTE_ROWS), lambda m, *_: (m, 0, 0))],
        out_specs=pl.BlockSpec(memory_space=pl.ANY),
        scratch_shapes=[pltpu.VMEM((2 * PAIR, STAGE_ROWS, dm), F32), pltpu.VMEM((MOE_SUB, dm), F32),
                        pltpu.SemaphoreType.DMA((2 * PAIR,)), pltpu.SemaphoreType.DMA((2,))],
    )
    return pl.pallas_call(
        _dispatch_kernel,
        grid_spec=grid_spec,
        out_shape=jax.ShapeDtypeStruct((nrows, dm), F32),
        compiler_params=_params(("arbitrary",)),
        name="moe_dispatch",
    )(nq, fill_start, fill_count, gq3, h, idxt, rankt, tabt)


def _moe_kernel(ib_ref, ie_ref, lo_ref, hi_ref, nxt_ref, xs_ref, bg_ref, bu_ref, bd_ref,
                wg_hbm, wu_hbm, wd_hbm, ys_ref, wf, wb, wsem):
    i = pl.program_id(0)
    prev = jnp.maximum(i - 1, 0)
    lo = lo_ref[i]
    hi = hi_ref[i]

    def weight_copies(e):
        return [pltpu.make_async_copy(w.at[e], wf.at[j], wsem) for j, w in enumerate((wg_hbm, wu_hbm, wd_hbm))]

    @pl.when(jnp.logical_and(lo >= 0, hi > lo))
    def _():
        @pl.when(jnp.logical_or(i == 0, ie_ref[i] != ie_ref[prev]))
        def _():
            @pl.when(i == 0)
            def _():
                for cp in weight_copies(ie_ref[i]):
                    cp.start()

            for cp in weight_copies(ie_ref[i]):
                cp.wait()
            for j in range(3):
                wb[j] = wf[j].astype(BF16)

            @pl.when(nxt_ref[i] >= 0)
            def _():
                for cp in weight_copies(nxt_ref[i]):
                    cp.start()

        def expert(rows):
            x = xs_ref[rows, :].astype(BF16)
            g = jnp.minimum(_dot(x, wb[0]) + bg_ref[...], SWIGLU_LIMIT)
            u = jnp.clip(_dot(x, wb[1]) + bu_ref[...], -SWIGLU_LIMIT, SWIGLU_LIMIT)
            act = (u + 1.0) * (g * jax.nn.sigmoid(SWIGLU_ALPHA * g))
            return _dot(act.astype(BF16), wb[2]) + bd_ref[...]

        whole = jnp.logical_and(lo == 0, hi == MOE_ROWS)

        @pl.when(whole)
        def _():
            for j in range(MOE_ROWS // MOE_CHUNK):
                rows = slice(j * MOE_CHUNK, (j + 1) * MOE_CHUNK)
                ys_ref[rows, :] = expert(rows)

        @pl.when(jnp.logical_not(whole))
        def _():
            @pl.when(jnp.logical_or(i == 0, ib_ref[i] != ib_ref[prev]))
            def _():
                ys_ref[...] = jnp.zeros_like(ys_ref)

            mine = [jnp.logical_and(lo < (j + 1) * MOE_SUB, hi > j * MOE_SUB) for j in range(MOE_ROWS // MOE_SUB)]
            for j in range(0, MOE_ROWS // MOE_SUB, 2):
                for first, second in ((True, True), (True, False), (False, True)):
                    rows = slice((j if first else j + 1) * MOE_SUB, (j + 2 if second else j + 1) * MOE_SUB)
                    cond = jnp.logical_and(mine[j] if first else jnp.logical_not(mine[j]),
                                           mine[j + 1] if second else jnp.logical_not(mine[j + 1]))

                    @pl.when(cond)
                    def _(rows=rows):
                        ys_ref[rows, :] = expert(rows)

    @pl.when(lo < 0)
    def _():
        ys_ref[...] = jnp.zeros_like(ys_ref)


def _moe(xs, items, w_gate, b_gate, w_up, b_up, w_down, b_down):
    nrows = xs.shape[0]
    ne, dm, df = w_gate.shape
    assert dm == df
    nitems = items[0].shape[0]
    bspec = lambda c: pl.BlockSpec((None, 1, c), lambda i, ib, ie, *_: (ie[i], 0, 0))
    rspec = pl.BlockSpec((MOE_ROWS, dm), lambda i, ib, *_: (ib[i], 0))
    hbm = pl.BlockSpec(memory_space=pl.ANY)
    grid_spec = pltpu.PrefetchScalarGridSpec(
        num_scalar_prefetch=5,
        grid=(nitems,),
        in_specs=[rspec, bspec(df), bspec(df), bspec(dm), hbm, hbm, hbm],
        out_specs=rspec,
        scratch_shapes=[pltpu.VMEM((3, dm, df), F32), pltpu.VMEM((3, dm, df), BF16),
                        pltpu.SemaphoreType.DMA(())],
    )
    return pl.pallas_call(
        _moe_kernel,
        grid_spec=grid_spec,
        out_shape=jax.ShapeDtypeStruct((nrows, dm), F32),
        compiler_params=_params(("arbitrary",)),
        name="moe_experts",
    )(*items, xs, b_gate.reshape(ne, 1, df), b_up.reshape(ne, 1, df), b_down.reshape(ne, 1, dm),
      w_gate, w_up, w_down)


def _work_items(counts, nrows):
    nblk = nrows // MOE_ROWS
    nitems = nblk + N_EXPERTS - 1
    ends = jnp.cumsum(counts)
    starts = ends - counts
    b0 = jnp.arange(nblk, dtype=I32)[:, None] * MOE_ROWS
    lo = jnp.maximum(starts[None, :], b0)
    hi = jnp.minimum(ends[None, :], b0 + MOE_ROWS)
    nonempty = (hi > lo).reshape(-1)
    csum = jnp.cumsum(nonempty.astype(I32))
    j = jnp.arange(nitems, dtype=I32)
    pos = jnp.sum(csum[None, :] <= j[:, None], axis=1).astype(I32)
    used = j < csum[-1]
    pos = jnp.where(used, pos, jnp.max(jnp.where(nonempty, jnp.arange(nonempty.shape[0], dtype=I32), 0)))
    ib = pos // N_EXPERTS
    ie = pos % N_EXPERTS
    ilo = jnp.where(used, lo.reshape(-1)[pos] - ib * MOE_ROWS, 0)
    ihi = jnp.where(used, hi.reshape(-1)[pos] - ib * MOE_ROWS, 0)
    spare_blk = ib + 1 + (j - csum[-1])
    fill = jnp.logical_and(jnp.logical_not(used), spare_blk < nblk)
    ib = jnp.where(used, ib, jnp.minimum(spare_blk, nblk - 1))
    ilo = jnp.where(fill, -1, ilo)
    ordinal = jnp.cumsum(jnp.concatenate([jnp.zeros((1,), I32), (ie[1:] != ie[:-1]).astype(I32)]))
    first_next = jnp.sum(ordinal[None, :] <= ordinal[:, None], axis=1)
    has_next = first_next < nitems
    nxt = jnp.where(has_next, ie[jnp.minimum(first_next, nitems - 1)], -1)
    return ib.astype(I32), ie.astype(I32), ilo.astype(I32), ihi.astype(I32), nxt.astype(I32)


def _combine_kernel(nq_ref, gq_cur, gq_nxt, h_ref, idx_ref, rank_ref, gate_ref, tab_ref, g2_ref, b2_ref, ys_hbm,
                    o_ref, stage, sem):
    s = pl.program_id(0)
    last = pl.num_programs(0) - 1
    base = (s % 2) * PAIR

    def start_fetch(gq_ref, step, slot0):
        for j in range(PAIR):
            _for_octets(nq_ref[jnp.minimum(step, last) * PAIR + j], lambda q, pri, j=j: pltpu.make_async_copy(
                _octet(ys_hbm, gq_ref[j, 0, q]), _octet(stage.at[slot0 + j], q),
                sem.at[slot0 + j]).start(priority=pri))

    @pl.when(s == 0)
    def _():
        stage[...] = jnp.zeros_like(stage)
        start_fetch(gq_cur, 0, 0)

    @pl.when(s < last)
    def _():
        start_fetch(gq_nxt, s + 1, PAIR - base)

    for j in range(PAIR):
        _wait_octets(nq_ref[s * PAIR + j], stage.at[base + j], sem.at[base + j])

    lane_e = lax.broadcasted_iota(I32, (ROUTE_ROWS, LANES), 1)
    lane_p = lax.broadcasted_iota(I32, (ROUTE_ROWS, STAGE_ROWS), 1).astype(F32)
    wsels = []
    for j in range(PAIR):
        rows = slice(j * ROUTE_ROWS, (j + 1) * ROUTE_ROWS)
        idx = idx_ref[rows, :]
        rank = rank_ref[rows, :].astype(F32)
        gates = gate_ref[rows, :]
        wsel = jnp.zeros((ROUTE_ROWS, STAGE_ROWS), F32)
        for k in range(TOP_K):
            hot = lane_e == idx[:, k:k + 1]
            lpos = jnp.sum(jnp.where(hot, tab_ref[j], 0.0), axis=-1, keepdims=True) + rank[:, k:k + 1]
            wsel = jnp.where(lane_p == lpos, gates[:, k:k + 1], wsel)
        wsels.append(wsel.astype(BF16))
    ys = [_dot(wsels[j], stage[base + j].astype(BF16)) for j in range(PAIR)]
    for j in range(PAIR):
        rows = slice(j * ROUTE_ROWS, (j + 1) * ROUTE_ROWS)
        z = DN_ALPHA * h_ref[rows, :] + ys[j]
        mu = jnp.mean(z, axis=-1, keepdims=True)
        zc = z - mu
        var = jnp.mean(zc * zc, axis=-1, keepdims=True)
        o_ref[rows, :] = zc * lax.rsqrt(var + LN_EPS) * g2_ref[...] + b2_ref[...]


def _combine(h, ys, idx, rank, gates, tab, gq3, nq, g2, b2):
    t, dm = h.shape
    step = PAIR * ROUTE_ROWS
    ns = t // step
    rowblk = lambda w: pl.BlockSpec((step, w), lambda m, nq: (m, 0))
    const = pl.BlockSpec((1, dm), lambda m, nq: (0, 0))
    qspec = lambda f: pl.BlockSpec((PAIR, 1, STAGE_OCTETS), lambda m, nq: (f(m), 0, 0), memory_space=pltpu.SMEM)
    grid_spec = pltpu.PrefetchScalarGridSpec(
        num_scalar_prefetch=1,
        grid=(ns,),
        in_specs=[qspec(lambda m: m), qspec(lambda m: jnp.minimum(m + 1, ns - 1)),
                  rowblk(dm), rowblk(LANES), rowblk(LANES), rowblk(LANES),
                  pl.BlockSpec((PAIR, 1, LANES), lambda m, nq: (m, 0, 0)), const, const,
                  pl.BlockSpec(memory_space=pl.ANY)],
        out_specs=rowblk(dm),
        scratch_shapes=[pltpu.VMEM((2 * PAIR, STAGE_ROWS, dm), F32), pltpu.SemaphoreType.DMA((2 * PAIR,))],
    )
    return pl.pallas_call(
        _combine_kernel,
        grid_spec=grid_spec,
        out_shape=jax.ShapeDtypeStruct((t, dm), F32),
        compiler_params=_params(("arbitrary",)),
        name="combine_ln",
    )(nq, gq3, gq3, h, idx, rank, gates, tab, g2, b2, ys)


def _route_tables(runs, total, nrows):
    nt = runs.shape[0]
    nxt = jnp.concatenate([runs[1:], total[None, :]], axis=0)
    c8 = (nxt - runs + OCTET - 1) // OCTET
    q_end = jnp.cumsum(c8, axis=1)
    q0 = q_end - c8
    nq = q_end[:, -1]
    per_expert = jnp.sum(c8, axis=0)
    sub8 = MOE_SUB // OCTET
    per_pad = (per_expert + sub8 - 1) // sub8 * sub8
    start8 = jnp.cumsum(per_pad) - per_pad
    g8 = start8[None, :] + jnp.cumsum(c8, axis=0) - c8
    tab = OCTET * q0 - runs
    q = jnp.arange(STAGE_OCTETS, dtype=I32)
    e_of_q = jnp.minimum(jnp.sum(q[None, :, None] >= q_end[:, None, :], axis=2), N_EXPERTS - 1)
    pick = e_of_q[:, :, None] == jnp.arange(N_EXPERTS, dtype=I32)[None, None, :]
    gq = jnp.sum(jnp.where(pick, (g8 - q0)[:, None, :], 0), axis=2) + q[None, :]
    gq = jnp.where(q[None, :] < nq[:, None], gq, 0).astype(I32)
    tot8 = jnp.sum(per_pad)
    fill_start = jnp.concatenate([start8 + per_expert, (tot8 // sub8)[None]]).astype(I32)
    fill_count = jnp.concatenate([per_pad - per_expert, (nrows // MOE_SUB - tot8 // sub8)[None]]).astype(I32)
    return (tab, gq.reshape(nt, 1, STAGE_OCTETS), nq.astype(I32), fill_start, fill_count,
            (per_pad * OCTET).astype(I32))


def _layer(x, w_in, sgu_w, sgu_b, sgu_ln_g, sgu_ln_b, mix_norm_g, w_out, ln1_g, ln1_b,
           w_router, b_router, w_gate, b_gate, w_up, b_up, w_down, b_down, ln2_g, ln2_b):
    batch, seq, dm = x.shape
    width = dm // 2
    t = batch * seq
    assert seq % ATTN_SPAN == 0 and dm % (2 * LANES) == 0 and w_router.shape[-1] == N_EXPERTS
    x2 = x.reshape(t, dm)

    wc = jnp.tril(sgu_w)
    wpair = jnp.concatenate([wc[0::2], wc[1::2]], axis=-1).astype(BF16)
    sbias = jnp.repeat(sgu_b.T, HEAD_DIM, axis=1)
    grp = jnp.arange(width) // HEAD_DIM
    gmat = jnp.where(grp[:, None] == grp[None, :], 1.0 / HEAD_DIM, 0.0).astype(BF16)
    row = lambda v: v.reshape(1, -1)

    a_n, q1, k1, v1, q4, k4, v4, q16, k16, v16 = _project(
        x2, w_in, gmat, wpair, sbias, row(sgu_ln_g), row(sgu_ln_b),
        row(mix_norm_g[:width]), batch, seq)
    b = _attention((q1, k1, v1), (q4, k4, v4), (q16, k16, v16), batch, seq)

    wr = jnp.pad(w_router, ((0, 0), (0, LANES - N_EXPERTS))).astype(BF16)
    br = jnp.concatenate([b_router.astype(F32), jnp.full((LANES - N_EXPERTS,), -1e30, F32)]).reshape(1, LANES)
    ti = jnp.arange(ROUTE_ROWS)
    tri = (ti[None, :] < ti[:, None]).astype(BF16)
    h, hb, idx, gates, rank, idxt, rankt, runs, cnt = _outproj(
        x2, a_n.reshape(t, width), b.reshape(t, width), w_out,
        row(mix_norm_g[width:]), row(ln1_g), row(ln1_b), wr, br, tri)

    nt = t // ROUTE_ROWS
    worst = t * TOP_K + nt * N_EXPERTS * (OCTET - 1) + N_EXPERTS * (MOE_SUB - OCTET)
    nrows = (worst + MOE_ROWS - 1) // MOE_ROWS * MOE_ROWS
    runs = runs.reshape(nt, LANES)[:, :N_EXPERTS].astype(I32)
    total = cnt[0, :N_EXPERTS].astype(I32)
    tab, gq3, nq, fill_start, fill_count, rows_e = _route_tables(runs, total, nrows)
    tabf = tab.astype(F32)
    tab_lane = jnp.pad(tabf, ((0, 0), (0, LANES - N_EXPERTS))).reshape(nt, 1, LANES)
    tab_sub = jnp.broadcast_to(tabf[:, :, None], (nt, N_EXPERTS, ROUTE_ROWS))

    xs = _dispatch(hb, idxt, rankt, tab_sub, gq3, nq, fill_start, fill_count, nrows)
    ys = _moe(xs, _work_items(rows_e, nrows), w_gate, b_gate, w_up, b_up, w_down, b_down)
    out = _combine(h, ys, idx, rank, gates, tab_lane, gq3, nq, row(ln2_g), row(ln2_b))
    return out.reshape(batch, seq, dm)


def kernel(x, w_in, sgu_w, sgu_b, sgu_ln_g, sgu_ln_b, mix_norm_g, w_out, ln1_g, ln1_b, w_router, b_router,
           w_gate, b_gate, w_up, b_up, w_down, b_down, ln2_g, ln2_b):
    assert w_in.shape[0] == DEPTH
    return _layer(x, w_in[0], sgu_w[0], sgu_b[0], sgu_ln_g[0], sgu_ln_b[0], mix_norm_g[0], w_out[0],
                  ln1_g[0], ln1_b[0], w_router[0], b_router[0], w_gate[0], b_gate[0], w_up[0], b_up[0],
                  w_down[0], b_down[0], ln2_g[0], ln2_b[0])
```

```python
import jax
import jax.numpy as jnp
from jax import lax
from jax.experimental import pallas as pl
from jax.experimental.pallas import tpu as pltpu

F32 = jnp.float32
BF16 = jnp.bfloat16
I32 = jnp.int32

LN_EPS = 1e-5
HEAD_DIM = 64
SGU_CHUNK = 128
BAND_BLOCK = 128
DILATIONS = (1, 4, 16)
ATTN_SPAN = BAND_BLOCK * DILATIONS[-1]
N_EXPERTS = 32
TOP_K = 4
SWIGLU_ALPHA = 1.702
SWIGLU_LIMIT = 7.0
DEPTH = 1
DN_ALPHA = (2 * DEPTH) ** 0.25
LOG2E = 1.4426950408889634

LANES = 128
PROJ_ROWS = 512
OUT_ROWS = 1024
MOE_ROWS = 1024
MOE_CHUNK = 512
MOE_SUB = 256
ROUTE_ROWS = 256
OCTET = 8
STAGE_ROWS = ROUTE_ROWS * TOP_K + N_EXPERTS * OCTET
STAGE_OCTETS = STAGE_ROWS // OCTET
PAIR = 2
VMEM_LIMIT = 56 * 1024 * 1024


def _params(sem):
    return pltpu.CompilerParams(dimension_semantics=sem, vmem_limit_bytes=VMEM_LIMIT)


def _dot(a, b):
    return jnp.dot(a, b, preferred_element_type=F32)


def _proj_kernel(x_ref, w_ref, gmat_ref, wpair_ref, sbias_ref, lng_ref, lnb_ref, mg_ref,
                 a_ref, q1_ref, k1_ref, v1_ref, q4_ref, k4_ref, v4_ref, q16_ref, k16_ref, v16_ref,
                 a_scr, t_scr, t4_scr, wb_scr):
    width = a_ref.shape[-1]

    @pl.when(jnp.logical_and(pl.program_id(0) == 0, pl.program_id(1) == 0))
    def _():
        wb_scr[...] = w_ref[...].astype(BF16)

    xb = x_ref[...].astype(BF16)

    def proj(c):
        return _dot(xb, wb_scr[:, c * width:(c + 1) * width])

    u = jax.nn.gelu(proj(0))
    v = jax.nn.gelu(proj(1))
    mean = _dot(v.astype(BF16), gmat_ref[...])
    d = v - mean
    var = _dot((d * d).astype(BF16), gmat_ref[...])
    vn = (d * lax.rsqrt(var + LN_EPS) * lng_ref[...] + lnb_ref[...]).astype(BF16)

    lane = lax.broadcasted_iota(I32, (SGU_CHUNK, LANES), 1)
    low = lane < HEAD_DIM
    zero = jnp.zeros((SGU_CHUNK, LANES), BF16)
    for c in range(PROJ_ROWS // SGU_CHUNK):
        rows = slice(c * SGU_CHUNK, (c + 1) * SGU_CHUNK)
        for j in range(width // LANES):
            cols = slice(j * LANES, (j + 1) * LANES)
            vp = vn[rows, cols]
            rhs = jnp.concatenate([jnp.where(low, vp, zero), jnp.where(low, zero, vp)], axis=0)
            gate = _dot(wpair_ref[j], rhs) + sbias_ref[:, cols]
            a_scr[rows, cols] = u[rows, cols] * gate
    a = a_scr[...]
    ms = jnp.mean(a * a, axis=-1, keepdims=True)
    a_ref[...] = (a * lax.rsqrt(ms + LN_EPS) * mg_ref[...]).astype(BF16)

    outs = ((q1_ref, q4_ref, q16_ref), (k1_ref, k4_ref, k16_ref), (v1_ref, v4_ref, v16_ref))
    for c, (o1, o4, o16) in enumerate(outs):
        t = proj(2 + c)
        if c == 0:
            t = t * (HEAD_DIM ** -0.5 * LOG2E)
        o1[...] = t.astype(BF16)
        for j in range(width // LANES):
            cols = slice(j * LANES, (j + 1) * LANES)
            t_scr[j] = t[:, cols]
            for b in range(4):
                t4 = t_scr[j, pl.ds(b, PROJ_ROWS // 4, stride=4), :]
                o4[b, :, cols] = t4.astype(BF16)
                t4_scr[b] = t4
                for a in range(4):
                    o16[4 * a + b, :, cols] = t4_scr[b, pl.ds(a, PROJ_ROWS // 16, stride=4), :].astype(BF16)


def _project(x2, w_in, gmat, wpair, sbias, lng, lnb, mg, batch, seq):
    t, dm = x2.shape
    width = dm // 2
    nt = seq // PROJ_ROWS
    per_span = ATTN_SPAN // PROJ_ROWS
    const = lambda *shape: pl.BlockSpec(shape, lambda b, m: (0,) * len(shape))
    o1 = jax.ShapeDtypeStruct((batch, seq, width), BF16)
    o4 = jax.ShapeDtypeStruct((batch, seq // 512, 4, BAND_BLOCK, width), BF16)
    o16 = jax.ShapeDtypeStruct((batch, seq // ATTN_SPAN, 16, BAND_BLOCK, width), BF16)
    s1 = pl.BlockSpec((None, PROJ_ROWS, width), lambda b, m: (b, m, 0))
    s4 = pl.BlockSpec((None, None, 4, BAND_BLOCK, width), lambda b, m: (b, m, 0, 0, 0))
    s16 = pl.BlockSpec((None, None, 16, PROJ_ROWS // 16, width),
                       lambda b, m: (b, m // per_span, 0, m % per_span, 0))
    return pl.pallas_call(
        _proj_kernel,
        grid=(batch, nt),
        in_specs=[pl.BlockSpec((PROJ_ROWS, dm), lambda b, m: (b * nt + m, 0)),
                  pl.BlockSpec(w_in.shape, lambda b, m: (0, 0), pipeline_mode=pl.Buffered(1)),
                  const(*gmat.shape), const(*wpair.shape), const(*sbias.shape),
                  const(1, width), const(1, width), const(1, width)],
        out_specs=[s1] + [s1, s1, s1] + [s4, s4, s4] + [s16, s16, s16],
        out_shape=[o1] + [o1, o1, o1] + [o4, o4, o4] + [o16, o16, o16],
        scratch_shapes=[pltpu.VMEM((PROJ_ROWS, width), F32),
                        pltpu.VMEM((width // LANES, PROJ_ROWS, LANES), F32),
                        pltpu.VMEM((4, PROJ_ROWS // 4, LANES), F32),
                        pltpu.VMEM(w_in.shape, BF16)],
        compiler_params=_params(("arbitrary", "arbitrary")),
        name="proj_sgu",
    )(x2, w_in, gmat, wpair, sbias, lng, lnb, mg)


def _attn_kernel(q1, k1, v1, kp1, vp1, q4, k4, v4, kp4, vp4, q16, k16, v16, kp16, vp16,
                 o_ref, buf1, buf4, out_scr):
    not_first = pl.program_id(1) > 0
    blk = BAND_BLOCK
    row = lax.broadcasted_iota(I32, (2 * blk, 2 * blk), 0) % blk
    col = lax.broadcasted_iota(I32, (2 * blk, 2 * blk), 1)
    band = jnp.logical_or(jnp.logical_and(col < blk, col >= row), jnp.logical_and(col >= blk, col - blk <= row))
    neg = jnp.where(not_first, 0.0, -jnp.inf).astype(F32)
    bias = jnp.where(band, 0.0, -jnp.inf).astype(F32)
    bias_first = bias + jnp.where(col < blk, neg, 0.0)
    low = lax.broadcasted_iota(I32, (blk, LANES), 1) < HEAD_DIM
    zero = jnp.zeros((blk, LANES), BF16)
    ones = jnp.ones((2 * blk, LANES), BF16)

    def unit(q, kprev, kcur, vprev, vcur, maybe_first):
        kk = jnp.concatenate([kprev, kcur], axis=0)
        vv = jnp.concatenate([vprev, vcur], axis=0)
        qq = jnp.concatenate([jnp.where(low, q, zero), jnp.where(low, zero, q)], axis=0)
        s = lax.dot_general(qq, kk, (((1,), (1,)), ((), ())), preferred_element_type=F32)
        s = s + (bias_first if maybe_first else bias)
        m2 = jnp.max(s, axis=-1, keepdims=True)
        p = jnp.exp2(s - m2)
        o2 = _dot(p.astype(BF16), jnp.concatenate([vv, ones], axis=1))
        o = jnp.where(low, o2[:blk, :LANES], o2[blk:, :LANES])
        m = jnp.where(low, m2[:blk], m2[blk:])
        l = jnp.where(low, o2[:blk, LANES:], o2[blk:, LANES:])
        return o, jnp.broadcast_to(m, o.shape), l

    def put(buf, rows, oml):
        for a in range(3):
            buf[a, rows, :] = oml[a]


    for n in range(ATTN_SPAN // blk):
        cur = pl.ds(n * blk, blk)
        if n == 0:
            put(buf1, cur, unit(q1[cur, :], kp1[...], k1[cur, :], vp1[...], v1[cur, :], True))
        else:
            prev = pl.ds((n - 1) * blk, blk)
            put(buf1, cur, unit(q1[cur, :], k1[prev, :], k1[cur, :], v1[prev, :], v1[cur, :], False))

    for s in range(ATTN_SPAN // (4 * blk)):
        for r in range(4):
            rows = pl.ds((4 * s + r) * blk, blk)
            if s == 0:
                put(buf4, rows, unit(q4[0, r], kp4[r], k4[0, r], vp4[r], v4[0, r], True))
            else:
                put(buf4, rows, unit(q4[s, r], k4[s - 1, r], k4[s, r], v4[s - 1, r], v4[s, r], False))

    per = blk // (ATTN_SPAN // (4 * blk))
    for r in range(16):
        o16, m16, l16 = unit(q16[r], kp16[r], k16[r], vp16[r], v16[r], True)
        tok = pl.ds(r, blk, stride=16)
        o1, m1, l1 = (buf1[a, tok, :] for a in range(3))
        o4, m4, l4 = (jnp.concatenate(
            [buf4[a, pl.ds((4 * s + r % 4) * blk + r // 4, per, stride=4), :]
             for s in range(ATTN_SPAN // (4 * blk))], axis=0) for a in range(3))
        mn = jnp.maximum(jnp.maximum(m1, m4), m16)
        e1 = jnp.exp2(m1 - mn)
        e4 = jnp.exp2(m4 - mn)
        e16 = jnp.exp2(m16 - mn)
        num = o1 * e1 + o4 * e4 + o16 * e16
        den = l1 * e1 + l4 * e4 + l16 * e16
        out_scr[tok, :] = num / den

    o_ref[...] = out_scr[...].astype(BF16)


def _attention(qkv1, qkv4, qkv16, batch, seq):
    width = qkv1[0].shape[-1]
    nspan = seq // ATTN_SPAN
    blk = BAND_BLOCK
    n1 = ATTN_SPAN // blk
    n4 = ATTN_SPAN // (4 * blk)

    cur1 = pl.BlockSpec((None, ATTN_SPAN, LANES), lambda b, i, p: (b, i, p))
    prev1 = pl.BlockSpec((None, None, blk, LANES), lambda b, i, p: (b, jnp.maximum(i * n1 - 1, 0), 0, p))
    cur4 = pl.BlockSpec((None, n4, 4, blk, LANES), lambda b, i, p: (b, i, 0, 0, p))
    prev4 = pl.BlockSpec((None, None, 4, blk, LANES), lambda b, i, p: (b, jnp.maximum(i * n4 - 1, 0), 0, 0, p))
    cur16 = pl.BlockSpec((None, None, 16, blk, LANES), lambda b, i, p: (b, i, 0, 0, p))
    prev16 = pl.BlockSpec((None, None, 16, blk, LANES), lambda b, i, p: (b, jnp.maximum(i - 1, 0), 0, 0, p))

    q1, k1, v1 = qkv1
    q4, k4, v4 = qkv4
    q16, k16, v16 = qkv16
    k1b = k1.reshape(batch, seq // blk, blk, width)
    v1b = v1.reshape(batch, seq // blk, blk, width)
    return pl.pallas_call(
        _attn_kernel,
        grid=(batch, nspan, width // LANES),
        in_specs=[cur1, cur1, cur1, prev1, prev1,
                  cur4, cur4, cur4, prev4, prev4,
                  cur16, cur16, cur16, prev16, prev16],
        out_specs=pl.BlockSpec((None, ATTN_SPAN, LANES), lambda b, i, p: (b, i, p)),
        out_shape=jax.ShapeDtypeStruct((batch, seq, width), BF16),
        scratch_shapes=[pltpu.VMEM((3, ATTN_SPAN, LANES), F32), pltpu.VMEM((3, ATTN_SPAN, LANES), F32),
                        pltpu.VMEM((ATTN_SPAN, LANES), F32)],
        compiler_params=_params(("parallel", "parallel", "parallel")),
        name="dilated_attn",
    )(q1, k1, v1, k1b, v1b, q4, k4, v4, k4, v4, q16, k16, v16, k16, v16)


def _outproj_kernel(x_ref, a_ref, b_ref, wo_ref, mgb_ref, g1_ref, b1_ref, wr_ref, br_ref, tri_ref,
                    h_ref, idx_ref, gate_ref, rank_ref, idxt_ref, rankt_ref, runs_ref, cnt_ref, run_scr, wob_scr):
    @pl.when(pl.program_id(0) == 0)
    def _():
        run_scr[...] = jnp.zeros_like(run_scr)
        wob_scr[...] = wo_ref[...].astype(BF16)

    width = a_ref.shape[-1]
    lane = lax.broadcasted_iota(I32, (ROUTE_ROWS, LANES), 1).astype(F32)

    def spread(cols):
        out = jnp.zeros((ROUTE_ROWS, LANES), F32)
        for k, cval in enumerate(cols):
            out = jnp.where(lane == float(k), cval, out)
        return out

    subs = [slice(j * ROUTE_ROWS, (j + 1) * ROUTE_ROWS) for j in range(OUT_ROWS // ROUTE_ROWS)]
    works = []
    for rows in subs:
        bf = b_ref[rows, :].astype(F32)
        bn = (bf * lax.rsqrt(jnp.mean(bf * bf, axis=-1, keepdims=True) + LN_EPS) * mgb_ref[...]).astype(BF16)
        mixed = _dot(a_ref[rows, :], wob_scr[0:width, :]) + _dot(bn, wob_scr[width:2 * width, :])
        z = DN_ALPHA * x_ref[rows, :] + mixed
        mu = jnp.mean(z, axis=-1, keepdims=True)
        zc = z - mu
        var = jnp.mean(zc * zc, axis=-1, keepdims=True)
        h = zc * lax.rsqrt(var + LN_EPS) * g1_ref[...] + b1_ref[...]
        h_ref[rows, :] = h
        works.append(_dot(h.astype(BF16), wr_ref[...]) + br_ref[...])

    vals = [[] for _ in subs]
    idxs = [[] for _ in subs]
    hots = [[] for _ in subs]
    for _ in range(TOP_K):
        for j in range(len(subs)):
            mv = jnp.max(works[j], axis=-1, keepdims=True)
            ix = jnp.min(jnp.where(works[j] == mv, lane, float(LANES)), axis=-1, keepdims=True)
            hot = lane == ix
            works[j] = jnp.where(hot, -jnp.inf, works[j])
            vals[j].append(mv)
            idxs[j].append(ix)
            hots[j].append(hot)

    members = []
    for j in range(len(subs)):
        member = jnp.zeros((ROUTE_ROWS, LANES), F32)
        for hot in hots[j]:
            member = jnp.where(hot, 1.0, member)
        members.append(member)
    withins = [_dot(tri_ref[...], member.astype(BF16)) for member in members]
    run = run_scr[...]
    for j, rows in enumerate(subs):
        runs_ref[j] = run
        before = withins[j] + run
        ranks = [jnp.sum(jnp.where(hot, before, 0.0), axis=-1, keepdims=True) for hot in hots[j]]
        run = run + jnp.sum(members[j], axis=0, keepdims=True)
        exps = [jnp.exp(v - vals[j][0]) for v in vals[j]]
        den = exps[0] + exps[1] + exps[2] + exps[3]
        idx_all = spread(idxs[j])
        rank_all = spread(ranks)
        idx_ref[rows, :] = idx_all.astype(I32)
        gate_ref[rows, :] = spread([e / den for e in exps])
        rank_ref[rows, :] = rank_all.astype(I32)
        idxt_ref[:, rows] = idx_all.T[:OCTET].astype(I32)
        rankt_ref[:, rows] = rank_all.T[:OCTET].astype(I32)
    run_scr[...] = run
    cnt_ref[...] = run


def _outproj(x2, a_n, b2, w_out, mgb, g1, b1, wr, br, tri):
    t, dm = x2.shape
    width = dm // 2
    const = lambda *shape: pl.BlockSpec(shape, lambda m: (0,) * len(shape))
    rowblk = lambda w: pl.BlockSpec((OUT_ROWS, w), lambda m: (m, 0))
    colblk = pl.BlockSpec((OCTET, OUT_ROWS), lambda m: (0, m))
    sub = OUT_ROWS // ROUTE_ROWS
    return pl.pallas_call(
        _outproj_kernel,
        grid=(t // OUT_ROWS,),
        in_specs=[rowblk(dm), rowblk(width), rowblk(width),
                  pl.BlockSpec((dm, dm), lambda m: (0, 0), pipeline_mode=pl.Buffered(1)), const(1, width),
                  const(1, dm), const(1, dm), const(dm, LANES), const(1, LANES),
                  const(ROUTE_ROWS, ROUTE_ROWS)],
        out_specs=[rowblk(dm), rowblk(LANES), rowblk(LANES), rowblk(LANES), colblk, colblk,
                   pl.BlockSpec((None, sub, 1, LANES), lambda m: (m, 0, 0, 0)), const(1, LANES)],
        out_shape=[jax.ShapeDtypeStruct((t, dm), F32), jax.ShapeDtypeStruct((t, LANES), I32),
                   jax.ShapeDtypeStruct((t, LANES), F32), jax.ShapeDtypeStruct((t, LANES), I32),
                   jax.ShapeDtypeStruct((OCTET, t), I32), jax.ShapeDtypeStruct((OCTET, t), I32),
                   jax.ShapeDtypeStruct((t // OUT_ROWS, sub, 1, LANES), F32),
                   jax.ShapeDtypeStruct((1, LANES), F32)],
        scratch_shapes=[pltpu.VMEM((1, LANES), F32), pltpu.VMEM((dm, dm), BF16)],
        compiler_params=_params(("arbitrary",)),
        name="outproj_router",
    )(x2, a_n, b2, w_out, mgb, g1, b1, wr, br, tri)


def _octet(ref, q):
    return ref.at[pl.ds(pl.multiple_of(q * OCTET, OCTET), OCTET), :]


def _sub_block(ref, b):
    return ref.at[pl.ds(pl.multiple_of(b * MOE_SUB, MOE_SUB), MOE_SUB), :]


def _for_octets(n, start):
    def body(i, c):
        for j in range(4):
            start(4 * i + j, j % 2)
        return c

    lax.fori_loop(0, n // 4, body, 0)
    done = (n // 4) * 4

    @pl.when((n & 2) != 0)
    def _():
        start(done, 0)
        start(done + 1, 1)

    @pl.when((n & 1) != 0)
    def _():
        start(n - 1, 0)


def _wait_octets(n, ref, sem):
    for s in (128, 64, 32, 16, 8, 4, 2, 1):
        @pl.when((n & s) != 0)
        def _():
            d = ref.at[pl.ds(0, s * OCTET), :]
            pltpu.make_async_copy(d, d, sem).wait()


def _dispatch_kernel(nq_ref, fstart_ref, fcount_ref, gq_ref, h_ref, idxt_ref, rankt_ref, tabt_ref, xs_hbm,
                     stage, zeros, sem, zsem):
    s = pl.program_id(0)
    base = (s % 2) * PAIR
    eid = lax.broadcasted_iota(I32, (N_EXPERTS, ROUTE_ROWS), 0)
    pos = lax.broadcasted_iota(I32, (STAGE_ROWS, ROUTE_ROWS), 0).astype(F32)
    sels = []
    for j in range(PAIR):
        cols = slice(j * ROUTE_ROWS, (j + 1) * ROUTE_ROWS)
        sel = None
        for k in range(TOP_K):
            hot = eid == idxt_ref[k:k + 1, cols]
            lpos = (jnp.sum(jnp.where(hot, tabt_ref[j], 0.0), axis=0, keepdims=True)
                    + rankt_ref[k:k + 1, cols].astype(F32))
            hit = pos == lpos
            sel = hit if sel is None else jnp.logical_or(sel, hit)
        sels.append(jnp.where(sel, 1.0, 0.0))
    for j in range(PAIR):
        rows = slice(j * ROUTE_ROWS, (j + 1) * ROUTE_ROWS)
        stage[base + j] = _dot(sels[j], h_ref[rows, :])

    @pl.when(s == 0)
    def _():
        zeros[...] = jnp.zeros_like(zeros)
        for g in range(N_EXPERTS):
            def zstart(i, c, g=g):
                pltpu.make_async_copy(_octet(zeros, 0), _octet(xs_hbm, fstart_ref[g] + i), zsem.at[0]).start()
                return c

            lax.fori_loop(0, fcount_ref[g], zstart, 0)

        def tstart(i, c):
            pltpu.make_async_copy(zeros, _sub_block(xs_hbm, fstart_ref[N_EXPERTS] + i), zsem.at[1]).start()
            return c

        lax.fori_loop(0, fcount_ref[N_EXPERTS], tstart, 0)

    for j in range(PAIR):
        _for_octets(nq_ref[s * PAIR + j], lambda q, pri, j=j: pltpu.make_async_copy(
            _octet(stage.at[base + j], q), _octet(xs_hbm, gq_ref[j, 0, q]), sem.at[base + j]).start(priority=pri))

    @pl.when(s > 0)
    def _():
        for j in range(PAIR):
            _wait_octets(nq_ref[jnp.maximum(s - 1, 0) * PAIR + j], xs_hbm, sem.at[PAIR - base + j])

    @pl.when(s == pl.num_programs(0) - 1)
    def _():
        for j in range(PAIR):
            _wait_octets(nq_ref[s * PAIR + j], xs_hbm, sem.at[base + j])

        def zwait(i, c):
            pltpu.make_async_copy(_octet(zeros, 0), _octet(xs_hbm, 0), zsem.at[0]).wait()
            return c

        for g in range(N_EXPERTS):
            lax.fori_loop(0, fcount_ref[g], zwait, 0)

        def twait(i, c):
            pltpu.make_async_copy(zeros, _sub_block(xs_hbm, 0), zsem.at[1]).wait()
            return c

        lax.fori_loop(0, fcount_ref[N_EXPERTS], twait, 0)


def _dispatch(h, idxt, rankt, tabt, gq3, nq, fill_start, fill_count, nrows):
    t, dm = h.shape
    step = PAIR * ROUTE_ROWS
    grid_spec = pltpu.PrefetchScalarGridSpec(
        num_scalar_prefetch=3,
        grid=(t // step,),
        in_specs=[pl.BlockSpec((PAIR, 1, STAGE_OCTETS), lambda m, *_: (m, 0, 0), memory_space=pltpu.SMEM),
                  pl.BlockSpec((step, dm), lambda m, *_: (m, 0)),
                  pl.BlockSpec((OCTET, step), lambda m, *_: (0, m)),
                  pl.BlockSpec((OCTET, step), lambda m, *_: (0, m)),
                  pl.BlockSpec((PAIR, N_EXPERTS, ROUTE_ROWS), lambda m, *_: (m, 0, 0))],
        out_specs=pl.BlockSpec(memory_space=pl.ANY),
        scratch_shapes=[pltpu.VMEM((2 * PAIR, STAGE_ROWS, dm), F32), pltpu.VMEM((MOE_SUB, dm), F32),
                        pltpu.SemaphoreType.DMA((2 * PAIR,)), pltpu.SemaphoreType.DMA((2,))],
    )
    return pl.pallas_call(
        _dispatch_kernel,
        grid_spec=grid_spec,
        out_shape=jax.ShapeDtypeStruct((nrows, dm), F32),
        compiler_params=_params(("arbitrary",)),
        name="moe_dispatch",
    )(nq, fill_start, fill_count, gq3, h, idxt, rankt, tabt)


def _moe_kernel(ib_ref, ie_ref, lo_ref, hi_ref, nxt_ref, slot_ref, tailb_ref, xs_ref, bg_ref, bu_ref, bd_ref,
                wg_hbm, wu_hbm, wd_hbm, ys_ref, wf, wsem):
    i = pl.program_id(0)
    prev = jnp.maximum(i - 1, 0)
    lo = lo_ref[i]
    hi = hi_ref[i]

    def weight_copies(e, sl):
        return [pltpu.make_async_copy(w.at[e], wf.at[sl, j], wsem.at[sl])
                for j, w in enumerate((wg_hbm, wu_hbm, wd_hbm))]

    sl = slot_ref[i]

    @pl.when(jnp.logical_and(lo >= 0, hi > lo))
    def _():
        @pl.when(jnp.logical_or(i == 0, ie_ref[i] != ie_ref[prev]))
        def _():
            @pl.when(i == 0)
            def _():
                for cp in weight_copies(ie_ref[i], sl):
                    cp.start()

            for cp in weight_copies(ie_ref[i], sl):
                cp.wait()

            @pl.when(nxt_ref[i] >= 0)
            def _():
                for cp in weight_copies(nxt_ref[i], 1 - sl):
                    cp.start()

        def expert(rows):
            x = xs_ref[rows, :]
            g = jnp.minimum(_dot(x, wf[sl, 0]) + bg_ref[...], SWIGLU_LIMIT)
            u = jnp.clip(_dot(x, wf[sl, 1]) + bu_ref[...], -SWIGLU_LIMIT, SWIGLU_LIMIT)
            act = (u + 1.0) * (g * jax.nn.sigmoid(SWIGLU_ALPHA * g))
            return _dot(act, wf[sl, 2]) + bd_ref[...]

        whole = jnp.logical_and(lo == 0, hi == MOE_ROWS)

        @pl.when(whole)
        def _():
            for j in range(MOE_ROWS // MOE_CHUNK):
                rows = slice(j * MOE_CHUNK, (j + 1) * MOE_CHUNK)
                ys_ref[rows, :] = expert(rows)

        @pl.when(jnp.logical_not(whole))
        def _():
            @pl.when(jnp.logical_and(jnp.logical_or(i == 0, ib_ref[i] != ib_ref[prev]), ib_ref[i] == tailb_ref[0]))
            def _():
                ys_ref[...] = jnp.zeros_like(ys_ref)

            mine = [jnp.logical_and(lo < (j + 1) * MOE_SUB, hi > j * MOE_SUB) for j in range(MOE_ROWS // MOE_SUB)]
            for j in range(0, MOE_ROWS // MOE_SUB, 2):
                for first, second in ((True, True), (True, False), (False, True)):
                    rows = slice((j if first else j + 1) * MOE_SUB, (j + 2 if second else j + 1) * MOE_SUB)
                    cond = jnp.logical_and(mine[j] if first else jnp.logical_not(mine[j]),
                                           mine[j + 1] if second else jnp.logical_not(mine[j + 1]))

                    @pl.when(cond)
                    def _(rows=rows):
                        ys_ref[rows, :] = expert(rows)

    @pl.when(lo < 0)
    def _():
        ys_ref[...] = jnp.zeros_like(ys_ref)


def _moe(xs, items, w_gate, b_gate, w_up, b_up, w_down, b_down):
    nrows = xs.shape[0]
    ne, dm, df = w_gate.shape
    assert dm == df
    nitems = items[0].shape[0]
    bspec = lambda c: pl.BlockSpec((None, 1, c), lambda i, ib, ie, *_: (ie[i], 0, 0))
    rspec = pl.BlockSpec((MOE_ROWS, dm), lambda i, ib, *_: (ib[i], 0))
    hbm = pl.BlockSpec(memory_space=pl.ANY)
    grid_spec = pltpu.PrefetchScalarGridSpec(
        num_scalar_prefetch=7,
        grid=(nitems,),
        in_specs=[rspec, bspec(df), bspec(df), bspec(dm), hbm, hbm, hbm],
        out_specs=rspec,
        scratch_shapes=[pltpu.VMEM((2, 3, dm, df), F32), pltpu.SemaphoreType.DMA((2,))],
    )
    return pl.pallas_call(
        _moe_kernel,
        grid_spec=grid_spec,
        out_shape=jax.ShapeDtypeStruct((nrows, dm), F32),
        compiler_params=_params(("arbitrary",)),
        name="moe_experts",
    )(*items, xs, b_gate.reshape(ne, 1, df), b_up.reshape(ne, 1, df), b_down.reshape(ne, 1, dm),
      w_gate, w_up, w_down)


def _work_items(counts, nrows):
    nblk = nrows // MOE_ROWS
    nitems = nblk + N_EXPERTS - 1
    ends = jnp.cumsum(counts)
    starts = ends - counts
    b0 = jnp.arange(nblk, dtype=I32)[:, None] * MOE_ROWS
    lo = jnp.maximum(starts[None, :], b0)
    hi = jnp.minimum(ends[None, :], b0 + MOE_ROWS)
    nonempty = (hi > lo).reshape(-1)
    csum = jnp.cumsum(nonempty.astype(I32))
    j = jnp.arange(nitems, dtype=I32)
    pos = jnp.sum(csum[None, :] <= j[:, None], axis=1).astype(I32)
    used = j < csum[-1]
    pos = jnp.where(used, pos, jnp.max(jnp.where(nonempty, jnp.arange(nonempty.shape[0], dtype=I32), 0)))
    ib = pos // N_EXPERTS
    ie = pos % N_EXPERTS
    ilo = jnp.where(used, lo.reshape(-1)[pos] - ib * MOE_ROWS, 0)
    ihi = jnp.where(used, hi.reshape(-1)[pos] - ib * MOE_ROWS, 0)
    spare_blk = ib + 1 + (j - csum[-1])
    fill = jnp.logical_and(jnp.logical_not(used), spare_blk < nblk)
    tail_blk = ib[-1:]
    ib = jnp.where(used, ib, jnp.minimum(spare_blk, nblk - 1))
    ilo = jnp.where(fill, -1, ilo)
    ordinal = jnp.cumsum(jnp.concatenate([jnp.zeros((1,), I32), (ie[1:] != ie[:-1]).astype(I32)]))
    first_next = jnp.sum(ordinal[None, :] <= ordinal[:, None], axis=1)
    has_next = first_next < nitems
    nxt = jnp.where(has_next, ie[jnp.minimum(first_next, nitems - 1)], -1)
    return (ib.astype(I32), ie.astype(I32), ilo.astype(I32), ihi.astype(I32), nxt.astype(I32),
            (ordinal % 2).astype(I32), tail_blk.astype(I32))


def _combine_kernel(nq_ref, gq_cur, gq_nxt, h_ref, idx_ref, rank_ref, gate_ref, tab_ref, g2_ref, b2_ref, ys_hbm,
                    o_ref, stage, sem):
    s = pl.program_id(0)
    last = pl.num_programs(0) - 1
    base = (s % 2) * PAIR

    def start_fetch(gq_ref, step, slot0):
        for j in range(PAIR):
            _for_octets(nq_ref[jnp.minimum(step, last) * PAIR + j], lambda q, pri, j=j: pltpu.make_async_copy(
                _octet(ys_hbm, gq_ref[j, 0, q]), _octet(stage.at[slot0 + j], q),
                sem.at[slot0 + j]).start(priority=pri))

    @pl.when(s == 0)
    def _():
        stage[...] = jnp.zeros_like(stage)
        start_fetch(gq_cur, 0, 0)

    @pl.when(s < last)
    def _():
        start_fetch(gq_nxt, s + 1, PAIR - base)

    for j in range(PAIR):
        _wait_octets(nq_ref[s * PAIR + j], stage.at[base + j], sem.at[base + j])

    lane_e = lax.broadcasted_iota(I32, (ROUTE_ROWS, LANES), 1)
    lane_p = lax.broadcasted_iota(I32, (ROUTE_ROWS, STAGE_ROWS), 1).astype(F32)
    wsels = []
    for j in range(PAIR):
        rows = slice(j * ROUTE_ROWS, (j + 1) * ROUTE_ROWS)
        idx = idx_ref[rows, :]
        rank = rank_ref[rows, :].astype(F32)
        gates = gate_ref[rows, :]
        wsel = jnp.zeros((ROUTE_ROWS, STAGE_ROWS), F32)
        for k in range(TOP_K):
            hot = lane_e == idx[:, k:k + 1]
            lpos = jnp.sum(jnp.where(hot, tab_ref[j], 0.0), axis=-1, keepdims=True) + rank[:, k:k + 1]
            wsel = jnp.where(lane_p == lpos, gates[:, k:k + 1], wsel)
        wsels.append(wsel)
    ys = [_dot(wsels[j], stage[base + j]) for j in range(PAIR)]
    for j in range(PAIR):
        rows = slice(j * ROUTE_ROWS, (j + 1) * ROUTE_ROWS)
        z = DN_ALPHA * h_ref[rows, :] + ys[j]
        mu = jnp.mean(z, axis=-1, keepdims=True)
        zc = z - mu
        var = jnp.mean(zc * zc, axis=-1, keepdims=True)
        o_ref[rows, :] = zc * lax.rsqrt(var + LN_EPS) * g2_ref[...] + b2_ref[...]


def _combine(h, ys, idx, rank, gates, tab, gq3, nq, g2, b2):
    t, dm = h.shape
    step = PAIR * ROUTE_ROWS
    ns = t // step
    rowblk = lambda w: pl.BlockSpec((step, w), lambda m, nq: (m, 0))
    const = pl.BlockSpec((1, dm), lambda m, nq: (0, 0))
    qspec = lambda f: pl.BlockSpec((PAIR, 1, STAGE_OCTETS), lambda m, nq: (f(m), 0, 0), memory_space=pltpu.SMEM)
    grid_spec = pltpu.PrefetchScalarGridSpec(
        num_scalar_prefetch=1,
        grid=(ns,),
        in_specs=[qspec(lambda m: m), qspec(lambda m: jnp.minimum(m + 1, ns - 1)),
                  rowblk(dm), rowblk(LANES), rowblk(LANES), rowblk(LANES),
                  pl.BlockSpec((PAIR, 1, LANES), lambda m, nq: (m, 0, 0)), const, const,
                  pl.BlockSpec(memory_space=pl.ANY)],
        out_specs=rowblk(dm),
        scratch_shapes=[pltpu.VMEM((2 * PAIR, STAGE_ROWS, dm), F32), pltpu.SemaphoreType.DMA((2 * PAIR,))],
    )
    return pl.pallas_call(
        _combine_kernel,
        grid_spec=grid_spec,
        out_shape=jax.ShapeDtypeStruct((t, dm), F32),
        compiler_params=_params(("arbitrary",)),
        name="combine_ln",
    )(nq, gq3, gq3, h, idx, rank, gates, tab, g2, b2, ys)


def _route_tables(runs, total, nrows):
    nt = runs.shape[0]
    nxt = jnp.concatenate([runs[1:], total[None, :]], axis=0)
    c8 = (nxt - runs + OCTET - 1) // OCTET
    q_end = jnp.cumsum(c8, axis=1)
    q0 = q_end - c8
    nq = q_end[:, -1]
    per_expert = jnp.sum(c8, axis=0)
    sub8 = MOE_SUB // OCTET
    per_pad = (per_expert + sub8 - 1) // sub8 * sub8
    start8 = jnp.cumsum(per_pad) - per_pad
    g8 = start8[None, :] + jnp.cumsum(c8, axis=0) - c8
    tab = OCTET * q0 - runs
    q = jnp.arange(STAGE_OCTETS, dtype=I32)
    e_of_q = jnp.minimum(jnp.sum(q[None, :, None] >= q_end[:, None, :], axis=2), N_EXPERTS - 1)
    pick = e_of_q[:, :, None] == jnp.arange(N_EXPERTS, dtype=I32)[None, None, :]
    gq = jnp.sum(jnp.where(pick, (g8 - q0)[:, None, :], 0), axis=2) + q[None, :]
    gq = jnp.where(q[None, :] < nq[:, None], gq, 0).astype(I32)
    tot8 = jnp.sum(per_pad)
    fill_start = jnp.concatenate([start8 + per_expert, (tot8 // sub8)[None]]).astype(I32)
    fill_count = jnp.concatenate([per_pad - per_expert, (nrows // MOE_SUB - tot8 // sub8)[None]]).astype(I32)
    return (tab, gq.reshape(nt, 1, STAGE_OCTETS), nq.astype(I32), fill_start, fill_count,
            (per_pad * OCTET).astype(I32))


def _layer(x, w_in, sgu_w, sgu_b, sgu_ln_g, sgu_ln_b, mix_norm_g, w_out, ln1_g, ln1_b,
           w_router, b_router, w_gate, b_gate, w_up, b_up, w_down, b_down, ln2_g, ln2_b):
    batch, seq, dm = x.shape
    width = dm // 2
    t = batch * seq
    assert seq % ATTN_SPAN == 0 and dm % (2 * LANES) == 0 and w_router.shape[-1] == N_EXPERTS
    x2 = x.reshape(t, dm)

    wc = jnp.tril(sgu_w)
    wpair = jnp.concatenate([wc[0::2], wc[1::2]], axis=-1).astype(BF16)
    sbias = jnp.repeat(sgu_b.T, HEAD_DIM, axis=1)
    grp = jnp.arange(width) // HEAD_DIM
    gmat = jnp.where(grp[:, None] == grp[None, :], 1.0 / HEAD_DIM, 0.0).astype(BF16)
    row = lambda v: v.reshape(1, -1)

    a_n, q1, k1, v1, q4, k4, v4, q16, k16, v16 = _project(
        x2, w_in, gmat, wpair, sbias, row(sgu_ln_g), row(sgu_ln_b),
        row(mix_norm_g[:width]), batch, seq)
    b = _attention((q1, k1, v1), (q4, k4, v4), (q16, k16, v16), batch, seq)

    wr = jnp.pad(w_router, ((0, 0), (0, LANES - N_EXPERTS))).astype(BF16)
    br = jnp.concatenate([b_router.astype(F32), jnp.full((LANES - N_EXPERTS,), -1e30, F32)]).reshape(1, LANES)
    ti = jnp.arange(ROUTE_ROWS)
    tri = (ti[None, :] < ti[:, None]).astype(BF16)
    h, idx, gates, rank, idxt, rankt, runs, cnt = _outproj(
        x2, a_n.reshape(t, width), b.reshape(t, width), w_out,
        row(mix_norm_g[width:]), row(ln1_g), row(ln1_b), wr, br, tri)

    nt = t // ROUTE_ROWS
    worst = t * TOP_K + nt * N_EXPERTS * (OCTET - 1) + N_EXPERTS * (MOE_SUB - OCTET)
    nrows = (worst + MOE_ROWS - 1) // MOE_ROWS * MOE_ROWS
    runs = runs.reshape(nt, LANES)[:, :N_EXPERTS].astype(I32)
    total = cnt[0, :N_EXPERTS].astype(I32)
    tab, gq3, nq, fill_start, fill_count, rows_e = _route_tables(runs, total, nrows)
    tabf = tab.astype(F32)
    tab_lane = jnp.pad(tabf, ((0, 0), (0, LANES - N_EXPERTS))).reshape(nt, 1, LANES)
    tab_sub = jnp.broadcast_to(tabf[:, :, None], (nt, N_EXPERTS, ROUTE_ROWS))

    xs = _dispatch(h, idxt, rankt, tab_sub, gq3, nq, fill_start, fill_count, nrows)
    ys = _moe(xs, _work_items(rows_e, nrows), w_gate, b_gate, w_up, b_up, w_down, b_down)
    out = _combine(h, ys, idx, rank, gates, tab_lane, gq3, nq, row(ln2_g), row(ln2_b))
    return out.reshape(batch, seq, dm)


def kernel(x, w_in, sgu_w, sgu_b, sgu_ln_g, sgu_ln_b, mix_norm_g, w_out, ln1_g, ln1_b, w_router, b_router,
           w_gate, b_gate, w_up, b_up, w_down, b_down, ln2_g, ln2_b):
    assert w_in.shape[0] == DEPTH
    return _layer(x, w_in[0], sgu_w[0], sgu_b[0], sgu_ln_g[0], sgu_ln_b[0], mix_norm_g[0], w_out[0],
                  ln1_g[0], ln1_b[0], w_router[0], b_router[0], w_gate[0], b_gate[0], w_up[0], b_up[0],
                  w_down[0], b_down[0], ln2_g[0], ln2_b[0])
```
